```python
import jax, jax.numpy as jnp
from jax import lax
import numpy as np

D_MODEL = 2048
BATCH = 8
SEQ = 8192
DEPTH = 4

CTX_LEN = 256
GRID_W = 64
MIX_W = 2 * D_MODEL
SSD_W = MIX_W // 2
SSD_HEADS = 32
SSD_HEAD_DIM = SSD_W // SSD_HEADS
SSD_GROUPS = 8
SSD_STATE = 128
SSD_CHUNK = 128
CONV_W = 5
MLP_W = MIX_W - SSD_W
MLP_GROUPS = 16
MLP_GROUP_DIM = MLP_W // MLP_GROUPS
MLP_CHUNK = 128
GN = SSD_GROUPS * SSD_STATE
XBC_W = SSD_W + 2 * GN
DT_W = 2 * SSD_HEADS
IN_W = XBC_W + DT_W + SSD_W + 3 * MLP_W
EPS = 1e-6

kernel_name = "hybrid_ssd_chunkmlp_prefix_dit"


def _rmsnorm(x, g):
    xf = x.astype(jnp.float32)
    r = lax.rsqrt(jnp.mean(xf * xf, axis=-1, keepdims=True) + EPS)
    return (xf * r).astype(x.dtype) * g


def _dwconv_rows(x, w, b, n_rows, row_len):
    bsz, L, C = x.shape
    pad = CONV_W // 2
    xp = jnp.pad(x.reshape(bsz, n_rows, row_len, C), ((0, 0), (0, 0), (pad, pad), (0, 0)))
    y = b
    for k in range(CONV_W):
        y = y + xp[:, :, k:k + row_len] * w[k]
    return y.reshape(bsz, L, C)


def _ssd(xh, dt, A, Bm, Cm, h0, with_output):
    f32 = jnp.float32
    bsz, L, H, P = xh.shape
    G, N, Q = SSD_GROUPS, SSD_STATE, SSD_CHUNK
    R = H // G
    nc = L // Q
    x = xh.astype(f32).reshape(bsz, nc, Q, G, R, P)
    dtc = dt.reshape(bsz, nc, Q, G, R)
    a = dtc * A.reshape(G, R)
    xdt = x * dtc[..., None]
    Bc = Bm.astype(f32).reshape(bsz, nc, Q, G, N)
    Cc = Cm.astype(f32).reshape(bsz, nc, Q, G, N)
    acs = jnp.cumsum(a, axis=2)
    a_tot = acs[:, :, -1]
    decay_to_end = jnp.exp(a_tot[:, :, None] - acs)
    states = jnp.einsum('bckgn,bckgr,bckgrp->bcgrpn', Bc, decay_to_end, xdt)

    def step(h, inp):
        s, at = inp
        return h * jnp.exp(at)[..., None, None] + s, h

    hT, h_starts = lax.scan(step, h0.reshape(bsz, G, R, P, N),
                            (jnp.moveaxis(states, 1, 0), jnp.moveaxis(a_tot, 1, 0)))
    final = hT.reshape(bsz, H, P, N)
    if not with_output:
        return None, final
    h_starts = jnp.moveaxis(h_starts, 0, 1)
    CB = jnp.einsum('bcqgn,bckgn->bcgqk', Cc, Bc)
    acs_t = jnp.moveaxis(acs, 2, -1)
    diff = acs_t[..., :, None] - acs_t[..., None, :]
    mask = jnp.tril(jnp.ones((Q, Q), dtype=bool))
    Lmat = jnp.exp(jnp.where(mask, diff, -jnp.inf))
    y_diag = jnp.einsum('bcgqk,bcgrqk,bckgrp->bcqgrp', CB, Lmat, xdt)
    y_off = jnp.einsum('bcqgn,bcgrpn,bcqgr->bcqgrp', Cc, h_starts, jnp.exp(acs))
    return (y_diag + y_off).reshape(bsz, L, H, P), final


def _ssd_prep(z, conv_w, conv_b, dt_bias, a_log, n_rows, row_len):
    bsz, L, _ = z.shape
    xbc = jax.nn.silu(_dwconv_rows(z[..., :XBC_W], conv_w, conv_b, n_rows, row_len))
    xh = xbc[..., :SSD_W].reshape(bsz, L, SSD_HEADS, SSD_HEAD_DIM)
    Bm = xbc[..., SSD_W:SSD_W + GN].reshape(bsz, L, SSD_GROUPS, SSD_STATE)
    Cm = xbc[..., SSD_W + GN:XBC_W].reshape(bsz, L, SSD_GROUPS, SSD_STATE)
    dt_raw = z[..., XBC_W:XBC_W + DT_W].astype(jnp.float32).reshape(bsz, L, 2, SSD_HEADS)
    dt = jax.nn.softplus(dt_raw + dt_bias.astype(jnp.float32))
    A = -jnp.exp(a_log.astype(jnp.float32))
    return xh, Bm, Cm, dt, A


def _ssd_bidir(xh, Bm, Cm, dt, A, h0f, h0b, with_output):
    flip = lambda t: jnp.flip(t, axis=1)
    y_f, h_f = _ssd(xh, dt[:, :, 0], A[0], Bm, Cm, h0f, with_output)
    y_b, h_b = _ssd(flip(xh), flip(dt[:, :, 1]), A[1], flip(Bm), flip(Cm), h0b, with_output)
    if not with_output:
        return None, h_f, h_b
    return y_f + flip(y_b), h_f, h_b


def _mix_out(z, xh, y_ssd, d_skip, g_ssd, g_v, w_s, b_s, g_mlp, w_out):
    bsz, L, _ = z.shape
    o = XBC_W + DT_W
    z_ssd = z[..., o:o + SSD_W]
    u = z[..., o + SSD_W:o + SSD_W + MLP_W]
    v = z[..., o + SSD_W + MLP_W:o + SSD_W + 2 * MLP_W]
    z_mlp = z[..., o + SSD_W + 2 * MLP_W:]
    y = (y_ssd + d_skip.astype(jnp.float32)[:, None] * xh.astype(jnp.float32))
    y = y.reshape(bsz, L, SSD_W).astype(z.dtype)
    y_a = _rmsnorm(y * jax.nn.silu(z_ssd), g_ssd)
    vn = _rmsnorm(v, g_v).reshape(bsz, L // MLP_CHUNK, MLP_CHUNK, MLP_GROUPS, MLP_GROUP_DIM)
    sg = jnp.einsum('gqk,bckgd->bcqgd', w_s, vn) + jnp.swapaxes(b_s, 0, 1)[:, :, None]
    y_b = _rmsnorm(u * sg.reshape(bsz, L, MLP_W) * jax.nn.silu(z_mlp), g_mlp)
    return jnp.concatenate([y_a, y_b], axis=-1) @ w_out


def _fwd_setup_inputs(seed: int = 0) -> dict:
    key = jax.random.key(seed)
    ks = jax.random.split(key, 20)
    D = D_MODEL
    nrm = jax.random.normal
    x = nrm(ks[0], (BATCH, SEQ, D), jnp.float32)
    c = nrm(ks[1], (BATCH, D), jnp.float32)
    ctx = nrm(ks[2], (BATCH, CTX_LEN, D), jnp.float32)
    c_ctx = nrm(ks[3], (D,), jnp.float32)
    w_ada = nrm(ks[4], (DEPTH, D, 3 * D), jnp.float32) * (0.5 * D ** -0.5)
    b_ada = 0.01 * nrm(ks[5], (DEPTH, 3 * D), jnp.float32)
    g_pre = 1.0 + 0.05 * nrm(ks[6], (DEPTH, D), jnp.float32)
    g_post = 1.0 + 0.05 * nrm(ks[7], (DEPTH, D), jnp.float32)
    w_in = nrm(ks[8], (DEPTH, D, IN_W), jnp.float32) * D ** -0.5
    conv_w = nrm(ks[9], (DEPTH, CONV_W, XBC_W), jnp.float32) * CONV_W ** -0.5
    conv_b = 0.01 * nrm(ks[10], (DEPTH, XBC_W), jnp.float32)
    u_dt = jax.random.uniform(ks[11], (DEPTH, 2, SSD_HEADS), jnp.float32)
    dt0 = jnp.exp(u_dt * (np.log(0.1) - np.log(0.001)) + np.log(0.001))
    dt_bias = dt0 + jnp.log(-jnp.expm1(-dt0))
    a_log = jnp.log(jax.random.uniform(ks[12], (DEPTH, 2, SSD_HEADS), jnp.float32, 1.0, 16.0))
    d_skip = 1.0 + 0.1 * nrm(ks[13], (DEPTH, SSD_HEADS), jnp.float32)
    g_ssd = 1.0 + 0.05 * nrm(ks[14], (DEPTH, SSD_W), jnp.float32)
    g_v = 1.0 + 0.05 * nrm(ks[15], (DEPTH, MLP_W), jnp.float32)
    w_s = nrm(ks[16], (DEPTH, MLP_GROUPS, MLP_CHUNK, MLP_CHUNK), jnp.float32) * (0.5 * MLP_CHUNK ** -0.5)
    b_s = 1.0 + 0.05 * nrm(ks[17], (DEPTH, MLP_GROUPS, MLP_CHUNK), jnp.float32)
    g_mlp = 1.0 + 0.05 * nrm(ks[18], (DEPTH, MLP_W), jnp.float32)
    w_out = nrm(ks[19], (DEPTH, MIX_W, D), jnp.float32) * MIX_W ** -0.5
    return {"x": x, "c": c, "ctx": ctx, "c_ctx": c_ctx, "w_ada": w_ada, "b_ada": b_ada,
            "g_pre": g_pre, "g_post": g_post, "w_in": w_in, "conv_w": conv_w, "conv_b": conv_b,
            "dt_bias": dt_bias, "a_log": a_log, "d_skip": d_skip, "g_ssd": g_ssd, "g_v": g_v,
            "w_s": w_s, "b_s": b_s, "g_mlp": g_mlp, "w_out": w_out}


def _fwd_reference(x, c, ctx, c_ctx, w_ada, b_ada, g_pre, g_post, w_in, conv_w, conv_b,
              dt_bias, a_log, d_skip, g_ssd, g_v, w_s, b_s, g_mlp, w_out):
    bsz, L, _ = x.shape
    ROWS = L // GRID_W
    sc = jax.nn.silu(c)
    scc = jax.nn.silu(c_ctx)
    h0 = jnp.zeros((bsz, SSD_HEADS, SSD_HEAD_DIM, SSD_STATE), jnp.float32)
    for l in range(DEPTH):
        last = l == DEPTH - 1
        shift, scale, gate = jnp.split(sc @ w_ada[l] + b_ada[l], 3, axis=-1)
        shift_c, scale_c, gate_c = jnp.split(scc @ w_ada[l] + b_ada[l], 3, axis=-1)
        hc = _rmsnorm(ctx, g_pre[l]) * (1.0 + scale_c) + shift_c
        zc = hc @ (w_in[l][:, :XBC_W + DT_W] if last else w_in[l])
        xh_c, B_c, C_c, dt_c, A = _ssd_prep(zc, conv_w[l], conv_b[l], dt_bias[l], a_log[l], 1, CTX_LEN)
        y_c, h_f, h_b = _ssd_bidir(xh_c, B_c, C_c, dt_c, A, h0, h0, not last)
        hx = _rmsnorm(x, g_pre[l]) * (1.0 + scale[:, None]) + shift[:, None]
        zx = hx @ w_in[l]
        xh, Bm, Cm, dt, A = _ssd_prep(zx, conv_w[l], conv_b[l], dt_bias[l], a_log[l], ROWS, GRID_W)
        y_x, _, _ = _ssd_bidir(xh, Bm, Cm, dt, A, h_f, h_b, True)
        out = _rmsnorm(_mix_out(zx, xh, y_x, d_skip[l], g_ssd[l], g_v[l], w_s[l], b_s[l],
                                g_mlp[l], w_out[l]), g_post[l])
        x = x + gate[:, None] * out
        if not last:
            out_c = _rmsnorm(_mix_out(zc, xh_c, y_c, d_skip[l], g_ssd[l], g_v[l], w_s[l], b_s[l],
                                      g_mlp[l], w_out[l]), g_post[l])
            ctx = ctx + gate_c * out_c
    return x


import jax as _jax
import jax.numpy as _jnp

TWIN_FORMAT = 'train_step'
FWD_PARAMS = ['x', 'c', 'ctx', 'c_ctx', 'w_ada', 'b_ada', 'g_pre', 'g_post', 'w_in', 'conv_w', 'conv_b', 'dt_bias', 'a_log', 'd_skip', 'g_ssd', 'g_v', 'w_s', 'b_s', 'g_mlp', 'w_out']
TWIN_WEIGHTS = ['c_ctx', 'w_ada', 'b_ada', 'g_pre', 'g_post', 'w_in', 'conv_w', 'conv_b', 'dt_bias', 'a_log', 'd_skip', 'g_ssd', 'g_v', 'w_s', 'b_s', 'g_mlp', 'w_out']
TWIN_DIFF_INPUT = 'x'
TWIN_INPUTS = ['x', 'c', 'ctx', 'c_ctx', 'w_ada', 'b_ada', 'g_pre', 'g_post', 'w_in', 'conv_w', 'conv_b', 'dt_bias', 'a_log', 'd_skip', 'g_ssd', 'g_v', 'w_s', 'b_s', 'g_mlp', 'w_out', 'loss_target', 'm_c_ctx', 'm_w_ada', 'm_b_ada', 'm_g_pre', 'm_g_post', 'm_w_in', 'm_conv_w', 'm_conv_b', 'm_dt_bias', 'm_a_log', 'm_d_skip', 'm_g_ssd', 'm_g_v', 'm_w_s', 'm_b_s', 'm_g_mlp', 'm_w_out', 'v_c_ctx', 'v_w_ada', 'v_b_ada', 'v_g_pre', 'v_g_post', 'v_w_in', 'v_conv_w', 'v_conv_b', 'v_dt_bias', 'v_a_log', 'v_d_skip', 'v_g_ssd', 'v_g_v', 'v_w_s', 'v_b_s', 'v_g_mlp', 'v_w_out']
TWIN_OUTPUTS = ['loss', 'grad_x', 'grad_c_ctx', 'grad_w_ada', 'grad_b_ada', 'grad_g_pre', 'grad_g_post', 'grad_w_in', 'grad_conv_w', 'grad_conv_b', 'grad_dt_bias', 'grad_a_log', 'grad_d_skip', 'grad_g_ssd', 'grad_g_v', 'grad_w_s', 'grad_b_s', 'grad_g_mlp', 'grad_w_out', 'delta_c_ctx', 'delta_w_ada', 'delta_b_ada', 'delta_g_pre', 'delta_g_post', 'delta_w_in', 'delta_conv_w', 'delta_conv_b', 'delta_dt_bias', 'delta_a_log', 'delta_d_skip', 'delta_g_ssd', 'delta_g_v', 'delta_w_s', 'delta_b_s', 'delta_g_mlp', 'delta_w_out', 'new_m_c_ctx', 'new_m_w_ada', 'new_m_b_ada', 'new_m_g_pre', 'new_m_g_post', 'new_m_w_in', 'new_m_conv_w', 'new_m_conv_b', 'new_m_dt_bias', 'new_m_a_log', 'new_m_d_skip', 'new_m_g_ssd', 'new_m_g_v', 'new_m_w_s', 'new_m_b_s', 'new_m_g_mlp', 'new_m_w_out', 'new_v_c_ctx', 'new_v_w_ada', 'new_v_b_ada', 'new_v_g_pre', 'new_v_g_post', 'new_v_w_in', 'new_v_conv_w', 'new_v_conv_b', 'new_v_dt_bias', 'new_v_a_log', 'new_v_d_skip', 'new_v_g_ssd', 'new_v_g_v', 'new_v_w_s', 'new_v_b_s', 'new_v_g_mlp', 'new_v_w_out']
TWIN_LEAF_KINDS = {'loss': 'loss', 'grad_x': 'grad_x', 'grad_c_ctx': 'grad_w', 'grad_w_ada': 'grad_w', 'grad_b_ada': 'grad_w', 'grad_g_pre': 'grad_w', 'grad_g_post': 'grad_w', 'grad_w_in': 'grad_w', 'grad_conv_w': 'grad_w', 'grad_conv_b': 'grad_w', 'grad_dt_bias': 'grad_w', 'grad_a_log': 'grad_w', 'grad_d_skip': 'grad_w', 'grad_g_ssd': 'grad_w', 'grad_g_v': 'grad_w', 'grad_w_s': 'grad_w', 'grad_b_s': 'grad_w', 'grad_g_mlp': 'grad_w', 'grad_w_out': 'grad_w', 'delta_c_ctx': 'delta_w', 'delta_w_ada': 'delta_w', 'delta_b_ada': 'delta_w', 'delta_g_pre': 'delta_w', 'delta_g_post': 'delta_w', 'delta_w_in': 'delta_w', 'delta_conv_w': 'delta_w', 'delta_conv_b': 'delta_w', 'delta_dt_bias': 'delta_w', 'delta_a_log': 'delta_w', 'delta_d_skip': 'delta_w', 'delta_g_ssd': 'delta_w', 'delta_g_v': 'delta_w', 'delta_w_s': 'delta_w', 'delta_b_s': 'delta_w', 'delta_g_mlp': 'delta_w', 'delta_w_out': 'delta_w', 'new_m_c_ctx': 'new_m', 'new_m_w_ada': 'new_m', 'new_m_b_ada': 'new_m', 'new_m_g_pre': 'new_m', 'new_m_g_post': 'new_m', 'new_m_w_in': 'new_m', 'new_m_conv_w': 'new_m', 'new_m_conv_b': 'new_m', 'new_m_dt_bias': 'new_m', 'new_m_a_log': 'new_m', 'new_m_d_skip': 'new_m', 'new_m_g_ssd': 'new_m', 'new_m_g_v': 'new_m', 'new_m_w_s': 'new_m', 'new_m_b_s': 'new_m', 'new_m_g_mlp': 'new_m', 'new_m_w_out': 'new_m', 'new_v_c_ctx': 'new_v', 'new_v_w_ada': 'new_v', 'new_v_b_ada': 'new_v', 'new_v_g_pre': 'new_v', 'new_v_g_post': 'new_v', 'new_v_w_in': 'new_v', 'new_v_conv_w': 'new_v', 'new_v_conv_b': 'new_v', 'new_v_dt_bias': 'new_v', 'new_v_a_log': 'new_v', 'new_v_d_skip': 'new_v', 'new_v_g_ssd': 'new_v', 'new_v_g_v': 'new_v', 'new_v_w_s': 'new_v', 'new_v_b_s': 'new_v', 'new_v_g_mlp': 'new_v', 'new_v_w_out': 'new_v'}


def _forward(args):
    return _fwd_reference(*[args[k] for k in FWD_PARAMS])


def _output_shape():
    def fwd():
        inp = _fwd_setup_inputs(0)
        return _fwd_reference(*[inp[k] for k in FWD_PARAMS])
    out = _jax.eval_shape(fwd)
    return out.shape, out.dtype

N_MICROBATCH = 1
ADAM_LR = 0.001
ADAM_B1 = 0.9
ADAM_B2 = 0.999
ADAM_EPS = 1e-08
ADAM_WD = 0.01
ADAM_STEP = 10
PER_EXAMPLE_BATCH_AXIS = {'x': 0, 'c': 0, 'ctx': 0, 'loss_target': 0}
SHARED_INPUTS = []
_WEIGHT_DTYPES = {'c_ctx': _jnp.float32, 'w_ada': _jnp.float32, 'b_ada': _jnp.float32, 'g_pre': _jnp.float32, 'g_post': _jnp.float32, 'w_in': _jnp.float32, 'conv_w': _jnp.float32, 'conv_b': _jnp.float32, 'dt_bias': _jnp.float32, 'a_log': _jnp.float32, 'd_skip': _jnp.float32, 'g_ssd': _jnp.float32, 'g_v': _jnp.float32, 'w_s': _jnp.float32, 'b_s': _jnp.float32, 'g_mlp': _jnp.float32, 'w_out': _jnp.float32}
MOMENT_SCALE = {'c_ctx': 7.655628e-03, 'w_ada': 1.173778e+00, 'b_ada': 2.533483e+00, 'g_pre': 9.708292e-02, 'g_post': 3.257035e+00, 'w_in': 4.819247e-02, 'conv_w': 5.404735e-02, 'conv_b': 1.091828e-01, 'dt_bias': 1.331406e-01, 'a_log': 1.656499e-01, 'd_skip': 2.231590e-01, 'g_ssd': 9.383191e-02, 'g_v': 2.189584e-02, 'w_s': 4.178626e-02, 'b_s': 4.094183e-02, 'g_mlp': 5.606758e-02, 'w_out': 1.029721e-01}


def _to_microbatches(a, axis):
    t = _jnp.moveaxis(a, axis, 0)
    t = t.reshape((N_MICROBATCH, t.shape[0] // N_MICROBATCH) + t.shape[1:])
    return _jnp.moveaxis(t, 1, axis + 1)


def setup_inputs(seed: int = 0) -> dict:
    inp = _fwd_setup_inputs(seed)
    key = _jax.random.fold_in(_jax.random.key(seed), 7919)
    shape, _ = _output_shape()
    out = dict(inp)
    out["loss_target"] = _jax.random.normal(_jax.random.fold_in(key, 0), shape, _jnp.float32)
    for i, name in enumerate(TWIN_WEIGHTS):
        w = inp[name].astype(_jnp.float32)
        if MOMENT_SCALE is None:
            s = _jnp.sqrt(_jnp.mean(_jnp.square(w)) + 1e-30)
        else:
            s = MOMENT_SCALE[name]
        km, kv = _jax.random.split(_jax.random.fold_in(key, i + 1))
        out[name] = w
        out["m_" + name] = s * _jax.random.normal(km, w.shape, _jnp.float32)
        out["v_" + name] = (s * s) * _jax.random.uniform(kv, w.shape, _jnp.float32, 0.5, 1.5)
    if N_MICROBATCH > 1:
        for name, axis in PER_EXAMPLE_BATCH_AXIS.items():
            out[name] = _to_microbatches(out[name], axis)
    return {'x': out['x'], 'c': out['c'], 'ctx': out['ctx'], 'c_ctx': out['c_ctx'], 'w_ada': out['w_ada'], 'b_ada': out['b_ada'], 'g_pre': out['g_pre'], 'g_post': out['g_post'], 'w_in': out['w_in'], 'conv_w': out['conv_w'], 'conv_b': out['conv_b'], 'dt_bias': out['dt_bias'], 'a_log': out['a_log'], 'd_skip': out['d_skip'], 'g_ssd': out['g_ssd'], 'g_v': out['g_v'], 'w_s': out['w_s'], 'b_s': out['b_s'], 'g_mlp': out['g_mlp'], 'w_out': out['w_out'], 'loss_target': out['loss_target'], 'm_c_ctx': out['m_c_ctx'], 'm_w_ada': out['m_w_ada'], 'm_b_ada': out['m_b_ada'], 'm_g_pre': out['m_g_pre'], 'm_g_post': out['m_g_post'], 'm_w_in': out['m_w_in'], 'm_conv_w': out['m_conv_w'], 'm_conv_b': out['m_conv_b'], 'm_dt_bias': out['m_dt_bias'], 'm_a_log': out['m_a_log'], 'm_d_skip': out['m_d_skip'], 'm_g_ssd': out['m_g_ssd'], 'm_g_v': out['m_g_v'], 'm_w_s': out['m_w_s'], 'm_b_s': out['m_b_s'], 'm_g_mlp': out['m_g_mlp'], 'm_w_out': out['m_w_out'], 'v_c_ctx': out['v_c_ctx'], 'v_w_ada': out['v_w_ada'], 'v_b_ada': out['v_b_ada'], 'v_g_pre': out['v_g_pre'], 'v_g_post': out['v_g_post'], 'v_w_in': out['v_w_in'], 'v_conv_w': out['v_conv_w'], 'v_conv_b': out['v_conv_b'], 'v_dt_bias': out['v_dt_bias'], 'v_a_log': out['v_a_log'], 'v_d_skip': out['v_d_skip'], 'v_g_ssd': out['v_g_ssd'], 'v_g_v': out['v_g_v'], 'v_w_s': out['v_w_s'], 'v_b_s': out['v_b_s'], 'v_g_mlp': out['v_g_mlp'], 'v_w_out': out['v_w_out']}


def _loss(weights, diff, rest, loss_target):
    with _jax.named_scope("forward"):
        args = {**rest, TWIN_DIFF_INPUT: diff, **{k: w.astype(_WEIGHT_DTYPES[k]) for k, w in weights.items()}}
        y = _forward(args)
    with _jax.named_scope("loss_head"):
        err = _jnp.square(y.astype(_jnp.float32) - loss_target)
        return 0.5 * _jnp.sum(_jnp.mean(err, axis=-1)) if err.ndim else 0.5 * err


def _adamw(w, g, m, v):
    m = ADAM_B1 * m + (1.0 - ADAM_B1) * g
    v = ADAM_B2 * v + (1.0 - ADAM_B2) * _jnp.square(g)
    m_hat = m / (1.0 - ADAM_B1 ** ADAM_STEP)
    v_hat = v / (1.0 - ADAM_B2 ** ADAM_STEP)
    delta = -ADAM_LR * (m_hat / (_jnp.sqrt(v_hat) + ADAM_EPS) + ADAM_WD * w)
    return delta, m, v


def reference(x, c, ctx, c_ctx, w_ada, b_ada, g_pre, g_post, w_in, conv_w, conv_b, dt_bias, a_log, d_skip, g_ssd, g_v, w_s, b_s, g_mlp, w_out, loss_target, m_c_ctx, m_w_ada, m_b_ada, m_g_pre, m_g_post, m_w_in, m_conv_w, m_conv_b, m_dt_bias, m_a_log, m_d_skip, m_g_ssd, m_g_v, m_w_s, m_b_s, m_g_mlp, m_w_out, v_c_ctx, v_w_ada, v_b_ada, v_g_pre, v_g_post, v_w_in, v_conv_w, v_conv_b, v_dt_bias, v_a_log, v_d_skip, v_g_ssd, v_g_v, v_w_s, v_b_s, v_g_mlp, v_w_out):
    given = dict(x=x, c=c, ctx=ctx, c_ctx=c_ctx, w_ada=w_ada, b_ada=b_ada, g_pre=g_pre, g_post=g_post, w_in=w_in, conv_w=conv_w, conv_b=conv_b, dt_bias=dt_bias, a_log=a_log, d_skip=d_skip, g_ssd=g_ssd, g_v=g_v, w_s=w_s, b_s=b_s, g_mlp=g_mlp, w_out=w_out, loss_target=loss_target, m_c_ctx=m_c_ctx, m_w_ada=m_w_ada, m_b_ada=m_b_ada, m_g_pre=m_g_pre, m_g_post=m_g_post, m_w_in=m_w_in, m_conv_w=m_conv_w, m_conv_b=m_conv_b, m_dt_bias=m_dt_bias, m_a_log=m_a_log, m_d_skip=m_d_skip, m_g_ssd=m_g_ssd, m_g_v=m_g_v, m_w_s=m_w_s, m_b_s=m_b_s, m_g_mlp=m_g_mlp, m_w_out=m_w_out, v_c_ctx=v_c_ctx, v_w_ada=v_w_ada, v_b_ada=v_b_ada, v_g_pre=v_g_pre, v_g_post=v_g_post, v_w_in=v_w_in, v_conv_w=v_conv_w, v_conv_b=v_conv_b, v_dt_bias=v_dt_bias, v_a_log=v_a_log, v_d_skip=v_d_skip, v_g_ssd=v_g_ssd, v_g_v=v_g_v, v_w_s=v_w_s, v_b_s=v_b_s, v_g_mlp=v_g_mlp, v_w_out=v_w_out)
    weights = {n: given[n] for n in TWIN_WEIGHTS}
    shared = {n: given[n] for n in SHARED_INPUTS}
    per_example = {n: given[n] for n in ['x', 'c', 'ctx']}
    grad_fn = _jax.value_and_grad(_loss, argnums=(0, 1))

    def one_microbatch(ex, loss_target):
        ex = dict(ex)
        diff = ex.pop(TWIN_DIFF_INPUT)
        return grad_fn(weights, diff, {**shared, **ex}, loss_target)

    if N_MICROBATCH == 1:
        loss, (grad_w, grad_x) = one_microbatch(per_example, given["loss_target"])
    else:
        def body(carry, xs):
            loss_sum, grad_sum = carry
            l_k, (gw_k, gx_k) = one_microbatch(xs[0], xs[1])
            with _jax.named_scope("update"):
                return (loss_sum + l_k, _jax.tree.map(_jnp.add, grad_sum, gw_k)), gx_k

        init = (_jnp.zeros((), _jnp.float32), _jax.tree.map(_jnp.zeros_like, weights))
        (loss, grad_w), grad_x = _jax.lax.scan(body, init, (per_example, given["loss_target"]))
    with _jax.named_scope("update"):
        delta_w, new_m, new_v = {}, {}, {}
        for n in TWIN_WEIGHTS:
            delta_w[n], new_m[n], new_v[n] = _adamw(weights[n], grad_w[n], given["m_" + n], given["v_" + n])
    return (loss, grad_x, *[grad_w[n] for n in TWIN_WEIGHTS], *[delta_w[n] for n in TWIN_WEIGHTS],
            *[new_m[n] for n in TWIN_WEIGHTS], *[new_v[n] for n in TWIN_WEIGHTS])
```

```python
import jax
import jax.numpy as jnp
from jax import lax
from jax.experimental import pallas as pl
from jax.experimental.pallas import tpu as pltpu

F32 = jnp.float32
BF16 = jnp.bfloat16
EPS = 1e-6
Q = 128
TB = 256
ROW = 64
HP = 64
LANES = 128
CONV_TAPS = 5
VMEM_LIMIT = 48 * 1024 * 1024
HI = lax.Precision.HIGHEST
SUM_BLOCK_BYTES = 4 * 1024 * 1024
MESH = pl.DeviceIdType.MESH
ANY = pl.BlockSpec(memory_space=pl.ANY)

ADAM_LR, ADAM_B1, ADAM_B2, ADAM_EPS, ADAM_WD, ADAM_STEP = 0.001, 0.9, 0.999, 1e-08, 0.01, 10

NN = (((1,), (0,)), ((), ()))
NT = (((1,), (1,)), ((), ()))
TN = (((0,), (0,)), ((), ()))


def _dot(a, b, dims=NN, prec=None):
    return lax.dot_general(a, b, dims, precision=prec, preferred_element_type=F32)


def _params(*sem):
    if sem:
        return pltpu.CompilerParams(vmem_limit_bytes=VMEM_LIMIT, dimension_semantics=sem)
    return pltpu.CompilerParams(vmem_limit_bytes=VMEM_LIMIT)


def _tile(dim, cands):
    for t in cands:
        if dim % t == 0:
            return t
    return dim


def _sigmoid(x):
    return 1.0 / (1.0 + jnp.exp(-x))


def _softplus(x):
    e = jnp.exp(-jnp.abs(x))
    u = 1.0 + e
    um1 = u - 1.0
    l1p = jnp.where(um1 == 0.0, e, jnp.log(u) * (e / jnp.where(um1 == 0.0, 1.0, um1)))
    return jnp.maximum(x, 0.0) + l1p


def _rms(x):
    return lax.rsqrt(jnp.mean(x * x, axis=-1, keepdims=True) + EPS)


def _rms_bwd(x, r, t):
    return r * t - x * (r * r * r) * jnp.mean(x * t, axis=-1, keepdims=True)


def _mm(a, b, dims, name, acc=None, out_dtype=F32):
    (ca,), (cb,) = dims[0]
    m, k = a.shape[1 - ca], a.shape[ca]
    n = b.shape[1 - cb]
    tm = _tile(m, (768, 512, 384, 256, 128))
    tn = _tile(n, (1024, 512, 256, 128))
    tk = k if k <= 2048 else _tile(k, (2048, 768, 512, 384, 256, 128))
    nk = k // tk
    a_spec = pl.BlockSpec((tm, tk), lambda i, j, kk: (i, kk)) if ca == 1 else pl.BlockSpec((tk, tm), lambda i, j, kk: (kk, i))
    b_spec = pl.BlockSpec((tk, tn), lambda i, j, kk: (kk, j)) if cb == 0 else pl.BlockSpec((tn, tk), lambda i, j, kk: (j, kk))
    o_spec = pl.BlockSpec((tm, tn), lambda i, j, kk: (i, j))
    has_acc = acc is not None

    def body(*refs):
        if has_acc:
            a_ref, b_ref, c_ref, o_ref, acc_ref = refs
        else:
            a_ref, b_ref, o_ref, acc_ref = refs
        kk = pl.program_id(2)

        @pl.when(kk == 0)
        def _():
            acc_ref[...] = c_ref[...] if has_acc else jnp.zeros_like(acc_ref)

        acc_ref[...] += _dot(a_ref[...].astype(BF16), b_ref[...].astype(BF16), dims)

        @pl.when(kk == nk - 1)
        def _():
            o_ref[...] = acc_ref[...].astype(out_dtype)

    return pl.pallas_call(
        body, name=name, grid=(m // tm, n // tn, nk),
        in_specs=[a_spec, b_spec] + ([o_spec] if has_acc else []),
        out_specs=o_spec, out_shape=jax.ShapeDtypeStruct((m, n), out_dtype),
        scratch_shapes=[pltpu.VMEM((tm, tn), F32)],
        compiler_params=_params("arbitrary", "arbitrary", "arbitrary"),
    )(*((a, b, acc) if has_acc else (a, b)))


def _which(i):
    return jnp.minimum(i, 1)


def _pre_fwd(x, g_pre, mods):
    t, d = x.shape

    def body(x_ref, g_ref, m_ref, o_ref):
        xb = x_ref[...]
        xn = xb * _rms(xb) * g_ref[...]
        o_ref[...] = (xn * (1.0 + m_ref[0, 1:2, :]) + m_ref[0, 0:1, :]).astype(BF16)

    return pl.pallas_call(
        body, name="pre_fwd", grid=(t // TB,),
        in_specs=[pl.BlockSpec((TB, d), lambda i: (i, 0)), pl.BlockSpec((1, d), lambda i: (0, 0)),
                  pl.BlockSpec((1, 8, d), lambda i: (_which(i), 0, 0))],
        out_specs=pl.BlockSpec((TB, d), lambda i: (i, 0)),
        out_shape=jax.ShapeDtypeStruct((t, d), BF16), compiler_params=_params("arbitrary"),
    )(x, g_pre, mods)


def _pre_bwd(x, d_hx, d_up, g_pre, mods):
    t, d = x.shape

    def body(x_ref, dh_ref, du_ref, g_ref, m_ref, dx_ref, acc_ref):
        i = pl.program_id(0)

        @pl.when(i <= 1)
        def _():
            acc_ref[...] = jnp.zeros_like(acc_ref)

        xb = x_ref[...]
        dh = dh_ref[...]
        r = _rms(xb)
        xr = xb * r
        d_xn = dh * (1.0 + m_ref[0, 1:2, :])
        dx_ref[...] = du_ref[...] + _rms_bwd(xb, r, d_xn * g_ref[...])
        acc_ref[0, 0:1, :] += jnp.sum(dh, axis=0, keepdims=True)
        acc_ref[0, 1:2, :] += jnp.sum(dh * (xr * g_ref[...]), axis=0, keepdims=True)
        acc_ref[0, 2:3, :] += jnp.sum(d_xn * xr, axis=0, keepdims=True)

    blk = pl.BlockSpec((TB, d), lambda i: (i, 0))
    return pl.pallas_call(
        body, name="pre_bwd", grid=(t // TB,),
        in_specs=[blk, blk, blk, pl.BlockSpec((1, d), lambda i: (0, 0)),
                  pl.BlockSpec((1, 8, d), lambda i: (_which(i), 0, 0))],
        out_specs=[blk, pl.BlockSpec((1, 8, d), lambda i: (_which(i), 0, 0))],
        out_shape=[jax.ShapeDtypeStruct((t, d), F32), jax.ShapeDtypeStruct((2, 8, d), F32)],
        compiler_params=_params("arbitrary"),
    )(x, d_hx, d_up, g_pre, mods)


def _post_fwd(o, x, g_post, mods):
    t, d = x.shape

    def body(o_ref, x_ref, g_ref, m_ref, y_ref):
        ob = o_ref[...]
        y_ref[...] = x_ref[...] + m_ref[0, 2:3, :] * (ob * _rms(ob) * g_ref[...])

    blk = pl.BlockSpec((TB, d), lambda i: (i, 0))
    return pl.pallas_call(
        body, name="post_fwd", grid=(t // TB,),
        in_specs=[blk, blk, pl.BlockSpec((1, d), lambda i: (0, 0)), pl.BlockSpec((1, 8, d), lambda i: (_which(i), 0, 0))],
        out_specs=blk, out_shape=jax.ShapeDtypeStruct((t, d), F32), compiler_params=_params("arbitrary"),
    )(o, x, g_post, mods)


def _post_bwd(d_y, o, g_post, mods):
    t, d = o.shape

    def body(dy_ref, o_ref, g_ref, m_ref, do_ref, acc_ref):
        i = pl.program_id(0)

        @pl.when(i <= 1)
        def _():
            acc_ref[...] = jnp.zeros_like(acc_ref)

        ob = o_ref[...]
        dy = dy_ref[...]
        r = _rms(ob)
        orr = ob * r
        d_out = dy * m_ref[0, 2:3, :]
        do_ref[...] = _rms_bwd(ob, r, d_out * g_ref[...]).astype(BF16)
        acc_ref[0, 0:1, :] += jnp.sum(dy * (orr * g_ref[...]), axis=0, keepdims=True)
        acc_ref[0, 1:2, :] += jnp.sum(d_out * orr, axis=0, keepdims=True)

    blk = pl.BlockSpec((TB, d), lambda i: (i, 0))
    return pl.pallas_call(
        body, name="post_bwd", grid=(t // TB,),
        in_specs=[blk, blk, pl.BlockSpec((1, d), lambda i: (0, 0)), pl.BlockSpec((1, 8, d), lambda i: (_which(i), 0, 0))],
        out_specs=[blk, pl.BlockSpec((1, 8, d), lambda i: (_which(i), 0, 0))],
        out_shape=[jax.ShapeDtypeStruct((t, d), BF16), jax.ShapeDtypeStruct((2, 8, d), F32)],
        compiler_params=_params("arbitrary"),
    )(d_y, o, g_post, mods)


def _loss_grad(xf, target):
    t, d = xf.shape

    def body(x_ref, t_ref, loss_ref, dx_ref):
        i = pl.program_id(0)

        @pl.when(i == 0)
        def _():
            loss_ref[...] = jnp.zeros_like(loss_ref)
            dx_ref[...] = jnp.zeros_like(dx_ref)

        @pl.when(i > 0)
        def _():
            err = x_ref[...] - t_ref[...]
            loss_ref[...] += jnp.sum(err * err).reshape(1, 1)
            dx_ref[...] = err * (1.0 / d)

    return pl.pallas_call(
        body, name="loss_grad", grid=(t // TB,),
        in_specs=[pl.BlockSpec((TB, d), lambda i: (i, 0)), pl.BlockSpec((TB, d), lambda i: (jnp.maximum(i - 1, 0), 0))],
        out_specs=[pl.BlockSpec((1, 1), lambda i: (0, 0)), pl.BlockSpec((TB, d), lambda i: (i, 0))],
        out_shape=[jax.ShapeDtypeStruct((1, 1), F32), jax.ShapeDtypeStruct((t, d), F32)],
        compiler_params=_params("arbitrary"),
    )(xf, target)


def _conv_terms(zb, pos, row_len):
    out = []
    for k in range(CONV_TAPS):
        o = k - CONV_TAPS // 2
        sh = zb if o == 0 else pltpu.roll(zb, (-o) % TB, 0)
        out.append(jnp.where((pos + o >= 0) & (pos + o < row_len), sh, 0.0))
    return out


def _row_pos(i, ctx_len):
    row_len = jnp.where(i == 0, ctx_len, ROW)
    pos = lax.broadcasted_iota(jnp.int32, (TB, 1), 0) & (row_len - 1)
    return pos, row_len


def _conv_fwd(z_xbc, conv_w8, conv_b, ctx_len):
    t, c = z_xbc.shape
    tc = _tile(c, (1024, 512, 256, 128))

    def body(z_ref, w_ref, b_ref, o_ref):
        pos, row_len = _row_pos(pl.program_id(1), ctx_len)
        terms = _conv_terms(z_ref[...], pos, row_len)
        pre = b_ref[...]
        for k in range(CONV_TAPS):
            pre = pre + terms[k] * w_ref[k:k + 1, :]
        o_ref[...] = pre * _sigmoid(pre)

    return pl.pallas_call(
        body, name="conv_fwd", grid=(c // tc, t // TB),
        in_specs=[pl.BlockSpec((TB, tc), lambda j, i: (i, j)), pl.BlockSpec((8, tc), lambda j, i: (0, j)),
                  pl.BlockSpec((1, tc), lambda j, i: (0, j))],
        out_specs=pl.BlockSpec((TB, tc), lambda j, i: (i, j)),
        out_shape=jax.ShapeDtypeStruct((t, c), F32), compiler_params=_params("arbitrary", "arbitrary"),
    )(z_xbc, conv_w8, conv_b)


def _conv_bwd(z_xbc, d_xbc2, d_y, d_skip_row, conv_w8, conv_b, ctx_len):
    t, c = z_xbc.shape
    d = d_y.shape[1]
    tc = _tile(d, (1024, 512, 256, 128))
    nskip = d // tc

    def body(z_ref, g2_ref, dy_ref, ds_ref, w_ref, b_ref, dz_ref, dw_ref, db_ref):
        j, i = pl.program_id(0), pl.program_id(1)

        @pl.when(i == 0)
        def _():
            dw_ref[...] = jnp.zeros_like(dw_ref)
            db_ref[...] = jnp.zeros_like(db_ref)

        pos, row_len = _row_pos(i, ctx_len)
        terms = _conv_terms(z_ref[...], pos, row_len)
        pre = b_ref[...]
        for k in range(CONV_TAPS):
            pre = pre + terms[k] * w_ref[k:k + 1, :]
        sig = _sigmoid(pre)
        skip = jnp.where(j < nskip, 1.0, 0.0) * ds_ref[...]
        g = g2_ref[0] + g2_ref[1] + dy_ref[...] * skip
        d_pre = g * (sig * (1.0 + pre * (1.0 - sig)))
        db_ref[...] += jnp.sum(d_pre, axis=0, keepdims=True)
        dz = jnp.zeros_like(d_pre)
        for k in range(CONV_TAPS):
            o = k - CONV_TAPS // 2
            dw_ref[k:k + 1, :] += jnp.sum(d_pre * terms[k], axis=0, keepdims=True)
            sh = d_pre if o == 0 else pltpu.roll(d_pre, o % TB, 0)
            dz = dz + jnp.where((pos - o >= 0) & (pos - o < row_len), sh, 0.0) * w_ref[k:k + 1, :]
        dz_ref[...] = dz.astype(BF16)

    jd = lambda j: jnp.minimum(j, nskip - 1)
    return pl.pallas_call(
        body, name="conv_bwd", grid=(c // tc, t // TB),
        in_specs=[pl.BlockSpec((TB, tc), lambda j, i: (i, j)), pl.BlockSpec((2, TB, tc), lambda j, i: (0, i, j)),
                  pl.BlockSpec((TB, tc), lambda j, i: (i, jd(j))), pl.BlockSpec((1, tc), lambda j, i: (0, jd(j))),
                  pl.BlockSpec((8, tc), lambda j, i: (0, j)), pl.BlockSpec((1, tc), lambda j, i: (0, j))],
        out_specs=[pl.BlockSpec((TB, tc), lambda j, i: (i, j)), pl.BlockSpec((8, tc), lambda j, i: (0, j)),
                   pl.BlockSpec((1, tc), lambda j, i: (0, j))],
        out_shape=[jax.ShapeDtypeStruct((t, c), BF16), jax.ShapeDtypeStruct((8, c), F32), jax.ShapeDtypeStruct((1, c), F32)],
        compiler_params=_params("arbitrary", "arbitrary"),
    )(z_xbc, d_xbc2, d_y, d_skip_row, conv_w8, conv_b)


def _scan_chunk(dirn, s, nch, ncc):
    bwd = jnp.where(s < ncc, ncc - 1 - s, nch - 1 - (s - ncc))
    return jnp.where(dirn == 0, s, bwd)


def _ssd_decays(dt_ref, dtb_ref, alog_ref, tri):
    raw = dt_ref[...] + dtb_ref[0]
    dt = _softplus(raw)
    a_neg = -jnp.exp(alog_ref[0])
    a = dt * a_neg
    s = _dot(tri, a, NN, HI)
    stot = jnp.sum(a, axis=0, keepdims=True)
    return raw, dt, a_neg, s, stot, s.T


def _pair_cols(v, h0, left):
    return jnp.where(left, v[:, h0:h0 + 1], v[:, h0 + 1:h0 + 2])


def _ssd_fwd(xbc, z_dt, dtb, alog, tri, d, ctx_len):
    t = xbc.shape[0]
    nch, ncc = t // Q, ctx_len // Q
    heads = d // HP
    groups = heads // 4
    gn = groups * LANES

    def body(xbc_ref, dt_ref, dtb_ref, alog_ref, tri_ref, y_ref, hs_ref, h_scr):
        @pl.when(pl.program_id(1) == 0)
        def _():
            h_scr[...] = jnp.zeros_like(h_scr)

        tri = tri_ref[0]
        mask = tri > 0.0
        _, dt, _, s, stot, s_t = _ssd_decays(dt_ref, dtb_ref, alog_ref, tri)
        es = jnp.exp(s)
        e = jnp.exp(stot - s)
        etot = jnp.exp(stot)
        left = lax.broadcasted_iota(jnp.int32, (Q, LANES), 1) < HP
        top = lax.broadcasted_iota(jnp.int32, (LANES, 1), 0) < HP
        hs_ref[0, 0] = h_scr[...]
        for g in range(groups):
            bb = xbc_ref[:, d + g * LANES:d + (g + 1) * LANES].astype(BF16)
            cb = xbc_ref[:, d + gn + g * LANES:d + gn + (g + 1) * LANES].astype(BF16)
            cbt = _dot(cb, bb, NT)
            for pr in (2 * g, 2 * g + 1):
                h0 = 2 * pr
                cols = slice(pr * LANES, (pr + 1) * LANES)
                xd = xbc_ref[:, cols] * _pair_cols(dt, h0, left)
                xdb = xd.astype(BF16)
                res = []
                for h in (h0, h0 + 1):
                    lm = jnp.exp(jnp.where(mask, s[:, h:h + 1] - s_t[h:h + 1, :], -jnp.inf))
                    res.append(_dot((cbt * lm).astype(BF16), xdb))
                hp = h_scr[cols, :]
                y_off = _dot(cb, hp.astype(BF16), NT) * _pair_cols(es, h0, left)
                y_ref[0, :, cols] = jnp.where(left, res[0], res[1]) + y_off
                st = _dot((xd * _pair_cols(e, h0, left)).astype(BF16), bb, TN)
                h_scr[cols, :] = hp * jnp.where(top, etot[:, h0:h0 + 1], etot[:, h0 + 1:h0 + 2]) + st

    cidx = lambda dd, ss: _scan_chunk(dd, ss, nch, ncc)
    return pl.pallas_call(
        body, name="ssd_fwd", grid=(2, nch),
        in_specs=[pl.BlockSpec((Q, 2 * d), lambda dd, ss: (cidx(dd, ss), 0)),
                  pl.BlockSpec((Q, LANES), lambda dd, ss: (cidx(dd, ss), dd)),
                  pl.BlockSpec((1, 1, LANES), lambda dd, ss: (dd, 0, 0)),
                  pl.BlockSpec((1, 1, LANES), lambda dd, ss: (dd, 0, 0)),
                  pl.BlockSpec((1, Q, Q), lambda dd, ss: (dd, 0, 0))],
        out_specs=[pl.BlockSpec((1, Q, d), lambda dd, ss: (dd, cidx(dd, ss), 0)),
                   pl.BlockSpec((1, 1, d, LANES), lambda dd, ss: (dd, cidx(dd, ss), 0, 0))],
        out_shape=[jax.ShapeDtypeStruct((2, t, d), F32), jax.ShapeDtypeStruct((2, nch, d, LANES), F32)],
        scratch_shapes=[pltpu.VMEM((d, LANES), F32)],
        compiler_params=_params("arbitrary", "arbitrary"),
    )(xbc, z_dt, dtb, alog, tri)


def _ssd_bwd(xbc, z_dt, dtb, alog, tri, tri_t, d_y, hs, d, ctx_len):
    t = xbc.shape[0]
    nch, ncc = t // Q, ctx_len // Q
    heads = d // HP
    groups = heads // 4
    gn = groups * LANES

    def body(xbc_ref, dt_ref, dtb_ref, alog_ref, tri_ref, trit_ref, dy_ref, hs_ref, dx_ref, dzdt_ref, dbias_ref, dalog_ref, dh_scr):
        @pl.when(pl.program_id(1) == 0)
        def _():
            dh_scr[...] = jnp.zeros_like(dh_scr)
            dbias_ref[...] = jnp.zeros_like(dbias_ref)
            dalog_ref[...] = jnp.zeros_like(dalog_ref)

        tri = tri_ref[0]
        mask = tri > 0.0
        raw, dt, a_neg, s, stot, s_t = _ssd_decays(dt_ref, dtb_ref, alog_ref, tri)
        es = jnp.exp(s)
        e = jnp.exp(stot - s)
        etot = jnp.exp(stot)
        lane = lax.broadcasted_iota(jnp.int32, (Q, LANES), 1)
        row = lax.broadcasted_iota(jnp.int32, (LANES, Q), 0)
        left = lane < HP
        top = lax.broadcasted_iota(jnp.int32, (LANES, 1), 0) < HP
        ds_col = jnp.zeros((Q, LANES), F32)
        ds_row = jnp.zeros((LANES, Q), F32)
        d_dt = jnp.zeros((Q, LANES), F32)
        d_stot = jnp.zeros((1, LANES), F32)
        for g in range(groups):
            bb = xbc_ref[:, d + g * LANES:d + (g + 1) * LANES].astype(BF16)
            cb = xbc_ref[:, d + gn + g * LANES:d + gn + (g + 1) * LANES].astype(BF16)
            cbt = _dot(cb, bb, NT)
            d_cbt = jnp.zeros((Q, Q), F32)
            d_b = jnp.zeros((Q, LANES), F32)
            d_c = jnp.zeros((Q, LANES), F32)
            for pr in (2 * g, 2 * g + 1):
                h0 = 2 * pr
                cols = slice(pr * LANES, (pr + 1) * LANES)
                xp = xbc_ref[:, cols]
                dtp = _pair_cols(dt, h0, left)
                xd = xp * dtp
                xdb = xd.astype(BF16)
                dyp = dy_ref[:, cols]
                dyb = dyp.astype(BF16)
                hp = hs_ref[0, 0, cols, :]
                hpb = hp.astype(BF16)
                dhp = dh_scr[cols, :]
                dhb = dhp.astype(BF16)
                esp = _pair_cols(es, h0, left)
                ep = _pair_cols(e, h0, left)
                parts = []
                for hh, h in enumerate((h0, h0 + 1)):
                    mine = left if hh == 0 else jnp.logical_not(left)
                    lm = jnp.exp(jnp.where(mask, s[:, h:h + 1] - s_t[h:h + 1, :], -jnp.inf))
                    m = cbt * lm
                    gm = _dot(jnp.where(mine, dyp, 0.0).astype(BF16), xdb, NT)
                    d_cbt = d_cbt + gm * lm
                    w = gm * m
                    ds_col = ds_col + jnp.where(lane == h, jnp.sum(w, axis=1, keepdims=True), 0.0)
                    ds_row = ds_row - jnp.where(row == h, jnp.sum(w, axis=0, keepdims=True), 0.0)
                    parts.append(_dot(m.astype(BF16), dyb, TN))
                bdh = _dot(bb, dhb, NT)
                d_xd = jnp.where(left, parts[0], parts[1]) + ep * bdh
                y_raw = _dot(cb, hpb, NT)
                dye = dyp * esp
                dyeb = dye.astype(BF16)
                d_c = d_c + _dot(dyeb, hpb, NN)
                d_b = d_b + _dot((xd * ep).astype(BF16), dhb, NN)
                ek = xd * bdh * ep
                yo = dye * y_raw
                dd = d_xd * xp
                hd = dhp * hp
                for hh, h in enumerate((h0, h0 + 1)):
                    mine = left if hh == 0 else jnp.logical_not(left)
                    ek_h = jnp.sum(jnp.where(mine, ek, 0.0), axis=1, keepdims=True)
                    yo_h = jnp.sum(jnp.where(mine, yo, 0.0), axis=1, keepdims=True)
                    ds_col = ds_col + jnp.where(lane == h, yo_h - ek_h, 0.0)
                    d_dt = d_dt + jnp.where(lane == h, jnp.sum(jnp.where(mine, dd, 0.0), axis=1, keepdims=True), 0.0)
                    rows_h = top if hh == 0 else jnp.logical_not(top)
                    hd_h = jnp.sum(jnp.where(rows_h, hd, 0.0)) * etot[:, h:h + 1]
                    d_stot = d_stot + jnp.where(lane[0:1, :] == h, jnp.sum(ek_h, axis=0, keepdims=True) + hd_h, 0.0)
                dx_ref[0, :, cols] = d_xd * dtp
                dh_scr[cols, :] = dhp * jnp.where(top, etot[:, h0:h0 + 1], etot[:, h0 + 1:h0 + 2]) + _dot(dyeb, cb, TN)
            d_cbb = d_cbt.astype(BF16)
            dx_ref[0, :, d + g * LANES:d + (g + 1) * LANES] = d_b + _dot(d_cbb, cb, TN)
            dx_ref[0, :, d + gn + g * LANES:d + gn + (g + 1) * LANES] = d_c + _dot(d_cbb, bb, NN)
        d_s = ds_col + ds_row.T
        d_a = _dot(trit_ref[0], d_s, NN, HI) + d_stot
        valid = lane < heads
        d_dt_tot = jnp.where(valid, d_a * a_neg + d_dt, 0.0)
        d_raw = d_dt_tot * _sigmoid(raw)
        dzdt_ref[...] = d_raw.astype(BF16)
        dbias_ref[0] += jnp.sum(d_raw, axis=0, keepdims=True)
        dalog_ref[0] += jnp.sum(jnp.where(valid, d_a * dt, 0.0), axis=0, keepdims=True) * a_neg

    cidx = lambda dd, ss: _scan_chunk(dd, nch - 1 - ss, nch, ncc)
    return pl.pallas_call(
        body, name="ssd_bwd", grid=(2, nch),
        in_specs=[pl.BlockSpec((Q, 2 * d), lambda dd, ss: (cidx(dd, ss), 0)),
                  pl.BlockSpec((Q, LANES), lambda dd, ss: (cidx(dd, ss), dd)),
                  pl.BlockSpec((1, 1, LANES), lambda dd, ss: (dd, 0, 0)),
                  pl.BlockSpec((1, 1, LANES), lambda dd, ss: (dd, 0, 0)),
                  pl.BlockSpec((1, Q, Q), lambda dd, ss: (dd, 0, 0)),
                  pl.BlockSpec((1, Q, Q), lambda dd, ss: (dd, 0, 0)),
                  pl.BlockSpec((Q, d), lambda dd, ss: (cidx(dd, ss), 0)),
                  pl.BlockSpec((1, 1, d, LANES), lambda dd, ss: (dd, cidx(dd, ss), 0, 0))],
        out_specs=[pl.BlockSpec((1, Q, 2 * d), lambda dd, ss: (dd, cidx(dd, ss), 0)),
                   pl.BlockSpec((Q, LANES), lambda dd, ss: (cidx(dd, ss), dd)),
                   pl.BlockSpec((1, 1, LANES), lambda dd, ss: (dd, 0, 0)),
                   pl.BlockSpec((1, 1, LANES), lambda dd, ss: (dd, 0, 0))],
        out_shape=[jax.ShapeDtypeStruct((2, t, 2 * d), F32), jax.ShapeDtypeStruct((t, 2 * LANES), BF16),
                   jax.ShapeDtypeStruct((2, 1, LANES), F32), jax.ShapeDtypeStruct((2, 1, LANES), F32)],
        scratch_shapes=[pltpu.VMEM((d, LANES), F32)],
        compiler_params=_params("arbitrary", "arbitrary"),
    )(xbc, z_dt, dtb, alog, tri, tri_t, d_y, hs)


def _mix_common(zm_ref, y2_ref, xh_ref, dsk_ref, gv_ref, ws_ref, bst_ref, d):
    groups = d // LANES
    z_ssd, u, v, z_mlp = (zm_ref[:, k * d:(k + 1) * d] for k in range(4))
    y = y2_ref[0] + y2_ref[1] + dsk_ref[...] * xh_ref[...]
    sig_a = _sigmoid(z_ssd)
    ya_pre = y * (z_ssd * sig_a)
    r_v = _rms(v)
    vn = (v * r_v * gv_ref[...]).astype(BF16)
    sg = jnp.concatenate(
        [_dot(ws_ref[g].astype(BF16), vn[:, g * LANES:(g + 1) * LANES]) + bst_ref[:, g:g + 1] for g in range(groups)], axis=1)
    sig_m = _sigmoid(z_mlp)
    yb_pre = u * sg * (z_mlp * sig_m)
    return z_ssd, u, v, z_mlp, y, sig_a, ya_pre, r_v, vn, sg, sig_m, yb_pre


def _mix_fwd(z_mid, y2, xbc, dsk_row, g_ssd, g_v, g_mlp, w_s, b_st):
    t = z_mid.shape[0]
    d = z_mid.shape[1] // 4
    groups = d // LANES

    def body(zm_ref, y2_ref, xh_ref, dsk_ref, ga_ref, gv_ref, gm_ref, ws_ref, bst_ref, o_ref):
        (_, _, _, _, _, _, ya_pre, _, _, _, _, yb_pre) = _mix_common(zm_ref, y2_ref, xh_ref, dsk_ref, gv_ref, ws_ref, bst_ref, d)
        o_ref[:, :d] = (ya_pre * _rms(ya_pre) * ga_ref[...]).astype(BF16)
        o_ref[:, d:] = (yb_pre * _rms(yb_pre) * gm_ref[...]).astype(BF16)

    row = pl.BlockSpec((1, d), lambda i: (0, 0))
    return pl.pallas_call(
        body, name="mix_fwd", grid=(t // Q,),
        in_specs=[pl.BlockSpec((Q, 4 * d), lambda i: (i, 0)), pl.BlockSpec((2, Q, d), lambda i: (0, i, 0)),
                  pl.BlockSpec((Q, d), lambda i: (i, 0)), row, row, row, row,
                  pl.BlockSpec((groups, Q, Q), lambda i: (0, 0, 0)), pl.BlockSpec((Q, LANES), lambda i: (0, 0))],
        out_specs=pl.BlockSpec((Q, 2 * d), lambda i: (i, 0)),
        out_shape=jax.ShapeDtypeStruct((t, 2 * d), BF16), compiler_params=_params("arbitrary"),
    )(z_mid, y2, xbc, dsk_row, g_ssd, g_v, g_mlp, w_s, b_st)


def _mix_bwd(z_mid, y2, xbc, d_ycat, dsk_row, g_ssd, g_v, g_mlp, w_s, w_st, b_st, ind_head, ind_group):
    t = z_mid.shape[0]
    d = z_mid.shape[1] // 4
    groups = d // LANES
    nsteps = t // Q

    def body(zm_ref, y2_ref, xh_ref, dyc_ref, dsk_ref, ga_ref, gv_ref, gm_ref, ws_ref, wst_ref, bst_ref, ih_ref, ig_ref,
             dzm_ref, dy_ref, vec_ref, dws_ref, dbs_ref, dsk_acc, dsg_acc):
        i = pl.program_id(0)

        @pl.when(i == 0)
        def _():
            vec_ref[...] = jnp.zeros_like(vec_ref)
            dws_ref[...] = jnp.zeros_like(dws_ref)
            dsk_acc[...] = jnp.zeros_like(dsk_acc)
            dsg_acc[...] = jnp.zeros_like(dsg_acc)

        (z_ssd, u, v, z_mlp, y, sig_a, ya_pre, r_v, vn, sg, sig_m, yb_pre) = _mix_common(
            zm_ref, y2_ref, xh_ref, dsk_ref, gv_ref, ws_ref, bst_ref, d)
        d_ya = dyc_ref[:, :d]
        r_a = _rms(ya_pre)
        vec_ref[0:1, :] += jnp.sum(d_ya * (ya_pre * r_a), axis=0, keepdims=True)
        d_ya_pre = _rms_bwd(ya_pre, r_a, d_ya * ga_ref[...])
        d_y = d_ya_pre * (z_ssd * sig_a)
        dy_ref[...] = d_y
        dsk_acc[...] += jnp.sum(d_y * xh_ref[...], axis=0, keepdims=True)
        dzm_ref[:, 0:d] = (d_ya_pre * y * (sig_a * (1.0 + z_ssd * (1.0 - sig_a)))).astype(BF16)
        d_yb = dyc_ref[:, d:]
        r_b = _rms(yb_pre)
        vec_ref[2:3, :] += jnp.sum(d_yb * (yb_pre * r_b), axis=0, keepdims=True)
        d_yb_pre = _rms_bwd(yb_pre, r_b, d_yb * gm_ref[...])
        silu_m = z_mlp * sig_m
        dzm_ref[:, d:2 * d] = (d_yb_pre * sg * silu_m).astype(BF16)
        dzm_ref[:, 3 * d:4 * d] = (d_yb_pre * u * sg * (sig_m * (1.0 + z_mlp * (1.0 - sig_m)))).astype(BF16)
        d_sg = d_yb_pre * u * silu_m
        dsg_acc[...] += d_sg
        d_sgb = d_sg.astype(BF16)
        d_vn = []
        for g in range(groups):
            cols = slice(g * LANES, (g + 1) * LANES)
            dws_ref[g] += _dot(d_sgb[:, cols], vn[:, cols], NT)
            d_vn.append(_dot(wst_ref[g].astype(BF16), d_sgb[:, cols]))
        d_vn = jnp.concatenate(d_vn, axis=1)
        vec_ref[1:2, :] += jnp.sum(d_vn * (v * r_v), axis=0, keepdims=True)
        dzm_ref[:, 2 * d:3 * d] = _rms_bwd(v, r_v, d_vn * gv_ref[...]).astype(BF16)

        @pl.when(i == nsteps - 1)
        def _():
            vec_ref[3:4, 0:LANES] = _dot(dsk_acc[...], ih_ref[...], NN, HI)
            dbs_ref[...] = _dot(dsg_acc[...], ig_ref[...], NN, HI)

    row = pl.BlockSpec((1, d), lambda i: (0, 0))
    wsp = pl.BlockSpec((groups, Q, Q), lambda i: (0, 0, 0))
    ind = pl.BlockSpec((d, LANES), lambda i: (0, 0))
    return pl.pallas_call(
        body, name="mix_bwd", grid=(nsteps,),
        in_specs=[pl.BlockSpec((Q, 4 * d), lambda i: (i, 0)), pl.BlockSpec((2, Q, d), lambda i: (0, i, 0)),
                  pl.BlockSpec((Q, d), lambda i: (i, 0)), pl.BlockSpec((Q, 2 * d), lambda i: (i, 0)),
                  row, row, row, row, wsp, wsp, pl.BlockSpec((Q, LANES), lambda i: (0, 0)), ind, ind],
        out_specs=[pl.BlockSpec((Q, 4 * d), lambda i: (i, 0)), pl.BlockSpec((Q, d), lambda i: (i, 0)),
                   pl.BlockSpec((8, d), lambda i: (0, 0)), wsp, pl.BlockSpec((Q, LANES), lambda i: (0, 0))],
        out_shape=[jax.ShapeDtypeStruct((t, 4 * d), BF16), jax.ShapeDtypeStruct((t, d), F32),
                   jax.ShapeDtypeStruct((8, d), F32), jax.ShapeDtypeStruct((groups, Q, Q), F32),
                   jax.ShapeDtypeStruct((Q, LANES), F32)],
        scratch_shapes=[pltpu.VMEM((1, d), F32), pltpu.VMEM((Q, d), F32)],
        compiler_params=_params("arbitrary"),
    )(z_mid, y2, xbc, d_ycat, dsk_row, g_ssd, g_v, g_mlp, w_s, w_st, b_st, ind_head, ind_group)


def _ada_fwd(c16, w_ada, b_loc):
    depth, d, n = w_ada.shape
    tn = _tile(n, (512, 256, 128))

    def body(c_ref, w_ref, b_ref, o_ref):
        cv = c_ref[...]
        o_ref[0] = _dot(cv * _sigmoid(cv), w_ref[0], NN, HI) + b_ref[0]

    return pl.pallas_call(
        body, name="ada_fwd", grid=(depth, n // tn),
        in_specs=[pl.BlockSpec((16, d), lambda l, j: (0, 0)), pl.BlockSpec((1, d, tn), lambda l, j: (l, 0, j)),
                  pl.BlockSpec((1, 1, tn), lambda l, j: (l, 0, j))],
        out_specs=pl.BlockSpec((1, 16, tn), lambda l, j: (l, 0, j)),
        out_shape=jax.ShapeDtypeStruct((depth, 16, n), F32), compiler_params=_params("arbitrary", "arbitrary"),
    )(c16, w_ada, b_loc)


def _ada_bwd(c_t, dm_loc, w_ada):
    depth, d, n = w_ada.shape
    tn = _tile(n, (512, 256, 128))

    def body(s_ref, dm_ref, w_ref, gw_ref, dsc_ref):
        @pl.when((pl.program_id(0) == 0) & (pl.program_id(1) == 0))
        def _():
            dsc_ref[...] = jnp.zeros_like(dsc_ref)

        cv = s_ref[...]
        gw_ref[0] = _dot(cv * _sigmoid(cv), dm_ref[0], NN, HI)
        dsc_ref[...] += _dot(dm_ref[0, 8:16, :], w_ref[0], NT, HI)

    return pl.pallas_call(
        body, name="ada_bwd", grid=(depth, n // tn),
        in_specs=[pl.BlockSpec((d, LANES), lambda l, j: (0, 0)), pl.BlockSpec((1, LANES, tn), lambda l, j: (l, 0, j)),
                  pl.BlockSpec((1, d, tn), lambda l, j: (l, 0, j))],
        out_specs=[pl.BlockSpec((1, d, tn), lambda l, j: (l, 0, j)), pl.BlockSpec((8, d), lambda l, j: (0, 0))],
        out_shape=[jax.ShapeDtypeStruct((depth, d, n), F32), jax.ShapeDtypeStruct((8, d), F32)],
        compiler_params=_params("arbitrary", "arbitrary"),
    )(c_t, dm_loc, w_ada)


def _rowsum(x):
    depth, r, n = x.shape

    def body(x_ref, o_ref):
        o_ref[0] = jnp.sum(x_ref[0], axis=0, keepdims=True)

    return pl.pallas_call(
        body, name="rowsum", grid=(depth,),
        in_specs=[pl.BlockSpec((1, r, n), lambda l: (l, 0, 0))], out_specs=pl.BlockSpec((1, 1, n), lambda l: (l, 0, 0)),
        out_shape=jax.ShapeDtypeStruct((depth, 1, n), F32), compiler_params=_params("arbitrary"),
    )(x)


def _cctx_grad(d_scc, c_ctx_row):
    def body(g_ref, c_ref, o_ref):
        cv = c_ref[...]
        sig = _sigmoid(cv)
        o_ref[...] = g_ref[...] * (sig * (1.0 + cv * (1.0 - sig)))

    return pl.pallas_call(body, name="cctx_grad", out_shape=jax.ShapeDtypeStruct(c_ctx_row.shape, F32))(d_scc, c_ctx_row)


def _sum_lead(x, name):
    k, r, c = x.shape
    tr = _tile(r, [tt for tt in (512, 256, 128, 64, 32, 16, 8) if k * tt * c * x.dtype.itemsize <= SUM_BLOCK_BYTES])

    def body(x_ref, o_ref):
        acc = x_ref[0].astype(F32)
        for e in range(1, k):
            acc = acc + x_ref[e].astype(F32)
        o_ref[...] = acc

    return pl.pallas_call(
        body, name=name, grid=(r // tr,),
        in_specs=[pl.BlockSpec((k, tr, c), lambda i: (0, i, 0))], out_specs=pl.BlockSpec((tr, c), lambda i: (i, 0)),
        out_shape=jax.ShapeDtypeStruct((r, c), F32), compiler_params=_params("arbitrary"),
    )(x)


def _adamw(w, g, m, v, name, g2=None):
    r, c = w.shape
    tr = _tile(r, (128, 64, 32, 16, 8))
    two = g2 is not None
    bc1 = 1.0 - ADAM_B1 ** ADAM_STEP
    bc2 = 1.0 - ADAM_B2 ** ADAM_STEP

    def body(*refs):
        if two:
            w_ref, g_ref, g2_ref, m_ref, v_ref, go_ref, d_ref, mo_ref, vo_ref = refs
            gr = g_ref[...] + g2_ref[...]
        else:
            w_ref, g_ref, m_ref, v_ref, go_ref, d_ref, mo_ref, vo_ref = refs
            gr = g_ref[...]
        mn = ADAM_B1 * m_ref[...] + (1.0 - ADAM_B1) * gr
        vn = ADAM_B2 * v_ref[...] + (1.0 - ADAM_B2) * (gr * gr)
        go_ref[...] = gr
        mo_ref[...] = mn
        vo_ref[...] = vn
        d_ref[...] = -ADAM_LR * ((mn / bc1) / (jnp.sqrt(vn / bc2) + ADAM_EPS) + ADAM_WD * w_ref[...])

    blk = pl.BlockSpec((tr, c), lambda i: (i, 0))
    ins = (w, g, g2, m, v) if two else (w, g, m, v)
    return pl.pallas_call(
        body, name=name, grid=(r // tr,), in_specs=[blk] * len(ins), out_specs=[blk] * 4,
        out_shape=[jax.ShapeDtypeStruct((r, c), F32)] * 4, compiler_params=_params("arbitrary"),
    )(*ins)


def _flip(pos, k):
    x, y, c = pos
    return (x ^ ((k >> 2) & 1), y ^ ((k >> 1) & 1), c ^ (k & 1))


def _lin(pos):
    return 4 * pos[0] + 2 * pos[1] + pos[2]


def _chip(pos):
    return 2 * pos[0] + pos[1]


def _here():
    return (lax.axis_index("x"), lax.axis_index("y"), lax.axis_index("c"))


def _all_gather8(x, name):
    def body(x_ref, o_ref, send, recv, own):
        me = _here()
        mine = pltpu.make_async_copy(x_ref, o_ref.at[_lin(me)], own)
        mine.start()
        out = [pltpu.make_async_remote_copy(src_ref=x_ref, dst_ref=o_ref.at[_lin(me)], send_sem=send.at[k - 1],
                                            recv_sem=recv.at[k - 1], device_id=_flip(me, k), device_id_type=MESH)
               for k in range(1, 8)]
        for cp in out:
            cp.start()
        for k in range(1, 8):
            peer = _flip(me, k)
            pltpu.make_async_remote_copy(src_ref=x_ref, dst_ref=o_ref.at[_lin(peer)], send_sem=send.at[k - 1],
                                         recv_sem=recv.at[k - 1], device_id=peer, device_id_type=MESH).wait_recv()
        for cp in out:
            cp.wait_send()
        mine.wait()

    return pl.pallas_call(
        body, name=name, in_specs=[ANY], out_specs=ANY, out_shape=jax.ShapeDtypeStruct((8,) + x.shape, x.dtype),
        scratch_shapes=[pltpu.SemaphoreType.DMA((7,)), pltpu.SemaphoreType.DMA((7,)), pltpu.SemaphoreType.DMA],
    )(x)


def _all_gather4(x, name):
    def body(x_ref, o_ref, send, recv, own):
        me = _here()
        mine = pltpu.make_async_copy(x_ref, o_ref.at[_chip(me)], own)
        mine.start()
        out = [pltpu.make_async_remote_copy(src_ref=x_ref, dst_ref=o_ref.at[_chip(me)], send_sem=send.at[j],
                                            recv_sem=recv.at[j], device_id=_flip(me, k), device_id_type=MESH)
               for j, k in enumerate((2, 4, 6))]
        for cp in out:
            cp.start()
        for j, k in enumerate((2, 4, 6)):
            peer = _flip(me, k)
            pltpu.make_async_remote_copy(src_ref=x_ref, dst_ref=o_ref.at[_chip(peer)], send_sem=send.at[j],
                                         recv_sem=recv.at[j], device_id=peer, device_id_type=MESH).wait_recv()
        for cp in out:
            cp.wait_send()
        mine.wait()

    return pl.pallas_call(
        body, name=name, in_specs=[ANY], out_specs=ANY, out_shape=jax.ShapeDtypeStruct((4,) + x.shape, x.dtype),
        scratch_shapes=[pltpu.SemaphoreType.DMA((3,)), pltpu.SemaphoreType.DMA((3,)), pltpu.SemaphoreType.DMA],
    )(x)


def _scatter4(x, name):
    def body(x_ref, o_ref, send, recv, own):
        me = _here()
        mine = pltpu.make_async_copy(x_ref.at[_chip(me)], o_ref.at[_chip(me)], own)
        mine.start()
        out = [pltpu.make_async_remote_copy(src_ref=x_ref.at[_chip(_flip(me, k))], dst_ref=o_ref.at[_chip(me)], send_sem=send.at[j],
                                            recv_sem=recv.at[j], device_id=_flip(me, k), device_id_type=MESH)
               for j, k in enumerate((2, 4, 6))]
        for cp in out:
            cp.start()
        for j, k in enumerate((2, 4, 6)):
            peer = _flip(me, k)
            pltpu.make_async_remote_copy(src_ref=x_ref.at[_chip(me)], dst_ref=o_ref.at[_chip(peer)], send_sem=send.at[j],
                                         recv_sem=recv.at[j], device_id=peer, device_id_type=MESH).wait_recv()
        for cp in out:
            cp.wait_send()
        mine.wait()

    return pl.pallas_call(
        body, name=name, in_specs=[ANY], out_specs=ANY, out_shape=jax.ShapeDtypeStruct(x.shape, x.dtype),
        scratch_shapes=[pltpu.SemaphoreType.DMA((3,)), pltpu.SemaphoreType.DMA((3,)), pltpu.SemaphoreType.DMA],
    )(x)


def _sibling_swap(x, name):
    def body(x_ref, o_ref, send, recv):
        me = _here()
        cp = pltpu.make_async_remote_copy(src_ref=x_ref, dst_ref=o_ref, send_sem=send, recv_sem=recv,
                                          device_id=_flip(me, 1), device_id_type=MESH)
        cp.start()
        cp.wait()

    return pl.pallas_call(
        body, name=name, in_specs=[ANY], out_specs=ANY, out_shape=jax.ShapeDtypeStruct(x.shape, x.dtype),
        scratch_shapes=[pltpu.SemaphoreType.DMA, pltpu.SemaphoreType.DMA],
    )(x)


def _pad_lanes(a, width):
    return jnp.pad(a, [(0, 0)] * (a.ndim - 1) + [(0, width - a.shape[-1])])


def kernel(x, c, ctx, c_ctx, w_ada, b_ada, g_pre, g_post, w_in, conv_w, conv_b, dt_bias, a_log, d_skip, g_ssd, g_v, w_s, b_s, g_mlp, w_out, loss_target, m_c_ctx, m_w_ada, m_b_ada, m_g_pre, m_g_post, m_w_in, m_conv_w, m_conv_b, m_dt_bias, m_a_log, m_d_skip, m_g_ssd, m_g_v, m_w_s, m_b_s, m_g_mlp, m_w_out, v_c_ctx, v_w_ada, v_b_ada, v_g_pre, v_g_post, v_w_in, v_conv_w, v_conv_b, v_dt_bias, v_a_log, v_d_skip, v_g_ssd, v_g_v, v_w_s, v_b_s, v_g_mlp, v_w_out):
    depth, d = g_pre.shape
    seq, ctx_len = x.shape[1], ctx.shape[1]
    heads = d // HP
    in_w = 6 * d + 2 * heads
    groups_mlp = d // LANES
    t = ctx_len + seq
    assert ctx_len == TB and seq % TB == 0 and TB % ROW == 0 and heads % 4 == 0 and heads <= LANES and d % LANES == 0
    assert w_in.shape == (depth, d, in_w // 4)

    xi, yi, ci = lax.axis_index("x"), lax.axis_index("y"), lax.axis_index("c")
    chip = 2 * xi + yi
    me = 4 * xi + 2 * yi + ci

    n_ada = 3 * d // 4
    c_all = _all_gather8(c, "ag_c")[:, 0, :]
    c16 = jnp.concatenate([c_all, c_ctx[None, :], jnp.zeros((7, d), F32)], axis=0)
    b_loc = lax.dynamic_slice_in_dim(b_ada, chip * n_ada, n_ada, axis=1)[:, None, :]
    mods_loc = _all_gather8(_ada_fwd(c16, w_ada, b_loc).reshape(depth * 16, n_ada), "ag_mods")
    mods_loc = mods_loc.reshape(4, 2, depth, 16, n_ada)[:, 0]
    mods_full = jnp.moveaxis(mods_loc, 0, 2).reshape(depth, 16, 3 * d)
    mods_x = lax.dynamic_index_in_dim(mods_full, me, axis=1, keepdims=False).reshape(depth, 3, d)
    mods_c = mods_full[:, 8, :].reshape(depth, 3, d)
    mods = _pad_rows8(jnp.stack([mods_c, mods_x], axis=1))

    w_in_all = _all_gather4(w_in.astype(BF16), "ag_w_in")
    w_in_full = jnp.moveaxis(w_in_all, 0, 2).reshape(depth, d, in_w)
    w_xbc = w_in_full[:, :, :2 * d]
    w_dt = jnp.concatenate([_pad_lanes(w_in_full[:, :, 2 * d:2 * d + heads], LANES),
                            _pad_lanes(w_in_full[:, :, 2 * d + heads:2 * d + 2 * heads], LANES)], axis=2)
    w_mid = w_in_full[:, :, 2 * d + 2 * heads:]
    w_out_full = _all_gather4(w_out.astype(BF16), "ag_w_out")
    w_out_full = jnp.moveaxis(w_out_full, 0, 1).reshape(depth, 2 * d, d)
    conv_w_full = jnp.moveaxis(_all_gather4(conv_w, "ag_conv_w"), 0, 2).reshape(depth, CONV_TAPS, 2 * d)
    conv_w8 = jnp.pad(conv_w_full, ((0, 0), (0, 8 - CONV_TAPS), (0, 0)))

    tri = jnp.stack([jnp.tril(jnp.ones((Q, Q), F32)), jnp.triu(jnp.ones((Q, Q), F32))])
    tri_t = jnp.swapaxes(tri, 1, 2)
    dtb = _pad_lanes(dt_bias, LANES)[:, :, None, :]
    alog = _pad_lanes(a_log, LANES)[:, :, None, :]
    dsk_row = jnp.repeat(d_skip, HP, axis=1)[:, None, :]
    w_st = jnp.swapaxes(w_s, 2, 3)
    b_st = _pad_lanes(jnp.swapaxes(b_s, 1, 2), LANES)
    chan = jnp.arange(d)
    ind_head = (chan[:, None] // HP == jnp.arange(LANES)[None, :]).astype(F32)
    ind_group = (chan[:, None] // LANES == jnp.arange(LANES)[None, :]).astype(F32)

    stream = jnp.concatenate([ctx[0], x[0]], axis=0)
    saved = []
    for l in range(depth):
        hx = _pre_fwd(stream, g_pre[l][None], mods[l])
        z_xbc = _mm(hx, w_xbc[l], NN, "in_xbc")
        z_mid = _mm(hx, w_mid[l], NN, "in_mid")
        z_dt = _mm(hx, w_dt[l], NN, "in_dt")
        xbc = _conv_fwd(z_xbc, conv_w8[l], conv_b[l][None], ctx_len)
        y2, hs = _ssd_fwd(xbc, z_dt, dtb[l], alog[l], tri, d, ctx_len)
        ycat = _mix_fwd(z_mid, y2, xbc, dsk_row[l], g_ssd[l][None], g_v[l][None], g_mlp[l][None], w_s[l], b_st[l])
        o = _mm(ycat, w_out_full[l], NN, "out_proj")
        saved.append((stream, hx, z_xbc, z_mid, z_dt, xbc, y2, hs, ycat, o))
        stream = _post_fwd(o, stream, g_post[l][None], mods[l])

    sq, d_stream = _loss_grad(stream, loss_target[0])
    loss = lax.psum(0.5 / d * sq[0, 0], ("x", "y", "c"))

    small = []
    dmods = []
    gw_in, gw_out = [], []
    for l in reversed(range(depth)):
        x_in, hx, z_xbc, z_mid, z_dt, xbc, y2, hs, ycat, o = saved[l]
        d_o, acc_post = _post_bwd(d_stream, o, g_post[l][None], mods[l])
        d_ycat = _mm(d_o, w_out_full[l], NT, "d_ycat")
        gw_out.append(_mm(ycat, d_o, TN, "dw_out", out_dtype=BF16))
        dz_mid, d_y, vec, d_ws, d_bs = _mix_bwd(z_mid, y2, xbc, d_ycat, dsk_row[l], g_ssd[l][None], g_v[l][None], g_mlp[l][None],
                                                w_s[l], w_st[l], b_st[l], ind_head, ind_group)
        d_xbc2, dz_dt, d_bias, d_alog = _ssd_bwd(xbc, z_dt, dtb[l], alog[l], tri, tri_t, d_y, hs, d, ctx_len)
        dz_xbc, d_cw, d_cb = _conv_bwd(z_xbc, d_xbc2, d_y, dsk_row[l], conv_w8[l], conv_b[l][None], ctx_len)
        d_hx = _mm(dz_xbc, w_xbc[l], NT, "dhx_xbc")
        d_hx = _mm(dz_mid, w_mid[l], NT, "dhx_mid", acc=d_hx)
        d_hx = _mm(dz_dt, w_dt[l], NT, "dhx_dt", acc=d_hx)
        g_xbc = _mm(hx, dz_xbc, TN, "dw_xbc", out_dtype=BF16)
        g_mid = _mm(hx, dz_mid, TN, "dw_mid", out_dtype=BF16)
        g_dt = _mm(hx, dz_dt, TN, "dw_dt", out_dtype=BF16)
        gw_in.append(jnp.concatenate([g_xbc, g_dt[:, :heads], g_dt[:, LANES:LANES + heads], g_mid], axis=1))
        d_stream, acc_pre = _pre_bwd(x_in, d_hx, d_stream, g_pre[l][None], mods[l])
        dmods.append(jnp.concatenate([acc_pre[:, 0], acc_pre[:, 1], acc_post[:, 0]], axis=1))
        small.append(dict(
            g_pre=acc_pre[0, 2] + acc_pre[1, 2], g_post=acc_post[0, 1] + acc_post[1, 1], conv_w=d_cw[:CONV_TAPS], conv_b=d_cb[0],
            dt_bias=d_bias[:, 0, :heads], a_log=d_alog[:, 0, :heads], d_skip=vec[3, :heads], g_ssd=vec[0], g_v=vec[1],
            w_s=d_ws, b_s=d_bs[:, :groups_mlp].T, g_mlp=vec[2]))
    small.reverse(), dmods.reverse(), gw_in.reverse(), gw_out.reverse()
    grad_x = d_stream[ctx_len:][None]

    names = ["g_pre", "g_post", "conv_w", "conv_b", "dt_bias", "a_log", "d_skip", "g_ssd", "g_v", "w_s", "b_s", "g_mlp"]
    stacked = {n: jnp.stack([small[l][n] for l in range(depth)]) for n in names}
    flat = jnp.concatenate([stacked[n].reshape(-1) for n in names])
    n_flat = flat.shape[0]
    n_pad = -(-n_flat // (8 * LANES)) * (8 * LANES)
    packed = jnp.pad(flat, (0, n_pad - n_flat)).reshape(n_pad // LANES, LANES)
    summed = _sum_lead(_all_gather8(packed, "ag_small"), "sum_small").reshape(-1)
    grads, off = {}, 0
    for n in names:
        size = stacked[n].size
        grads[n] = summed[off:off + size].reshape(stacked[n].shape)
        off += size

    dm_all = _all_gather8(jnp.stack(dmods).reshape(depth * 2, 3 * d), "ag_dmods").reshape(8, depth, 2, 3 * d)
    dm_ctx = _sum_lead(dm_all[:, :, 0, :], "sum_dm_ctx")
    dm16 = jnp.concatenate([jnp.moveaxis(dm_all[:, :, 1, :], 0, 1), dm_ctx[:, None, :], jnp.zeros((depth, 7, 3 * d), F32)], axis=1)
    grads["b_ada"] = _rowsum(dm16)[:, 0, :]
    dm_loc = jnp.pad(lax.dynamic_slice_in_dim(dm16, chip * n_ada, n_ada, axis=2), ((0, 0), (0, LANES - 16), (0, 0)))
    c_t = jnp.pad(c16.T, ((0, 0), (0, LANES - 16)))
    g_w_ada, d_scc_part = _ada_bwd(c_t, dm_loc, w_ada)
    d_scc = _sum_lead(_all_gather8(d_scc_part, "ag_dscc").reshape(4, 2, 8, d)[:, 0], "sum_dscc")
    grads["c_ctx"] = _cctx_grad(d_scc[0:1], c_ctx[None])[0]
    grads["conv_w"] = lax.dynamic_slice_in_dim(grads["conv_w"], chip * (2 * d // 4), 2 * d // 4, axis=2)

    def reduce_big(g_parts, name):
        got = _scatter4(g_parts, "rs_" + name)
        part = _sum_lead(got.reshape(4, -1, got.shape[-1]), "sum_" + name)
        return part, _sibling_swap(part, "swap_" + name)

    q_in = in_w // 4
    g_in_parts = jnp.moveaxis(jnp.stack(gw_in).reshape(depth, d, 4, q_in), 2, 0)
    g_in_a, g_in_b = reduce_big(g_in_parts, "w_in")
    g_out_parts = jnp.moveaxis(jnp.stack(gw_out).reshape(depth, 4, 2 * d // 4, d), 1, 0)
    g_out_a, g_out_b = reduce_big(g_out_parts, "w_out")

    weights = dict(c_ctx=c_ctx, w_ada=w_ada, b_ada=b_ada, g_pre=g_pre, g_post=g_post, w_in=w_in, conv_w=conv_w, conv_b=conv_b,
                   dt_bias=dt_bias, a_log=a_log, d_skip=d_skip, g_ssd=g_ssd, g_v=g_v, w_s=w_s, b_s=b_s, g_mlp=g_mlp, w_out=w_out)
    m_in = dict(c_ctx=m_c_ctx, w_ada=m_w_ada, b_ada=m_b_ada, g_pre=m_g_pre, g_post=m_g_post, w_in=m_w_in, conv_w=m_conv_w,
                conv_b=m_conv_b, dt_bias=m_dt_bias, a_log=m_a_log, d_skip=m_d_skip, g_ssd=m_g_ssd, g_v=m_g_v, w_s=m_w_s,
                b_s=m_b_s, g_mlp=m_g_mlp, w_out=m_w_out)
    v_in = dict(c_ctx=v_c_ctx, w_ada=v_w_ada, b_ada=v_b_ada, g_pre=v_g_pre, g_post=v_g_post, w_in=v_w_in, conv_w=v_conv_w,
                conv_b=v_conv_b, dt_bias=v_dt_bias, a_log=v_a_log, d_skip=v_d_skip, g_ssd=v_g_ssd, g_v=v_g_v, w_s=v_w_s,
                b_s=v_b_s, g_mlp=v_g_mlp, w_out=v_w_out)
    order = list(weights)
    results = {}
    big = {"w_in": (g_in_a, g_in_b), "w_out": (g_out_a, g_out_b), "w_ada": (g_w_ada.reshape(depth * d, n_ada), None)}
    for n, (ga, gb) in big.items():
        shp = weights[n].shape
        two = lambda a: a.reshape(-1, shp[-1])
        results[n] = [r.reshape(shp) for r in _adamw(two(weights[n]), ga, two(m_in[n]), two(v_in[n]), "adamw_" + n, g2=gb)]
    rest = [n for n in order if n not in big]

    def pack(tree):
        f = jnp.concatenate([tree[n].reshape(-1) for n in rest])
        padn = -(-f.shape[0] // (8 * LANES)) * (8 * LANES)
        return jnp.pad(f, (0, padn - f.shape[0])).reshape(padn // LANES, LANES)

    outs = _adamw(pack(weights), pack(grads), pack(m_in), pack(v_in), "adamw_small")
    off = 0
    for n in rest:
        size, shp = weights[n].size, weights[n].shape
        results[n] = [o_.reshape(-1)[off:off + size].reshape(shp) for o_ in outs]
        off += size

    return (loss, grad_x, *[results[n][0] for n in order], *[results[n][1] for n in order],
            *[results[n][2] for n in order], *[results[n][3] for n in order])


def _pad_rows8(a):
    return jnp.pad(a, [(0, 0)] * (a.ndim - 2) + [(0, 8 - a.shape[-2]), (0, 0)])
```

```python
import jax
import jax.numpy as jnp
from jax import lax
from jax.experimental import pallas as pl
from jax.experimental.pallas import tpu as pltpu

F32 = jnp.float32
BF16 = jnp.bfloat16
EPS = 1e-6
Q = 128
TB = 256
ROW = 64
HP = 64
LANES = 128
CONV_TAPS = 5
VMEM_LIMIT = 48 * 1024 * 1024
HI = lax.Precision.HIGHEST
SUM_BLOCK_BYTES = 4 * 1024 * 1024
ADAM_BLOCK_BYTES = 1024 * 1024
PACK_ROWS = 1024
MESH = pl.DeviceIdType.MESH
ANY = pl.BlockSpec(memory_space=pl.ANY)

ADAM_LR, ADAM_B1, ADAM_B2, ADAM_EPS, ADAM_WD, ADAM_STEP = 0.001, 0.9, 0.999, 1e-08, 0.01, 10

NN = (((1,), (0,)), ((), ()))
NT = (((1,), (1,)), ((), ()))
TN = (((0,), (0,)), ((), ()))


def _dot(a, b, dims=NN, prec=None):
    return lax.dot_general(a, b, dims, precision=prec, preferred_element_type=F32)


def _params(*sem):
    if sem:
        return pltpu.CompilerParams(vmem_limit_bytes=VMEM_LIMIT, dimension_semantics=sem)
    return pltpu.CompilerParams(vmem_limit_bytes=VMEM_LIMIT)


def _tile(dim, cands):
    for t in cands:
        if dim % t == 0:
            return t
    return dim


def _sigmoid(x):
    return 1.0 / (1.0 + jnp.exp(-x))


def _softplus(x):
    e = jnp.exp(-jnp.abs(x))
    u = 1.0 + e
    um1 = u - 1.0
    l1p = jnp.where(um1 == 0.0, e, jnp.log(u) * (e / jnp.where(um1 == 0.0, 1.0, um1)))
    return jnp.maximum(x, 0.0) + l1p


def _rms(x):
    return lax.rsqrt(jnp.mean(x * x, axis=-1, keepdims=True) + EPS)


def _rms_bwd(x, r, t):
    return r * t - x * (r * r * r) * jnp.mean(x * t, axis=-1, keepdims=True)


def _mm(a, b, dims, name, acc=None, out_dtype=F32):
    (ca,), (cb,) = dims[0]
    m, k = a.shape[1 - ca], a.shape[ca]
    n = b.shape[1 - cb]
    tm = _tile(m, (768, 512, 384, 256, 128))
    tn = _tile(n, (1024, 512, 256, 128))
    tk = k if k <= 2048 else _tile(k, (2048, 768, 512, 384, 256, 128))
    nk = k // tk
    a_spec = pl.BlockSpec((tm, tk), lambda i, j, kk: (i, kk)) if ca == 1 else pl.BlockSpec((tk, tm), lambda i, j, kk: (kk, i))
    b_spec = pl.BlockSpec((tk, tn), lambda i, j, kk: (kk, j)) if cb == 0 else pl.BlockSpec((tn, tk), lambda i, j, kk: (j, kk))
    o_spec = pl.BlockSpec((tm, tn), lambda i, j, kk: (i, j))
    has_acc = acc is not None

    def body(*refs):
        if has_acc:
            a_ref, b_ref, c_ref, o_ref, acc_ref = refs
        else:
            a_ref, b_ref, o_ref, acc_ref = refs
        kk = pl.program_id(2)

        @pl.when(kk == 0)
        def _():
            acc_ref[...] = c_ref[...] if has_acc else jnp.zeros_like(acc_ref)

        acc_ref[...] += _dot(a_ref[...].astype(BF16), b_ref[...].astype(BF16), dims)

        @pl.when(kk == nk - 1)
        def _():
            o_ref[...] = acc_ref[...].astype(out_dtype)

    return pl.pallas_call(
        body, name=name, grid=(m // tm, n // tn, nk),
        in_specs=[a_spec, b_spec] + ([o_spec] if has_acc else []),
        out_specs=o_spec, out_shape=jax.ShapeDtypeStruct((m, n), out_dtype),
        scratch_shapes=[pltpu.VMEM((tm, tn), F32)],
        compiler_params=_params("arbitrary", "arbitrary", "arbitrary"),
    )(*((a, b, acc) if has_acc else (a, b)))


def _which(i):
    return jnp.minimum(i, 1)


def _pre_fwd(x, g_pre, mods):
    t, d = x.shape

    def body(x_ref, g_ref, m_ref, o_ref):
        xb = x_ref[...]
        xn = xb * _rms(xb) * g_ref[...]
        o_ref[...] = (xn * (1.0 + m_ref[0, 1:2, :]) + m_ref[0, 0:1, :]).astype(BF16)

    return pl.pallas_call(
        body, name="pre_fwd", grid=(t // TB,),
        in_specs=[pl.BlockSpec((TB, d), lambda i: (i, 0)), pl.BlockSpec((1, d), lambda i: (0, 0)),
                  pl.BlockSpec((1, 8, d), lambda i: (_which(i), 0, 0))],
        out_specs=pl.BlockSpec((TB, d), lambda i: (i, 0)),
        out_shape=jax.ShapeDtypeStruct((t, d), BF16), compiler_params=_params("arbitrary"),
    )(x, g_pre, mods)


def _pre_bwd(x, d_hx, d_up, g_pre, mods):
    t, d = x.shape

    def body(x_ref, dh_ref, du_ref, g_ref, m_ref, dx_ref, acc_ref):
        i = pl.program_id(0)

        @pl.when(i <= 1)
        def _():
            acc_ref[...] = jnp.zeros_like(acc_ref)

        xb = x_ref[...]
        dh = dh_ref[...]
        r = _rms(xb)
        xr = xb * r
        d_xn = dh * (1.0 + m_ref[0, 1:2, :])
        dx_ref[...] = du_ref[...] + _rms_bwd(xb, r, d_xn * g_ref[...])
        acc_ref[0, 0:1, :] += jnp.sum(dh, axis=0, keepdims=True)
        acc_ref[0, 1:2, :] += jnp.sum(dh * (xr * g_ref[...]), axis=0, keepdims=True)
        acc_ref[0, 2:3, :] += jnp.sum(d_xn * xr, axis=0, keepdims=True)

    blk = pl.BlockSpec((TB, d), lambda i: (i, 0))
    return pl.pallas_call(
        body, name="pre_bwd", grid=(t // TB,),
        in_specs=[blk, blk, blk, pl.BlockSpec((1, d), lambda i: (0, 0)),
                  pl.BlockSpec((1, 8, d), lambda i: (_which(i), 0, 0))],
        out_specs=[blk, pl.BlockSpec((1, 8, d), lambda i: (_which(i), 0, 0))],
        out_shape=[jax.ShapeDtypeStruct((t, d), F32), jax.ShapeDtypeStruct((2, 8, d), F32)],
        compiler_params=_params("arbitrary"),
    )(x, d_hx, d_up, g_pre, mods)


def _post_fwd(o, x, g_post, mods):
    t, d = x.shape

    def body(o_ref, x_ref, g_ref, m_ref, y_ref):
        ob = o_ref[...]
        y_ref[...] = x_ref[...] + m_ref[0, 2:3, :] * (ob * _rms(ob) * g_ref[...])

    blk = pl.BlockSpec((TB, d), lambda i: (i, 0))
    return pl.pallas_call(
        body, name="post_fwd", grid=(t // TB,),
        in_specs=[blk, blk, pl.BlockSpec((1, d), lambda i: (0, 0)), pl.BlockSpec((1, 8, d), lambda i: (_which(i), 0, 0))],
        out_specs=blk, out_shape=jax.ShapeDtypeStruct((t, d), F32), compiler_params=_params("arbitrary"),
    )(o, x, g_post, mods)


def _post_bwd(d_y, o, g_post, mods):
    t, d = o.shape

    def body(dy_ref, o_ref, g_ref, m_ref, do_ref, acc_ref):
        i = pl.program_id(0)

        @pl.when(i <= 1)
        def _():
            acc_ref[...] = jnp.zeros_like(acc_ref)

        ob = o_ref[...]
        dy = dy_ref[...]
        r = _rms(ob)
        orr = ob * r
        d_out = dy * m_ref[0, 2:3, :]
        do_ref[...] = _rms_bwd(ob, r, d_out * g_ref[...]).astype(BF16)
        acc_ref[0, 0:1, :] += jnp.sum(dy * (orr * g_ref[...]), axis=0, keepdims=True)
        acc_ref[0, 1:2, :] += jnp.sum(d_out * orr, axis=0, keepdims=True)

    blk = pl.BlockSpec((TB, d), lambda i: (i, 0))
    return pl.pallas_call(
        body, name="post_bwd", grid=(t // TB,),
        in_specs=[blk, blk, pl.BlockSpec((1, d), lambda i: (0, 0)), pl.BlockSpec((1, 8, d), lambda i: (_which(i), 0, 0))],
        out_specs=[blk, pl.BlockSpec((1, 8, d), lambda i: (_which(i), 0, 0))],
        out_shape=[jax.ShapeDtypeStruct((t, d), BF16), jax.ShapeDtypeStruct((2, 8, d), F32)],
        compiler_params=_params("arbitrary"),
    )(d_y, o, g_post, mods)


def _loss_grad(xf, target):
    t, d = xf.shape

    def body(x_ref, t_ref, loss_ref, dx_ref):
        i = pl.program_id(0)

        @pl.when(i == 0)
        def _():
            loss_ref[...] = jnp.zeros_like(loss_ref)
            dx_ref[...] = jnp.zeros_like(dx_ref)

        @pl.when(i > 0)
        def _():
            err = x_ref[...] - t_ref[...]
            loss_ref[...] += jnp.sum(err * err).reshape(1, 1)
            dx_ref[...] = err * (1.0 / d)

    return pl.pallas_call(
        body, name="loss_grad", grid=(t // TB,),
        in_specs=[pl.BlockSpec((TB, d), lambda i: (i, 0)), pl.BlockSpec((TB, d), lambda i: (jnp.maximum(i - 1, 0), 0))],
        out_specs=[pl.BlockSpec((1, 1), lambda i: (0, 0)), pl.BlockSpec((TB, d), lambda i: (i, 0))],
        out_shape=[jax.ShapeDtypeStruct((1, 1), F32), jax.ShapeDtypeStruct((t, d), F32)],
        compiler_params=_params("arbitrary"),
    )(xf, target)


def _conv_terms(zb, pos, row_len):
    out = []
    for k in range(CONV_TAPS):
        o = k - CONV_TAPS // 2
        sh = zb if o == 0 else pltpu.roll(zb, (-o) % TB, 0)
        out.append(jnp.where((pos + o >= 0) & (pos + o < row_len), sh, 0.0))
    return out


def _row_pos(i, ctx_len):
    row_len = jnp.where(i == 0, ctx_len, ROW)
    pos = lax.broadcasted_iota(jnp.int32, (TB, 1), 0) & (row_len - 1)
    return pos, row_len


def _conv_fwd(z_xbc, conv_w8, conv_b, ctx_len):
    t, c = z_xbc.shape
    tc = _tile(c, (1024, 512, 256, 128))

    def body(z_ref, w_ref, b_ref, o_ref):
        pos, row_len = _row_pos(pl.program_id(1), ctx_len)
        terms = _conv_terms(z_ref[...], pos, row_len)
        pre = b_ref[...]
        for k in range(CONV_TAPS):
            pre = pre + terms[k] * w_ref[k:k + 1, :]
        o_ref[...] = pre * _sigmoid(pre)

    return pl.pallas_call(
        body, name="conv_fwd", grid=(c // tc, t // TB),
        in_specs=[pl.BlockSpec((TB, tc), lambda j, i: (i, j)), pl.BlockSpec((8, tc), lambda j, i: (0, j)),
                  pl.BlockSpec((1, tc), lambda j, i: (0, j))],
        out_specs=pl.BlockSpec((TB, tc), lambda j, i: (i, j)),
        out_shape=jax.ShapeDtypeStruct((t, c), F32), compiler_params=_params("arbitrary", "arbitrary"),
    )(z_xbc, conv_w8, conv_b)


def _conv_bwd(z_xbc, d_xbc2, d_y, d_skip_row, conv_w8, conv_b, ctx_len):
    t, c = z_xbc.shape
    d = d_y.shape[1]
    tc = _tile(d, (1024, 512, 256, 128))
    nskip = d // tc

    def body(z_ref, g2_ref, dy_ref, ds_ref, w_ref, b_ref, dz_ref, dw_ref, db_ref):
        j, i = pl.program_id(0), pl.program_id(1)

        @pl.when(i == 0)
        def _():
            dw_ref[...] = jnp.zeros_like(dw_ref)
            db_ref[...] = jnp.zeros_like(db_ref)

        pos, row_len = _row_pos(i, ctx_len)
        terms = _conv_terms(z_ref[...], pos, row_len)
        pre = b_ref[...]
        for k in range(CONV_TAPS):
            pre = pre + terms[k] * w_ref[k:k + 1, :]
        sig = _sigmoid(pre)
        skip = jnp.where(j < nskip, 1.0, 0.0) * ds_ref[...]
        g = g2_ref[0] + g2_ref[1] + dy_ref[...] * skip
        d_pre = g * (sig * (1.0 + pre * (1.0 - sig)))
        db_ref[...] += jnp.sum(d_pre, axis=0, keepdims=True)
        dz = jnp.zeros_like(d_pre)
        for k in range(CONV_TAPS):
            o = k - CONV_TAPS // 2
            dw_ref[k:k + 1, :] += jnp.sum(d_pre * terms[k], axis=0, keepdims=True)
            sh = d_pre if o == 0 else pltpu.roll(d_pre, o % TB, 0)
            dz = dz + jnp.where((pos - o >= 0) & (pos - o < row_len), sh, 0.0) * w_ref[k:k + 1, :]
        dz_ref[...] = dz.astype(BF16)

    jd = lambda j: jnp.minimum(j, nskip - 1)
    return pl.pallas_call(
        body, name="conv_bwd", grid=(c // tc, t // TB),
        in_specs=[pl.BlockSpec((TB, tc), lambda j, i: (i, j)), pl.BlockSpec((2, TB, tc), lambda j, i: (0, i, j)),
                  pl.BlockSpec((TB, tc), lambda j, i: (i, jd(j))), pl.BlockSpec((1, tc), lambda j, i: (0, jd(j))),
                  pl.BlockSpec((8, tc), lambda j, i: (0, j)), pl.BlockSpec((1, tc), lambda j, i: (0, j))],
        out_specs=[pl.BlockSpec((TB, tc), lambda j, i: (i, j)), pl.BlockSpec((8, tc), lambda j, i: (0, j)),
                   pl.BlockSpec((1, tc), lambda j, i: (0, j))],
        out_shape=[jax.ShapeDtypeStruct((t, c), BF16), jax.ShapeDtypeStruct((8, c), F32), jax.ShapeDtypeStruct((1, c), F32)],
        compiler_params=_params("arbitrary", "arbitrary"),
    )(z_xbc, d_xbc2, d_y, d_skip_row, conv_w8, conv_b)


def _scan_chunk(dirn, s, nch, ncc):
    bwd = jnp.where(s < ncc, ncc - 1 - s, nch - 1 - (s - ncc))
    return jnp.where(dirn == 0, s, bwd)


def _ssd_decays(dt_ref, dtb_ref, alog_ref, tri):
    raw = dt_ref[...] + dtb_ref[0]
    dt = _softplus(raw)
    a_neg = -jnp.exp(alog_ref[0])
    a = dt * a_neg
    s = _dot(tri, a, NN, HI)
    stot = jnp.sum(a, axis=0, keepdims=True)
    return raw, dt, a_neg, s, stot, s.T


def _split(v):
    hi = v.astype(BF16)
    return hi, (v - hi.astype(F32)).astype(BF16)


def _expand(v, indt_ref):
    hi, lo = _split(v)
    return _dot(hi, indt_ref[...]) + _dot(lo, indt_ref[...])


def _head_sums(v, ind_ref):
    hi, lo = _split(v)
    return _dot(hi, ind_ref[...]) + _dot(lo, ind_ref[...])


def _ssd_fwd(xbc, z_dt, dtb, alog, tri, ind_t, d, ctx_len):
    t = xbc.shape[0]
    nch, ncc = t // Q, ctx_len // Q
    heads = d // HP
    groups = heads // 4
    gn = groups * LANES

    def body(xbc_ref, dt_ref, dtb_ref, alog_ref, tri_ref, indt_ref, y_ref, hs_ref, h_scr, xdb_scr, xde_scr, esx_scr):
        @pl.when(pl.program_id(1) == 0)
        def _():
            h_scr[...] = jnp.zeros_like(h_scr)

        tri = tri_ref[0]
        mask = tri > 0.0
        _, dt, _, s, stot, s_t = _ssd_decays(dt_ref, dtb_ref, alog_ref, tri)
        esx_scr[...] = _expand(jnp.exp(s), indt_ref)
        etot_x = _expand(jnp.broadcast_to(jnp.exp(stot), (8, LANES)), indt_ref)[0:1]
        xd = xbc_ref[:, :d] * _expand(dt, indt_ref)
        xdb_scr[...] = xd.astype(BF16)
        xde_scr[...] = (xd * _expand(jnp.exp(stot - s), indt_ref)).astype(BF16)
        left = lax.broadcasted_iota(jnp.int32, (Q, LANES), 1) < HP
        hs_ref[0, 0] = h_scr[...]
        for g in range(groups):
            b32 = xbc_ref[:, d + g * LANES:d + (g + 1) * LANES]
            bb = b32.astype(BF16)
            bbt = b32.T.astype(BF16)
            cb = xbc_ref[:, d + gn + g * LANES:d + gn + (g + 1) * LANES].astype(BF16)
            cbt = _dot(cb, bb, NT)
            for pr in (2 * g, 2 * g + 1):
                h0 = 2 * pr
                cols = slice(pr * LANES, (pr + 1) * LANES)
                xdb = xdb_scr[:, cols]
                res = []
                for h in (h0, h0 + 1):
                    lm = jnp.exp(jnp.where(mask, s[:, h:h + 1] - s_t[h:h + 1, :], -jnp.inf))
                    res.append(_dot((cbt * lm).astype(BF16), xdb))
                hp = h_scr[:, cols]
                y_ref[0, :, cols] = jnp.where(left, res[0], res[1]) + _dot(cb, hp.astype(BF16)) * esx_scr[:, cols]
                h_scr[:, cols] = hp * etot_x[:, cols] + _dot(bbt, xde_scr[:, cols])

    cidx = lambda dd, ss: _scan_chunk(dd, ss, nch, ncc)
    return pl.pallas_call(
        body, name="ssd_fwd", grid=(2, nch),
        in_specs=[pl.BlockSpec((Q, 2 * d), lambda dd, ss: (cidx(dd, ss), 0)),
                  pl.BlockSpec((Q, LANES), lambda dd, ss: (cidx(dd, ss), dd)),
                  pl.BlockSpec((1, 1, LANES), lambda dd, ss: (dd, 0, 0)),
                  pl.BlockSpec((1, 1, LANES), lambda dd, ss: (dd, 0, 0)),
                  pl.BlockSpec((1, Q, Q), lambda dd, ss: (dd, 0, 0)),
                  pl.BlockSpec((LANES, d), lambda dd, ss: (0, 0))],
        out_specs=[pl.BlockSpec((1, Q, d), lambda dd, ss: (dd, cidx(dd, ss), 0)),
                   pl.BlockSpec((1, 1, LANES, d), lambda dd, ss: (dd, cidx(dd, ss), 0, 0))],
        out_shape=[jax.ShapeDtypeStruct((2, t, d), F32), jax.ShapeDtypeStruct((2, nch, LANES, d), F32)],
        scratch_shapes=[pltpu.VMEM((LANES, d), F32), pltpu.VMEM((Q, d), BF16), pltpu.VMEM((Q, d), BF16), pltpu.VMEM((Q, d), F32)],
        compiler_params=_params("arbitrary", "arbitrary"),
    )(xbc, z_dt, dtb, alog, tri, ind_t)


def _ssd_bwd(xbc, z_dt, dtb, alog, tri, tri_t, ind_t, ind, d_y, y2, hs, d, ctx_len):
    t = xbc.shape[0]
    nch, ncc = t // Q, ctx_len // Q
    heads = d // HP
    groups = heads // 4
    gn = groups * LANES

    def body(xbc_ref, dt_ref, dtb_ref, alog_ref, tri_ref, trit_ref, indt_ref, ind_ref, dy_ref, y_ref, hs_ref,
             dx_ref, dzdt_ref, dbias_ref, dalog_ref, dh_scr, dtx_scr, ex_scr, xdb_scr, xde_scr, dyb_scr, dye_scr, dxd_scr, bdh_scr):
        @pl.when(pl.program_id(1) == 0)
        def _():
            dh_scr[...] = jnp.zeros_like(dh_scr)
            dbias_ref[...] = jnp.zeros_like(dbias_ref)
            dalog_ref[...] = jnp.zeros_like(dalog_ref)

        tri = tri_ref[0]
        mask = tri > 0.0
        mask_t = trit_ref[0] > 0.0
        raw, dt, a_neg, s, stot, s_t = _ssd_decays(dt_ref, dtb_ref, alog_ref, tri)
        etot = jnp.exp(stot)
        etot_x = _expand(jnp.broadcast_to(etot, (8, LANES)), indt_ref)[0:1]
        dtx_scr[...] = _expand(dt, indt_ref)
        ex_scr[...] = _expand(jnp.exp(stot - s), indt_ref)
        xd = xbc_ref[:, :d] * dtx_scr[...]
        xdb_scr[...] = xd.astype(BF16)
        xde_scr[...] = (xd * ex_scr[...]).astype(BF16)
        dyb_scr[...] = dy_ref[...].astype(BF16)
        dye_scr[...] = (dy_ref[...] * _expand(jnp.exp(s), indt_ref)).astype(BF16)
        hd_cols = jnp.sum(dh_scr[...] * hs_ref[0, 0], axis=0, keepdims=True)
        left = lax.broadcasted_iota(jnp.int32, (Q, LANES), 1) < HP
        for g in range(groups):
            b32 = xbc_ref[:, d + g * LANES:d + (g + 1) * LANES]
            c32 = xbc_ref[:, d + gn + g * LANES:d + gn + (g + 1) * LANES]
            bb, cb = b32.astype(BF16), c32.astype(BF16)
            c_t = c32.T.astype(BF16)
            cbt = _dot(cb, bb, NT)
            cbt_t = _dot(bb, cb, NT)
            d_cbt = jnp.zeros((Q, Q), F32)
            d_b = jnp.zeros((Q, LANES), F32)
            d_c = jnp.zeros((Q, LANES), F32)
            for pr in (2 * g, 2 * g + 1):
                h0 = 2 * pr
                cols = slice(pr * LANES, (pr + 1) * LANES)
                xdb = xdb_scr[:, cols]
                dyb = dyb_scr[:, cols]
                dyeb = dye_scr[:, cols]
                hpb = hs_ref[0, 0, :, cols].astype(BF16)
                dhp = dh_scr[:, cols]
                dhb = dhp.astype(BF16)
                parts = []
                for hh, h in enumerate((h0, h0 + 1)):
                    mine = left if hh == 0 else jnp.logical_not(left)
                    diff = s[:, h:h + 1] - s_t[h:h + 1, :]
                    lm = jnp.exp(jnp.where(mask, diff, -jnp.inf))
                    lm_t = jnp.exp(jnp.where(mask_t, -diff, -jnp.inf))
                    gm = _dot(jnp.where(mine, dyb, jnp.zeros_like(dyb)), xdb, NT)
                    d_cbt = d_cbt + gm * lm
                    parts.append(_dot((cbt_t * lm_t).astype(BF16), dyb))
                dxd_scr[:, cols] = jnp.where(left, parts[0], parts[1])
                bdh_scr[:, cols] = _dot(bb, dhb)
                d_c = d_c + _dot(dyeb, hpb, NT)
                d_b = d_b + _dot(xde_scr[:, cols], dhb, NT)
                dh_scr[:, cols] = dhp * etot_x[:, cols] + _dot(c_t, dyeb)
            dx_ref[0, :, d + g * LANES:d + (g + 1) * LANES] = d_b + _dot(d_cbt.T.astype(BF16), cb)
            dx_ref[0, :, d + gn + g * LANES:d + gn + (g + 1) * LANES] = d_c + _dot(d_cbt.astype(BF16), bb)
        x = xbc_ref[:, :d]
        ebdh = ex_scr[...] * bdh_scr[...]
        d_xd = dxd_scr[...] + ebdh
        dx_ref[0, :, :d] = d_xd * dtx_scr[...]
        r_dy = _head_sums(dyb_scr[...].astype(F32) * y_ref[0], ind_ref)
        r_diag = _head_sums(xdb_scr[...].astype(F32) * dxd_scr[...], ind_ref)
        r_e = _head_sums(x * dtx_scr[...] * ebdh, ind_ref)
        r_dx = _head_sums(d_xd * x, ind_ref)
        hd = _head_sums(jnp.broadcast_to(hd_cols, (8, d)), ind_ref)[0:1]
        d_s = r_dy - r_diag - r_e
        d_stot = jnp.sum(r_e, axis=0, keepdims=True) + etot * hd
        d_a = _dot(trit_ref[0], d_s, NN, HI) + d_stot
        valid = lax.broadcasted_iota(jnp.int32, (Q, LANES), 1) < heads
        d_dt_tot = jnp.where(valid, d_a * a_neg + r_dx, 0.0)
        d_raw = d_dt_tot * _sigmoid(raw)
        dzdt_ref[...] = d_raw.astype(BF16)
        dbias_ref[0] += jnp.sum(d_raw, axis=0, keepdims=True)
        dalog_ref[0] += jnp.sum(jnp.where(valid, d_a * dt, 0.0), axis=0, keepdims=True) * a_neg

    cidx = lambda dd, ss: _scan_chunk(dd, nch - 1 - ss, nch, ncc)
    full = lambda shape: pltpu.VMEM(shape, F32)
    half = lambda shape: pltpu.VMEM(shape, BF16)
    return pl.pallas_call(
        body, name="ssd_bwd", grid=(2, nch),
        in_specs=[pl.BlockSpec((Q, 2 * d), lambda dd, ss: (cidx(dd, ss), 0)),
                  pl.BlockSpec((Q, LANES), lambda dd, ss: (cidx(dd, ss), dd)),
                  pl.BlockSpec((1, 1, LANES), lambda dd, ss: (dd, 0, 0)),
                  pl.BlockSpec((1, 1, LANES), lambda dd, ss: (dd, 0, 0)),
                  pl.BlockSpec((1, Q, Q), lambda dd, ss: (dd, 0, 0)),
                  pl.BlockSpec((1, Q, Q), lambda dd, ss: (dd, 0, 0)),
                  pl.BlockSpec((LANES, d), lambda dd, ss: (0, 0)),
                  pl.BlockSpec((d, LANES), lambda dd, ss: (0, 0)),
                  pl.BlockSpec((Q, d), lambda dd, ss: (cidx(dd, ss), 0)),
                  pl.BlockSpec((1, Q, d), lambda dd, ss: (dd, cidx(dd, ss), 0)),
                  pl.BlockSpec((1, 1, LANES, d), lambda dd, ss: (dd, cidx(dd, ss), 0, 0))],
        out_specs=[pl.BlockSpec((1, Q, 2 * d), lambda dd, ss: (dd, cidx(dd, ss), 0)),
                   pl.BlockSpec((Q, LANES), lambda dd, ss: (cidx(dd, ss), dd)),
                   pl.BlockSpec((1, 1, LANES), lambda dd, ss: (dd, 0, 0)),
                   pl.BlockSpec((1, 1, LANES), lambda dd, ss: (dd, 0, 0))],
        out_shape=[jax.ShapeDtypeStruct((2, t, 2 * d), F32), jax.ShapeDtypeStruct((t, 2 * LANES), BF16),
                   jax.ShapeDtypeStruct((2, 1, LANES), F32), jax.ShapeDtypeStruct((2, 1, LANES), F32)],
        scratch_shapes=[full((LANES, d)), full((Q, d)), full((Q, d)), half((Q, d)), half((Q, d)), half((Q, d)), half((Q, d)),
                        full((Q, d)), full((Q, d))],
        compiler_params=_params("arbitrary", "arbitrary"),
    )(xbc, z_dt, dtb, alog, tri, tri_t, ind_t, ind, d_y, y2, hs)


def _mix_common(zm_ref, y2_ref, xh_ref, dsk_ref, gv_ref, ws_ref, bst_ref, d):
    groups = d // LANES
    z_ssd, u, v, z_mlp = (zm_ref[:, k * d:(k + 1) * d] for k in range(4))
    y = y2_ref[0] + y2_ref[1] + dsk_ref[...] * xh_ref[...]
    sig_a = _sigmoid(z_ssd)
    ya_pre = y * (z_ssd * sig_a)
    r_v = _rms(v)
    vn = (v * r_v * gv_ref[...]).astype(BF16)
    sg = jnp.concatenate(
        [_dot(ws_ref[g].astype(BF16), vn[:, g * LANES:(g + 1) * LANES]) + bst_ref[:, g:g + 1] for g in range(groups)], axis=1)
    sig_m = _sigmoid(z_mlp)
    yb_pre = u * sg * (z_mlp * sig_m)
    return z_ssd, u, v, z_mlp, y, sig_a, ya_pre, r_v, vn, sg, sig_m, yb_pre


def _mix_fwd(z_mid, y2, xbc, dsk_row, g_ssd, g_v, g_mlp, w_s, b_st):
    t = z_mid.shape[0]
    d = z_mid.shape[1] // 4
    groups = d // LANES

    def body(zm_ref, y2_ref, xh_ref, dsk_ref, ga_ref, gv_ref, gm_ref, ws_ref, bst_ref, o_ref):
        (_, _, _, _, _, _, ya_pre, _, _, _, _, yb_pre) = _mix_common(zm_ref, y2_ref, xh_ref, dsk_ref, gv_ref, ws_ref, bst_ref, d)
        o_ref[:, :d] = (ya_pre * _rms(ya_pre) * ga_ref[...]).astype(BF16)
        o_ref[:, d:] = (yb_pre * _rms(yb_pre) * gm_ref[...]).astype(BF16)

    row = pl.BlockSpec((1, d), lambda i: (0, 0))
    return pl.pallas_call(
        body, name="mix_fwd", grid=(t // Q,),
        in_specs=[pl.BlockSpec((Q, 4 * d), lambda i: (i, 0)), pl.BlockSpec((2, Q, d), lambda i: (0, i, 0)),
                  pl.BlockSpec((Q, d), lambda i: (i, 0)), row, row, row, row,
                  pl.BlockSpec((groups, Q, Q), lambda i: (0, 0, 0)), pl.BlockSpec((Q, LANES), lambda i: (0, 0))],
        out_specs=pl.BlockSpec((Q, 2 * d), lambda i: (i, 0)),
        out_shape=jax.ShapeDtypeStruct((t, 2 * d), BF16), compiler_params=_params("arbitrary"),
    )(z_mid, y2, xbc, dsk_row, g_ssd, g_v, g_mlp, w_s, b_st)


def _mix_bwd(z_mid, y2, xbc, d_ycat, dsk_row, g_ssd, g_v, g_mlp, w_s, w_st, b_st, ind_head, ind_group):
    t = z_mid.shape[0]
    d = z_mid.shape[1] // 4
    groups = d // LANES
    nsteps = t // Q

    def body(zm_ref, y2_ref, xh_ref, dyc_ref, dsk_ref, ga_ref, gv_ref, gm_ref, ws_ref, wst_ref, bst_ref, ih_ref, ig_ref,
             dzm_ref, dy_ref, vec_ref, dws_ref, dbs_ref, dsk_acc, dsg_acc):
        i = pl.program_id(0)

        @pl.when(i == 0)
        def _():
            vec_ref[...] = jnp.zeros_like(vec_ref)
            dws_ref[...] = jnp.zeros_like(dws_ref)
            dsk_acc[...] = jnp.zeros_like(dsk_acc)
            dsg_acc[...] = jnp.zeros_like(dsg_acc)

        (z_ssd, u, v, z_mlp, y, sig_a, ya_pre, r_v, vn, sg, sig_m, yb_pre) = _mix_common(
            zm_ref, y2_ref, xh_ref, dsk_ref, gv_ref, ws_ref, bst_ref, d)
        d_ya = dyc_ref[:, :d]
        r_a = _rms(ya_pre)
        vec_ref[0:1, :] += jnp.sum(d_ya * (ya_pre * r_a), axis=0, keepdims=True)
        d_ya_pre = _rms_bwd(ya_pre, r_a, d_ya * ga_ref[...])
        d_y = d_ya_pre * (z_ssd * sig_a)
        dy_ref[...] = d_y
        dsk_acc[...] += jnp.sum(d_y * xh_ref[...], axis=0, keepdims=True)
        dzm_ref[:, 0:d] = (d_ya_pre * y * (sig_a * (1.0 + z_ssd * (1.0 - sig_a)))).astype(BF16)
        d_yb = dyc_ref[:, d:]
        r_b = _rms(yb_pre)
        vec_ref[2:3, :] += jnp.sum(d_yb * (yb_pre * r_b), axis=0, keepdims=True)
        d_yb_pre = _rms_bwd(yb_pre, r_b, d_yb * gm_ref[...])
        silu_m = z_mlp * sig_m
        dzm_ref[:, d:2 * d] = (d_yb_pre * sg * silu_m).astype(BF16)
        dzm_ref[:, 3 * d:4 * d] = (d_yb_pre * u * sg * (sig_m * (1.0 + z_mlp * (1.0 - sig_m)))).astype(BF16)
        d_sg = d_yb_pre * u * silu_m
        dsg_acc[...] += d_sg
        d_sgb = d_sg.astype(BF16)
        d_vn = []
        for g in range(groups):
            cols = slice(g * LANES, (g + 1) * LANES)
            dws_ref[g] += _dot(d_sgb[:, cols], vn[:, cols], NT)
            d_vn.append(_dot(wst_ref[g].astype(BF16), d_sgb[:, cols]))
        d_vn = jnp.concatenate(d_vn, axis=1)
        vec_ref[1:2, :] += jnp.sum(d_vn * (v * r_v), axis=0, keepdims=True)
        dzm_ref[:, 2 * d:3 * d] = _rms_bwd(v, r_v, d_vn * gv_ref[...]).astype(BF16)

        @pl.when(i == nsteps - 1)
        def _():
            vec_ref[3:4, 0:LANES] = _dot(dsk_acc[...], ih_ref[...], NN, HI)
            dbs_ref[...] = _dot(dsg_acc[...], ig_ref[...], NN, HI)

    row = pl.BlockSpec((1, d), lambda i: (0, 0))
    wsp = pl.BlockSpec((groups, Q, Q), lambda i: (0, 0, 0))
    ind = pl.BlockSpec((d, LANES), lambda i: (0, 0))
    return pl.pallas_call(
        body, name="mix_bwd", grid=(nsteps,),
        in_specs=[pl.BlockSpec((Q, 4 * d), lambda i: (i, 0)), pl.BlockSpec((2, Q, d), lambda i: (0, i, 0)),
                  pl.BlockSpec((Q, d), lambda i: (i, 0)), pl.BlockSpec((Q, 2 * d), lambda i: (i, 0)),
                  row, row, row, row, wsp, wsp, pl.BlockSpec((Q, LANES), lambda i: (0, 0)), ind, ind],
        out_specs=[pl.BlockSpec((Q, 4 * d), lambda i: (i, 0)), pl.BlockSpec((Q, d), lambda i: (i, 0)),
                   pl.BlockSpec((8, d), lambda i: (0, 0)), wsp, pl.BlockSpec((Q, LANES), lambda i: (0, 0))],
        out_shape=[jax.ShapeDtypeStruct((t, 4 * d), BF16), jax.ShapeDtypeStruct((t, d), F32),
                   jax.ShapeDtypeStruct((8, d), F32), jax.ShapeDtypeStruct((groups, Q, Q), F32),
                   jax.ShapeDtypeStruct((Q, LANES), F32)],
        scratch_shapes=[pltpu.VMEM((1, d), F32), pltpu.VMEM((Q, d), F32)],
        compiler_params=_params("arbitrary"),
    )(z_mid, y2, xbc, d_ycat, dsk_row, g_ssd, g_v, g_mlp, w_s, w_st, b_st, ind_head, ind_group)


def _ada_fwd(c16, w_ada, b_loc):
    depth, d, n = w_ada.shape
    tn = _tile(n, (512, 256, 128))

    def body(c_ref, w_ref, b_ref, o_ref):
        cv = c_ref[...]
        o_ref[0] = _dot(cv * _sigmoid(cv), w_ref[0], NN, HI) + b_ref[0]

    return pl.pallas_call(
        body, name="ada_fwd", grid=(depth, n // tn),
        in_specs=[pl.BlockSpec((16, d), lambda l, j: (0, 0)), pl.BlockSpec((1, d, tn), lambda l, j: (l, 0, j)),
                  pl.BlockSpec((1, 1, tn), lambda l, j: (l, 0, j))],
        out_specs=pl.BlockSpec((1, 16, tn), lambda l, j: (l, 0, j)),
        out_shape=jax.ShapeDtypeStruct((depth, 16, n), F32), compiler_params=_params("arbitrary", "arbitrary"),
    )(c16, w_ada, b_loc)


def _ada_bwd(c_t, dm_loc, w_ada):
    depth, d, n = w_ada.shape
    tn = _tile(n, (512, 256, 128))

    def body(s_ref, dm_ref, w_ref, gw_ref, dsc_ref):
        @pl.when((pl.program_id(0) == 0) & (pl.program_id(1) == 0))
        def _():
            dsc_ref[...] = jnp.zeros_like(dsc_ref)

        cv = s_ref[...]
        gw_ref[0] = _dot(cv * _sigmoid(cv), dm_ref[0], NN, HI)
        dsc_ref[...] += _dot(dm_ref[0, 8:16, :], w_ref[0], NT, HI)

    return pl.pallas_call(
        body, name="ada_bwd", grid=(depth, n // tn),
        in_specs=[pl.BlockSpec((d, LANES), lambda l, j: (0, 0)), pl.BlockSpec((1, LANES, tn), lambda l, j: (l, 0, j)),
                  pl.BlockSpec((1, d, tn), lambda l, j: (l, 0, j))],
        out_specs=[pl.BlockSpec((1, d, tn), lambda l, j: (l, 0, j)), pl.BlockSpec((8, d), lambda l, j: (0, 0))],
        out_shape=[jax.ShapeDtypeStruct((depth, d, n), F32), jax.ShapeDtypeStruct((8, d), F32)],
        compiler_params=_params("arbitrary", "arbitrary"),
    )(c_t, dm_loc, w_ada)


def _rowsum(x):
    depth, r, n = x.shape

    def body(x_ref, o_ref):
        o_ref[0] = jnp.sum(x_ref[0], axis=0, keepdims=True)

    return pl.pallas_call(
        body, name="rowsum", grid=(depth,),
        in_specs=[pl.BlockSpec((1, r, n), lambda l: (l, 0, 0))], out_specs=pl.BlockSpec((1, 1, n), lambda l: (l, 0, 0)),
        out_shape=jax.ShapeDtypeStruct((depth, 1, n), F32), compiler_params=_params("arbitrary"),
    )(x)


def _cctx_grad(d_scc, c_ctx_row):
    def body(g_ref, c_ref, o_ref):
        cv = c_ref[...]
        sig = _sigmoid(cv)
        o_ref[...] = g_ref[...] * (sig * (1.0 + cv * (1.0 - sig)))

    return pl.pallas_call(body, name="cctx_grad", out_shape=jax.ShapeDtypeStruct(c_ctx_row.shape, F32))(d_scc, c_ctx_row)


def _sum_lead(x, name):
    k, r, c = x.shape
    tr = _tile(r, [tt for tt in (1024, 512, 256, 128, 64, 32, 16, 8) if k * tt * c * x.dtype.itemsize <= SUM_BLOCK_BYTES])

    def body(x_ref, o_ref):
        acc = x_ref[0].astype(F32)
        for e in range(1, k):
            acc = acc + x_ref[e].astype(F32)
        o_ref[...] = acc

    return pl.pallas_call(
        body, name=name, grid=(r // tr,),
        in_specs=[pl.BlockSpec((k, tr, c), lambda i: (0, i, 0))], out_specs=pl.BlockSpec((tr, c), lambda i: (i, 0)),
        out_shape=jax.ShapeDtypeStruct((r, c), F32), compiler_params=_params("arbitrary"),
    )(x)


def _adamw(w, g, m, v, name, g2=None):
    r, c = w.shape
    tr = _tile(r, [tt for tt in (2048, 1024, 512, 256, 128, 64, 32, 16, 8) if tt * c * 4 <= ADAM_BLOCK_BYTES])
    two = g2 is not None
    bc1 = 1.0 - ADAM_B1 ** ADAM_STEP
    bc2 = 1.0 - ADAM_B2 ** ADAM_STEP

    def body(*refs):
        if two:
            w_ref, g_ref, g2_ref, m_ref, v_ref, go_ref, d_ref, mo_ref, vo_ref = refs
            gr = g_ref[...] + g2_ref[...]
        else:
            w_ref, g_ref, m_ref, v_ref, go_ref, d_ref, mo_ref, vo_ref = refs
            gr = g_ref[...]
        mn = ADAM_B1 * m_ref[...] + (1.0 - ADAM_B1) * gr
        vn = ADAM_B2 * v_ref[...] + (1.0 - ADAM_B2) * (gr * gr)
        go_ref[...] = gr
        mo_ref[...] = mn
        vo_ref[...] = vn
        d_ref[...] = -ADAM_LR * ((mn / bc1) / (jnp.sqrt(vn / bc2) + ADAM_EPS) + ADAM_WD * w_ref[...])

    blk = pl.BlockSpec((tr, c), lambda i: (i, 0))
    ins = (w, g, g2, m, v) if two else (w, g, m, v)
    return pl.pallas_call(
        body, name=name, grid=(r // tr,), in_specs=[blk] * len(ins), out_specs=[blk] * 4,
        out_shape=[jax.ShapeDtypeStruct((r, c), F32)] * 4, compiler_params=_params("arbitrary"),
    )(*ins)


def _flip(pos, k):
    x, y, c = pos
    return (x ^ ((k >> 2) & 1), y ^ ((k >> 1) & 1), c ^ (k & 1))


def _lin(pos):
    return 4 * pos[0] + 2 * pos[1] + pos[2]


def _chip(pos):
    return 2 * pos[0] + pos[1]


def _here():
    return (lax.axis_index("x"), lax.axis_index("y"), lax.axis_index("c"))


def _all_gather8(x, name):
    def body(x_ref, o_ref, send, recv, own):
        me = _here()
        mine = pltpu.make_async_copy(x_ref, o_ref.at[_lin(me)], own)
        mine.start()
        out = [pltpu.make_async_remote_copy(src_ref=x_ref, dst_ref=o_ref.at[_lin(me)], send_sem=send.at[k - 1],
                                            recv_sem=recv.at[k - 1], device_id=_flip(me, k), device_id_type=MESH)
               for k in range(1, 8)]
        for cp in out:
            cp.start()
        for k in range(1, 8):
            peer = _flip(me, k)
            pltpu.make_async_remote_copy(src_ref=x_ref, dst_ref=o_ref.at[_lin(peer)], send_sem=send.at[k - 1],
                                         recv_sem=recv.at[k - 1], device_id=peer, device_id_type=MESH).wait_recv()
        for cp in out:
            cp.wait_send()
        mine.wait()

    return pl.pallas_call(
        body, name=name, in_specs=[ANY], out_specs=ANY, out_shape=jax.ShapeDtypeStruct((8,) + x.shape, x.dtype),
        scratch_shapes=[pltpu.SemaphoreType.DMA((7,)), pltpu.SemaphoreType.DMA((7,)), pltpu.SemaphoreType.DMA],
    )(x)


def _all_gather4(x, name):
    def body(x_ref, o_ref, send, recv, own):
        me = _here()
        mine = pltpu.make_async_copy(x_ref, o_ref.at[_chip(me)], own)
        mine.start()
        out = [pltpu.make_async_remote_copy(src_ref=x_ref, dst_ref=o_ref.at[_chip(me)], send_sem=send.at[j],
                                            recv_sem=recv.at[j], device_id=_flip(me, k), device_id_type=MESH)
               for j, k in enumerate((2, 4, 6))]
        for cp in out:
            cp.start()
        for j, k in enumerate((2, 4, 6)):
            peer = _flip(me, k)
            pltpu.make_async_remote_copy(src_ref=x_ref, dst_ref=o_ref.at[_chip(peer)], send_sem=send.at[j],
                                         recv_sem=recv.at[j], device_id=peer, device_id_type=MESH).wait_recv()
        for cp in out:
            cp.wait_send()
        mine.wait()

    return pl.pallas_call(
        body, name=name, in_specs=[ANY], out_specs=ANY, out_shape=jax.ShapeDtypeStruct((4,) + x.shape, x.dtype),
        scratch_shapes=[pltpu.SemaphoreType.DMA((3,)), pltpu.SemaphoreType.DMA((3,)), pltpu.SemaphoreType.DMA],
    )(x)


def _scatter4(x, name):
    def body(x_ref, o_ref, send, recv, own):
        me = _here()
        mine = pltpu.make_async_copy(x_ref.at[_chip(me)], o_ref.at[_chip(me)], own)
        mine.start()
        out = [pltpu.make_async_remote_copy(src_ref=x_ref.at[_chip(_flip(me, k))], dst_ref=o_ref.at[_chip(me)], send_sem=send.at[j],
                                            recv_sem=recv.at[j], device_id=_flip(me, k), device_id_type=MESH)
               for j, k in enumerate((2, 4, 6))]
        for cp in out:
            cp.start()
        for j, k in enumerate((2, 4, 6)):
            peer = _flip(me, k)
            pltpu.make_async_remote_copy(src_ref=x_ref.at[_chip(me)], dst_ref=o_ref.at[_chip(peer)], send_sem=send.at[j],
                                         recv_sem=recv.at[j], device_id=peer, device_id_type=MESH).wait_recv()
        for cp in out:
            cp.wait_send()
        mine.wait()

    return pl.pallas_call(
        body, name=name, in_specs=[ANY], out_specs=ANY, out_shape=jax.ShapeDtypeStruct(x.shape, x.dtype),
        scratch_shapes=[pltpu.SemaphoreType.DMA((3,)), pltpu.SemaphoreType.DMA((3,)), pltpu.SemaphoreType.DMA],
    )(x)


def _sibling_swap(x, name):
    def body(x_ref, o_ref, send, recv):
        me = _here()
        cp = pltpu.make_async_remote_copy(src_ref=x_ref, dst_ref=o_ref, send_sem=send, recv_sem=recv,
                                          device_id=_flip(me, 1), device_id_type=MESH)
        cp.start()
        cp.wait()

    return pl.pallas_call(
        body, name=name, in_specs=[ANY], out_specs=ANY, out_shape=jax.ShapeDtypeStruct(x.shape, x.dtype),
        scratch_shapes=[pltpu.SemaphoreType.DMA, pltpu.SemaphoreType.DMA],
    )(x)


def _pad_lanes(a, width):
    return jnp.pad(a, [(0, 0)] * (a.ndim - 1) + [(0, width - a.shape[-1])])


def kernel(x, c, ctx, c_ctx, w_ada, b_ada, g_pre, g_post, w_in, conv_w, conv_b, dt_bias, a_log, d_skip, g_ssd, g_v, w_s, b_s, g_mlp, w_out, loss_target, m_c_ctx, m_w_ada, m_b_ada, m_g_pre, m_g_post, m_w_in, m_conv_w, m_conv_b, m_dt_bias, m_a_log, m_d_skip, m_g_ssd, m_g_v, m_w_s, m_b_s, m_g_mlp, m_w_out, v_c_ctx, v_w_ada, v_b_ada, v_g_pre, v_g_post, v_w_in, v_conv_w, v_conv_b, v_dt_bias, v_a_log, v_d_skip, v_g_ssd, v_g_v, v_w_s, v_b_s, v_g_mlp, v_w_out):
    depth, d = g_pre.shape
    seq, ctx_len = x.shape[1], ctx.shape[1]
    heads = d // HP
    in_w = 6 * d + 2 * heads
    groups_mlp = d // LANES
    t = ctx_len + seq
    assert ctx_len == TB and seq % TB == 0 and TB % ROW == 0 and heads % 4 == 0 and heads <= LANES and d % LANES == 0
    assert w_in.shape == (depth, d, in_w // 4)

    xi, yi, ci = lax.axis_index("x"), lax.axis_index("y"), lax.axis_index("c")
    chip = 2 * xi + yi
    me = 4 * xi + 2 * yi + ci

    n_ada = 3 * d // 4
    c_all = _all_gather8(c, "ag_c")[:, 0, :]
    c16 = jnp.concatenate([c_all, c_ctx[None, :], jnp.zeros((7, d), F32)], axis=0)
    b_loc = lax.dynamic_slice_in_dim(b_ada, chip * n_ada, n_ada, axis=1)[:, None, :]
    mods_loc = _all_gather8(_ada_fwd(c16, w_ada, b_loc).reshape(depth * 16, n_ada), "ag_mods")
    mods_loc = mods_loc.reshape(4, 2, depth, 16, n_ada)[:, 0]
    mods_full = jnp.moveaxis(mods_loc, 0, 2).reshape(depth, 16, 3 * d)
    mods_x = lax.dynamic_index_in_dim(mods_full, me, axis=1, keepdims=False).reshape(depth, 3, d)
    mods_c = mods_full[:, 8, :].reshape(depth, 3, d)
    mods = _pad_rows8(jnp.stack([mods_c, mods_x], axis=1))

    w_in_all = _all_gather4(w_in.astype(BF16), "ag_w_in")
    w_in_full = jnp.moveaxis(w_in_all, 0, 2).reshape(depth, d, in_w)
    w_xbc = w_in_full[:, :, :2 * d]
    w_dt = jnp.concatenate([_pad_lanes(w_in_full[:, :, 2 * d:2 * d + heads], LANES),
                            _pad_lanes(w_in_full[:, :, 2 * d + heads:2 * d + 2 * heads], LANES)], axis=2)
    w_mid = w_in_full[:, :, 2 * d + 2 * heads:]
    w_out_full = _all_gather4(w_out.astype(BF16), "ag_w_out")
    w_out_full = jnp.moveaxis(w_out_full, 0, 1).reshape(depth, 2 * d, d)
    conv_w_full = jnp.moveaxis(_all_gather4(conv_w, "ag_conv_w"), 0, 2).reshape(depth, CONV_TAPS, 2 * d)
    conv_w8 = jnp.pad(conv_w_full, ((0, 0), (0, 8 - CONV_TAPS), (0, 0)))

    tri = jnp.stack([jnp.tril(jnp.ones((Q, Q), F32)), jnp.triu(jnp.ones((Q, Q), F32))])
    tri_t = jnp.swapaxes(tri, 1, 2)
    dtb = _pad_lanes(dt_bias, LANES)[:, :, None, :]
    alog = _pad_lanes(a_log, LANES)[:, :, None, :]
    dsk_row = jnp.repeat(d_skip, HP, axis=1)[:, None, :]
    w_st = jnp.swapaxes(w_s, 2, 3)
    b_st = _pad_lanes(jnp.swapaxes(b_s, 1, 2), LANES)
    chan = jnp.arange(d)
    ind_head = (chan[:, None] // HP == jnp.arange(LANES)[None, :]).astype(F32)
    ind_b, ind_t = ind_head.astype(BF16), ind_head.T.astype(BF16)
    ind_group = (chan[:, None] // LANES == jnp.arange(LANES)[None, :]).astype(F32)

    stream = jnp.concatenate([ctx[0], x[0]], axis=0)
    saved = []
    for l in range(depth):
        hx = _pre_fwd(stream, g_pre[l][None], mods[l])
        z_xbc = _mm(hx, w_xbc[l], NN, "in_xbc")
        z_mid = _mm(hx, w_mid[l], NN, "in_mid")
        z_dt = _mm(hx, w_dt[l], NN, "in_dt")
        xbc = _conv_fwd(z_xbc, conv_w8[l], conv_b[l][None], ctx_len)
        y2, hs = _ssd_fwd(xbc, z_dt, dtb[l], alog[l], tri, ind_t, d, ctx_len)
        ycat = _mix_fwd(z_mid, y2, xbc, dsk_row[l], g_ssd[l][None], g_v[l][None], g_mlp[l][None], w_s[l], b_st[l])
        o = _mm(ycat, w_out_full[l], NN, "out_proj")
        saved.append((stream, hx, z_xbc, z_mid, z_dt, xbc, y2, hs, ycat, o))
        stream = _post_fwd(o, stream, g_post[l][None], mods[l])

    sq, d_stream = _loss_grad(stream, loss_target[0])
    loss = lax.psum(0.5 / d * sq[0, 0], ("x", "y", "c"))

    small = []
    dmods = []
    gw_in, gw_out = [], []
    for l in reversed(range(depth)):
        x_in, hx, z_xbc, z_mid, z_dt, xbc, y2, hs, ycat, o = saved[l]
        d_o, acc_post = _post_bwd(d_stream, o, g_post[l][None], mods[l])
        d_ycat = _mm(d_o, w_out_full[l], NT, "d_ycat")
        gw_out.append(_mm(ycat, d_o, TN, "dw_out", out_dtype=BF16))
        dz_mid, d_y, vec, d_ws, d_bs = _mix_bwd(z_mid, y2, xbc, d_ycat, dsk_row[l], g_ssd[l][None], g_v[l][None], g_mlp[l][None],
                                                w_s[l], w_st[l], b_st[l], ind_head, ind_group)
        d_xbc2, dz_dt, d_bias, d_alog = _ssd_bwd(xbc, z_dt, dtb[l], alog[l], tri, tri_t, ind_t, ind_b, d_y, y2, hs, d, ctx_len)
        dz_xbc, d_cw, d_cb = _conv_bwd(z_xbc, d_xbc2, d_y, dsk_row[l], conv_w8[l], conv_b[l][None], ctx_len)
        d_hx = _mm(dz_xbc, w_xbc[l], NT, "dhx_xbc")
        d_hx = _mm(dz_mid, w_mid[l], NT, "dhx_mid", acc=d_hx)
        d_hx = _mm(dz_dt, w_dt[l], NT, "dhx_dt", acc=d_hx)
        g_xbc = _mm(hx, dz_xbc, TN, "dw_xbc", out_dtype=BF16)
        g_mid = _mm(hx, dz_mid, TN, "dw_mid", out_dtype=BF16)
        g_dt = _mm(hx, dz_dt, TN, "dw_dt", out_dtype=BF16)
        gw_in.append(jnp.concatenate([g_xbc, g_dt[:, :heads], g_dt[:, LANES:LANES + heads], g_mid], axis=1))
        d_stream, acc_pre = _pre_bwd(x_in, d_hx, d_stream, g_pre[l][None], mods[l])
        dmods.append(jnp.concatenate([acc_pre[:, 0], acc_pre[:, 1], acc_post[:, 0]], axis=1))
        small.append(dict(
            g_pre=acc_pre[0, 2] + acc_pre[1, 2], g_post=acc_post[0, 1] + acc_post[1, 1], conv_w=d_cw[:CONV_TAPS], conv_b=d_cb[0],
            dt_bias=d_bias[:, 0, :heads], a_log=d_alog[:, 0, :heads], d_skip=vec[3, :heads], g_ssd=vec[0], g_v=vec[1],
            w_s=d_ws, b_s=d_bs[:, :groups_mlp].T, g_mlp=vec[2]))
    small.reverse(), dmods.reverse(), gw_in.reverse(), gw_out.reverse()
    grad_x = d_stream[ctx_len:][None]

    names = ["g_pre", "g_post", "conv_w", "conv_b", "dt_bias", "a_log", "d_skip", "g_ssd", "g_v", "w_s", "b_s", "g_mlp"]
    stacked = {n: jnp.stack([small[l][n] for l in range(depth)]) for n in names}
    flat = jnp.concatenate([stacked[n].reshape(-1) for n in names])
    n_flat = flat.shape[0]
    n_pad = -(-n_flat // (PACK_ROWS * LANES)) * (PACK_ROWS * LANES)
    packed = jnp.pad(flat, (0, n_pad - n_flat)).reshape(n_pad // LANES, LANES)
    summed = _sum_lead(_all_gather8(packed, "ag_small"), "sum_small").reshape(-1)
    grads, off = {}, 0
    for n in names:
        size = stacked[n].size
        grads[n] = summed[off:off + size].reshape(stacked[n].shape)
        off += size

    dm_all = _all_gather8(jnp.stack(dmods).reshape(depth * 2, 3 * d), "ag_dmods").reshape(8, depth, 2, 3 * d)
    dm_ctx = _sum_lead(dm_all[:, :, 0, :], "sum_dm_ctx")
    dm16 = jnp.concatenate([jnp.moveaxis(dm_all[:, :, 1, :], 0, 1), dm_ctx[:, None, :], jnp.zeros((depth, 7, 3 * d), F32)], axis=1)
    grads["b_ada"] = _rowsum(dm16)[:, 0, :]
    dm_loc = jnp.pad(lax.dynamic_slice_in_dim(dm16, chip * n_ada, n_ada, axis=2), ((0, 0), (0, LANES - 16), (0, 0)))
    c_t = jnp.pad(c16.T, ((0, 0), (0, LANES - 16)))
    g_w_ada, d_scc_part = _ada_bwd(c_t, dm_loc, w_ada)
    d_scc = _sum_lead(_all_gather8(d_scc_part, "ag_dscc").reshape(4, 2, 8, d)[:, 0], "sum_dscc")
    grads["c_ctx"] = _cctx_grad(d_scc[0:1], c_ctx[None])[0]
    grads["conv_w"] = lax.dynamic_slice_in_dim(grads["conv_w"], chip * (2 * d // 4), 2 * d // 4, axis=2)

    def reduce_big(g_parts, name):
        got = _scatter4(g_parts, "rs_" + name)
        part = _sum_lead(got.reshape(4, -1, got.shape[-1]), "sum_" + name)
        return part, _sibling_swap(part, "swap_" + name)

    q_in = in_w // 4
    g_in_parts = jnp.moveaxis(jnp.stack(gw_in).reshape(depth, d, 4, q_in), 2, 0)
    g_in_a, g_in_b = reduce_big(g_in_parts, "w_in")
    g_out_parts = jnp.moveaxis(jnp.stack(gw_out).reshape(depth, 4, 2 * d // 4, d), 1, 0)
    g_out_a, g_out_b = reduce_big(g_out_parts, "w_out")

    weights = dict(c_ctx=c_ctx, w_ada=w_ada, b_ada=b_ada, g_pre=g_pre, g_post=g_post, w_in=w_in, conv_w=conv_w, conv_b=conv_b,
                   dt_bias=dt_bias, a_log=a_log, d_skip=d_skip, g_ssd=g_ssd, g_v=g_v, w_s=w_s, b_s=b_s, g_mlp=g_mlp, w_out=w_out)
    m_in = dict(c_ctx=m_c_ctx, w_ada=m_w_ada, b_ada=m_b_ada, g_pre=m_g_pre, g_post=m_g_post, w_in=m_w_in, conv_w=m_conv_w,
                conv_b=m_conv_b, dt_bias=m_dt_bias, a_log=m_a_log, d_skip=m_d_skip, g_ssd=m_g_ssd, g_v=m_g_v, w_s=m_w_s,
                b_s=m_b_s, g_mlp=m_g_mlp, w_out=m_w_out)
    v_in = dict(c_ctx=v_c_ctx, w_ada=v_w_ada, b_ada=v_b_ada, g_pre=v_g_pre, g_post=v_g_post, w_in=v_w_in, conv_w=v_conv_w,
                conv_b=v_conv_b, dt_bias=v_dt_bias, a_log=v_a_log, d_skip=v_d_skip, g_ssd=v_g_ssd, g_v=v_g_v, w_s=v_w_s,
                b_s=v_b_s, g_mlp=v_g_mlp, w_out=v_w_out)
    order = list(weights)
    results = {}
    big = {"w_in": (g_in_a, g_in_b), "w_out": (g_out_a, g_out_b), "w_ada": (g_w_ada.reshape(depth * d, n_ada), None)}
    for n, (ga, gb) in big.items():
        shp = weights[n].shape
        two = lambda a: a.reshape(-1, shp[-1])
        results[n] = [r.reshape(shp) for r in _adamw(two(weights[n]), ga, two(m_in[n]), two(v_in[n]), "adamw_" + n, g2=gb)]
    rest = [n for n in order if n not in big]

    def pack(tree):
        f = jnp.concatenate([tree[n].reshape(-1) for n in rest])
        padn = -(-f.shape[0] // (PACK_ROWS * LANES)) * (PACK_ROWS * LANES)
        return jnp.pad(f, (0, padn - f.shape[0])).reshape(padn // LANES, LANES)

    outs = _adamw(pack(weights), pack(grads), pack(m_in), pack(v_in), "adamw_small")
    off = 0
    for n in rest:
        size, shp = weights[n].size, weights[n].shape
        results[n] = [o_.reshape(-1)[off:off + size].reshape(shp) for o_ in outs]
        off += size

    return (loss, grad_x, *[results[n][0] for n in order], *[results[n][1] for n in order],
            *[results[n][2] for n in order], *[results[n][3] for n in order])


def _pad_rows8(a):
    return jnp.pad(a, [(0, 0)] * (a.ndim - 2) + [(0, 8 - a.shape[-2]), (0, 0)])
```

```python
import jax
import jax.numpy as jnp
from jax import lax
from jax.experimental import pallas as pl
from jax.experimental.pallas import tpu as pltpu

F32 = jnp.float32
BF16 = jnp.bfloat16
EPS = 1e-6
Q = 128
TB = 256
ROW = 64
HP = 64
LANES = 128
CONV_TAPS = 5
VMEM_LIMIT = 48 * 1024 * 1024
HI = lax.Precision.HIGHEST
SUM_BLOCK_BYTES = 4 * 1024 * 1024
ADAM_BLOCK_BYTES = 1024 * 1024
PACK_ROWS = 1024
MESH = pl.DeviceIdType.MESH
ANY = pl.BlockSpec(memory_space=pl.ANY)

ADAM_LR, ADAM_B1, ADAM_B2, ADAM_EPS, ADAM_WD, ADAM_STEP = 0.001, 0.9, 0.999, 1e-08, 0.01, 10

NN = (((1,), (0,)), ((), ()))
NT = (((1,), (1,)), ((), ()))
TN = (((0,), (0,)), ((), ()))


def _dot(a, b, dims=NN, prec=None):
    return lax.dot_general(a, b, dims, precision=prec, preferred_element_type=F32)


def _params(*sem):
    if sem:
        return pltpu.CompilerParams(vmem_limit_bytes=VMEM_LIMIT, dimension_semantics=sem)
    return pltpu.CompilerParams(vmem_limit_bytes=VMEM_LIMIT)


def _tile(dim, cands):
    for t in cands:
        if dim % t == 0:
            return t
    return dim


def _sigmoid(x):
    return 1.0 / (1.0 + jnp.exp(-x))


def _softplus(x):
    e = jnp.exp(-jnp.abs(x))
    u = 1.0 + e
    um1 = u - 1.0
    l1p = jnp.where(um1 == 0.0, e, jnp.log(u) * (e / jnp.where(um1 == 0.0, 1.0, um1)))
    return jnp.maximum(x, 0.0) + l1p


def _rms(x):
    return lax.rsqrt(jnp.mean(x * x, axis=-1, keepdims=True) + EPS)


def _rms_bwd(x, r, t):
    return r * t - x * (r * r * r) * jnp.mean(x * t, axis=-1, keepdims=True)


def _mm(a, b, dims, name, acc=None, out_dtype=F32, side=None):
    (ca,), (cb,) = dims[0]
    m, k = a.shape[1 - ca], a.shape[ca]
    n = b.shape[1 - cb]
    tm = _tile(m, (768, 512, 384, 256, 128))
    tn = _tile(n, (1024, 512, 256, 128))
    tk = k if k <= 2048 else _tile(k, (2048, 768, 512, 384, 256, 128))
    nk = k // tk
    a_spec = pl.BlockSpec((tm, tk), lambda i, j, kk: (i, kk)) if ca == 1 else pl.BlockSpec((tk, tm), lambda i, j, kk: (kk, i))
    b_spec = pl.BlockSpec((tk, tn), lambda i, j, kk: (kk, j)) if cb == 0 else pl.BlockSpec((tn, tk), lambda i, j, kk: (j, kk))
    o_spec = pl.BlockSpec((tm, tn), lambda i, j, kk: (i, j))
    has_acc = acc is not None

    def body(*refs):
        if has_acc:
            a_ref, b_ref, c_ref, o_ref, acc_ref = refs
        else:
            a_ref, b_ref, o_ref, acc_ref = refs
        kk = pl.program_id(2)

        @pl.when(kk == 0)
        def _():
            acc_ref[...] = c_ref[...] if has_acc else jnp.zeros_like(acc_ref)

        acc_ref[...] += _dot(a_ref[...].astype(BF16), b_ref[...].astype(BF16), dims)

        @pl.when(kk == nk - 1)
        def _():
            o_ref[...] = acc_ref[...].astype(out_dtype)

    res = _hosted_call(
        body, side, name=name, grid=(m // tm, n // tn, nk),
        in_specs=[a_spec, b_spec] + ([o_spec] if has_acc else []),
        out_specs=[o_spec], out_shape=[jax.ShapeDtypeStruct((m, n), out_dtype)],
        scratch_shapes=[pltpu.VMEM((tm, tn), F32)], args=(a, b, acc) if has_acc else (a, b))
    return res[0] if side is None else res


def _which(i):
    return jnp.minimum(i, 1)


def _pre_fwd(x, g_pre, mods):
    t, d = x.shape

    def body(x_ref, g_ref, m_ref, o_ref):
        xb = x_ref[...]
        xn = xb * _rms(xb) * g_ref[...]
        o_ref[...] = (xn * (1.0 + m_ref[0, 1:2, :]) + m_ref[0, 0:1, :]).astype(BF16)

    return pl.pallas_call(
        body, name="pre_fwd", grid=(t // TB,),
        in_specs=[pl.BlockSpec((TB, d), lambda i: (i, 0)), pl.BlockSpec((1, d), lambda i: (0, 0)),
                  pl.BlockSpec((1, 8, d), lambda i: (_which(i), 0, 0))],
        out_specs=pl.BlockSpec((TB, d), lambda i: (i, 0)),
        out_shape=jax.ShapeDtypeStruct((t, d), BF16), compiler_params=_params("arbitrary"),
    )(x, g_pre, mods)


def _pre_bwd(x, d_hx, d_up, g_pre, mods):
    t, d = x.shape

    def body(x_ref, dh_ref, du_ref, g_ref, m_ref, dx_ref, acc_ref):
        i = pl.program_id(0)

        @pl.when(i <= 1)
        def _():
            acc_ref[...] = jnp.zeros_like(acc_ref)

        xb = x_ref[...]
        dh = dh_ref[...]
        r = _rms(xb)
        xr = xb * r
        d_xn = dh * (1.0 + m_ref[0, 1:2, :])
        dx_ref[...] = du_ref[...] + _rms_bwd(xb, r, d_xn * g_ref[...])
        acc_ref[0, 0:1, :] += jnp.sum(dh, axis=0, keepdims=True)
        acc_ref[0, 1:2, :] += jnp.sum(dh * (xr * g_ref[...]), axis=0, keepdims=True)
        acc_ref[0, 2:3, :] += jnp.sum(d_xn * xr, axis=0, keepdims=True)

    blk = pl.BlockSpec((TB, d), lambda i: (i, 0))
    return pl.pallas_call(
        body, name="pre_bwd", grid=(t // TB,),
        in_specs=[blk, blk, blk, pl.BlockSpec((1, d), lambda i: (0, 0)),
                  pl.BlockSpec((1, 8, d), lambda i: (_which(i), 0, 0))],
        out_specs=[blk, pl.BlockSpec((1, 8, d), lambda i: (_which(i), 0, 0))],
        out_shape=[jax.ShapeDtypeStruct((t, d), F32), jax.ShapeDtypeStruct((2, 8, d), F32)],
        compiler_params=_params("arbitrary"),
    )(x, d_hx, d_up, g_pre, mods)


def _post_fwd(o, x, g_post, mods):
    t, d = x.shape

    def body(o_ref, x_ref, g_ref, m_ref, y_ref):
        ob = o_ref[...]
        y_ref[...] = x_ref[...] + m_ref[0, 2:3, :] * (ob * _rms(ob) * g_ref[...])

    blk = pl.BlockSpec((TB, d), lambda i: (i, 0))
    return pl.pallas_call(
        body, name="post_fwd", grid=(t // TB,),
        in_specs=[blk, blk, pl.BlockSpec((1, d), lambda i: (0, 0)), pl.BlockSpec((1, 8, d), lambda i: (_which(i), 0, 0))],
        out_specs=blk, out_shape=jax.ShapeDtypeStruct((t, d), F32), compiler_params=_params("arbitrary"),
    )(o, x, g_post, mods)


def _post_bwd(d_y, o, g_post, mods):
    t, d = o.shape

    def body(dy_ref, o_ref, g_ref, m_ref, do_ref, acc_ref):
        i = pl.program_id(0)

        @pl.when(i <= 1)
        def _():
            acc_ref[...] = jnp.zeros_like(acc_ref)

        ob = o_ref[...]
        dy = dy_ref[...]
        r = _rms(ob)
        orr = ob * r
        d_out = dy * m_ref[0, 2:3, :]
        do_ref[...] = _rms_bwd(ob, r, d_out * g_ref[...]).astype(BF16)
        acc_ref[0, 0:1, :] += jnp.sum(dy * (orr * g_ref[...]), axis=0, keepdims=True)
        acc_ref[0, 1:2, :] += jnp.sum(d_out * orr, axis=0, keepdims=True)

    blk = pl.BlockSpec((TB, d), lambda i: (i, 0))
    return pl.pallas_call(
        body, name="post_bwd", grid=(t // TB,),
        in_specs=[blk, blk, pl.BlockSpec((1, d), lambda i: (0, 0)), pl.BlockSpec((1, 8, d), lambda i: (_which(i), 0, 0))],
        out_specs=[blk, pl.BlockSpec((1, 8, d), lambda i: (_which(i), 0, 0))],
        out_shape=[jax.ShapeDtypeStruct((t, d), BF16), jax.ShapeDtypeStruct((2, 8, d), F32)],
        compiler_params=_params("arbitrary"),
    )(d_y, o, g_post, mods)


def _loss_grad(xf, target):
    t, d = xf.shape

    def body(x_ref, t_ref, loss_ref, dx_ref):
        i = pl.program_id(0)

        @pl.when(i == 0)
        def _():
            loss_ref[...] = jnp.zeros_like(loss_ref)
            dx_ref[...] = jnp.zeros_like(dx_ref)

        @pl.when(i > 0)
        def _():
            err = x_ref[...] - t_ref[...]
            loss_ref[...] += jnp.sum(err * err).reshape(1, 1)
            dx_ref[...] = err * (1.0 / d)

    return pl.pallas_call(
        body, name="loss_grad", grid=(t // TB,),
        in_specs=[pl.BlockSpec((TB, d), lambda i: (i, 0)), pl.BlockSpec((TB, d), lambda i: (jnp.maximum(i - 1, 0), 0))],
        out_specs=[pl.BlockSpec((1, 1), lambda i: (0, 0)), pl.BlockSpec((TB, d), lambda i: (i, 0))],
        out_shape=[jax.ShapeDtypeStruct((1, 1), F32), jax.ShapeDtypeStruct((t, d), F32)],
        compiler_params=_params("arbitrary"),
    )(xf, target)


def _conv_terms(zb, pos, row_len):
    out = []
    for k in range(CONV_TAPS):
        o = k - CONV_TAPS // 2
        sh = zb if o == 0 else pltpu.roll(zb, (-o) % TB, 0)
        out.append(jnp.where((pos + o >= 0) & (pos + o < row_len), sh, 0.0))
    return out


def _row_pos(i, ctx_len):
    row_len = jnp.where(i == 0, ctx_len, ROW)
    pos = lax.broadcasted_iota(jnp.int32, (TB, 1), 0) & (row_len - 1)
    return pos, row_len


def _conv_fwd(z_xbc, conv_w8, conv_b, ctx_len):
    t, c = z_xbc.shape
    tc = _tile(c, (1024, 512, 256, 128))

    def body(z_ref, w_ref, b_ref, o_ref):
        pos, row_len = _row_pos(pl.program_id(1), ctx_len)
        terms = _conv_terms(z_ref[...], pos, row_len)
        pre = b_ref[...]
        for k in range(CONV_TAPS):
            pre = pre + terms[k] * w_ref[k:k + 1, :]
        o_ref[...] = pre * _sigmoid(pre)

    return pl.pallas_call(
        body, name="conv_fwd", grid=(c // tc, t // TB),
        in_specs=[pl.BlockSpec((TB, tc), lambda j, i: (i, j)), pl.BlockSpec((8, tc), lambda j, i: (0, j)),
                  pl.BlockSpec((1, tc), lambda j, i: (0, j))],
        out_specs=pl.BlockSpec((TB, tc), lambda j, i: (i, j)),
        out_shape=jax.ShapeDtypeStruct((t, c), F32), compiler_params=_params("arbitrary", "arbitrary"),
    )(z_xbc, conv_w8, conv_b)


def _conv_bwd(z_xbc, d_xbc2, d_y, d_skip_row, conv_w8, conv_b, ctx_len):
    t, c = z_xbc.shape
    d = d_y.shape[1]
    tc = _tile(d, (1024, 512, 256, 128))
    nskip = d // tc

    def body(z_ref, g2_ref, dy_ref, ds_ref, w_ref, b_ref, dz_ref, dw_ref, db_ref):
        j, i = pl.program_id(0), pl.program_id(1)

        @pl.when(i == 0)
        def _():
            dw_ref[...] = jnp.zeros_like(dw_ref)
            db_ref[...] = jnp.zeros_like(db_ref)

        pos, row_len = _row_pos(i, ctx_len)
        terms = _conv_terms(z_ref[...], pos, row_len)
        pre = b_ref[...]
        for k in range(CONV_TAPS):
            pre = pre + terms[k] * w_ref[k:k + 1, :]
        sig = _sigmoid(pre)
        skip = jnp.where(j < nskip, 1.0, 0.0) * ds_ref[...]
        g = g2_ref[0] + g2_ref[1] + dy_ref[...] * skip
        d_pre = g * (sig * (1.0 + pre * (1.0 - sig)))
        db_ref[...] += jnp.sum(d_pre, axis=0, keepdims=True)
        dz = jnp.zeros_like(d_pre)
        for k in range(CONV_TAPS):
            o = k - CONV_TAPS // 2
            dw_ref[k:k + 1, :] += jnp.sum(d_pre * terms[k], axis=0, keepdims=True)
            sh = d_pre if o == 0 else pltpu.roll(d_pre, o % TB, 0)
            dz = dz + jnp.where((pos - o >= 0) & (pos - o < row_len), sh, 0.0) * w_ref[k:k + 1, :]
        dz_ref[...] = dz.astype(BF16)

    jd = lambda j: jnp.minimum(j, nskip - 1)
    return pl.pallas_call(
        body, name="conv_bwd", grid=(c // tc, t // TB),
        in_specs=[pl.BlockSpec((TB, tc), lambda j, i: (i, j)), pl.BlockSpec((2, TB, tc), lambda j, i: (0, i, j)),
                  pl.BlockSpec((TB, tc), lambda j, i: (i, jd(j))), pl.BlockSpec((1, tc), lambda j, i: (0, jd(j))),
                  pl.BlockSpec((8, tc), lambda j, i: (0, j)), pl.BlockSpec((1, tc), lambda j, i: (0, j))],
        out_specs=[pl.BlockSpec((TB, tc), lambda j, i: (i, j)), pl.BlockSpec((8, tc), lambda j, i: (0, j)),
                   pl.BlockSpec((1, tc), lambda j, i: (0, j))],
        out_shape=[jax.ShapeDtypeStruct((t, c), BF16), jax.ShapeDtypeStruct((8, c), F32), jax.ShapeDtypeStruct((1, c), F32)],
        compiler_params=_params("arbitrary", "arbitrary"),
    )(z_xbc, d_xbc2, d_y, d_skip_row, conv_w8, conv_b)


def _scan_chunk(dirn, s, nch, ncc):
    bwd = jnp.where(s < ncc, ncc - 1 - s, nch - 1 - (s - ncc))
    return jnp.where(dirn == 0, s, bwd)


def _ssd_decays(dt_ref, dtb_ref, alog_ref, tri):
    raw = dt_ref[...] + dtb_ref[0]
    dt = _softplus(raw)
    a_neg = -jnp.exp(alog_ref[0])
    a = dt * a_neg
    s = _dot(tri, a, NN, HI)
    stot = jnp.sum(a, axis=0, keepdims=True)
    return raw, dt, a_neg, s, stot, s.T


def _split(v):
    hi = v.astype(BF16)
    return hi, (v - hi.astype(F32)).astype(BF16)


def _expand(v, indt_ref):
    hi, lo = _split(v)
    return _dot(hi, indt_ref[...]) + _dot(lo, indt_ref[...])


def _head_sums(v, ind_ref):
    hi, lo = _split(v)
    return _dot(hi, ind_ref[...]) + _dot(lo, ind_ref[...])


def _ssd_fwd(xbc, z_dt, dtb, alog, tri, ind_t, d, ctx_len, side=None):
    t = xbc.shape[0]
    nch, ncc = t // Q, ctx_len // Q
    heads = d // HP
    groups = heads // 4
    gn = groups * LANES

    def body(xbc_ref, dt_ref, dtb_ref, alog_ref, tri_ref, indt_ref, y_ref, hs_ref, h_scr, xdb_scr, xde_scr, esx_scr):
        @pl.when(pl.program_id(1) == 0)
        def _():
            h_scr[...] = jnp.zeros_like(h_scr)

        tri = tri_ref[0]
        mask = tri > 0.0
        _, dt, _, s, stot, s_t = _ssd_decays(dt_ref, dtb_ref, alog_ref, tri)
        esx_scr[...] = _expand(jnp.exp(s), indt_ref)
        etot_x = _expand(jnp.broadcast_to(jnp.exp(stot), (8, LANES)), indt_ref)[0:1]
        xd = xbc_ref[:, :d] * _expand(dt, indt_ref)
        xdb_scr[...] = xd.astype(BF16)
        xde_scr[...] = (xd * _expand(jnp.exp(stot - s), indt_ref)).astype(BF16)
        left = lax.broadcasted_iota(jnp.int32, (Q, LANES), 1) < HP
        hs_ref[0, 0] = h_scr[...]
        for g in range(groups):
            b32 = xbc_ref[:, d + g * LANES:d + (g + 1) * LANES]
            bb = b32.astype(BF16)
            bbt = b32.T.astype(BF16)
            cb = xbc_ref[:, d + gn + g * LANES:d + gn + (g + 1) * LANES].astype(BF16)
            cbt = _dot(cb, bb, NT)
            for pr in (2 * g, 2 * g + 1):
                h0 = 2 * pr
                cols = slice(pr * LANES, (pr + 1) * LANES)
                xdb = xdb_scr[:, cols]
                res = []
                for h in (h0, h0 + 1):
                    lm = jnp.exp(jnp.where(mask, s[:, h:h + 1] - s_t[h:h + 1, :], -jnp.inf))
                    res.append(_dot((cbt * lm).astype(BF16), xdb))
                hp = h_scr[:, cols]
                y_ref[0, :, cols] = jnp.where(left, res[0], res[1]) + _dot(cb, hp.astype(BF16)) * esx_scr[:, cols]
                h_scr[:, cols] = hp * etot_x[:, cols] + _dot(bbt, xde_scr[:, cols])

    cidx = lambda dd, ss: _scan_chunk(dd, ss, nch, ncc)
    return _hosted_call(
        body, side, name="ssd_fwd", grid=(2, nch),
        in_specs=[pl.BlockSpec((Q, 2 * d), lambda dd, ss: (cidx(dd, ss), 0)),
                  pl.BlockSpec((Q, LANES), lambda dd, ss: (cidx(dd, ss), dd)),
                  pl.BlockSpec((1, 1, LANES), lambda dd, ss: (dd, 0, 0)),
                  pl.BlockSpec((1, 1, LANES), lambda dd, ss: (dd, 0, 0)),
                  pl.BlockSpec((1, Q, Q), lambda dd, ss: (dd, 0, 0)),
                  pl.BlockSpec((LANES, d), lambda dd, ss: (0, 0))],
        out_specs=[pl.BlockSpec((1, Q, d), lambda dd, ss: (dd, cidx(dd, ss), 0)),
                   pl.BlockSpec((1, 1, LANES, d), lambda dd, ss: (dd, cidx(dd, ss), 0, 0))],
        out_shape=[jax.ShapeDtypeStruct((2, t, d), F32), jax.ShapeDtypeStruct((2, nch, LANES, d), F32)],
        scratch_shapes=[pltpu.VMEM((LANES, d), F32), pltpu.VMEM((Q, d), BF16), pltpu.VMEM((Q, d), BF16), pltpu.VMEM((Q, d), F32)],
        args=(xbc, z_dt, dtb, alog, tri, ind_t))


def _ssd_bwd(xbc, z_dt, dtb, alog, tri, tri_t, ind_t, ind, d_y, y2, hs, d, ctx_len, side=None):
    t = xbc.shape[0]
    nch, ncc = t // Q, ctx_len // Q
    heads = d // HP
    groups = heads // 4
    gn = groups * LANES

    def body(xbc_ref, dt_ref, dtb_ref, alog_ref, tri_ref, trit_ref, indt_ref, ind_ref, dy_ref, y_ref, hs_ref,
             dx_ref, dzdt_ref, dbias_ref, dalog_ref, dh_scr, dtx_scr, ex_scr, xdb_scr, xde_scr, dyb_scr, dye_scr, dxd_scr, bdh_scr):
        @pl.when(pl.program_id(1) == 0)
        def _():
            dh_scr[...] = jnp.zeros_like(dh_scr)
            dbias_ref[...] = jnp.zeros_like(dbias_ref)
            dalog_ref[...] = jnp.zeros_like(dalog_ref)

        tri = tri_ref[0]
        mask = tri > 0.0
        mask_t = trit_ref[0] > 0.0
        raw, dt, a_neg, s, stot, s_t = _ssd_decays(dt_ref, dtb_ref, alog_ref, tri)
        etot = jnp.exp(stot)
        etot_x = _expand(jnp.broadcast_to(etot, (8, LANES)), indt_ref)[0:1]
        dtx_scr[...] = _expand(dt, indt_ref)
        ex_scr[...] = _expand(jnp.exp(stot - s), indt_ref)
        xd = xbc_ref[:, :d] * dtx_scr[...]
        xdb_scr[...] = xd.astype(BF16)
        xde_scr[...] = (xd * ex_scr[...]).astype(BF16)
        dyb_scr[...] = dy_ref[...].astype(BF16)
        dye_scr[...] = (dy_ref[...] * _expand(jnp.exp(s), indt_ref)).astype(BF16)
        hd_cols = jnp.sum(dh_scr[...] * hs_ref[0, 0], axis=0, keepdims=True)
        left = lax.broadcasted_iota(jnp.int32, (Q, LANES), 1) < HP
        for g in range(groups):
            b32 = xbc_ref[:, d + g * LANES:d + (g + 1) * LANES]
            c32 = xbc_ref[:, d + gn + g * LANES:d + gn + (g + 1) * LANES]
            bb, cb = b32.astype(BF16), c32.astype(BF16)
            c_t = c32.T.astype(BF16)
            cbt = _dot(cb, bb, NT)
            cbt_t = _dot(bb, cb, NT)
            d_cbt = jnp.zeros((Q, Q), F32)
            d_b = jnp.zeros((Q, LANES), F32)
            d_c = jnp.zeros((Q, LANES), F32)
            for pr in (2 * g, 2 * g + 1):
                h0 = 2 * pr
                cols = slice(pr * LANES, (pr + 1) * LANES)
                xdb = xdb_scr[:, cols]
                dyb = dyb_scr[:, cols]
                dyeb = dye_scr[:, cols]
                hpb = hs_ref[0, 0, :, cols].astype(BF16)
                dhp = dh_scr[:, cols]
                dhb = dhp.astype(BF16)
                parts = []
                for hh, h in enumerate((h0, h0 + 1)):
                    mine = left if hh == 0 else jnp.logical_not(left)
                    diff = s[:, h:h + 1] - s_t[h:h + 1, :]
                    lm = jnp.exp(jnp.where(mask, diff, -jnp.inf))
                    lm_t = jnp.exp(jnp.where(mask_t, -diff, -jnp.inf))
                    gm = _dot(jnp.where(mine, dyb, jnp.zeros_like(dyb)), xdb, NT)
                    d_cbt = d_cbt + gm * lm
                    parts.append(_dot((cbt_t * lm_t).astype(BF16), dyb))
                dxd_scr[:, cols] = jnp.where(left, parts[0], parts[1])
                bdh_scr[:, cols] = _dot(bb, dhb)
                d_c = d_c + _dot(dyeb, hpb, NT)
                d_b = d_b + _dot(xde_scr[:, cols], dhb, NT)
                dh_scr[:, cols] = dhp * etot_x[:, cols] + _dot(c_t, dyeb)
            dx_ref[0, :, d + g * LANES:d + (g + 1) * LANES] = d_b + _dot(d_cbt.T.astype(BF16), cb)
            dx_ref[0, :, d + gn + g * LANES:d + gn + (g + 1) * LANES] = d_c + _dot(d_cbt.astype(BF16), bb)
        x = xbc_ref[:, :d]
        ebdh = ex_scr[...] * bdh_scr[...]
        d_xd = dxd_scr[...] + ebdh
        dx_ref[0, :, :d] = d_xd * dtx_scr[...]
        r_dy = _head_sums(dyb_scr[...].astype(F32) * y_ref[0], ind_ref)
        r_diag = _head_sums(xdb_scr[...].astype(F32) * dxd_scr[...], ind_ref)
        r_e = _head_sums(x * dtx_scr[...] * ebdh, ind_ref)
        r_dx = _head_sums(d_xd * x, ind_ref)
        hd = _head_sums(jnp.broadcast_to(hd_cols, (8, d)), ind_ref)[0:1]
        d_s = r_dy - r_diag - r_e
        d_stot = jnp.sum(r_e, axis=0, keepdims=True) + etot * hd
        d_a = _dot(trit_ref[0], d_s, NN, HI) + d_stot
        valid = lax.broadcasted_iota(jnp.int32, (Q, LANES), 1) < heads
        d_dt_tot = jnp.where(valid, d_a * a_neg + r_dx, 0.0)
        d_raw = d_dt_tot * _sigmoid(raw)
        dzdt_ref[...] = d_raw.astype(BF16)
        dbias_ref[0] += jnp.sum(d_raw, axis=0, keepdims=True)
        dalog_ref[0] += jnp.sum(jnp.where(valid, d_a * dt, 0.0), axis=0, keepdims=True) * a_neg

    cidx = lambda dd, ss: _scan_chunk(dd, nch - 1 - ss, nch, ncc)
    full = lambda shape: pltpu.VMEM(shape, F32)
    half = lambda shape: pltpu.VMEM(shape, BF16)
    return _hosted_call(
        body, side, name="ssd_bwd", grid=(2, nch),
        in_specs=[pl.BlockSpec((Q, 2 * d), lambda dd, ss: (cidx(dd, ss), 0)),
                  pl.BlockSpec((Q, LANES), lambda dd, ss: (cidx(dd, ss), dd)),
                  pl.BlockSpec((1, 1, LANES), lambda dd, ss: (dd, 0, 0)),
                  pl.BlockSpec((1, 1, LANES), lambda dd, ss: (dd, 0, 0)),
                  pl.BlockSpec((1, Q, Q), lambda dd, ss: (dd, 0, 0)),
                  pl.BlockSpec((1, Q, Q), lambda dd, ss: (dd, 0, 0)),
                  pl.BlockSpec((LANES, d), lambda dd, ss: (0, 0)),
                  pl.BlockSpec((d, LANES), lambda dd, ss: (0, 0)),
                  pl.BlockSpec((Q, d), lambda dd, ss: (cidx(dd, ss), 0)),
                  pl.BlockSpec((1, Q, d), lambda dd, ss: (dd, cidx(dd, ss), 0)),
                  pl.BlockSpec((1, 1, LANES, d), lambda dd, ss: (dd, cidx(dd, ss), 0, 0))],
        out_specs=[pl.BlockSpec((1, Q, 2 * d), lambda dd, ss: (dd, cidx(dd, ss), 0)),
                   pl.BlockSpec((Q, LANES), lambda dd, ss: (cidx(dd, ss), dd)),
                   pl.BlockSpec((1, 1, LANES), lambda dd, ss: (dd, 0, 0)),
                   pl.BlockSpec((1, 1, LANES), lambda dd, ss: (dd, 0, 0))],
        out_shape=[jax.ShapeDtypeStruct((2, t, 2 * d), F32), jax.ShapeDtypeStruct((t, 2 * LANES), BF16),
                   jax.ShapeDtypeStruct((2, 1, LANES), F32), jax.ShapeDtypeStruct((2, 1, LANES), F32)],
        scratch_shapes=[full((LANES, d)), full((Q, d)), full((Q, d)), half((Q, d)), half((Q, d)), half((Q, d)), half((Q, d)),
                        full((Q, d)), full((Q, d))],
        args=(xbc, z_dt, dtb, alog, tri, tri_t, ind_t, ind, d_y, y2, hs))


def _mix_common(zm_ref, y2_ref, xh_ref, dsk_ref, gv_ref, ws_ref, bst_ref, d):
    groups = d // LANES
    z_ssd, u, v, z_mlp = (zm_ref[:, k * d:(k + 1) * d] for k in range(4))
    y = y2_ref[0] + y2_ref[1] + dsk_ref[...] * xh_ref[...]
    sig_a = _sigmoid(z_ssd)
    ya_pre = y * (z_ssd * sig_a)
    r_v = _rms(v)
    vn = (v * r_v * gv_ref[...]).astype(BF16)
    sg = jnp.concatenate(
        [_dot(ws_ref[g].astype(BF16), vn[:, g * LANES:(g + 1) * LANES]) + bst_ref[:, g:g + 1] for g in range(groups)], axis=1)
    sig_m = _sigmoid(z_mlp)
    yb_pre = u * sg * (z_mlp * sig_m)
    return z_ssd, u, v, z_mlp, y, sig_a, ya_pre, r_v, vn, sg, sig_m, yb_pre


def _mix_fwd(z_mid, y2, xbc, dsk_row, g_ssd, g_v, g_mlp, w_s, b_st):
    t = z_mid.shape[0]
    d = z_mid.shape[1] // 4
    groups = d // LANES

    def body(zm_ref, y2_ref, xh_ref, dsk_ref, ga_ref, gv_ref, gm_ref, ws_ref, bst_ref, o_ref):
        (_, _, _, _, _, _, ya_pre, _, _, _, _, yb_pre) = _mix_common(zm_ref, y2_ref, xh_ref, dsk_ref, gv_ref, ws_ref, bst_ref, d)
        o_ref[:, :d] = (ya_pre * _rms(ya_pre) * ga_ref[...]).astype(BF16)
        o_ref[:, d:] = (yb_pre * _rms(yb_pre) * gm_ref[...]).astype(BF16)

    row = pl.BlockSpec((1, d), lambda i: (0, 0))
    return pl.pallas_call(
        body, name="mix_fwd", grid=(t // Q,),
        in_specs=[pl.BlockSpec((Q, 4 * d), lambda i: (i, 0)), pl.BlockSpec((2, Q, d), lambda i: (0, i, 0)),
                  pl.BlockSpec((Q, d), lambda i: (i, 0)), row, row, row, row,
                  pl.BlockSpec((groups, Q, Q), lambda i: (0, 0, 0)), pl.BlockSpec((Q, LANES), lambda i: (0, 0))],
        out_specs=pl.BlockSpec((Q, 2 * d), lambda i: (i, 0)),
        out_shape=jax.ShapeDtypeStruct((t, 2 * d), BF16), compiler_params=_params("arbitrary"),
    )(z_mid, y2, xbc, dsk_row, g_ssd, g_v, g_mlp, w_s, b_st)


def _mix_bwd(z_mid, y2, xbc, d_ycat, dsk_row, g_ssd, g_v, g_mlp, w_s, w_st, b_st, ind_head, ind_group):
    t = z_mid.shape[0]
    d = z_mid.shape[1] // 4
    groups = d // LANES
    nsteps = t // Q

    def body(zm_ref, y2_ref, xh_ref, dyc_ref, dsk_ref, ga_ref, gv_ref, gm_ref, ws_ref, wst_ref, bst_ref, ih_ref, ig_ref,
             dzm_ref, dy_ref, vec_ref, dws_ref, dbs_ref, dsk_acc, dsg_acc):
        i = pl.program_id(0)

        @pl.when(i == 0)
        def _():
            vec_ref[...] = jnp.zeros_like(vec_ref)
            dws_ref[...] = jnp.zeros_like(dws_ref)
            dsk_acc[...] = jnp.zeros_like(dsk_acc)
            dsg_acc[...] = jnp.zeros_like(dsg_acc)

        (z_ssd, u, v, z_mlp, y, sig_a, ya_pre, r_v, vn, sg, sig_m, yb_pre) = _mix_common(
            zm_ref, y2_ref, xh_ref, dsk_ref, gv_ref, ws_ref, bst_ref, d)
        d_ya = dyc_ref[:, :d]
        r_a = _rms(ya_pre)
        vec_ref[0:1, :] += jnp.sum(d_ya * (ya_pre * r_a), axis=0, keepdims=True)
        d_ya_pre = _rms_bwd(ya_pre, r_a, d_ya * ga_ref[...])
        d_y = d_ya_pre * (z_ssd * sig_a)
        dy_ref[...] = d_y
        dsk_acc[...] += jnp.sum(d_y * xh_ref[...], axis=0, keepdims=True)
        dzm_ref[:, 0:d] = (d_ya_pre * y * (sig_a * (1.0 + z_ssd * (1.0 - sig_a)))).astype(BF16)
        d_yb = dyc_ref[:, d:]
        r_b = _rms(yb_pre)
        vec_ref[2:3, :] += jnp.sum(d_yb * (yb_pre * r_b), axis=0, keepdims=True)
        d_yb_pre = _rms_bwd(yb_pre, r_b, d_yb * gm_ref[...])
        silu_m = z_mlp * sig_m
        dzm_ref[:, d:2 * d] = (d_yb_pre * sg * silu_m).astype(BF16)
        dzm_ref[:, 3 * d:4 * d] = (d_yb_pre * u * sg * (sig_m * (1.0 + z_mlp * (1.0 - sig_m)))).astype(BF16)
        d_sg = d_yb_pre * u * silu_m
        dsg_acc[...] += d_sg
        d_sgb = d_sg.astype(BF16)
        d_vn = []
        for g in range(groups):
            cols = slice(g * LANES, (g + 1) * LANES)
            dws_ref[g] += _dot(d_sgb[:, cols], vn[:, cols], NT)
            d_vn.append(_dot(wst_ref[g].astype(BF16), d_sgb[:, cols]))
        d_vn = jnp.concatenate(d_vn, axis=1)
        vec_ref[1:2, :] += jnp.sum(d_vn * (v * r_v), axis=0, keepdims=True)
        dzm_ref[:, 2 * d:3 * d] = _rms_bwd(v, r_v, d_vn * gv_ref[...]).astype(BF16)

        @pl.when(i == nsteps - 1)
        def _():
            vec_ref[3:4, 0:LANES] = _dot(dsk_acc[...], ih_ref[...], NN, HI)
            dbs_ref[...] = _dot(dsg_acc[...], ig_ref[...], NN, HI)

    row = pl.BlockSpec((1, d), lambda i: (0, 0))
    wsp = pl.BlockSpec((groups, Q, Q), lambda i: (0, 0, 0))
    ind = pl.BlockSpec((d, LANES), lambda i: (0, 0))
    return pl.pallas_call(
        body, name="mix_bwd", grid=(nsteps,),
        in_specs=[pl.BlockSpec((Q, 4 * d), lambda i: (i, 0)), pl.BlockSpec((2, Q, d), lambda i: (0, i, 0)),
                  pl.BlockSpec((Q, d), lambda i: (i, 0)), pl.BlockSpec((Q, 2 * d), lambda i: (i, 0)),
                  row, row, row, row, wsp, wsp, pl.BlockSpec((Q, LANES), lambda i: (0, 0)), ind, ind],
        out_specs=[pl.BlockSpec((Q, 4 * d), lambda i: (i, 0)), pl.BlockSpec((Q, d), lambda i: (i, 0)),
                   pl.BlockSpec((8, d), lambda i: (0, 0)), wsp, pl.BlockSpec((Q, LANES), lambda i: (0, 0))],
        out_shape=[jax.ShapeDtypeStruct((t, 4 * d), BF16), jax.ShapeDtypeStruct((t, d), F32),
                   jax.ShapeDtypeStruct((8, d), F32), jax.ShapeDtypeStruct((groups, Q, Q), F32),
                   jax.ShapeDtypeStruct((Q, LANES), F32)],
        scratch_shapes=[pltpu.VMEM((1, d), F32), pltpu.VMEM((Q, d), F32)],
        compiler_params=_params("arbitrary"),
    )(z_mid, y2, xbc, d_ycat, dsk_row, g_ssd, g_v, g_mlp, w_s, w_st, b_st, ind_head, ind_group)


def _ada_fwd(c16, w_ada, b_loc):
    depth, d, n = w_ada.shape
    tn = _tile(n, (512, 256, 128))

    def body(c_ref, w_ref, b_ref, o_ref):
        cv = c_ref[...]
        o_ref[0] = _dot(cv * _sigmoid(cv), w_ref[0], NN, HI) + b_ref[0]

    return pl.pallas_call(
        body, name="ada_fwd", grid=(depth, n // tn),
        in_specs=[pl.BlockSpec((16, d), lambda l, j: (0, 0)), pl.BlockSpec((1, d, tn), lambda l, j: (l, 0, j)),
                  pl.BlockSpec((1, 1, tn), lambda l, j: (l, 0, j))],
        out_specs=pl.BlockSpec((1, 16, tn), lambda l, j: (l, 0, j)),
        out_shape=jax.ShapeDtypeStruct((depth, 16, n), F32), compiler_params=_params("arbitrary", "arbitrary"),
    )(c16, w_ada, b_loc)


def _ada_bwd(c_t, dm_loc, w_ada):
    depth, d, n = w_ada.shape
    tn = _tile(n, (512, 256, 128))

    def body(s_ref, dm_ref, w_ref, gw_ref, dsc_ref):
        @pl.when((pl.program_id(0) == 0) & (pl.program_id(1) == 0))
        def _():
            dsc_ref[...] = jnp.zeros_like(dsc_ref)

        cv = s_ref[...]
        gw_ref[0] = _dot(cv * _sigmoid(cv), dm_ref[0], NN, HI)
        dsc_ref[...] += _dot(dm_ref[0, 8:16, :], w_ref[0], NT, HI)

    return pl.pallas_call(
        body, name="ada_bwd", grid=(depth, n // tn),
        in_specs=[pl.BlockSpec((d, LANES), lambda l, j: (0, 0)), pl.BlockSpec((1, LANES, tn), lambda l, j: (l, 0, j)),
                  pl.BlockSpec((1, d, tn), lambda l, j: (l, 0, j))],
        out_specs=[pl.BlockSpec((1, d, tn), lambda l, j: (l, 0, j)), pl.BlockSpec((8, d), lambda l, j: (0, 0))],
        out_shape=[jax.ShapeDtypeStruct((depth, d, n), F32), jax.ShapeDtypeStruct((8, d), F32)],
        compiler_params=_params("arbitrary", "arbitrary"),
    )(c_t, dm_loc, w_ada)


def _rowsum(x):
    depth, r, n = x.shape

    def body(x_ref, o_ref):
        o_ref[0] = jnp.sum(x_ref[0], axis=0, keepdims=True)

    return pl.pallas_call(
        body, name="rowsum", grid=(depth,),
        in_specs=[pl.BlockSpec((1, r, n), lambda l: (l, 0, 0))], out_specs=pl.BlockSpec((1, 1, n), lambda l: (l, 0, 0)),
        out_shape=jax.ShapeDtypeStruct((depth, 1, n), F32), compiler_params=_params("arbitrary"),
    )(x)


def _cctx_grad(d_scc, c_ctx_row):
    def body(g_ref, c_ref, o_ref):
        cv = c_ref[...]
        sig = _sigmoid(cv)
        o_ref[...] = g_ref[...] * (sig * (1.0 + cv * (1.0 - sig)))

    return pl.pallas_call(body, name="cctx_grad", out_shape=jax.ShapeDtypeStruct(c_ctx_row.shape, F32))(d_scc, c_ctx_row)


def _sum_lead(x, name):
    k, r, c = x.shape
    tr = _tile(r, [tt for tt in (1024, 512, 256, 128, 64, 32, 16, 8) if k * tt * c * x.dtype.itemsize <= SUM_BLOCK_BYTES])

    def body(x_ref, o_ref):
        acc = x_ref[0].astype(F32)
        for e in range(1, k):
            acc = acc + x_ref[e].astype(F32)
        o_ref[...] = acc

    return pl.pallas_call(
        body, name=name, grid=(r // tr,),
        in_specs=[pl.BlockSpec((k, tr, c), lambda i: (0, i, 0))], out_specs=pl.BlockSpec((tr, c), lambda i: (i, 0)),
        out_shape=jax.ShapeDtypeStruct((r, c), F32), compiler_params=_params("arbitrary"),
    )(x)


def _adamw(w, g, m, v, name, g2=None):
    r, c = w.shape
    tr = _tile(r, [tt for tt in (2048, 1024, 512, 256, 128, 64, 32, 16, 8) if tt * c * 4 <= ADAM_BLOCK_BYTES])
    two = g2 is not None
    bc1 = 1.0 - ADAM_B1 ** ADAM_STEP
    bc2 = 1.0 - ADAM_B2 ** ADAM_STEP

    def body(*refs):
        if two:
            w_ref, g_ref, g2_ref, m_ref, v_ref, go_ref, d_ref, mo_ref, vo_ref = refs
            gr = g_ref[...] + g2_ref[...]
        else:
            w_ref, g_ref, m_ref, v_ref, go_ref, d_ref, mo_ref, vo_ref = refs
            gr = g_ref[...]
        mn = ADAM_B1 * m_ref[...] + (1.0 - ADAM_B1) * gr
        vn = ADAM_B2 * v_ref[...] + (1.0 - ADAM_B2) * (gr * gr)
        go_ref[...] = gr
        mo_ref[...] = mn
        vo_ref[...] = vn
        d_ref[...] = -ADAM_LR * ((mn / bc1) / (jnp.sqrt(vn / bc2) + ADAM_EPS) + ADAM_WD * w_ref[...])

    blk = pl.BlockSpec((tr, c), lambda i: (i, 0))
    ins = (w, g, g2, m, v) if two else (w, g, m, v)
    return pl.pallas_call(
        body, name=name, grid=(r // tr,), in_specs=[blk] * len(ins), out_specs=[blk] * 4,
        out_shape=[jax.ShapeDtypeStruct((r, c), F32)] * 4, compiler_params=_params("arbitrary"),
    )(*ins)


def _flip(pos, k):
    x, y, c = pos
    return (x ^ ((k >> 2) & 1), y ^ ((k >> 1) & 1), c ^ (k & 1))


def _lin(pos):
    return 4 * pos[0] + 2 * pos[1] + pos[2]


def _chip(pos):
    return 2 * pos[0] + pos[1]


def _here():
    return (lax.axis_index("x"), lax.axis_index("y"), lax.axis_index("c"))


def _all_gather8(x, name):
    def body(x_ref, o_ref, send, recv, own):
        me = _here()
        mine = pltpu.make_async_copy(x_ref, o_ref.at[_lin(me)], own)
        mine.start()
        out = [pltpu.make_async_remote_copy(src_ref=x_ref, dst_ref=o_ref.at[_lin(me)], send_sem=send.at[k - 1],
                                            recv_sem=recv.at[k - 1], device_id=_flip(me, k), device_id_type=MESH)
               for k in range(1, 8)]
        for cp in out:
            cp.start()
        for k in range(1, 8):
            peer = _flip(me, k)
            pltpu.make_async_remote_copy(src_ref=x_ref, dst_ref=o_ref.at[_lin(peer)], send_sem=send.at[k - 1],
                                         recv_sem=recv.at[k - 1], device_id=peer, device_id_type=MESH).wait_recv()
        for cp in out:
            cp.wait_send()
        mine.wait()

    return pl.pallas_call(
        body, name=name, in_specs=[ANY], out_specs=ANY, out_shape=jax.ShapeDtypeStruct((8,) + x.shape, x.dtype),
        scratch_shapes=[pltpu.SemaphoreType.DMA((7,)), pltpu.SemaphoreType.DMA((7,)), pltpu.SemaphoreType.DMA],
    )(x)


class _Exchange4:
    def __init__(self, x, scatter):
        self.x, self.scatter = x, scatter
        self.out_shape = jax.ShapeDtypeStruct(x.shape if scatter else (4,) + x.shape, x.dtype)
        self.scratch = [pltpu.SemaphoreType.DMA((3,)), pltpu.SemaphoreType.DMA((3,)), pltpu.SemaphoreType.DMA]

    def _copies(self, x_ref, o_ref, send, recv, own, arrivals):
        me = _here()
        part = (lambda pos: x_ref.at[_chip(pos)]) if self.scatter else (lambda pos: x_ref)
        local = pltpu.make_async_copy(part(me), o_ref.at[_chip(me)], own)
        outs, ins = [], []
        for j, k in enumerate((2, 4, 6)):
            peer = _flip(me, k)
            sems = dict(send_sem=send.at[j], recv_sem=recv.at[j], device_id=peer, device_id_type=MESH)
            outs.append(pltpu.make_async_remote_copy(src_ref=part(peer), dst_ref=o_ref.at[_chip(me)], **sems))
            if arrivals:
                ins.append(pltpu.make_async_remote_copy(src_ref=part(me), dst_ref=o_ref.at[_chip(peer)], **sems))
        return local, outs, ins

    def start(self, *refs):
        local, outs, _ = self._copies(*refs, arrivals=False)
        local.start()
        for cp in outs:
            cp.start()

    def wait(self, *refs):
        local, outs, ins = self._copies(*refs, arrivals=True)
        for cp in ins:
            cp.wait_recv()
        for cp in outs:
            cp.wait_send()
        local.wait()


def _exchange4(x, scatter, name):
    ex = _Exchange4(x, scatter)

    def body(*refs):
        ex.start(*refs)
        ex.wait(*refs)

    return pl.pallas_call(body, name=name, in_specs=[ANY], out_specs=ANY, out_shape=ex.out_shape, scratch_shapes=ex.scratch)(x)


def _hosted_call(body, side, *, name, grid, in_specs, out_specs, out_shape, scratch_shapes, args):
    n_in, n_out = len(in_specs), len(out_specs)
    params = _params(*(["arbitrary"] * len(grid)))
    if side is None:
        return pl.pallas_call(body, name=name, grid=grid, in_specs=in_specs, out_specs=out_specs, out_shape=out_shape,
                              scratch_shapes=scratch_shapes, compiler_params=params)(*args)

    def hosted(*refs):
        ins, x_ref = refs[:n_in], refs[n_in]
        outs, o_ref = refs[n_in + 1:n_in + 1 + n_out], refs[n_in + 1 + n_out]
        scratch, sems = refs[n_in + n_out + 2:-3], refs[-3:]
        ids = [pl.program_id(a) for a in range(len(grid))]
        first, last = ids[0] == 0, ids[0] == grid[0] - 1
        for a in range(1, len(grid)):
            first, last = first & (ids[a] == 0), last & (ids[a] == grid[a] - 1)

        @pl.when(first)
        def _():
            side.start(x_ref, o_ref, *sems)

        body(*ins, *outs, *scratch)

        @pl.when(last)
        def _():
            side.wait(x_ref, o_ref, *sems)

    return pl.pallas_call(hosted, name=name + "_x", grid=grid, in_specs=list(in_specs) + [ANY], out_specs=list(out_specs) + [ANY],
                          out_shape=list(out_shape) + [side.out_shape], scratch_shapes=list(scratch_shapes) + side.scratch,
                          compiler_params=params)(*args, side.x)


def _sibling_swap(x, name):
    def body(x_ref, o_ref, send, recv):
        me = _here()
        cp = pltpu.make_async_remote_copy(src_ref=x_ref, dst_ref=o_ref, send_sem=send, recv_sem=recv,
                                          device_id=_flip(me, 1), device_id_type=MESH)
        cp.start()
        cp.wait()

    return pl.pallas_call(
        body, name=name, in_specs=[ANY], out_specs=ANY, out_shape=jax.ShapeDtypeStruct(x.shape, x.dtype),
        scratch_shapes=[pltpu.SemaphoreType.DMA, pltpu.SemaphoreType.DMA],
    )(x)


def _pad_lanes(a, width):
    return jnp.pad(a, [(0, 0)] * (a.ndim - 1) + [(0, width - a.shape[-1])])


def kernel(x, c, ctx, c_ctx, w_ada, b_ada, g_pre, g_post, w_in, conv_w, conv_b, dt_bias, a_log, d_skip, g_ssd, g_v, w_s, b_s, g_mlp, w_out, loss_target, m_c_ctx, m_w_ada, m_b_ada, m_g_pre, m_g_post, m_w_in, m_conv_w, m_conv_b, m_dt_bias, m_a_log, m_d_skip, m_g_ssd, m_g_v, m_w_s, m_b_s, m_g_mlp, m_w_out, v_c_ctx, v_w_ada, v_b_ada, v_g_pre, v_g_post, v_w_in, v_conv_w, v_conv_b, v_dt_bias, v_a_log, v_d_skip, v_g_ssd, v_g_v, v_w_s, v_b_s, v_g_mlp, v_w_out):
    depth, d = g_pre.shape
    seq, ctx_len = x.shape[1], ctx.shape[1]
    heads = d // HP
    in_w = 6 * d + 2 * heads
    groups_mlp = d // LANES
    t = ctx_len + seq
    assert ctx_len == TB and seq % TB == 0 and TB % ROW == 0 and heads % 4 == 0 and heads <= LANES and d % LANES == 0
    assert w_in.shape == (depth, d, in_w // 4)

    xi, yi, ci = lax.axis_index("x"), lax.axis_index("y"), lax.axis_index("c")
    chip = 2 * xi + yi
    me = 4 * xi + 2 * yi + ci

    n_ada = 3 * d // 4
    c_all = _all_gather8(c, "ag_c")[:, 0, :]
    c16 = jnp.concatenate([c_all, c_ctx[None, :], jnp.zeros((7, d), F32)], axis=0)
    b_loc = lax.dynamic_slice_in_dim(b_ada, chip * n_ada, n_ada, axis=1)[:, None, :]
    mods_loc = _all_gather8(_ada_fwd(c16, w_ada, b_loc).reshape(depth * 16, n_ada), "ag_mods")
    mods_loc = mods_loc.reshape(4, 2, depth, 16, n_ada)[:, 0]
    mods_full = jnp.moveaxis(mods_loc, 0, 2).reshape(depth, 16, 3 * d)
    mods_x = lax.dynamic_index_in_dim(mods_full, me, axis=1, keepdims=False).reshape(depth, 3, d)
    mods_c = mods_full[:, 8, :].reshape(depth, 3, d)
    mods = _pad_rows8(jnp.stack([mods_c, mods_x], axis=1))

    w_in_b, w_out_b = w_in.astype(BF16), w_out.astype(BF16)

    def lay_out(w_in_all):
        full = jnp.moveaxis(w_in_all, 0, 1).reshape(d, in_w)
        w_dt_l = jnp.concatenate([_pad_lanes(full[:, 2 * d:2 * d + heads], LANES),
                                  _pad_lanes(full[:, 2 * d + heads:2 * d + 2 * heads], LANES)], axis=1)
        return full[:, :2 * d], full[:, 2 * d + 2 * heads:], w_dt_l

    w_in_all = _exchange4(w_in_b[0], False, "ag_w_in")
    conv_w_full = jnp.moveaxis(_exchange4(conv_w, False, "ag_conv_w"), 0, 2).reshape(depth, CONV_TAPS, 2 * d)
    conv_w8 = jnp.pad(conv_w_full, ((0, 0), (0, 8 - CONV_TAPS), (0, 0)))

    tri = jnp.stack([jnp.tril(jnp.ones((Q, Q), F32)), jnp.triu(jnp.ones((Q, Q), F32))])
    tri_t = jnp.swapaxes(tri, 1, 2)
    dtb = _pad_lanes(dt_bias, LANES)[:, :, None, :]
    alog = _pad_lanes(a_log, LANES)[:, :, None, :]
    dsk_row = jnp.repeat(d_skip, HP, axis=1)[:, None, :]
    w_st = jnp.swapaxes(w_s, 2, 3)
    b_st = _pad_lanes(jnp.swapaxes(b_s, 1, 2), LANES)
    chan = jnp.arange(d)
    ind_head = (chan[:, None] // HP == jnp.arange(LANES)[None, :]).astype(F32)
    ind_b, ind_t = ind_head.astype(BF16), ind_head.T.astype(BF16)
    ind_group = (chan[:, None] // LANES == jnp.arange(LANES)[None, :]).astype(F32)

    stream = jnp.concatenate([ctx[0], x[0]], axis=0)
    saved = []
    for l in range(depth):
        w_xbc, w_mid, w_dt = lay_out(w_in_all)
        more = l + 1 < depth
        hx = _pre_fwd(stream, g_pre[l][None], mods[l])
        z_xbc = _mm(hx, w_xbc, NN, "in_xbc")
        z_mid, w_out_all = _mm(hx, w_mid, NN, "in_mid", side=_Exchange4(w_out_b[l], False))
        w_o = w_out_all.reshape(2 * d, d)
        z_dt = _mm(hx, w_dt, NN, "in_dt")
        xbc = _conv_fwd(z_xbc, conv_w8[l], conv_b[l][None], ctx_len)
        y2, hs, *w_in_next = _ssd_fwd(xbc, z_dt, dtb[l], alog[l], tri, ind_t, d, ctx_len, side=_Exchange4(w_in_b[l + 1], False) if more else None)
        ycat = _mix_fwd(z_mid, y2, xbc, dsk_row[l], g_ssd[l][None], g_v[l][None], g_mlp[l][None], w_s[l], b_st[l])
        o = _mm(ycat, w_o, NN, "out_proj")
        saved.append((stream, hx, z_xbc, z_mid, z_dt, xbc, y2, hs, ycat, o, w_xbc, w_mid, w_dt, w_o))
        stream = _post_fwd(o, stream, g_post[l][None], mods[l])
        if more:
            w_in_all = w_in_next[0]

    sq, d_stream = _loss_grad(stream, loss_target[0])
    loss = lax.psum(0.5 / d * sq[0, 0], ("x", "y", "c"))

    small = []
    dmods = []
    q_in = in_w // 4
    got_in, got_out = [None] * depth, [None] * depth
    parts_in = None
    for l in reversed(range(depth)):
        x_in, hx, z_xbc, z_mid, z_dt, xbc, y2, hs, ycat, o, w_xbc, w_mid, w_dt, w_o = saved[l]
        d_o, acc_post = _post_bwd(d_stream, o, g_post[l][None], mods[l])
        d_ycat = _mm(d_o, w_o, NT, "d_ycat")
        g_out = _mm(ycat, d_o, TN, "dw_out", out_dtype=BF16)
        dz_mid, d_y, vec, d_ws, d_bs = _mix_bwd(z_mid, y2, xbc, d_ycat, dsk_row[l], g_ssd[l][None], g_v[l][None], g_mlp[l][None],
                                                w_s[l], w_st[l], b_st[l], ind_head, ind_group)
        d_xbc2, dz_dt, d_bias, d_alog, *got = _ssd_bwd(xbc, z_dt, dtb[l], alog[l], tri, tri_t, ind_t, ind_b, d_y, y2, hs, d, ctx_len,
                                                       side=_Exchange4(parts_in, True) if parts_in is not None else None)
        if parts_in is not None:
            got_in[l + 1] = got[0]
        dz_xbc, d_cw, d_cb = _conv_bwd(z_xbc, d_xbc2, d_y, dsk_row[l], conv_w8[l], conv_b[l][None], ctx_len)
        g_xbc, got_out[l] = _mm(hx, dz_xbc, TN, "dw_xbc", out_dtype=BF16, side=_Exchange4(g_out.reshape(4, 2 * d // 4, d), True))
        g_mid = _mm(hx, dz_mid, TN, "dw_mid", out_dtype=BF16)
        g_dt = _mm(hx, dz_dt, TN, "dw_dt", out_dtype=BF16)
        g_in = jnp.concatenate([g_xbc, g_dt[:, :heads], g_dt[:, LANES:LANES + heads], g_mid], axis=1)
        parts_in = jnp.moveaxis(g_in.reshape(d, 4, q_in), 1, 0)
        d_hx = _mm(dz_xbc, w_xbc, NT, "dhx_xbc")
        if l == 0:
            d_hx, got_in[0] = _mm(dz_mid, w_mid, NT, "dhx_mid", acc=d_hx, side=_Exchange4(parts_in, True))
        else:
            d_hx = _mm(dz_mid, w_mid, NT, "dhx_mid", acc=d_hx)
        d_hx = _mm(dz_dt, w_dt, NT, "dhx_dt", acc=d_hx)
        d_stream, acc_pre = _pre_bwd(x_in, d_hx, d_stream, g_pre[l][None], mods[l])
        dmods.append(jnp.concatenate([acc_pre[:, 0], acc_pre[:, 1], acc_post[:, 0]], axis=1))
        small.append(dict(
            g_pre=acc_pre[0, 2] + acc_pre[1, 2], g_post=acc_post[0, 1] + acc_post[1, 1], conv_w=d_cw[:CONV_TAPS], conv_b=d_cb[0],
            dt_bias=d_bias[:, 0, :heads], a_log=d_alog[:, 0, :heads], d_skip=vec[3, :heads], g_ssd=vec[0], g_v=vec[1],
            w_s=d_ws, b_s=d_bs[:, :groups_mlp].T, g_mlp=vec[2]))
    small.reverse(), dmods.reverse()
    grad_x = d_stream[ctx_len:][None]

    names = ["g_pre", "g_post", "conv_w", "conv_b", "dt_bias", "a_log", "d_skip", "g_ssd", "g_v", "w_s", "b_s", "g_mlp"]
    stacked = {n: jnp.stack([small[l][n] for l in range(depth)]) for n in names}
    flat = jnp.concatenate([stacked[n].reshape(-1) for n in names])
    n_flat = flat.shape[0]
    n_pad = -(-n_flat // (PACK_ROWS * LANES)) * (PACK_ROWS * LANES)
    packed = jnp.pad(flat, (0, n_pad - n_flat)).reshape(n_pad // LANES, LANES)
    summed = _sum_lead(_all_gather8(packed, "ag_small"), "sum_small").reshape(-1)
    grads, off = {}, 0
    for n in names:
        size = stacked[n].size
        grads[n] = summed[off:off + size].reshape(stacked[n].shape)
        off += size

    dm_all = _all_gather8(jnp.stack(dmods).reshape(depth * 2, 3 * d), "ag_dmods").reshape(8, depth, 2, 3 * d)
    dm_ctx = _sum_lead(dm_all[:, :, 0, :], "sum_dm_ctx")
    dm16 = jnp.concatenate([jnp.moveaxis(dm_all[:, :, 1, :], 0, 1), dm_ctx[:, None, :], jnp.zeros((depth, 7, 3 * d), F32)], axis=1)
    grads["b_ada"] = _rowsum(dm16)[:, 0, :]
    dm_loc = jnp.pad(lax.dynamic_slice_in_dim(dm16, chip * n_ada, n_ada, axis=2), ((0, 0), (0, LANES - 16), (0, 0)))
    c_t = jnp.pad(c16.T, ((0, 0), (0, LANES - 16)))
    g_w_ada, d_scc_part = _ada_bwd(c_t, dm_loc, w_ada)
    d_scc = _sum_lead(_all_gather8(d_scc_part, "ag_dscc").reshape(4, 2, 8, d)[:, 0], "sum_dscc")
    grads["c_ctx"] = _cctx_grad(d_scc[0:1], c_ctx[None])[0]
    grads["conv_w"] = lax.dynamic_slice_in_dim(grads["conv_w"], chip * (2 * d // 4), 2 * d // 4, axis=2)

    def reduce_big(got, name):
        part = jnp.concatenate([_sum_lead(g, "sum_" + name) for g in got], axis=0)
        return part, _sibling_swap(part, "swap_" + name)

    g_in_a, g_in_b = reduce_big(got_in, "w_in")
    g_out_a, g_out_b = reduce_big(got_out, "w_out")

    weights = dict(c_ctx=c_ctx, w_ada=w_ada, b_ada=b_ada, g_pre=g_pre, g_post=g_post, w_in=w_in, conv_w=conv_w, conv_b=conv_b,
                   dt_bias=dt_bias, a_log=a_log, d_skip=d_skip, g_ssd=g_ssd, g_v=g_v, w_s=w_s, b_s=b_s, g_mlp=g_mlp, w_out=w_out)
    m_in = dict(c_ctx=m_c_ctx, w_ada=m_w_ada, b_ada=m_b_ada, g_pre=m_g_pre, g_post=m_g_post, w_in=m_w_in, conv_w=m_conv_w,
                conv_b=m_conv_b, dt_bias=m_dt_bias, a_log=m_a_log, d_skip=m_d_skip, g_ssd=m_g_ssd, g_v=m_g_v, w_s=m_w_s,
                b_s=m_b_s, g_mlp=m_g_mlp, w_out=m_w_out)
    v_in = dict(c_ctx=v_c_ctx, w_ada=v_w_ada, b_ada=v_b_ada, g_pre=v_g_pre, g_post=v_g_post, w_in=v_w_in, conv_w=v_conv_w,
                conv_b=v_conv_b, dt_bias=v_dt_bias, a_log=v_a_log, d_skip=v_d_skip, g_ssd=v_g_ssd, g_v=v_g_v, w_s=v_w_s,
                b_s=v_b_s, g_mlp=v_g_mlp, w_out=v_w_out)
    order = list(weights)
    results = {}
    big = {"w_in": (g_in_a, g_in_b), "w_out": (g_out_a, g_out_b), "w_ada": (g_w_ada.reshape(depth * d, n_ada), None)}
    for n, (ga, gb) in big.items():
        shp = weights[n].shape
        two = lambda a: a.reshape(-1, shp[-1])
        results[n] = [r.reshape(shp) for r in _adamw(two(weights[n]), ga, two(m_in[n]), two(v_in[n]), "adamw_" + n, g2=gb)]
    rest = [n for n in order if n not in big]

    def pack(tree):
        f = jnp.concatenate([tree[n].reshape(-1) for n in rest])
        padn = -(-f.shape[0] // (PACK_ROWS * LANES)) * (PACK_ROWS * LANES)
        return jnp.pad(f, (0, padn - f.shape[0])).reshape(padn // LANES, LANES)

    outs = _adamw(pack(weights), pack(grads), pack(m_in), pack(v_in), "adamw_small")
    off = 0
    for n in rest:
        size, shp = weights[n].size, weights[n].shape
        results[n] = [o_.reshape(-1)[off:off + size].reshape(shp) for o_ in outs]
        off += size

    return (loss, grad_x, *[results[n][0] for n in order], *[results[n][1] for n in order],
            *[results[n][2] for n in order], *[results[n][3] for n in order])


def _pad_rows8(a):
    return jnp.pad(a, [(0, 0)] * (a.ndim - 2) + [(0, 8 - a.shape[-2]), (0, 0)])
```

```python
import jax
import jax.numpy as jnp
from jax import lax
from jax.experimental import pallas as pl
from jax.experimental.pallas import tpu as pltpu

F32 = jnp.float32
BF16 = jnp.bfloat16
EPS = 1e-6
Q = 128
TB = 256
ROW = 64
HP = 64
LANES = 128
CONV_TAPS = 5
VMEM_LIMIT = 48 * 1024 * 1024
HI = lax.Precision.HIGHEST
SUM_BLOCK_BYTES = 4 * 1024 * 1024
ADAM_BLOCK_BYTES = 1024 * 1024
PACK_ROWS = 1024
MESH = pl.DeviceIdType.MESH
ANY = pl.BlockSpec(memory_space=pl.ANY)

ADAM_LR, ADAM_B1, ADAM_B2, ADAM_EPS, ADAM_WD, ADAM_STEP = 0.001, 0.9, 0.999, 1e-08, 0.01, 10

NN = (((1,), (0,)), ((), ()))
NT = (((1,), (1,)), ((), ()))
TN = (((0,), (0,)), ((), ()))


def _dot(a, b, dims=NN, prec=None):
    return lax.dot_general(a, b, dims, precision=prec, preferred_element_type=F32)


def _params(*sem):
    if sem:
        return pltpu.CompilerParams(vmem_limit_bytes=VMEM_LIMIT, dimension_semantics=sem)
    return pltpu.CompilerParams(vmem_limit_bytes=VMEM_LIMIT)


def _tile(dim, cands):
    for t in cands:
        if dim % t == 0:
            return t
    return dim


def _sigmoid(x):
    return 1.0 / (1.0 + jnp.exp(-x))


def _softplus(x):
    e = jnp.exp(-jnp.abs(x))
    u = 1.0 + e
    um1 = u - 1.0
    l1p = jnp.where(um1 == 0.0, e, jnp.log(u) * (e / jnp.where(um1 == 0.0, 1.0, um1)))
    return jnp.maximum(x, 0.0) + l1p


def _rms(x):
    return lax.rsqrt(jnp.mean(x * x, axis=-1, keepdims=True) + EPS)


def _rms_bwd(x, r, t):
    return r * t - x * (r * r * r) * jnp.mean(x * t, axis=-1, keepdims=True)


def _mm(a, b, dims, name, acc=None, out_dtype=F32, side=None):
    (ca,), (cb,) = dims[0]
    m, k = a.shape[1 - ca], a.shape[ca]
    n = b.shape[1 - cb]
    tm = _tile(m, (1024, 768, 512, 384, 256, 128))
    tn = _tile(n, (1024, 512, 256, 128))
    tk = k if k <= 2048 else _tile(k, (2048, 768, 512, 384, 256, 128))
    nk = k // tk
    a_spec = pl.BlockSpec((tm, tk), lambda i, j, kk: (i, kk)) if ca == 1 else pl.BlockSpec((tk, tm), lambda i, j, kk: (kk, i))
    b_spec = pl.BlockSpec((tk, tn), lambda i, j, kk: (kk, j)) if cb == 0 else pl.BlockSpec((tn, tk), lambda i, j, kk: (j, kk))
    o_spec = pl.BlockSpec((tm, tn), lambda i, j, kk: (i, j))
    has_acc = acc is not None

    def body(*refs):
        if has_acc:
            a_ref, b_ref, c_ref, o_ref, acc_ref = refs
        else:
            a_ref, b_ref, o_ref, acc_ref = refs
        kk = pl.program_id(2)

        @pl.when(kk == 0)
        def _():
            acc_ref[...] = c_ref[...] if has_acc else jnp.zeros_like(acc_ref)

        acc_ref[...] += _dot(a_ref[...].astype(BF16), b_ref[...].astype(BF16), dims)

        @pl.when(kk == nk - 1)
        def _():
            o_ref[...] = acc_ref[...].astype(out_dtype)

    res = _hosted_call(
        body, side, name=name, grid=(m // tm, n // tn, nk),
        in_specs=[a_spec, b_spec] + ([o_spec] if has_acc else []),
        out_specs=[o_spec], out_shape=[jax.ShapeDtypeStruct((m, n), out_dtype)],
        scratch_shapes=[pltpu.VMEM((tm, tn), F32)], args=(a, b, acc) if has_acc else (a, b))
    return res[0] if side is None else res


def _which(i):
    return jnp.minimum(i, 1)


def _pre_fwd(x, g_pre, mods):
    t, d = x.shape

    def body(x_ref, g_ref, m_ref, o_ref, ot_ref):
        xb = x_ref[...]
        xn = xb * _rms(xb) * g_ref[...]
        hx = xn * (1.0 + m_ref[0, 1:2, :]) + m_ref[0, 0:1, :]
        o_ref[...] = hx.astype(BF16)
        ot_ref[...] = hx.T.astype(BF16)

    return pl.pallas_call(
        body, name="pre_fwd", grid=(t // TB,),
        in_specs=[pl.BlockSpec((TB, d), lambda i: (i, 0)), pl.BlockSpec((1, d), lambda i: (0, 0)),
                  pl.BlockSpec((1, 8, d), lambda i: (_which(i), 0, 0))],
        out_specs=[pl.BlockSpec((TB, d), lambda i: (i, 0)), pl.BlockSpec((d, TB), lambda i: (0, i))],
        out_shape=[jax.ShapeDtypeStruct((t, d), BF16), jax.ShapeDtypeStruct((d, t), BF16)], compiler_params=_params("arbitrary"),
    )(x, g_pre, mods)


def _pre_bwd(x, d_hx, d_up, g_pre, mods):
    t, d = x.shape

    def body(x_ref, dh_ref, du_ref, g_ref, m_ref, dx_ref, acc_ref):
        i = pl.program_id(0)

        @pl.when(i <= 1)
        def _():
            acc_ref[...] = jnp.zeros_like(acc_ref)

        xb = x_ref[...]
        dh = dh_ref[...]
        r = _rms(xb)
        xr = xb * r
        d_xn = dh * (1.0 + m_ref[0, 1:2, :])
        dx_ref[...] = du_ref[...] + _rms_bwd(xb, r, d_xn * g_ref[...])
        acc_ref[0, 0:1, :] += jnp.sum(dh, axis=0, keepdims=True)
        acc_ref[0, 1:2, :] += jnp.sum(dh * (xr * g_ref[...]), axis=0, keepdims=True)
        acc_ref[0, 2:3, :] += jnp.sum(d_xn * xr, axis=0, keepdims=True)

    blk = pl.BlockSpec((TB, d), lambda i: (i, 0))
    return pl.pallas_call(
        body, name="pre_bwd", grid=(t // TB,),
        in_specs=[blk, blk, blk, pl.BlockSpec((1, d), lambda i: (0, 0)),
                  pl.BlockSpec((1, 8, d), lambda i: (_which(i), 0, 0))],
        out_specs=[blk, pl.BlockSpec((1, 8, d), lambda i: (_which(i), 0, 0))],
        out_shape=[jax.ShapeDtypeStruct((t, d), F32), jax.ShapeDtypeStruct((2, 8, d), F32)],
        compiler_params=_params("arbitrary"),
    )(x, d_hx, d_up, g_pre, mods)


def _post_fwd(o, x, g_post, mods):
    t, d = x.shape

    def body(o_ref, x_ref, g_ref, m_ref, y_ref):
        ob = o_ref[...]
        y_ref[...] = x_ref[...] + m_ref[0, 2:3, :] * (ob * _rms(ob) * g_ref[...])

    blk = pl.BlockSpec((TB, d), lambda i: (i, 0))
    return pl.pallas_call(
        body, name="post_fwd", grid=(t // TB,),
        in_specs=[blk, blk, pl.BlockSpec((1, d), lambda i: (0, 0)), pl.BlockSpec((1, 8, d), lambda i: (_which(i), 0, 0))],
        out_specs=blk, out_shape=jax.ShapeDtypeStruct((t, d), F32), compiler_params=_params("arbitrary"),
    )(o, x, g_post, mods)


def _post_bwd(d_y, o, g_post, mods):
    t, d = o.shape

    def body(dy_ref, o_ref, g_ref, m_ref, do_ref, acc_ref):
        i = pl.program_id(0)

        @pl.when(i <= 1)
        def _():
            acc_ref[...] = jnp.zeros_like(acc_ref)

        ob = o_ref[...]
        dy = dy_ref[...]
        r = _rms(ob)
        orr = ob * r
        d_out = dy * m_ref[0, 2:3, :]
        do_ref[...] = _rms_bwd(ob, r, d_out * g_ref[...]).astype(BF16)
        acc_ref[0, 0:1, :] += jnp.sum(dy * (orr * g_ref[...]), axis=0, keepdims=True)
        acc_ref[0, 1:2, :] += jnp.sum(d_out * orr, axis=0, keepdims=True)

    blk = pl.BlockSpec((TB, d), lambda i: (i, 0))
    return pl.pallas_call(
        body, name="post_bwd", grid=(t // TB,),
        in_specs=[blk, blk, pl.BlockSpec((1, d), lambda i: (0, 0)), pl.BlockSpec((1, 8, d), lambda i: (_which(i), 0, 0))],
        out_specs=[blk, pl.BlockSpec((1, 8, d), lambda i: (_which(i), 0, 0))],
        out_shape=[jax.ShapeDtypeStruct((t, d), BF16), jax.ShapeDtypeStruct((2, 8, d), F32)],
        compiler_params=_params("arbitrary"),
    )(d_y, o, g_post, mods)


def _loss_grad(xf, target):
    t, d = xf.shape

    def body(x_ref, t_ref, loss_ref, dx_ref):
        i = pl.program_id(0)

        @pl.when(i == 0)
        def _():
            loss_ref[...] = jnp.zeros_like(loss_ref)
            dx_ref[...] = jnp.zeros_like(dx_ref)

        @pl.when(i > 0)
        def _():
            err = x_ref[...] - t_ref[...]
            loss_ref[...] += jnp.sum(err * err).reshape(1, 1)
            dx_ref[...] = err * (1.0 / d)

    return pl.pallas_call(
        body, name="loss_grad", grid=(t // TB,),
        in_specs=[pl.BlockSpec((TB, d), lambda i: (i, 0)), pl.BlockSpec((TB, d), lambda i: (jnp.maximum(i - 1, 0), 0))],
        out_specs=[pl.BlockSpec((1, 1), lambda i: (0, 0)), pl.BlockSpec((TB, d), lambda i: (i, 0))],
        out_shape=[jax.ShapeDtypeStruct((1, 1), F32), jax.ShapeDtypeStruct((t, d), F32)],
        compiler_params=_params("arbitrary"),
    )(xf, target)


def _conv_terms(zb, pos, row_len):
    out = []
    for k in range(CONV_TAPS):
        o = k - CONV_TAPS // 2
        sh = zb if o == 0 else pltpu.roll(zb, (-o) % TB, 0)
        out.append(jnp.where((pos + o >= 0) & (pos + o < row_len), sh, 0.0))
    return out


def _row_pos(i, ctx_len):
    row_len = jnp.where(i == 0, ctx_len, ROW)
    pos = lax.broadcasted_iota(jnp.int32, (TB, 1), 0) & (row_len - 1)
    return pos, row_len


def _conv_fwd(z_xbc, conv_w8, conv_b, ctx_len):
    t, c = z_xbc.shape
    tc = _tile(c, (1024, 512, 256, 128))

    def body(z_ref, w_ref, b_ref, o_ref):
        pos, row_len = _row_pos(pl.program_id(1), ctx_len)
        terms = _conv_terms(z_ref[...], pos, row_len)
        pre = b_ref[...]
        for k in range(CONV_TAPS):
            pre = pre + terms[k] * w_ref[k:k + 1, :]
        o_ref[...] = pre * _sigmoid(pre)

    return pl.pallas_call(
        body, name="conv_fwd", grid=(c // tc, t // TB),
        in_specs=[pl.BlockSpec((TB, tc), lambda j, i: (i, j)), pl.BlockSpec((8, tc), lambda j, i: (0, j)),
                  pl.BlockSpec((1, tc), lambda j, i: (0, j))],
        out_specs=pl.BlockSpec((TB, tc), lambda j, i: (i, j)),
        out_shape=jax.ShapeDtypeStruct((t, c), F32), compiler_params=_params("arbitrary", "arbitrary"),
    )(z_xbc, conv_w8, conv_b)


def _conv_bwd(z_xbc, d_xbc2, d_y, d_skip_row, conv_w8, conv_b, ctx_len):
    t, c = z_xbc.shape
    d = d_y.shape[1]
    tc = _tile(d, (1024, 512, 256, 128))
    nskip = d // tc

    def body(z_ref, g2_ref, dy_ref, ds_ref, w_ref, b_ref, dz_ref, dw_ref, db_ref):
        j, i = pl.program_id(0), pl.program_id(1)

        @pl.when(i == 0)
        def _():
            dw_ref[...] = jnp.zeros_like(dw_ref)
            db_ref[...] = jnp.zeros_like(db_ref)

        pos, row_len = _row_pos(i, ctx_len)
        terms = _conv_terms(z_ref[...], pos, row_len)
        pre = b_ref[...]
        for k in range(CONV_TAPS):
            pre = pre + terms[k] * w_ref[k:k + 1, :]
        sig = _sigmoid(pre)
        skip = jnp.where(j < nskip, 1.0, 0.0) * ds_ref[...]
        g = g2_ref[0] + g2_ref[1] + dy_ref[...] * skip
        d_pre = g * (sig * (1.0 + pre * (1.0 - sig)))
        db_ref[...] += jnp.sum(d_pre, axis=0, keepdims=True)
        dz = jnp.zeros_like(d_pre)
        for k in range(CONV_TAPS):
            o = k - CONV_TAPS // 2
            dw_ref[k:k + 1, :] += jnp.sum(d_pre * terms[k], axis=0, keepdims=True)
            sh = d_pre if o == 0 else pltpu.roll(d_pre, o % TB, 0)
            dz = dz + jnp.where((pos - o >= 0) & (pos - o < row_len), sh, 0.0) * w_ref[k:k + 1, :]
        dz_ref[...] = dz.astype(BF16)

    jd = lambda j: jnp.minimum(j, nskip - 1)
    return pl.pallas_call(
        body, name="conv_bwd", grid=(c // tc, t // TB),
        in_specs=[pl.BlockSpec((TB, tc), lambda j, i: (i, j)), pl.BlockSpec((2, TB, tc), lambda j, i: (0, i, j)),
                  pl.BlockSpec((TB, tc), lambda j, i: (i, jd(j))), pl.BlockSpec((1, tc), lambda j, i: (0, jd(j))),
                  pl.BlockSpec((8, tc), lambda j, i: (0, j)), pl.BlockSpec((1, tc), lambda j, i: (0, j))],
        out_specs=[pl.BlockSpec((TB, tc), lambda j, i: (i, j)), pl.BlockSpec((8, tc), lambda j, i: (0, j)),
                   pl.BlockSpec((1, tc), lambda j, i: (0, j))],
        out_shape=[jax.ShapeDtypeStruct((t, c), BF16), jax.ShapeDtypeStruct((8, c), F32), jax.ShapeDtypeStruct((1, c), F32)],
        compiler_params=_params("arbitrary", "arbitrary"),
    )(z_xbc, d_xbc2, d_y, d_skip_row, conv_w8, conv_b)


def _scan_chunk(dirn, s, nch, ncc):
    bwd = jnp.where(s < ncc, ncc - 1 - s, nch - 1 - (s - ncc))
    return jnp.where(dirn == 0, s, bwd)


def _ssd_decays(dt_ref, dtb_ref, alog_ref, tri):
    raw = dt_ref[...] + dtb_ref[0]
    dt = _softplus(raw)
    a_neg = -jnp.exp(alog_ref[0])
    a = dt * a_neg
    s = _dot(tri, a, NN, HI)
    stot = jnp.sum(a, axis=0, keepdims=True)
    return raw, dt, a_neg, s, stot, s.T


def _split(v):
    hi = v.astype(BF16)
    return hi, (v - hi.astype(F32)).astype(BF16)


def _expand(v, indt_ref):
    hi, lo = _split(v)
    return _dot(hi, indt_ref[...]) + _dot(lo, indt_ref[...])


def _head_sums(v, ind_ref):
    hi, lo = _split(v)
    return _dot(hi, ind_ref[...]) + _dot(lo, ind_ref[...])


def _ssd_fwd(xbc, z_dt, dtb, alog, tri, ind_t, d, ctx_len, side=None):
    t = xbc.shape[0]
    nch, ncc = t // Q, ctx_len // Q
    heads = d // HP
    groups = heads // 4
    gn = groups * LANES

    def body(xbc_ref, dt_ref, dtb_ref, alog_ref, tri_ref, indt_ref, y_ref, hs_ref, h_scr, xdb_scr, xde_scr, esx_scr):
        @pl.when(pl.program_id(1) == 0)
        def _():
            h_scr[...] = jnp.zeros_like(h_scr)

        tri = tri_ref[0]
        mask = tri > 0.0
        _, dt, _, s, stot, s_t = _ssd_decays(dt_ref, dtb_ref, alog_ref, tri)
        esx_scr[...] = _expand(jnp.exp(s), indt_ref)
        etot_x = _expand(jnp.broadcast_to(jnp.exp(stot), (8, LANES)), indt_ref)[0:1]
        xd = xbc_ref[:, :d] * _expand(dt, indt_ref)
        xdb_scr[...] = xd.astype(BF16)
        xde_scr[...] = (xd * _expand(jnp.exp(stot - s), indt_ref)).astype(BF16)
        left = lax.broadcasted_iota(jnp.int32, (Q, LANES), 1) < HP
        hs_ref[0, 0] = h_scr[...]
        for g in range(groups):
            b32 = xbc_ref[:, d + g * LANES:d + (g + 1) * LANES]
            bb = b32.astype(BF16)
            bbt = b32.T.astype(BF16)
            cb = xbc_ref[:, d + gn + g * LANES:d + gn + (g + 1) * LANES].astype(BF16)
            cbt = _dot(cb, bb, NT)
            for pr in (2 * g, 2 * g + 1):
                h0 = 2 * pr
                cols = slice(pr * LANES, (pr + 1) * LANES)
                xdb = xdb_scr[:, cols]
                res = []
                for h in (h0, h0 + 1):
                    lm = jnp.exp(jnp.where(mask, s[:, h:h + 1] - s_t[h:h + 1, :], -jnp.inf))
                    res.append(_dot((cbt * lm).astype(BF16), xdb))
                hp = h_scr[:, cols]
                y_ref[0, :, cols] = jnp.where(left, res[0], res[1]) + _dot(cb, hp.astype(BF16)) * esx_scr[:, cols]
                h_scr[:, cols] = hp * etot_x[:, cols] + _dot(bbt, xde_scr[:, cols])

    cidx = lambda dd, ss: _scan_chunk(dd, ss, nch, ncc)
    return _hosted_call(
        body, side, name="ssd_fwd", grid=(2, nch),
        in_specs=[pl.BlockSpec((Q, 2 * d), lambda dd, ss: (cidx(dd, ss), 0)),
                  pl.BlockSpec((Q, LANES), lambda dd, ss: (cidx(dd, ss), dd)),
                  pl.BlockSpec((1, 1, LANES), lambda dd, ss: (dd, 0, 0)),
                  pl.BlockSpec((1, 1, LANES), lambda dd, ss: (dd, 0, 0)),
                  pl.BlockSpec((1, Q, Q), lambda dd, ss: (dd, 0, 0)),
                  pl.BlockSpec((LANES, d), lambda dd, ss: (0, 0))],
        out_specs=[pl.BlockSpec((1, Q, d), lambda dd, ss: (dd, cidx(dd, ss), 0)),
                   pl.BlockSpec((1, 1, LANES, d), lambda dd, ss: (dd, cidx(dd, ss), 0, 0))],
        out_shape=[jax.ShapeDtypeStruct((2, t, d), F32), jax.ShapeDtypeStruct((2, nch, LANES, d), F32)],
        scratch_shapes=[pltpu.VMEM((LANES, d), F32), pltpu.VMEM((Q, d), BF16), pltpu.VMEM((Q, d), BF16), pltpu.VMEM((Q, d), F32)],
        args=(xbc, z_dt, dtb, alog, tri, ind_t))


def _ssd_bwd(xbc, z_dt, dtb, alog, tri, tri_t, ind_t, ind, d_y, y2, hs, d, ctx_len, side=None):
    t = xbc.shape[0]
    nch, ncc = t // Q, ctx_len // Q
    heads = d // HP
    groups = heads // 4
    gn = groups * LANES

    def body(xbc_ref, dt_ref, dtb_ref, alog_ref, tri_ref, trit_ref, indt_ref, ind_ref, dy_ref, y_ref, hs_ref,
             dx_ref, dzdt_ref, dbias_ref, dalog_ref, dh_scr, dtx_scr, ex_scr, xdb_scr, xde_scr, dyb_scr, dye_scr, dxd_scr, bdh_scr):
        @pl.when(pl.program_id(1) == 0)
        def _():
            dh_scr[...] = jnp.zeros_like(dh_scr)
            dbias_ref[...] = jnp.zeros_like(dbias_ref)
            dalog_ref[...] = jnp.zeros_like(dalog_ref)

        tri = tri_ref[0]
        mask = tri > 0.0
        mask_t = trit_ref[0] > 0.0
        raw, dt, a_neg, s, stot, s_t = _ssd_decays(dt_ref, dtb_ref, alog_ref, tri)
        etot = jnp.exp(stot)
        etot_x = _expand(jnp.broadcast_to(etot, (8, LANES)), indt_ref)[0:1]
        dtx_scr[...] = _expand(dt, indt_ref)
        ex_scr[...] = _expand(jnp.exp(stot - s), indt_ref)
        xd = xbc_ref[:, :d] * dtx_scr[...]
        xdb_scr[...] = xd.astype(BF16)
        xde_scr[...] = (xd * ex_scr[...]).astype(BF16)
        dyb_scr[...] = dy_ref[...].astype(BF16)
        dye_scr[...] = (dy_ref[...] * _expand(jnp.exp(s), indt_ref)).astype(BF16)
        hd_cols = jnp.sum(dh_scr[...] * hs_ref[0, 0], axis=0, keepdims=True)
        left = lax.broadcasted_iota(jnp.int32, (Q, LANES), 1) < HP
        for g in range(groups):
            b32 = xbc_ref[:, d + g * LANES:d + (g + 1) * LANES]
            c32 = xbc_ref[:, d + gn + g * LANES:d + gn + (g + 1) * LANES]
            bb, cb = b32.astype(BF16), c32.astype(BF16)
            c_t = c32.T.astype(BF16)
            cbt = _dot(cb, bb, NT)
            cbt_t = _dot(bb, cb, NT)
            d_cbt = jnp.zeros((Q, Q), F32)
            d_b = jnp.zeros((Q, LANES), F32)
            d_c = jnp.zeros((Q, LANES), F32)
            for pr in (2 * g, 2 * g + 1):
                h0 = 2 * pr
                cols = slice(pr * LANES, (pr + 1) * LANES)
                xdb = xdb_scr[:, cols]
                dyb = dyb_scr[:, cols]
                dyeb = dye_scr[:, cols]
                hpb = hs_ref[0, 0, :, cols].astype(BF16)
                dhp = dh_scr[:, cols]
                dhb = dhp.astype(BF16)
                parts = []
                for hh, h in enumerate((h0, h0 + 1)):
                    mine = left if hh == 0 else jnp.logical_not(left)
                    diff = s[:, h:h + 1] - s_t[h:h + 1, :]
                    lm = jnp.exp(jnp.where(mask, diff, -jnp.inf))
                    lm_t = jnp.exp(jnp.where(mask_t, -diff, -jnp.inf))
                    gm = _dot(jnp.where(mine, dyb, jnp.zeros_like(dyb)), xdb, NT)
                    d_cbt = d_cbt + gm * lm
                    parts.append(_dot((cbt_t * lm_t).astype(BF16), dyb))
                dxd_scr[:, cols] = jnp.where(left, parts[0], parts[1])
                bdh_scr[:, cols] = _dot(bb, dhb)
                d_c = d_c + _dot(dyeb, hpb, NT)
                d_b = d_b + _dot(xde_scr[:, cols], dhb, NT)
                dh_scr[:, cols] = dhp * etot_x[:, cols] + _dot(c_t, dyeb)
            dx_ref[0, :, d + g * LANES:d + (g + 1) * LANES] = d_b + _dot(d_cbt.T.astype(BF16), cb)
            dx_ref[0, :, d + gn + g * LANES:d + gn + (g + 1) * LANES] = d_c + _dot(d_cbt.astype(BF16), bb)
        x = xbc_ref[:, :d]
        ebdh = ex_scr[...] * bdh_scr[...]
        d_xd = dxd_scr[...] + ebdh
        dx_ref[0, :, :d] = d_xd * dtx_scr[...]
        r_dy = _head_sums(dyb_scr[...].astype(F32) * y_ref[0], ind_ref)
        r_diag = _head_sums(xdb_scr[...].astype(F32) * dxd_scr[...], ind_ref)
        r_e = _head_sums(x * dtx_scr[...] * ebdh, ind_ref)
        r_dx = _head_sums(d_xd * x, ind_ref)
        hd = _head_sums(jnp.broadcast_to(hd_cols, (8, d)), ind_ref)[0:1]
        d_s = r_dy - r_diag - r_e
        d_stot = jnp.sum(r_e, axis=0, keepdims=True) + etot * hd
        d_a = _dot(trit_ref[0], d_s, NN, HI) + d_stot
        valid = lax.broadcasted_iota(jnp.int32, (Q, LANES), 1) < heads
        d_dt_tot = jnp.where(valid, d_a * a_neg + r_dx, 0.0)
        d_raw = d_dt_tot * _sigmoid(raw)
        dzdt_ref[...] = d_raw.astype(BF16)
        dbias_ref[0] += jnp.sum(d_raw, axis=0, keepdims=True)
        dalog_ref[0] += jnp.sum(jnp.where(valid, d_a * dt, 0.0), axis=0, keepdims=True) * a_neg

    cidx = lambda dd, ss: _scan_chunk(dd, nch - 1 - ss, nch, ncc)
    full = lambda shape: pltpu.VMEM(shape, F32)
    half = lambda shape: pltpu.VMEM(shape, BF16)
    return _hosted_call(
        body, side, name="ssd_bwd", grid=(2, nch),
        in_specs=[pl.BlockSpec((Q, 2 * d), lambda dd, ss: (cidx(dd, ss), 0)),
                  pl.BlockSpec((Q, LANES), lambda dd, ss: (cidx(dd, ss), dd)),
                  pl.BlockSpec((1, 1, LANES), lambda dd, ss: (dd, 0, 0)),
                  pl.BlockSpec((1, 1, LANES), lambda dd, ss: (dd, 0, 0)),
                  pl.BlockSpec((1, Q, Q), lambda dd, ss: (dd, 0, 0)),
                  pl.BlockSpec((1, Q, Q), lambda dd, ss: (dd, 0, 0)),
                  pl.BlockSpec((LANES, d), lambda dd, ss: (0, 0)),
                  pl.BlockSpec((d, LANES), lambda dd, ss: (0, 0)),
                  pl.BlockSpec((Q, d), lambda dd, ss: (cidx(dd, ss), 0)),
                  pl.BlockSpec((1, Q, d), lambda dd, ss: (dd, cidx(dd, ss), 0)),
                  pl.BlockSpec((1, 1, LANES, d), lambda dd, ss: (dd, cidx(dd, ss), 0, 0))],
        out_specs=[pl.BlockSpec((1, Q, 2 * d), lambda dd, ss: (dd, cidx(dd, ss), 0)),
                   pl.BlockSpec((Q, LANES), lambda dd, ss: (cidx(dd, ss), dd)),
                   pl.BlockSpec((1, 1, LANES), lambda dd, ss: (dd, 0, 0)),
                   pl.BlockSpec((1, 1, LANES), lambda dd, ss: (dd, 0, 0))],
        out_shape=[jax.ShapeDtypeStruct((2, t, 2 * d), F32), jax.ShapeDtypeStruct((t, 2 * LANES), BF16),
                   jax.ShapeDtypeStruct((2, 1, LANES), F32), jax.ShapeDtypeStruct((2, 1, LANES), F32)],
        scratch_shapes=[full((LANES, d)), full((Q, d)), full((Q, d)), half((Q, d)), half((Q, d)), half((Q, d)), half((Q, d)),
                        full((Q, d)), full((Q, d))],
        args=(xbc, z_dt, dtb, alog, tri, tri_t, ind_t, ind, d_y, y2, hs))


def _mix_common(zm_ref, y2_ref, xh_ref, dsk_ref, gv_ref, ws_ref, bst_ref, d):
    groups = d // LANES
    z_ssd, u, v, z_mlp = (zm_ref[:, k * d:(k + 1) * d] for k in range(4))
    y = y2_ref[0] + y2_ref[1] + dsk_ref[...] * xh_ref[...]
    sig_a = _sigmoid(z_ssd)
    ya_pre = y * (z_ssd * sig_a)
    r_v = _rms(v)
    vn = (v * r_v * gv_ref[...]).astype(BF16)
    sg = jnp.concatenate(
        [_dot(ws_ref[g].astype(BF16), vn[:, g * LANES:(g + 1) * LANES]) + bst_ref[:, g:g + 1] for g in range(groups)], axis=1)
    sig_m = _sigmoid(z_mlp)
    yb_pre = u * sg * (z_mlp * sig_m)
    return z_ssd, u, v, z_mlp, y, sig_a, ya_pre, r_v, vn, sg, sig_m, yb_pre


def _mix_fwd(z_mid, y2, xbc, dsk_row, g_ssd, g_v, g_mlp, w_s, b_st):
    t = z_mid.shape[0]
    d = z_mid.shape[1] // 4
    groups = d // LANES

    def body(zm_ref, y2_ref, xh_ref, dsk_ref, ga_ref, gv_ref, gm_ref, ws_ref, bst_ref, o_ref, ot_ref):
        (_, _, _, _, _, _, ya_pre, _, _, _, _, yb_pre) = _mix_common(zm_ref, y2_ref, xh_ref, dsk_ref, gv_ref, ws_ref, bst_ref, d)
        y_a = ya_pre * _rms(ya_pre) * ga_ref[...]
        y_b = yb_pre * _rms(yb_pre) * gm_ref[...]
        o_ref[:, :d] = y_a.astype(BF16)
        o_ref[:, d:] = y_b.astype(BF16)
        ot_ref[:d, :] = y_a.T.astype(BF16)
        ot_ref[d:, :] = y_b.T.astype(BF16)

    row = pl.BlockSpec((1, d), lambda i: (0, 0))
    return pl.pallas_call(
        body, name="mix_fwd", grid=(t // Q,),
        in_specs=[pl.BlockSpec((Q, 4 * d), lambda i: (i, 0)), pl.BlockSpec((2, Q, d), lambda i: (0, i, 0)),
                  pl.BlockSpec((Q, d), lambda i: (i, 0)), row, row, row, row,
                  pl.BlockSpec((groups, Q, Q), lambda i: (0, 0, 0)), pl.BlockSpec((Q, LANES), lambda i: (0, 0))],
        out_specs=[pl.BlockSpec((Q, 2 * d), lambda i: (i, 0)), pl.BlockSpec((2 * d, Q), lambda i: (0, i))],
        out_shape=[jax.ShapeDtypeStruct((t, 2 * d), BF16), jax.ShapeDtypeStruct((2 * d, t), BF16)], compiler_params=_params("arbitrary"),
    )(z_mid, y2, xbc, dsk_row, g_ssd, g_v, g_mlp, w_s, b_st)


def _mix_bwd(z_mid, y2, xbc, d_ycat, dsk_row, g_ssd, g_v, g_mlp, w_s, w_st, b_st, ind_head, ind_group):
    t = z_mid.shape[0]
    d = z_mid.shape[1] // 4
    groups = d // LANES
    nsteps = t // Q

    def body(zm_ref, y2_ref, xh_ref, dyc_ref, dsk_ref, ga_ref, gv_ref, gm_ref, ws_ref, wst_ref, bst_ref, ih_ref, ig_ref,
             dzm_ref, dy_ref, vec_ref, dws_ref, dbs_ref, dsk_acc, dsg_acc):
        i = pl.program_id(0)

        @pl.when(i == 0)
        def _():
            vec_ref[...] = jnp.zeros_like(vec_ref)
            dws_ref[...] = jnp.zeros_like(dws_ref)
            dsk_acc[...] = jnp.zeros_like(dsk_acc)
            dsg_acc[...] = jnp.zeros_like(dsg_acc)

        (z_ssd, u, v, z_mlp, y, sig_a, ya_pre, r_v, vn, sg, sig_m, yb_pre) = _mix_common(
            zm_ref, y2_ref, xh_ref, dsk_ref, gv_ref, ws_ref, bst_ref, d)
        d_ya = dyc_ref[:, :d]
        r_a = _rms(ya_pre)
        vec_ref[0:1, :] += jnp.sum(d_ya * (ya_pre * r_a), axis=0, keepdims=True)
        d_ya_pre = _rms_bwd(ya_pre, r_a, d_ya * ga_ref[...])
        d_y = d_ya_pre * (z_ssd * sig_a)
        dy_ref[...] = d_y
        dsk_acc[...] += jnp.sum(d_y * xh_ref[...], axis=0, keepdims=True)
        dzm_ref[:, 0:d] = (d_ya_pre * y * (sig_a * (1.0 + z_ssd * (1.0 - sig_a)))).astype(BF16)
        d_yb = dyc_ref[:, d:]
        r_b = _rms(yb_pre)
        vec_ref[2:3, :] += jnp.sum(d_yb * (yb_pre * r_b), axis=0, keepdims=True)
        d_yb_pre = _rms_bwd(yb_pre, r_b, d_yb * gm_ref[...])
        silu_m = z_mlp * sig_m
        dzm_ref[:, d:2 * d] = (d_yb_pre * sg * silu_m).astype(BF16)
        dzm_ref[:, 3 * d:4 * d] = (d_yb_pre * u * sg * (sig_m * (1.0 + z_mlp * (1.0 - sig_m)))).astype(BF16)
        d_sg = d_yb_pre * u * silu_m
        dsg_acc[...] += d_sg
        d_sgb = d_sg.astype(BF16)
        d_vn = []
        for g in range(groups):
            cols = slice(g * LANES, (g + 1) * LANES)
            dws_ref[g] += _dot(d_sgb[:, cols], vn[:, cols], NT)
            d_vn.append(_dot(wst_ref[g].astype(BF16), d_sgb[:, cols]))
        d_vn = jnp.concatenate(d_vn, axis=1)
        vec_ref[1:2, :] += jnp.sum(d_vn * (v * r_v), axis=0, keepdims=True)
        dzm_ref[:, 2 * d:3 * d] = _rms_bwd(v, r_v, d_vn * gv_ref[...]).astype(BF16)

        @pl.when(i == nsteps - 1)
        def _():
            vec_ref[3:4, 0:LANES] = _dot(dsk_acc[...], ih_ref[...], NN, HI)
            dbs_ref[...] = _dot(dsg_acc[...], ig_ref[...], NN, HI)

    row = pl.BlockSpec((1, d), lambda i: (0, 0))
    wsp = pl.BlockSpec((groups, Q, Q), lambda i: (0, 0, 0))
    ind = pl.BlockSpec((d, LANES), lambda i: (0, 0))
    return pl.pallas_call(
        body, name="mix_bwd", grid=(nsteps,),
        in_specs=[pl.BlockSpec((Q, 4 * d), lambda i: (i, 0)), pl.BlockSpec((2, Q, d), lambda i: (0, i, 0)),
                  pl.BlockSpec((Q, d), lambda i: (i, 0)), pl.BlockSpec((Q, 2 * d), lambda i: (i, 0)),
                  row, row, row, row, wsp, wsp, pl.BlockSpec((Q, LANES), lambda i: (0, 0)), ind, ind],
        out_specs=[pl.BlockSpec((Q, 4 * d), lambda i: (i, 0)), pl.BlockSpec((Q, d), lambda i: (i, 0)),
                   pl.BlockSpec((8, d), lambda i: (0, 0)), wsp, pl.BlockSpec((Q, LANES), lambda i: (0, 0))],
        out_shape=[jax.ShapeDtypeStruct((t, 4 * d), BF16), jax.ShapeDtypeStruct((t, d), F32),
                   jax.ShapeDtypeStruct((8, d), F32), jax.ShapeDtypeStruct((groups, Q, Q), F32),
                   jax.ShapeDtypeStruct((Q, LANES), F32)],
        scratch_shapes=[pltpu.VMEM((1, d), F32), pltpu.VMEM((Q, d), F32)],
        compiler_params=_params("arbitrary"),
    )(z_mid, y2, xbc, d_ycat, dsk_row, g_ssd, g_v, g_mlp, w_s, w_st, b_st, ind_head, ind_group)


def _ada_fwd(c16, w_ada, b_loc):
    depth, d, n = w_ada.shape
    tn = _tile(n, (512, 256, 128))

    def body(c_ref, w_ref, b_ref, o_ref):
        cv = c_ref[...]
        o_ref[0] = _dot(cv * _sigmoid(cv), w_ref[0], NN, HI) + b_ref[0]

    return pl.pallas_call(
        body, name="ada_fwd", grid=(depth, n // tn),
        in_specs=[pl.BlockSpec((16, d), lambda l, j: (0, 0)), pl.BlockSpec((1, d, tn), lambda l, j: (l, 0, j)),
                  pl.BlockSpec((1, 1, tn), lambda l, j: (l, 0, j))],
        out_specs=pl.BlockSpec((1, 16, tn), lambda l, j: (l, 0, j)),
        out_shape=jax.ShapeDtypeStruct((depth, 16, n), F32), compiler_params=_params("arbitrary", "arbitrary"),
    )(c16, w_ada, b_loc)


def _ada_bwd(c_t, dm_loc, w_ada):
    depth, d, n = w_ada.shape
    tn = _tile(n, (512, 256, 128))

    def body(s_ref, dm_ref, w_ref, gw_ref, dsc_ref):
        @pl.when((pl.program_id(0) == 0) & (pl.program_id(1) == 0))
        def _():
            dsc_ref[...] = jnp.zeros_like(dsc_ref)

        cv = s_ref[...]
        gw_ref[0] = _dot(cv * _sigmoid(cv), dm_ref[0], NN, HI)
        dsc_ref[...] += _dot(dm_ref[0, 8:16, :], w_ref[0], NT, HI)

    return pl.pallas_call(
        body, name="ada_bwd", grid=(depth, n // tn),
        in_specs=[pl.BlockSpec((d, LANES), lambda l, j: (0, 0)), pl.BlockSpec((1, LANES, tn), lambda l, j: (l, 0, j)),
                  pl.BlockSpec((1, d, tn), lambda l, j: (l, 0, j))],
        out_specs=[pl.BlockSpec((1, d, tn), lambda l, j: (l, 0, j)), pl.BlockSpec((8, d), lambda l, j: (0, 0))],
        out_shape=[jax.ShapeDtypeStruct((depth, d, n), F32), jax.ShapeDtypeStruct((8, d), F32)],
        compiler_params=_params("arbitrary", "arbitrary"),
    )(c_t, dm_loc, w_ada)


def _rowsum(x):
    depth, r, n = x.shape

    def body(x_ref, o_ref):
        o_ref[0] = jnp.sum(x_ref[0], axis=0, keepdims=True)

    return pl.pallas_call(
        body, name="rowsum", grid=(depth,),
        in_specs=[pl.BlockSpec((1, r, n), lambda l: (l, 0, 0))], out_specs=pl.BlockSpec((1, 1, n), lambda l: (l, 0, 0)),
        out_shape=jax.ShapeDtypeStruct((depth, 1, n), F32), compiler_params=_params("arbitrary"),
    )(x)


def _cctx_grad(d_scc, c_ctx_row):
    def body(g_ref, c_ref, o_ref):
        cv = c_ref[...]
        sig = _sigmoid(cv)
        o_ref[...] = g_ref[...] * (sig * (1.0 + cv * (1.0 - sig)))

    return pl.pallas_call(body, name="cctx_grad", out_shape=jax.ShapeDtypeStruct(c_ctx_row.shape, F32))(d_scc, c_ctx_row)


def _sum_lead(x, name):
    k, r, c = x.shape
    tr = _tile(r, [tt for tt in (1024, 512, 256, 128, 64, 32, 16, 8) if k * tt * c * x.dtype.itemsize <= SUM_BLOCK_BYTES])

    def body(x_ref, o_ref):
        acc = x_ref[0].astype(F32)
        for e in range(1, k):
            acc = acc + x_ref[e].astype(F32)
        o_ref[...] = acc

    return pl.pallas_call(
        body, name=name, grid=(r // tr,),
        in_specs=[pl.BlockSpec((k, tr, c), lambda i: (0, i, 0))], out_specs=pl.BlockSpec((tr, c), lambda i: (i, 0)),
        out_shape=jax.ShapeDtypeStruct((r, c), F32), compiler_params=_params("arbitrary"),
    )(x)


def _adamw(w, g, m, v, name, g2=None):
    r, c = w.shape
    tr = _tile(r, [tt for tt in (2048, 1024, 512, 256, 128, 64, 32, 16, 8) if tt * c * 4 <= ADAM_BLOCK_BYTES])
    two = g2 is not None
    bc1 = 1.0 - ADAM_B1 ** ADAM_STEP
    bc2 = 1.0 - ADAM_B2 ** ADAM_STEP

    def body(*refs):
        if two:
            w_ref, g_ref, g2_ref, m_ref, v_ref, go_ref, d_ref, mo_ref, vo_ref = refs
            gr = g_ref[...] + g2_ref[...]
        else:
            w_ref, g_ref, m_ref, v_ref, go_ref, d_ref, mo_ref, vo_ref = refs
            gr = g_ref[...]
        mn = ADAM_B1 * m_ref[...] + (1.0 - ADAM_B1) * gr
        vn = ADAM_B2 * v_ref[...] + (1.0 - ADAM_B2) * (gr * gr)
        go_ref[...] = gr
        mo_ref[...] = mn
        vo_ref[...] = vn
        d_ref[...] = -ADAM_LR * ((mn / bc1) / (jnp.sqrt(vn / bc2) + ADAM_EPS) + ADAM_WD * w_ref[...])

    blk = pl.BlockSpec((tr, c), lambda i: (i, 0))
    ins = (w, g, g2, m, v) if two else (w, g, m, v)
    return pl.pallas_call(
        body, name=name, grid=(r // tr,), in_specs=[blk] * len(ins), out_specs=[blk] * 4,
        out_shape=[jax.ShapeDtypeStruct((r, c), F32)] * 4, compiler_params=_params("arbitrary"),
    )(*ins)


def _flip(pos, k):
    x, y, c = pos
    return (x ^ ((k >> 2) & 1), y ^ ((k >> 1) & 1), c ^ (k & 1))


def _lin(pos):
    return 4 * pos[0] + 2 * pos[1] + pos[2]


def _chip(pos):
    return 2 * pos[0] + pos[1]


def _here():
    return (lax.axis_index("x"), lax.axis_index("y"), lax.axis_index("c"))


def _all_gather8(x, name):
    def body(x_ref, o_ref, send, recv, own):
        me = _here()
        mine = pltpu.make_async_copy(x_ref, o_ref.at[_lin(me)], own)
        mine.start()
        out = [pltpu.make_async_remote_copy(src_ref=x_ref, dst_ref=o_ref.at[_lin(me)], send_sem=send.at[k - 1],
                                            recv_sem=recv.at[k - 1], device_id=_flip(me, k), device_id_type=MESH)
               for k in range(1, 8)]
        for cp in out:
            cp.start()
        for k in range(1, 8):
            peer = _flip(me, k)
            pltpu.make_async_remote_copy(src_ref=x_ref, dst_ref=o_ref.at[_lin(peer)], send_sem=send.at[k - 1],
                                         recv_sem=recv.at[k - 1], device_id=peer, device_id_type=MESH).wait_recv()
        for cp in out:
            cp.wait_send()
        mine.wait()

    return pl.pallas_call(
        body, name=name, in_specs=[ANY], out_specs=ANY, out_shape=jax.ShapeDtypeStruct((8,) + x.shape, x.dtype),
        scratch_shapes=[pltpu.SemaphoreType.DMA((7,)), pltpu.SemaphoreType.DMA((7,)), pltpu.SemaphoreType.DMA],
    )(x)


class _Exchange4:
    def __init__(self, x, scatter):
        self.x, self.scatter = x, scatter
        self.out_shape = jax.ShapeDtypeStruct(x.shape if scatter else (4,) + x.shape, x.dtype)
        self.scratch = [pltpu.SemaphoreType.DMA((3,)), pltpu.SemaphoreType.DMA((3,)), pltpu.SemaphoreType.DMA]

    def _copies(self, x_ref, o_ref, send, recv, own, arrivals):
        me = _here()
        part = (lambda pos: x_ref.at[_chip(pos)]) if self.scatter else (lambda pos: x_ref)
        local = pltpu.make_async_copy(part(me), o_ref.at[_chip(me)], own)
        outs, ins = [], []
        for j, k in enumerate((2, 4, 6)):
            peer = _flip(me, k)
            sems = dict(send_sem=send.at[j], recv_sem=recv.at[j], device_id=peer, device_id_type=MESH)
            outs.append(pltpu.make_async_remote_copy(src_ref=part(peer), dst_ref=o_ref.at[_chip(me)], **sems))
            if arrivals:
                ins.append(pltpu.make_async_remote_copy(src_ref=part(me), dst_ref=o_ref.at[_chip(peer)], **sems))
        return local, outs, ins

    def start(self, *refs):
        local, outs, _ = self._copies(*refs, arrivals=False)
        local.start()
        for cp in outs:
            cp.start()

    def wait(self, *refs):
        local, outs, ins = self._copies(*refs, arrivals=True)
        for cp in ins:
            cp.wait_recv()
        for cp in outs:
            cp.wait_send()
        local.wait()


def _exchange4(x, scatter, name):
    ex = _Exchange4(x, scatter)

    def body(*refs):
        ex.start(*refs)
        ex.wait(*refs)

    return pl.pallas_call(body, name=name, in_specs=[ANY], out_specs=ANY, out_shape=ex.out_shape, scratch_shapes=ex.scratch)(x)


def _hosted_call(body, side, *, name, grid, in_specs, out_specs, out_shape, scratch_shapes, args):
    n_in, n_out = len(in_specs), len(out_specs)
    params = _params(*(["arbitrary"] * len(grid)))
    if side is None:
        return pl.pallas_call(body, name=name, grid=grid, in_specs=in_specs, out_specs=out_specs, out_shape=out_shape,
                              scratch_shapes=scratch_shapes, compiler_params=params)(*args)

    def hosted(*refs):
        ins, x_ref = refs[:n_in], refs[n_in]
        outs, o_ref = refs[n_in + 1:n_in + 1 + n_out], refs[n_in + 1 + n_out]
        scratch, sems = refs[n_in + n_out + 2:-3], refs[-3:]
        ids = [pl.program_id(a) for a in range(len(grid))]
        first, last = ids[0] == 0, ids[0] == grid[0] - 1
        for a in range(1, len(grid)):
            first, last = first & (ids[a] == 0), last & (ids[a] == grid[a] - 1)

        @pl.when(first)
        def _():
            side.start(x_ref, o_ref, *sems)

        body(*ins, *outs, *scratch)

        @pl.when(last)
        def _():
            side.wait(x_ref, o_ref, *sems)

    return pl.pallas_call(hosted, name=name + "_x", grid=grid, in_specs=list(in_specs) + [ANY], out_specs=list(out_specs) + [ANY],
                          out_shape=list(out_shape) + [side.out_shape], scratch_shapes=list(scratch_shapes) + side.scratch,
                          compiler_params=params)(*args, side.x)


def _sibling_swap(x, name):
    def body(x_ref, o_ref, send, recv):
        me = _here()
        cp = pltpu.make_async_remote_copy(src_ref=x_ref, dst_ref=o_ref, send_sem=send, recv_sem=recv,
                                          device_id=_flip(me, 1), device_id_type=MESH)
        cp.start()
        cp.wait()

    return pl.pallas_call(
        body, name=name, in_specs=[ANY], out_specs=ANY, out_shape=jax.ShapeDtypeStruct(x.shape, x.dtype),
        scratch_shapes=[pltpu.SemaphoreType.DMA, pltpu.SemaphoreType.DMA],
    )(x)


def _pad_lanes(a, width):
    return jnp.pad(a, [(0, 0)] * (a.ndim - 1) + [(0, width - a.shape[-1])])


def kernel(x, c, ctx, c_ctx, w_ada, b_ada, g_pre, g_post, w_in, conv_w, conv_b, dt_bias, a_log, d_skip, g_ssd, g_v, w_s, b_s, g_mlp, w_out, loss_target, m_c_ctx, m_w_ada, m_b_ada, m_g_pre, m_g_post, m_w_in, m_conv_w, m_conv_b, m_dt_bias, m_a_log, m_d_skip, m_g_ssd, m_g_v, m_w_s, m_b_s, m_g_mlp, m_w_out, v_c_ctx, v_w_ada, v_b_ada, v_g_pre, v_g_post, v_w_in, v_conv_w, v_conv_b, v_dt_bias, v_a_log, v_d_skip, v_g_ssd, v_g_v, v_w_s, v_b_s, v_g_mlp, v_w_out):
    depth, d = g_pre.shape
    seq, ctx_len = x.shape[1], ctx.shape[1]
    heads = d // HP
    in_w = 6 * d + 2 * heads
    groups_mlp = d // LANES
    t = ctx_len + seq
    assert ctx_len == TB and seq % TB == 0 and TB % ROW == 0 and heads % 4 == 0 and heads <= LANES and d % LANES == 0
    assert w_in.shape == (depth, d, in_w // 4)

    xi, yi, ci = lax.axis_index("x"), lax.axis_index("y"), lax.axis_index("c")
    chip = 2 * xi + yi
    me = 4 * xi + 2 * yi + ci

    n_ada = 3 * d // 4
    c_all = _all_gather8(c, "ag_c")[:, 0, :]
    c16 = jnp.concatenate([c_all, c_ctx[None, :], jnp.zeros((7, d), F32)], axis=0)
    b_loc = lax.dynamic_slice_in_dim(b_ada, chip * n_ada, n_ada, axis=1)[:, None, :]
    mods_loc = _all_gather8(_ada_fwd(c16, w_ada, b_loc).reshape(depth * 16, n_ada), "ag_mods")
    mods_loc = mods_loc.reshape(4, 2, depth, 16, n_ada)[:, 0]
    mods_full = jnp.moveaxis(mods_loc, 0, 2).reshape(depth, 16, 3 * d)
    mods_x = lax.dynamic_index_in_dim(mods_full, me, axis=1, keepdims=False).reshape(depth, 3, d)
    mods_c = mods_full[:, 8, :].reshape(depth, 3, d)
    mods = _pad_rows8(jnp.stack([mods_c, mods_x], axis=1))

    w_in_b, w_out_b = w_in.astype(BF16), w_out.astype(BF16)

    def lay_out(w_in_rows):
        full = jnp.moveaxis(jnp.concatenate(w_in_rows, axis=1), 0, 1).reshape(d, in_w)
        w_dt_l = jnp.concatenate([_pad_lanes(full[:, 2 * d:2 * d + heads], LANES),
                                  _pad_lanes(full[:, 2 * d + heads:2 * d + 2 * heads], LANES)], axis=1)
        return full[:, :2 * d], full[:, 2 * d + 2 * heads:], w_dt_l

    w_in_rows = [_exchange4(w_in_b[0], False, "ag_w_in")]
    r_scan = 3 * d // 4
    conv_w_full = jnp.moveaxis(_exchange4(conv_w, False, "ag_conv_w"), 0, 2).reshape(depth, CONV_TAPS, 2 * d)
    conv_w8 = jnp.pad(conv_w_full, ((0, 0), (0, 8 - CONV_TAPS), (0, 0)))

    tri = jnp.stack([jnp.tril(jnp.ones((Q, Q), F32)), jnp.triu(jnp.ones((Q, Q), F32))])
    tri_t = jnp.swapaxes(tri, 1, 2)
    dtb = _pad_lanes(dt_bias, LANES)[:, :, None, :]
    alog = _pad_lanes(a_log, LANES)[:, :, None, :]
    dsk_row = jnp.repeat(d_skip, HP, axis=1)[:, None, :]
    w_st = jnp.swapaxes(w_s, 2, 3)
    b_st = _pad_lanes(jnp.swapaxes(b_s, 1, 2), LANES)
    chan = jnp.arange(d)
    ind_head = (chan[:, None] // HP == jnp.arange(LANES)[None, :]).astype(F32)
    ind_b, ind_t = ind_head.astype(BF16), ind_head.T.astype(BF16)
    ind_group = (chan[:, None] // LANES == jnp.arange(LANES)[None, :]).astype(F32)

    stream = jnp.concatenate([ctx[0], x[0]], axis=0)
    saved = []
    for l in range(depth):
        w_xbc, w_mid, w_dt = lay_out(w_in_rows)
        more = l + 1 < depth
        hx, hx_t = _pre_fwd(stream, g_pre[l][None], mods[l])
        z_xbc = _mm(hx, w_xbc, NN, "in_xbc")
        z_mid, w_out_all = _mm(hx, w_mid, NN, "in_mid", side=_Exchange4(w_out_b[l], False))
        w_o = w_out_all.reshape(2 * d, d)
        z_dt = _mm(hx, w_dt, NN, "in_dt")
        xbc = _conv_fwd(z_xbc, conv_w8[l], conv_b[l][None], ctx_len)
        y2, hs, *rows_a = _ssd_fwd(xbc, z_dt, dtb[l], alog[l], tri, ind_t, d, ctx_len,
                                   side=_Exchange4(w_in_b[l + 1, :r_scan], False) if more else None)
        ycat, ycat_t = _mix_fwd(z_mid, y2, xbc, dsk_row[l], g_ssd[l][None], g_v[l][None], g_mlp[l][None], w_s[l], b_st[l])
        if more:
            o, rows_b = _mm(ycat, w_o, NN, "out_proj", side=_Exchange4(w_in_b[l + 1, r_scan:], False))
            w_in_rows = [rows_a[0], rows_b]
        else:
            o = _mm(ycat, w_o, NN, "out_proj")
        saved.append((stream, hx_t, z_xbc, z_mid, z_dt, xbc, y2, hs, ycat_t, o, w_xbc, w_mid, w_dt, w_o))
        stream = _post_fwd(o, stream, g_post[l][None], mods[l])

    sq, d_stream = _loss_grad(stream, loss_target[0])
    loss = lax.psum(0.5 / d * sq[0, 0], ("x", "y", "c"))

    small = []
    dmods = []
    q_in = in_w // 4
    got_in, got_out = [None] * depth, [None] * depth
    parts_in = None
    for l in reversed(range(depth)):
        x_in, hx_t, z_xbc, z_mid, z_dt, xbc, y2, hs, ycat_t, o, w_xbc, w_mid, w_dt, w_o = saved[l]
        d_o, acc_post = _post_bwd(d_stream, o, g_post[l][None], mods[l])
        d_ycat = _mm(d_o, w_o, NT, "d_ycat")
        g_out = _mm(ycat_t, d_o, NN, "dw_out", out_dtype=BF16)
        dz_mid, d_y, vec, d_ws, d_bs = _mix_bwd(z_mid, y2, xbc, d_ycat, dsk_row[l], g_ssd[l][None], g_v[l][None], g_mlp[l][None],
                                                w_s[l], w_st[l], b_st[l], ind_head, ind_group)
        d_xbc2, dz_dt, d_bias, d_alog, *got = _ssd_bwd(xbc, z_dt, dtb[l], alog[l], tri, tri_t, ind_t, ind_b, d_y, y2, hs, d, ctx_len,
                                                       side=_Exchange4(parts_in, True) if parts_in is not None else None)
        if parts_in is not None:
            got_in[l + 1] = got[0]
        dz_xbc, d_cw, d_cb = _conv_bwd(z_xbc, d_xbc2, d_y, dsk_row[l], conv_w8[l], conv_b[l][None], ctx_len)
        g_xbc, got_out[l] = _mm(hx_t, dz_xbc, NN, "dw_xbc", out_dtype=BF16, side=_Exchange4(g_out.reshape(4, 2 * d // 4, d), True))
        g_mid = _mm(hx_t, dz_mid, NN, "dw_mid", out_dtype=BF16)
        g_dt = _mm(hx_t, dz_dt, NN, "dw_dt", out_dtype=BF16)
        g_in = jnp.concatenate([g_xbc, g_dt[:, :heads], g_dt[:, LANES:LANES + heads], g_mid], axis=1)
        parts_in = jnp.moveaxis(g_in.reshape(d, 4, q_in), 1, 0)
        if l == 0:
            r_a = max(LANES, d // 3 // LANES * LANES)
            d_hx, got_a = _mm(dz_xbc, w_xbc, NT, "dhx_xbc", side=_Exchange4(parts_in[:, :r_a], True))
            d_hx, got_b = _mm(dz_mid, w_mid, NT, "dhx_mid", acc=d_hx, side=_Exchange4(parts_in[:, r_a:], True))
            got_in[0] = jnp.concatenate([got_a, got_b], axis=1)
        else:
            d_hx = _mm(dz_xbc, w_xbc, NT, "dhx_xbc")
            d_hx = _mm(dz_mid, w_mid, NT, "dhx_mid", acc=d_hx)
        d_hx = _mm(dz_dt, w_dt, NT, "dhx_dt", acc=d_hx)
        d_stream, acc_pre = _pre_bwd(x_in, d_hx, d_stream, g_pre[l][None], mods[l])
        dmods.append(jnp.concatenate([acc_pre[:, 0], acc_pre[:, 1], acc_post[:, 0]], axis=1))
        small.append(dict(
            g_pre=acc_pre[0, 2] + acc_pre[1, 2], g_post=acc_post[0, 1] + acc_post[1, 1], conv_w=d_cw[:CONV_TAPS], conv_b=d_cb[0],
            dt_bias=d_bias[:, 0, :heads], a_log=d_alog[:, 0, :heads], d_skip=vec[3, :heads], g_ssd=vec[0], g_v=vec[1],
            w_s=d_ws, b_s=d_bs[:, :groups_mlp].T, g_mlp=vec[2]))
    small.reverse(), dmods.reverse()
    grad_x = d_stream[ctx_len:][None]

    names = ["g_pre", "g_post", "conv_w", "conv_b", "dt_bias", "a_log", "d_skip", "g_ssd", "g_v", "w_s", "b_s", "g_mlp"]
    stacked = {n: jnp.stack([small[l][n] for l in range(depth)]) for n in names}
    flat = jnp.concatenate([stacked[n].reshape(-1) for n in names])
    n_flat = flat.shape[0]
    n_pad = -(-n_flat // (PACK_ROWS * LANES)) * (PACK_ROWS * LANES)
    packed = jnp.pad(flat, (0, n_pad - n_flat)).reshape(n_pad // LANES, LANES)
    summed = _sum_lead(_all_gather8(packed, "ag_small"), "sum_small").reshape(-1)
    grads, off = {}, 0
    for n in names:
        size = stacked[n].size
        grads[n] = summed[off:off + size].reshape(stacked[n].shape)
        off += size

    dm_all = _all_gather8(jnp.stack(dmods).reshape(depth * 2, 3 * d), "ag_dmods").reshape(8, depth, 2, 3 * d)
    dm_ctx = _sum_lead(dm_all[:, :, 0, :], "sum_dm_ctx")
    dm16 = jnp.concatenate([jnp.moveaxis(dm_all[:, :, 1, :], 0, 1), dm_ctx[:, None, :], jnp.zeros((depth, 7, 3 * d), F32)], axis=1)
    grads["b_ada"] = _rowsum(dm16)[:, 0, :]
    dm_loc = jnp.pad(lax.dynamic_slice_in_dim(dm16, chip * n_ada, n_ada, axis=2), ((0, 0), (0, LANES - 16), (0, 0)))
    c_t = jnp.pad(c16.T, ((0, 0), (0, LANES - 16)))
    g_w_ada, d_scc_part = _ada_bwd(c_t, dm_loc, w_ada)
    d_scc = _sum_lead(_all_gather8(d_scc_part, "ag_dscc").reshape(4, 2, 8, d)[:, 0], "sum_dscc")
    grads["c_ctx"] = _cctx_grad(d_scc[0:1], c_ctx[None])[0]
    grads["conv_w"] = lax.dynamic_slice_in_dim(grads["conv_w"], chip * (2 * d // 4), 2 * d // 4, axis=2)

    def reduce_big(got, name):
        part = jnp.concatenate([_sum_lead(g, "sum_" + name) for g in got], axis=0)
        return part, _sibling_swap(part, "swap_" + name)

    g_in_a, g_in_b = reduce_big(got_in, "w_in")
    g_out_a, g_out_b = reduce_big(got_out, "w_out")

    weights = dict(c_ctx=c_ctx, w_ada=w_ada, b_ada=b_ada, g_pre=g_pre, g_post=g_post, w_in=w_in, conv_w=conv_w, conv_b=conv_b,
                   dt_bias=dt_bias, a_log=a_log, d_skip=d_skip, g_ssd=g_ssd, g_v=g_v, w_s=w_s, b_s=b_s, g_mlp=g_mlp, w_out=w_out)
    m_in = dict(c_ctx=m_c_ctx, w_ada=m_w_ada, b_ada=m_b_ada, g_pre=m_g_pre, g_post=m_g_post, w_in=m_w_in, conv_w=m_conv_w,
                conv_b=m_conv_b, dt_bias=m_dt_bias, a_log=m_a_log, d_skip=m_d_skip, g_ssd=m_g_ssd, g_v=m_g_v, w_s=m_w_s,
                b_s=m_b_s, g_mlp=m_g_mlp, w_out=m_w_out)
    v_in = dict(c_ctx=v_c_ctx, w_ada=v_w_ada, b_ada=v_b_ada, g_pre=v_g_pre, g_post=v_g_post, w_in=v_w_in, conv_w=v_conv_w,
                conv_b=v_conv_b, dt_bias=v_dt_bias, a_log=v_a_log, d_skip=v_d_skip, g_ssd=v_g_ssd, g_v=v_g_v, w_s=v_w_s,
                b_s=v_b_s, g_mlp=v_g_mlp, w_out=v_w_out)
    order = list(weights)
    results = {}
    big = {"w_in": (g_in_a, g_in_b), "w_out": (g_out_a, g_out_b), "w_ada": (g_w_ada.reshape(depth * d, n_ada), None)}
    for n, (ga, gb) in big.items():
        shp = weights[n].shape
        two = lambda a: a.reshape(-1, shp[-1])
        results[n] = [r.reshape(shp) for r in _adamw(two(weights[n]), ga, two(m_in[n]), two(v_in[n]), "adamw_" + n, g2=gb)]
    rest = [n for n in order if n not in big]

    def pack(tree):
        f = jnp.concatenate([tree[n].reshape(-1) for n in rest])
        padn = -(-f.shape[0] // (PACK_ROWS * LANES)) * (PACK_ROWS * LANES)
        return jnp.pad(f, (0, padn - f.shape[0])).reshape(padn // LANES, LANES)

    outs = _adamw(pack(weights), pack(grads), pack(m_in), pack(v_in), "adamw_small")
    off = 0
    for n in rest:
        size, shp = weights[n].size, weights[n].shape
        results[n] = [o_.reshape(-1)[off:off + size].reshape(shp) for o_ in outs]
        off += size

    return (loss, grad_x, *[results[n][0] for n in order], *[results[n][1] for n in order],
            *[results[n][2] for n in order], *[results[n][3] for n in order])


def _pad_rows8(a):
    return jnp.pad(a, [(0, 0)] * (a.ndim - 2) + [(0, 8 - a.shape[-2]), (0, 0)])
```

```python
import jax
import jax.numpy as jnp
from jax import lax
from jax.experimental import pallas as pl
from jax.experimental.pallas import tpu as pltpu

F32 = jnp.float32
BF16 = jnp.bfloat16
EPS = 1e-6
Q = 128
TB = 256
ROW = 64
HP = 64
LANES = 128
CONV_TAPS = 5
VMEM_LIMIT = 48 * 1024 * 1024
HI = lax.Precision.HIGHEST
SUM_BLOCK_BYTES = 4 * 1024 * 1024
ADAM_BLOCK_BYTES = 1024 * 1024
PACK_ROWS = 256
MESH = pl.DeviceIdType.MESH
ANY = pl.BlockSpec(memory_space=pl.ANY)

ADAM_LR, ADAM_B1, ADAM_B2, ADAM_EPS, ADAM_WD, ADAM_STEP = 0.001, 0.9, 0.999, 1e-08, 0.01, 10

NN = (((1,), (0,)), ((), ()))
NT = (((1,), (1,)), ((), ()))
TN = (((0,), (0,)), ((), ()))


def _dot(a, b, dims=NN, prec=None):
    return lax.dot_general(a, b, dims, precision=prec, preferred_element_type=F32)


def _params(*sem):
    if sem:
        return pltpu.CompilerParams(vmem_limit_bytes=VMEM_LIMIT, dimension_semantics=sem)
    return pltpu.CompilerParams(vmem_limit_bytes=VMEM_LIMIT)


def _tile(dim, cands):
    for t in cands:
        if dim % t == 0:
            return t
    return dim


def _sigmoid(x):
    return 1.0 / (1.0 + jnp.exp(-x))


def _softplus(x):
    e = jnp.exp(-jnp.abs(x))
    u = 1.0 + e
    um1 = u - 1.0
    l1p = jnp.where(um1 == 0.0, e, jnp.log(u) * (e / jnp.where(um1 == 0.0, 1.0, um1)))
    return jnp.maximum(x, 0.0) + l1p


def _rms(x):
    return lax.rsqrt(jnp.mean(x * x, axis=-1, keepdims=True) + EPS)


def _rms_bwd(x, r, t):
    return r * t - x * (r * r * r) * jnp.mean(x * t, axis=-1, keepdims=True)


def _mm(a, b, dims, name, acc=None, out_dtype=F32, side=None):
    (ca,), (cb,) = dims[0]
    m, k = a.shape[1 - ca], a.shape[ca]
    n = b.shape[1 - cb]
    tm = _tile(m, (1024, 768, 512, 384, 256, 128))
    tn = _tile(n, (1024, 512, 256, 128))
    tk = k if k <= 2048 else _tile(k, (2048, 768, 512, 384, 256, 128))
    nk = k // tk
    a_spec = pl.BlockSpec((tm, tk), lambda i, j, kk: (i, kk)) if ca == 1 else pl.BlockSpec((tk, tm), lambda i, j, kk: (kk, i))
    b_spec = pl.BlockSpec((tk, tn), lambda i, j, kk: (kk, j)) if cb == 0 else pl.BlockSpec((tn, tk), lambda i, j, kk: (j, kk))
    o_spec = pl.BlockSpec((tm, tn), lambda i, j, kk: (i, j))
    has_acc = acc is not None

    def body(*refs):
        if has_acc:
            a_ref, b_ref, c_ref, o_ref, acc_ref = refs
        else:
            a_ref, b_ref, o_ref, acc_ref = refs
        kk = pl.program_id(2)

        @pl.when(kk == 0)
        def _():
            acc_ref[...] = c_ref[...] if has_acc else jnp.zeros_like(acc_ref)

        acc_ref[...] += _dot(a_ref[...].astype(BF16), b_ref[...].astype(BF16), dims)

        @pl.when(kk == nk - 1)
        def _():
            o_ref[...] = acc_ref[...].astype(out_dtype)

    res = _hosted_call(
        body, side, name=name, grid=(m // tm, n // tn, nk),
        in_specs=[a_spec, b_spec] + ([o_spec] if has_acc else []),
        out_specs=[o_spec], out_shape=[jax.ShapeDtypeStruct((m, n), out_dtype)],
        scratch_shapes=[pltpu.VMEM((tm, tn), F32)], args=(a, b, acc) if has_acc else (a, b))
    return res[0] if side is None else res


def _which(i):
    return jnp.minimum(i, 1)


def _pre_fwd(x, g_pre, mods):
    t, d = x.shape

    def body(x_ref, g_ref, m_ref, o_ref, ot_ref):
        xb = x_ref[...]
        xn = xb * _rms(xb) * g_ref[...]
        hx = xn * (1.0 + m_ref[0, 1:2, :]) + m_ref[0, 0:1, :]
        o_ref[...] = hx.astype(BF16)
        ot_ref[...] = hx.T.astype(BF16)

    return pl.pallas_call(
        body, name="pre_fwd", grid=(t // TB,),
        in_specs=[pl.BlockSpec((TB, d), lambda i: (i, 0)), pl.BlockSpec((1, d), lambda i: (0, 0)),
                  pl.BlockSpec((1, 8, d), lambda i: (_which(i), 0, 0))],
        out_specs=[pl.BlockSpec((TB, d), lambda i: (i, 0)), pl.BlockSpec((d, TB), lambda i: (0, i))],
        out_shape=[jax.ShapeDtypeStruct((t, d), BF16), jax.ShapeDtypeStruct((d, t), BF16)], compiler_params=_params("arbitrary"),
    )(x, g_pre, mods)


def _pre_bwd(x, d_hx, d_up, g_pre, mods, side=None):
    t, d = x.shape

    def body(x_ref, dh_ref, du_ref, g_ref, m_ref, dx_ref, acc_ref):
        i = pl.program_id(0)

        @pl.when(i <= 1)
        def _():
            acc_ref[...] = jnp.zeros_like(acc_ref)

        xb = x_ref[...]
        dh = dh_ref[...]
        r = _rms(xb)
        xr = xb * r
        d_xn = dh * (1.0 + m_ref[0, 1:2, :])
        dx_ref[...] = du_ref[...] + _rms_bwd(xb, r, d_xn * g_ref[...])
        acc_ref[0, 0:1, :] += jnp.sum(dh, axis=0, keepdims=True)
        acc_ref[0, 1:2, :] += jnp.sum(dh * (xr * g_ref[...]), axis=0, keepdims=True)
        acc_ref[0, 2:3, :] += jnp.sum(d_xn * xr, axis=0, keepdims=True)

    blk = pl.BlockSpec((TB, d), lambda i: (i, 0))
    return _hosted_call(
        body, side, name="pre_bwd", grid=(t // TB,),
        in_specs=[blk, blk, blk, pl.BlockSpec((1, d), lambda i: (0, 0)),
                  pl.BlockSpec((1, 8, d), lambda i: (_which(i), 0, 0))],
        out_specs=[blk, pl.BlockSpec((1, 8, d), lambda i: (_which(i), 0, 0))],
        out_shape=[jax.ShapeDtypeStruct((t, d), F32), jax.ShapeDtypeStruct((2, 8, d), F32)],
        scratch_shapes=[], args=(x, d_hx, d_up, g_pre, mods))


def _post_fwd(o, x, g_post, mods):
    t, d = x.shape

    def body(o_ref, x_ref, g_ref, m_ref, y_ref):
        ob = o_ref[...]
        y_ref[...] = x_ref[...] + m_ref[0, 2:3, :] * (ob * _rms(ob) * g_ref[...])

    blk = pl.BlockSpec((TB, d), lambda i: (i, 0))
    return pl.pallas_call(
        body, name="post_fwd", grid=(t // TB,),
        in_specs=[blk, blk, pl.BlockSpec((1, d), lambda i: (0, 0)), pl.BlockSpec((1, 8, d), lambda i: (_which(i), 0, 0))],
        out_specs=blk, out_shape=jax.ShapeDtypeStruct((t, d), F32), compiler_params=_params("arbitrary"),
    )(o, x, g_post, mods)


def _post_bwd(d_y, o, g_post, mods):
    t, d = o.shape

    def body(dy_ref, o_ref, g_ref, m_ref, do_ref, acc_ref):
        i = pl.program_id(0)

        @pl.when(i <= 1)
        def _():
            acc_ref[...] = jnp.zeros_like(acc_ref)

        ob = o_ref[...]
        dy = dy_ref[...]
        r = _rms(ob)
        orr = ob * r
        d_out = dy * m_ref[0, 2:3, :]
        do_ref[...] = _rms_bwd(ob, r, d_out * g_ref[...]).astype(BF16)
        acc_ref[0, 0:1, :] += jnp.sum(dy * (orr * g_ref[...]), axis=0, keepdims=True)
        acc_ref[0, 1:2, :] += jnp.sum(d_out * orr, axis=0, keepdims=True)

    blk = pl.BlockSpec((TB, d), lambda i: (i, 0))
    return pl.pallas_call(
        body, name="post_bwd", grid=(t // TB,),
        in_specs=[blk, blk, pl.BlockSpec((1, d), lambda i: (0, 0)), pl.BlockSpec((1, 8, d), lambda i: (_which(i), 0, 0))],
        out_specs=[blk, pl.BlockSpec((1, 8, d), lambda i: (_which(i), 0, 0))],
        out_shape=[jax.ShapeDtypeStruct((t, d), BF16), jax.ShapeDtypeStruct((2, 8, d), F32)],
        compiler_params=_params("arbitrary"),
    )(d_y, o, g_post, mods)


def _loss_grad(xf, target):
    t, d = xf.shape

    def body(x_ref, t_ref, loss_ref, dx_ref):
        i = pl.program_id(0)

        @pl.when(i == 0)
        def _():
            loss_ref[...] = jnp.zeros_like(loss_ref)
            dx_ref[...] = jnp.zeros_like(dx_ref)

        @pl.when(i > 0)
        def _():
            err = x_ref[...] - t_ref[...]
            loss_ref[...] += jnp.sum(err * err).reshape(1, 1)
            dx_ref[...] = err * (1.0 / d)

    return pl.pallas_call(
        body, name="loss_grad", grid=(t // TB,),
        in_specs=[pl.BlockSpec((TB, d), lambda i: (i, 0)), pl.BlockSpec((TB, d), lambda i: (jnp.maximum(i - 1, 0), 0))],
        out_specs=[pl.BlockSpec((1, 1), lambda i: (0, 0)), pl.BlockSpec((TB, d), lambda i: (i, 0))],
        out_shape=[jax.ShapeDtypeStruct((1, 1), F32), jax.ShapeDtypeStruct((t, d), F32)],
        compiler_params=_params("arbitrary"),
    )(xf, target)


def _conv_terms(zb, pos, row_len):
    out = []
    for k in range(CONV_TAPS):
        o = k - CONV_TAPS // 2
        sh = zb if o == 0 else pltpu.roll(zb, (-o) % TB, 0)
        out.append(jnp.where((pos + o >= 0) & (pos + o < row_len), sh, 0.0))
    return out


def _row_pos(i, ctx_len):
    row_len = jnp.where(i == 0, ctx_len, ROW)
    pos = lax.broadcasted_iota(jnp.int32, (TB, 1), 0) & (row_len - 1)
    return pos, row_len


def _conv_fwd(z_xbc, conv_w8, conv_b, ctx_len):
    t, c = z_xbc.shape
    tc = _tile(c, (1024, 512, 256, 128))

    def body(z_ref, w_ref, b_ref, o_ref, ds_ref):
        pos, row_len = _row_pos(pl.program_id(1), ctx_len)
        terms = _conv_terms(z_ref[...], pos, row_len)
        pre = b_ref[...]
        for k in range(CONV_TAPS):
            pre = pre + terms[k] * w_ref[k:k + 1, :]
        sig = _sigmoid(pre)
        o_ref[...] = pre * sig
        ds_ref[...] = sig * (1.0 + pre * (1.0 - sig))

    blk = pl.BlockSpec((TB, tc), lambda j, i: (i, j))
    return pl.pallas_call(
        body, name="conv_fwd", grid=(c // tc, t // TB),
        in_specs=[blk, pl.BlockSpec((8, tc), lambda j, i: (0, j)), pl.BlockSpec((1, tc), lambda j, i: (0, j))],
        out_specs=[blk, blk], out_shape=[jax.ShapeDtypeStruct((t, c), F32)] * 2, compiler_params=_params("arbitrary", "arbitrary"),
    )(z_xbc, conv_w8, conv_b)


def _conv_bwd(z_xbc, dsilu, d_xbc2, d_y, d_skip_row, conv_w8, ctx_len):
    t, c = z_xbc.shape
    d = d_y.shape[1]
    tc = _tile(d, (1024, 512, 256, 128))
    nskip = d // tc

    def body(z_ref, dsl_ref, g2_ref, dy_ref, ds_ref, w_ref, dz_ref, dw_ref, db_ref):
        j, i = pl.program_id(0), pl.program_id(1)

        @pl.when(i == 0)
        def _():
            dw_ref[...] = jnp.zeros_like(dw_ref)
            db_ref[...] = jnp.zeros_like(db_ref)

        pos, row_len = _row_pos(i, ctx_len)
        skip = jnp.where(j < nskip, 1.0, 0.0) * ds_ref[...]
        d_pre = (g2_ref[0] + g2_ref[1] + dy_ref[...] * skip) * dsl_ref[...]
        db_ref[...] += jnp.sum(d_pre, axis=0, keepdims=True)
        zb = z_ref[...]
        dz = jnp.zeros_like(d_pre)
        for k in range(CONV_TAPS):
            o = k - CONV_TAPS // 2
            sh = d_pre if o == 0 else pltpu.roll(d_pre, o % TB, 0)
            sh = jnp.where((pos - o >= 0) & (pos - o < row_len), sh, 0.0)
            dw_ref[k:k + 1, :] += jnp.sum(sh * zb, axis=0, keepdims=True)
            dz = dz + sh * w_ref[k:k + 1, :]
        dz_ref[...] = dz.astype(BF16)

    jd = lambda j: jnp.minimum(j, nskip - 1)
    blk = pl.BlockSpec((TB, tc), lambda j, i: (i, j))
    return pl.pallas_call(
        body, name="conv_bwd", grid=(c // tc, t // TB),
        in_specs=[blk, blk, pl.BlockSpec((2, TB, tc), lambda j, i: (0, i, j)),
                  pl.BlockSpec((TB, tc), lambda j, i: (i, jd(j))), pl.BlockSpec((1, tc), lambda j, i: (0, jd(j))),
                  pl.BlockSpec((8, tc), lambda j, i: (0, j))],
        out_specs=[blk, pl.BlockSpec((8, tc), lambda j, i: (0, j)), pl.BlockSpec((1, tc), lambda j, i: (0, j))],
        out_shape=[jax.ShapeDtypeStruct((t, c), BF16), jax.ShapeDtypeStruct((8, c), F32), jax.ShapeDtypeStruct((1, c), F32)],
        compiler_params=_params("arbitrary", "arbitrary"),
    )(z_xbc, dsilu, d_xbc2, d_y, d_skip_row, conv_w8)


def _scan_chunk(dirn, s, nch, ncc):
    bwd = jnp.where(s < ncc, ncc - 1 - s, nch - 1 - (s - ncc))
    return jnp.where(dirn == 0, s, bwd)


def _ssd_decays(dt_ref, dtb_ref, alog_ref, tri):
    raw = dt_ref[...] + dtb_ref[0]
    dt = _softplus(raw)
    a_neg = -jnp.exp(alog_ref[0])
    a = dt * a_neg
    s = _dot(tri, a, NN, HI)
    stot = jnp.sum(a, axis=0, keepdims=True)
    return raw, dt, a_neg, s, stot, s.T


def _split(v):
    hi = v.astype(BF16)
    return hi, (v - hi.astype(F32)).astype(BF16)


def _expand(v, indt_ref):
    hi, lo = _split(v)
    return _dot(hi, indt_ref[...]) + _dot(lo, indt_ref[...])


def _head_sums(v, ind_ref):
    hi, lo = _split(v)
    return _dot(hi, ind_ref[...]) + _dot(lo, ind_ref[...])


def _ssd_fwd(xbc, z_dt, dtb, alog, tri, ind_t, d, ctx_len, side=None):
    t = xbc.shape[0]
    nch, ncc = t // Q, ctx_len // Q
    heads = d // HP
    groups = heads // 4
    gn = groups * LANES

    def body(xbc_ref, dt_ref, dtb_ref, alog_ref, tri_ref, indt_ref, y_ref, hs_ref, h_scr, xdb_scr, xde_scr, esx_scr):
        @pl.when(pl.program_id(1) == 0)
        def _():
            h_scr[...] = jnp.zeros_like(h_scr)

        tri = tri_ref[0]
        mask = tri > 0.0
        _, dt, _, s, stot, s_t = _ssd_decays(dt_ref, dtb_ref, alog_ref, tri)
        esx_scr[...] = _expand(jnp.exp(s), indt_ref)
        etot_x = _expand(jnp.broadcast_to(jnp.exp(stot), (8, LANES)), indt_ref)[0:1]
        xd = xbc_ref[:, :d] * _expand(dt, indt_ref)
        xdb_scr[...] = xd.astype(BF16)
        xde_scr[...] = (xd * _expand(jnp.exp(stot - s), indt_ref)).astype(BF16)
        left = lax.broadcasted_iota(jnp.int32, (Q, LANES), 1) < HP
        hs_ref[0, 0] = h_scr[...]
        for g in range(groups):
            b32 = xbc_ref[:, d + g * LANES:d + (g + 1) * LANES]
            bb = b32.astype(BF16)
            bbt = b32.T.astype(BF16)
            cb = xbc_ref[:, d + gn + g * LANES:d + gn + (g + 1) * LANES].astype(BF16)
            cbt = _dot(cb, bb, NT)
            for pr in (2 * g, 2 * g + 1):
                h0 = 2 * pr
                cols = slice(pr * LANES, (pr + 1) * LANES)
                xdb = xdb_scr[:, cols]
                res = []
                for h in (h0, h0 + 1):
                    lm = jnp.exp(jnp.where(mask, s[:, h:h + 1] - s_t[h:h + 1, :], -jnp.inf))
                    res.append(_dot((cbt * lm).astype(BF16), xdb))
                hp = h_scr[:, cols]
                y_ref[0, :, cols] = jnp.where(left, res[0], res[1]) + _dot(cb, hp.astype(BF16)) * esx_scr[:, cols]
                h_scr[:, cols] = hp * etot_x[:, cols] + _dot(bbt, xde_scr[:, cols])

    cidx = lambda dd, ss: _scan_chunk(dd, ss, nch, ncc)
    return _hosted_call(
        body, side, name="ssd_fwd", grid=(2, nch),
        in_specs=[pl.BlockSpec((Q, 2 * d), lambda dd, ss: (cidx(dd, ss), 0)),
                  pl.BlockSpec((Q, LANES), lambda dd, ss: (cidx(dd, ss), dd)),
                  pl.BlockSpec((1, 1, LANES), lambda dd, ss: (dd, 0, 0)),
                  pl.BlockSpec((1, 1, LANES), lambda dd, ss: (dd, 0, 0)),
                  pl.BlockSpec((1, Q, Q), lambda dd, ss: (dd, 0, 0)),
                  pl.BlockSpec((LANES, d), lambda dd, ss: (0, 0))],
        out_specs=[pl.BlockSpec((1, Q, d), lambda dd, ss: (dd, cidx(dd, ss), 0)),
                   pl.BlockSpec((1, 1, LANES, d), lambda dd, ss: (dd, cidx(dd, ss), 0, 0))],
        out_shape=[jax.ShapeDtypeStruct((2, t, d), F32), jax.ShapeDtypeStruct((2, nch, LANES, d), F32)],
        scratch_shapes=[pltpu.VMEM((LANES, d), F32), pltpu.VMEM((Q, d), BF16), pltpu.VMEM((Q, d), BF16), pltpu.VMEM((Q, d), F32)],
        args=(xbc, z_dt, dtb, alog, tri, ind_t))


def _ssd_bwd(xbc, z_dt, dtb, alog, tri, tri_t, ind_t, ind, d_y, y2, hs, d, ctx_len, side=None):
    t = xbc.shape[0]
    nch, ncc = t // Q, ctx_len // Q
    heads = d // HP
    groups = heads // 4
    gn = groups * LANES

    def body(xbc_ref, dt_ref, dtb_ref, alog_ref, tri_ref, trit_ref, indt_ref, ind_ref, dy_ref, y_ref, hs_ref,
             dx_ref, dzdt_ref, dbias_ref, dalog_ref, dh_scr, dtx_scr, ex_scr, xdb_scr, xde_scr, dyb_scr, dye_scr, dxd_scr, bdh_scr):
        @pl.when(pl.program_id(1) == 0)
        def _():
            dh_scr[...] = jnp.zeros_like(dh_scr)
            dbias_ref[...] = jnp.zeros_like(dbias_ref)
            dalog_ref[...] = jnp.zeros_like(dalog_ref)

        tri = tri_ref[0]
        mask = tri > 0.0
        mask_t = trit_ref[0] > 0.0
        raw, dt, a_neg, s, stot, s_t = _ssd_decays(dt_ref, dtb_ref, alog_ref, tri)
        etot = jnp.exp(stot)
        etot_x = _expand(jnp.broadcast_to(etot, (8, LANES)), indt_ref)[0:1]
        dtx_scr[...] = _expand(dt, indt_ref)
        ex_scr[...] = _expand(jnp.exp(stot - s), indt_ref)
        xd = xbc_ref[:, :d] * dtx_scr[...]
        xdb_scr[...] = xd.astype(BF16)
        xde_scr[...] = (xd * ex_scr[...]).astype(BF16)
        dyb_scr[...] = dy_ref[...].astype(BF16)
        dye_scr[...] = (dy_ref[...] * _dot(jnp.exp(s).astype(BF16), indt_ref[...])).astype(BF16)
        hd_cols = jnp.sum(dh_scr[...] * hs_ref[0, 0], axis=0, keepdims=True)
        left = lax.broadcasted_iota(jnp.int32, (Q, LANES), 1) < HP
        for g in range(groups):
            b32 = xbc_ref[:, d + g * LANES:d + (g + 1) * LANES]
            c32 = xbc_ref[:, d + gn + g * LANES:d + gn + (g + 1) * LANES]
            bb, cb = b32.astype(BF16), c32.astype(BF16)
            c_t = c32.T.astype(BF16)
            cbt = _dot(cb, bb, NT)
            cbt_t = _dot(bb, cb, NT)
            d_cbt = jnp.zeros((Q, Q), F32)
            d_b = jnp.zeros((Q, LANES), F32)
            d_c = jnp.zeros((Q, LANES), F32)
            for pr in (2 * g, 2 * g + 1):
                h0 = 2 * pr
                cols = slice(pr * LANES, (pr + 1) * LANES)
                xdb = xdb_scr[:, cols]
                dyb = dyb_scr[:, cols]
                dyeb = dye_scr[:, cols]
                hpb = hs_ref[0, 0, :, cols].astype(BF16)
                dhp = dh_scr[:, cols]
                dhb = dhp.astype(BF16)
                parts = []
                for hh, h in enumerate((h0, h0 + 1)):
                    mine = left if hh == 0 else jnp.logical_not(left)
                    diff = s[:, h:h + 1] - s_t[h:h + 1, :]
                    lm = jnp.exp(jnp.where(mask, diff, -jnp.inf))
                    lm_t = jnp.exp(jnp.where(mask_t, -diff, -jnp.inf))
                    gm = _dot(jnp.where(mine, dyb, jnp.zeros_like(dyb)), xdb, NT)
                    d_cbt = d_cbt + gm * lm
                    parts.append(_dot((cbt_t * lm_t).astype(BF16), dyb))
                dxd_scr[:, cols] = jnp.where(left, parts[0], parts[1])
                bdh_scr[:, cols] = _dot(bb, dhb)
                d_c = d_c + _dot(dyeb, hpb, NT)
                d_b = d_b + _dot(xde_scr[:, cols], dhb, NT)
                dh_scr[:, cols] = dhp * etot_x[:, cols] + _dot(c_t, dyeb)
            dx_ref[0, :, d + g * LANES:d + (g + 1) * LANES] = d_b + _dot(d_cbt.T.astype(BF16), cb)
            dx_ref[0, :, d + gn + g * LANES:d + gn + (g + 1) * LANES] = d_c + _dot(d_cbt.astype(BF16), bb)
        x = xbc_ref[:, :d]
        ebdh = ex_scr[...] * bdh_scr[...]
        d_xd = dxd_scr[...] + ebdh
        dx_ref[0, :, :d] = d_xd * dtx_scr[...]
        xe = x * dtx_scr[...] * ebdh
        d_s = _head_sums(dyb_scr[...].astype(F32) * y_ref[0] - xdb_scr[...].astype(F32) * dxd_scr[...] - xe, ind_ref)
        r_dx = _head_sums(d_xd * x, ind_ref)
        row8 = lax.broadcasted_iota(jnp.int32, (8, d), 0)
        tot = _head_sums(jnp.where(row8 == 0, jnp.sum(xe, axis=0, keepdims=True), jnp.where(row8 == 1, hd_cols, 0.0)), ind_ref)
        d_stot = tot[0:1] + etot * tot[1:2]
        d_a = _dot(trit_ref[0], d_s, NN, HI) + d_stot
        valid = lax.broadcasted_iota(jnp.int32, (Q, LANES), 1) < heads
        d_dt_tot = jnp.where(valid, d_a * a_neg + r_dx, 0.0)
        d_raw = d_dt_tot * _sigmoid(raw)
        dzdt_ref[...] = d_raw.astype(BF16)
        dbias_ref[0] += jnp.sum(d_raw, axis=0, keepdims=True)
        dalog_ref[0] += jnp.sum(jnp.where(valid, d_a * dt, 0.0), axis=0, keepdims=True) * a_neg

    cidx = lambda dd, ss: _scan_chunk(dd, nch - 1 - ss, nch, ncc)
    full = lambda shape: pltpu.VMEM(shape, F32)
    half = lambda shape: pltpu.VMEM(shape, BF16)
    return _hosted_call(
        body, side, name="ssd_bwd", grid=(2, nch),
        in_specs=[pl.BlockSpec((Q, 2 * d), lambda dd, ss: (cidx(dd, ss), 0)),
                  pl.BlockSpec((Q, LANES), lambda dd, ss: (cidx(dd, ss), dd)),
                  pl.BlockSpec((1, 1, LANES), lambda dd, ss: (dd, 0, 0)),
                  pl.BlockSpec((1, 1, LANES), lambda dd, ss: (dd, 0, 0)),
                  pl.BlockSpec((1, Q, Q), lambda dd, ss: (dd, 0, 0)),
                  pl.BlockSpec((1, Q, Q), lambda dd, ss: (dd, 0, 0)),
                  pl.BlockSpec((LANES, d), lambda dd, ss: (0, 0)),
                  pl.BlockSpec((d, LANES), lambda dd, ss: (0, 0)),
                  pl.BlockSpec((Q, d), lambda dd, ss: (cidx(dd, ss), 0)),
                  pl.BlockSpec((1, Q, d), lambda dd, ss: (dd, cidx(dd, ss), 0)),
                  pl.BlockSpec((1, 1, LANES, d), lambda dd, ss: (dd, cidx(dd, ss), 0, 0))],
        out_specs=[pl.BlockSpec((1, Q, 2 * d), lambda dd, ss: (dd, cidx(dd, ss), 0)),
                   pl.BlockSpec((Q, LANES), lambda dd, ss: (cidx(dd, ss), dd)),
                   pl.BlockSpec((1, 1, LANES), lambda dd, ss: (dd, 0, 0)),
                   pl.BlockSpec((1, 1, LANES), lambda dd, ss: (dd, 0, 0))],
        out_shape=[jax.ShapeDtypeStruct((2, t, 2 * d), F32), jax.ShapeDtypeStruct((t, 2 * LANES), BF16),
                   jax.ShapeDtypeStruct((2, 1, LANES), F32), jax.ShapeDtypeStruct((2, 1, LANES), F32)],
        scratch_shapes=[full((LANES, d)), full((Q, d)), full((Q, d)), half((Q, d)), half((Q, d)), half((Q, d)), half((Q, d)),
                        full((Q, d)), full((Q, d))],
        args=(xbc, z_dt, dtb, alog, tri, tri_t, ind_t, ind, d_y, y2, hs))


def _mix_common(zm_ref, y2_ref, xh_ref, dsk_ref, gv_ref, ws_ref, bst_ref, d):
    groups = d // LANES
    z_ssd, u, v, z_mlp = (zm_ref[:, k * d:(k + 1) * d] for k in range(4))
    y = y2_ref[0] + y2_ref[1] + dsk_ref[...] * xh_ref[...]
    sig_a = _sigmoid(z_ssd)
    ya_pre = y * (z_ssd * sig_a)
    r_v = _rms(v)
    vn = (v * r_v * gv_ref[...]).astype(BF16)
    sg = jnp.concatenate(
        [_dot(ws_ref[g].astype(BF16), vn[:, g * LANES:(g + 1) * LANES]) + bst_ref[:, g:g + 1] for g in range(groups)], axis=1)
    sig_m = _sigmoid(z_mlp)
    yb_pre = u * sg * (z_mlp * sig_m)
    return z_ssd, u, v, z_mlp, y, sig_a, ya_pre, r_v, vn, sg, sig_m, yb_pre


def _mix_fwd(z_mid, y2, xbc, dsk_row, g_ssd, g_v, g_mlp, w_s, b_st):
    t = z_mid.shape[0]
    d = z_mid.shape[1] // 4
    groups = d // LANES

    def body(zm_ref, y2_ref, xh_ref, dsk_ref, ga_ref, gv_ref, gm_ref, ws_ref, bst_ref, o_ref, ot_ref):
        (_, _, _, _, _, _, ya_pre, _, _, _, _, yb_pre) = _mix_common(zm_ref, y2_ref, xh_ref, dsk_ref, gv_ref, ws_ref, bst_ref, d)
        y_a = ya_pre * _rms(ya_pre) * ga_ref[...]
        y_b = yb_pre * _rms(yb_pre) * gm_ref[...]
        o_ref[:, :d] = y_a.astype(BF16)
        o_ref[:, d:] = y_b.astype(BF16)
        ot_ref[:d, :] = y_a.T.astype(BF16)
        ot_ref[d:, :] = y_b.T.astype(BF16)

    row = pl.BlockSpec((1, d), lambda i: (0, 0))
    return pl.pallas_call(
        body, name="mix_fwd", grid=(t // Q,),
        in_specs=[pl.BlockSpec((Q, 4 * d), lambda i: (i, 0)), pl.BlockSpec((2, Q, d), lambda i: (0, i, 0)),
                  pl.BlockSpec((Q, d), lambda i: (i, 0)), row, row, row, row,
                  pl.BlockSpec((groups, Q, Q), lambda i: (0, 0, 0)), pl.BlockSpec((Q, LANES), lambda i: (0, 0))],
        out_specs=[pl.BlockSpec((Q, 2 * d), lambda i: (i, 0)), pl.BlockSpec((2 * d, Q), lambda i: (0, i))],
        out_shape=[jax.ShapeDtypeStruct((t, 2 * d), BF16), jax.ShapeDtypeStruct((2 * d, t), BF16)], compiler_params=_params("arbitrary"),
    )(z_mid, y2, xbc, dsk_row, g_ssd, g_v, g_mlp, w_s, b_st)


def _mix_bwd(z_mid, y2, xbc, d_ycat, dsk_row, g_ssd, g_v, g_mlp, w_s, w_st, b_st, ind_head, ind_group):
    t = z_mid.shape[0]
    d = z_mid.shape[1] // 4
    groups = d // LANES
    nsteps = t // Q

    def body(zm_ref, y2_ref, xh_ref, dyc_ref, dsk_ref, ga_ref, gv_ref, gm_ref, ws_ref, wst_ref, bst_ref, ih_ref, ig_ref,
             dzm_ref, dy_ref, vec_ref, dws_ref, dbs_ref, dsk_acc, dsg_acc):
        i = pl.program_id(0)

        @pl.when(i == 0)
        def _():
            vec_ref[...] = jnp.zeros_like(vec_ref)
            dws_ref[...] = jnp.zeros_like(dws_ref)
            dsk_acc[...] = jnp.zeros_like(dsk_acc)
            dsg_acc[...] = jnp.zeros_like(dsg_acc)

        (z_ssd, u, v, z_mlp, y, sig_a, ya_pre, r_v, vn, sg, sig_m, yb_pre) = _mix_common(
            zm_ref, y2_ref, xh_ref, dsk_ref, gv_ref, ws_ref, bst_ref, d)
        d_ya = dyc_ref[:, :d]
        r_a = _rms(ya_pre)
        vec_ref[0:1, :] += jnp.sum(d_ya * (ya_pre * r_a), axis=0, keepdims=True)
        d_ya_pre = _rms_bwd(ya_pre, r_a, d_ya * ga_ref[...])
        d_y = d_ya_pre * (z_ssd * sig_a)
        dy_ref[...] = d_y
        dsk_acc[...] += jnp.sum(d_y * xh_ref[...], axis=0, keepdims=True)
        dzm_ref[:, 0:d] = (d_ya_pre * y * (sig_a * (1.0 + z_ssd * (1.0 - sig_a)))).astype(BF16)
        d_yb = dyc_ref[:, d:]
        r_b = _rms(yb_pre)
        vec_ref[2:3, :] += jnp.sum(d_yb * (yb_pre * r_b), axis=0, keepdims=True)
        d_yb_pre = _rms_bwd(yb_pre, r_b, d_yb * gm_ref[...])
        silu_m = z_mlp * sig_m
        dzm_ref[:, d:2 * d] = (d_yb_pre * sg * silu_m).astype(BF16)
        dzm_ref[:, 3 * d:4 * d] = (d_yb_pre * u * sg * (sig_m * (1.0 + z_mlp * (1.0 - sig_m)))).astype(BF16)
        d_sg = d_yb_pre * u * silu_m
        dsg_acc[...] += d_sg
        d_sgb = d_sg.astype(BF16)
        d_vn = []
        for g in range(groups):
            cols = slice(g * LANES, (g + 1) * LANES)
            dws_ref[g] += _dot(d_sgb[:, cols], vn[:, cols], NT)
            d_vn.append(_dot(wst_ref[g].astype(BF16), d_sgb[:, cols]))
        d_vn = jnp.concatenate(d_vn, axis=1)
        vec_ref[1:2, :] += jnp.sum(d_vn * (v * r_v), axis=0, keepdims=True)
        dzm_ref[:, 2 * d:3 * d] = _rms_bwd(v, r_v, d_vn * gv_ref[...]).astype(BF16)

        @pl.when(i == nsteps - 1)
        def _():
            vec_ref[3:4, 0:LANES] = _dot(dsk_acc[...], ih_ref[...], NN, HI)
            dbs_ref[...] = _dot(dsg_acc[...], ig_ref[...], NN, HI)

    row = pl.BlockSpec((1, d), lambda i: (0, 0))
    wsp = pl.BlockSpec((groups, Q, Q), lambda i: (0, 0, 0))
    ind = pl.BlockSpec((d, LANES), lambda i: (0, 0))
    return pl.pallas_call(
        body, name="mix_bwd", grid=(nsteps,),
        in_specs=[pl.BlockSpec((Q, 4 * d), lambda i: (i, 0)), pl.BlockSpec((2, Q, d), lambda i: (0, i, 0)),
                  pl.BlockSpec((Q, d), lambda i: (i, 0)), pl.BlockSpec((Q, 2 * d), lambda i: (i, 0)),
                  row, row, row, row, wsp, wsp, pl.BlockSpec((Q, LANES), lambda i: (0, 0)), ind, ind],
        out_specs=[pl.BlockSpec((Q, 4 * d), lambda i: (i, 0)), pl.BlockSpec((Q, d), lambda i: (i, 0)),
                   pl.BlockSpec((8, d), lambda i: (0, 0)), wsp, pl.BlockSpec((Q, LANES), lambda i: (0, 0))],
        out_shape=[jax.ShapeDtypeStruct((t, 4 * d), BF16), jax.ShapeDtypeStruct((t, d), F32),
                   jax.ShapeDtypeStruct((8, d), F32), jax.ShapeDtypeStruct((groups, Q, Q), F32),
                   jax.ShapeDtypeStruct((Q, LANES), F32)],
        scratch_shapes=[pltpu.VMEM((1, d), F32), pltpu.VMEM((Q, d), F32)],
        compiler_params=_params("arbitrary"),
    )(z_mid, y2, xbc, d_ycat, dsk_row, g_ssd, g_v, g_mlp, w_s, w_st, b_st, ind_head, ind_group)


def _ada_fwd(c16, w_ada, b_loc):
    depth, d, n = w_ada.shape
    tn = _tile(n, (512, 256, 128))

    def body(c_ref, w_ref, b_ref, o_ref):
        cv = c_ref[...]
        o_ref[0] = _dot(cv * _sigmoid(cv), w_ref[0], NN, HI) + b_ref[0]

    return pl.pallas_call(
        body, name="ada_fwd", grid=(depth, n // tn),
        in_specs=[pl.BlockSpec((16, d), lambda l, j: (0, 0)), pl.BlockSpec((1, d, tn), lambda l, j: (l, 0, j)),
                  pl.BlockSpec((1, 1, tn), lambda l, j: (l, 0, j))],
        out_specs=pl.BlockSpec((1, 16, tn), lambda l, j: (l, 0, j)),
        out_shape=jax.ShapeDtypeStruct((depth, 16, n), F32), compiler_params=_params("arbitrary", "arbitrary"),
    )(c16, w_ada, b_loc)


def _ada_bwd(c_t, dm_loc, w_ada):
    depth, d, n = w_ada.shape
    tn = _tile(n, (512, 256, 128))

    def body(s_ref, dm_ref, w_ref, gw_ref, dsc_ref):
        @pl.when((pl.program_id(0) == 0) & (pl.program_id(1) == 0))
        def _():
            dsc_ref[...] = jnp.zeros_like(dsc_ref)

        cv = s_ref[...]
        gw_ref[0] = _dot(cv * _sigmoid(cv), dm_ref[0], NN, HI)
        dsc_ref[...] += _dot(dm_ref[0, 8:16, :], w_ref[0], NT, HI)

    return pl.pallas_call(
        body, name="ada_bwd", grid=(depth, n // tn),
        in_specs=[pl.BlockSpec((d, LANES), lambda l, j: (0, 0)), pl.BlockSpec((1, LANES, tn), lambda l, j: (l, 0, j)),
                  pl.BlockSpec((1, d, tn), lambda l, j: (l, 0, j))],
        out_specs=[pl.BlockSpec((1, d, tn), lambda l, j: (l, 0, j)), pl.BlockSpec((8, d), lambda l, j: (0, 0))],
        out_shape=[jax.ShapeDtypeStruct((depth, d, n), F32), jax.ShapeDtypeStruct((8, d), F32)],
        compiler_params=_params("arbitrary", "arbitrary"),
    )(c_t, dm_loc, w_ada)


def _rowsum(x):
    depth, r, n = x.shape

    def body(x_ref, o_ref):
        o_ref[0] = jnp.sum(x_ref[0], axis=0, keepdims=True)

    return pl.pallas_call(
        body, name="rowsum", grid=(depth,),
        in_specs=[pl.BlockSpec((1, r, n), lambda l: (l, 0, 0))], out_specs=pl.BlockSpec((1, 1, n), lambda l: (l, 0, 0)),
        out_shape=jax.ShapeDtypeStruct((depth, 1, n), F32), compiler_params=_params("arbitrary"),
    )(x)


def _cctx_grad(d_scc, c_ctx_row):
    def body(g_ref, c_ref, o_ref):
        cv = c_ref[...]
        sig = _sigmoid(cv)
        o_ref[...] = g_ref[...] * (sig * (1.0 + cv * (1.0 - sig)))

    return pl.pallas_call(body, name="cctx_grad", out_shape=jax.ShapeDtypeStruct(c_ctx_row.shape, F32))(d_scc, c_ctx_row)


def _sum_lead(x, name):
    k, r, c = x.shape
    tr = _tile(r, [tt for tt in (1024, 512, 256, 128, 64, 32, 16, 8) if k * tt * c * x.dtype.itemsize <= SUM_BLOCK_BYTES])

    def body(x_ref, o_ref):
        acc = x_ref[0].astype(F32)
        for e in range(1, k):
            acc = acc + x_ref[e].astype(F32)
        o_ref[...] = acc

    return pl.pallas_call(
        body, name=name, grid=(r // tr,),
        in_specs=[pl.BlockSpec((k, tr, c), lambda i: (0, i, 0))], out_specs=pl.BlockSpec((tr, c), lambda i: (i, 0)),
        out_shape=jax.ShapeDtypeStruct((r, c), F32), compiler_params=_params("arbitrary"),
    )(x)


def _adamw(w, g, m, v, name, g2=None, side=None):
    r, c = w.shape
    tr = _tile(r, [tt for tt in (2048, 1024, 512, 256, 128, 64, 32, 16, 8) if tt * c * 4 <= ADAM_BLOCK_BYTES])
    two = g2 is not None
    bc1 = 1.0 - ADAM_B1 ** ADAM_STEP
    bc2 = 1.0 - ADAM_B2 ** ADAM_STEP

    def body(*refs):
        if two:
            w_ref, g_ref, g2_ref, m_ref, v_ref, go_ref, d_ref, mo_ref, vo_ref = refs
            gr = g_ref[...] + g2_ref[...]
        else:
            w_ref, g_ref, m_ref, v_ref, go_ref, d_ref, mo_ref, vo_ref = refs
            gr = g_ref[...]
        mn = ADAM_B1 * m_ref[...] + (1.0 - ADAM_B1) * gr
        vn = ADAM_B2 * v_ref[...] + (1.0 - ADAM_B2) * (gr * gr)
        go_ref[...] = gr
        mo_ref[...] = mn
        vo_ref[...] = vn
        d_ref[...] = -ADAM_LR * ((mn / bc1) / (jnp.sqrt(vn / bc2) + ADAM_EPS) + ADAM_WD * w_ref[...])

    blk = pl.BlockSpec((tr, c), lambda i: (i, 0))
    ins = (w, g, g2, m, v) if two else (w, g, m, v)
    return _hosted_call(body, side, name=name, grid=(r // tr,), in_specs=[blk] * len(ins), out_specs=[blk] * 4,
                        out_shape=[jax.ShapeDtypeStruct((r, c), F32)] * 4, scratch_shapes=[], args=ins)


def _flip(pos, k):
    x, y, c = pos
    return (x ^ ((k >> 2) & 1), y ^ ((k >> 1) & 1), c ^ (k & 1))


def _lin(pos):
    return 4 * pos[0] + 2 * pos[1] + pos[2]


def _chip(pos):
    return 2 * pos[0] + pos[1]


def _here():
    return (lax.axis_index("x"), lax.axis_index("y"), lax.axis_index("c"))


class _Exchange:
    def __init__(self, x, kind):
        self.x, self.kind = x, kind
        self.masks = {"gather4": (2, 4, 6), "scatter4": (2, 4, 6), "gather8": tuple(range(1, 8)), "swap": (1,)}[kind]
        self.slot = {"gather4": _chip, "scatter4": _chip, "gather8": _lin, "swap": None}[kind]
        lead = {"gather4": (4,), "scatter4": (), "gather8": (8,), "swap": ()}[kind]
        self.out_shape = jax.ShapeDtypeStruct(lead + x.shape, x.dtype)
        n = len(self.masks)
        self.scratch = [pltpu.SemaphoreType.DMA((n,)), pltpu.SemaphoreType.DMA((n,))] + ([] if kind == "swap" else [pltpu.SemaphoreType.DMA])

    def _copies(self, x_ref, o_ref, send, recv, *own, arrivals):
        me = _here()
        src = (lambda pos: x_ref.at[_chip(pos)]) if self.kind == "scatter4" else (lambda pos: x_ref)
        dst = (lambda pos: o_ref.at[self.slot(pos)]) if self.slot else (lambda pos: o_ref)
        local = [pltpu.make_async_copy(src(me), dst(me), own[0])] if own else []
        outs, ins = [], []
        for j, k in enumerate(self.masks):
            peer = _flip(me, k)
            sems = dict(send_sem=send.at[j], recv_sem=recv.at[j], device_id=peer, device_id_type=MESH)
            outs.append(pltpu.make_async_remote_copy(src_ref=src(peer), dst_ref=dst(me), **sems))
            if arrivals:
                ins.append(pltpu.make_async_remote_copy(src_ref=src(me), dst_ref=dst(peer), **sems))
        return local, outs, ins

    def start(self, *refs):
        local, outs, _ = self._copies(*refs, arrivals=False)
        for cp in local + outs:
            cp.start()

    def wait(self, *refs):
        local, outs, ins = self._copies(*refs, arrivals=True)
        for cp in ins:
            cp.wait_recv()
        for cp in outs:
            cp.wait_send()
        for cp in local:
            cp.wait()


def _exchange(x, kind, name):
    ex = _Exchange(x, kind)

    def body(*refs):
        ex.start(*refs)
        ex.wait(*refs)

    return pl.pallas_call(body, name=name, in_specs=[ANY], out_specs=ANY, out_shape=ex.out_shape, scratch_shapes=ex.scratch)(x)


def _hosted_call(body, side, *, name, grid, in_specs, out_specs, out_shape, scratch_shapes, args):
    sides = [] if side is None else list(side) if isinstance(side, (list, tuple)) else [side]
    n_in, n_out, ns = len(in_specs), len(out_specs), len(sides)
    params = _params(*(["arbitrary"] * len(grid)))
    if not sides:
        return pl.pallas_call(body, name=name, grid=grid, in_specs=in_specs, out_specs=out_specs, out_shape=out_shape,
                              scratch_shapes=scratch_shapes, compiler_params=params)(*args)
    n_sem = [len(s.scratch) for s in sides]

    def hosted(*refs):
        ins, xs = refs[:n_in], refs[n_in:n_in + ns]
        outs, os_ = refs[n_in + ns:n_in + ns + n_out], refs[n_in + ns + n_out:n_in + 2 * ns + n_out]
        rest = refs[n_in + 2 * ns + n_out:]
        scratch, sems = rest[:len(rest) - sum(n_sem)], list(rest[len(rest) - sum(n_sem):])
        per_side = [[sems.pop(0) for _ in range(k)] for k in n_sem]
        ids = [pl.program_id(a) for a in range(len(grid))]
        first, last = ids[0] == 0, ids[0] == grid[0] - 1
        for a in range(1, len(grid)):
            first, last = first & (ids[a] == 0), last & (ids[a] == grid[a] - 1)

        @pl.when(first)
        def _():
            for s, x_ref, o_ref, sm in zip(sides, xs, os_, per_side):
                s.start(x_ref, o_ref, *sm)

        body(*ins, *outs, *scratch)

        @pl.when(last)
        def _():
            for s, x_ref, o_ref, sm in zip(sides, xs, os_, per_side):
                s.wait(x_ref, o_ref, *sm)

    return pl.pallas_call(hosted, name=name + "_x_" + "_".join(s.kind for s in sides), grid=grid, in_specs=list(in_specs) + [ANY] * ns,
                          out_specs=list(out_specs) + [ANY] * ns, out_shape=list(out_shape) + [s.out_shape for s in sides],
                          scratch_shapes=list(scratch_shapes) + [sem for s in sides for sem in s.scratch],
                          compiler_params=params)(*args, *[s.x for s in sides])


def _pad_lanes(a, width):
    return jnp.pad(a, [(0, 0)] * (a.ndim - 1) + [(0, width - a.shape[-1])])


def kernel(x, c, ctx, c_ctx, w_ada, b_ada, g_pre, g_post, w_in, conv_w, conv_b, dt_bias, a_log, d_skip, g_ssd, g_v, w_s, b_s, g_mlp, w_out, loss_target, m_c_ctx, m_w_ada, m_b_ada, m_g_pre, m_g_post, m_w_in, m_conv_w, m_conv_b, m_dt_bias, m_a_log, m_d_skip, m_g_ssd, m_g_v, m_w_s, m_b_s, m_g_mlp, m_w_out, v_c_ctx, v_w_ada, v_b_ada, v_g_pre, v_g_post, v_w_in, v_conv_w, v_conv_b, v_dt_bias, v_a_log, v_d_skip, v_g_ssd, v_g_v, v_w_s, v_b_s, v_g_mlp, v_w_out):
    depth, d = g_pre.shape
    seq, ctx_len = x.shape[1], ctx.shape[1]
    heads = d // HP
    in_w = 6 * d + 2 * heads
    groups_mlp = d // LANES
    t = ctx_len + seq
    assert ctx_len == TB and seq % TB == 0 and TB % ROW == 0 and heads % 4 == 0 and heads <= LANES and d % LANES == 0
    assert w_in.shape == (depth, d, in_w // 4)

    xi, yi, ci = lax.axis_index("x"), lax.axis_index("y"), lax.axis_index("c")
    chip = 2 * xi + yi
    me = 4 * xi + 2 * yi + ci

    n_ada = 3 * d // 4
    c_all = _exchange(c, "gather8", "ag_c")[:, 0, :]
    c16 = jnp.concatenate([c_all, c_ctx[None, :], jnp.zeros((7, d), F32)], axis=0)
    b_loc = lax.dynamic_slice_in_dim(b_ada, chip * n_ada, n_ada, axis=1)[:, None, :]
    mods_loc = _exchange(_ada_fwd(c16, w_ada, b_loc).reshape(depth * 16, n_ada), "gather8", "ag_mods")
    mods_loc = mods_loc.reshape(4, 2, depth, 16, n_ada)[:, 0]
    mods_full = jnp.moveaxis(mods_loc, 0, 2).reshape(depth, 16, 3 * d)
    mods_x = lax.dynamic_index_in_dim(mods_full, me, axis=1, keepdims=False).reshape(depth, 3, d)
    mods_c = mods_full[:, 8, :].reshape(depth, 3, d)
    mods = _pad_rows8(jnp.stack([mods_c, mods_x], axis=1))

    w_in_b, w_out_b = w_in.astype(BF16), w_out.astype(BF16)

    def lay_out(w_in_rows):
        full = jnp.moveaxis(jnp.concatenate(w_in_rows, axis=1), 0, 1).reshape(d, in_w)
        w_dt_l = jnp.concatenate([_pad_lanes(full[:, 2 * d:2 * d + heads], LANES),
                                  _pad_lanes(full[:, 2 * d + heads:2 * d + 2 * heads], LANES)], axis=1)
        return full[:, :2 * d], full[:, 2 * d + 2 * heads:], w_dt_l

    w_in_rows = [_exchange(w_in_b[0], "gather4", "ag_w_in")]
    r_scan = 3 * d // 4
    conv_w_full = jnp.moveaxis(_exchange(conv_w, "gather4", "ag_conv_w"), 0, 2).reshape(depth, CONV_TAPS, 2 * d)
    conv_w8 = jnp.pad(conv_w_full, ((0, 0), (0, 8 - CONV_TAPS), (0, 0)))

    tri = jnp.stack([jnp.tril(jnp.ones((Q, Q), F32)), jnp.triu(jnp.ones((Q, Q), F32))])
    tri_t = jnp.swapaxes(tri, 1, 2)
    dtb = _pad_lanes(dt_bias, LANES)[:, :, None, :]
    alog = _pad_lanes(a_log, LANES)[:, :, None, :]
    dsk_row = jnp.repeat(d_skip, HP, axis=1)[:, None, :]
    w_st = jnp.swapaxes(w_s, 2, 3)
    b_st = _pad_lanes(jnp.swapaxes(b_s, 1, 2), LANES)
    chan = jnp.arange(d)
    ind_head = (chan[:, None] // HP == jnp.arange(LANES)[None, :]).astype(F32)
    ind_b, ind_t = ind_head.astype(BF16), ind_head.T.astype(BF16)
    ind_group = (chan[:, None] // LANES == jnp.arange(LANES)[None, :]).astype(F32)

    stream = jnp.concatenate([ctx[0], x[0]], axis=0)
    saved = []
    for l in range(depth):
        w_xbc, w_mid, w_dt = lay_out(w_in_rows)
        more = l + 1 < depth
        hx, hx_t = _pre_fwd(stream, g_pre[l][None], mods[l])
        z_xbc = _mm(hx, w_xbc, NN, "in_xbc")
        z_mid, w_out_all = _mm(hx, w_mid, NN, "in_mid", side=_Exchange(w_out_b[l], "gather4"))
        w_o = w_out_all.reshape(2 * d, d)
        z_dt = _mm(hx, w_dt, NN, "in_dt")
        xbc, dsilu = _conv_fwd(z_xbc, conv_w8[l], conv_b[l][None], ctx_len)
        y2, hs, *rows_a = _ssd_fwd(xbc, z_dt, dtb[l], alog[l], tri, ind_t, d, ctx_len,
                                   side=_Exchange(w_in_b[l + 1, :r_scan], "gather4") if more else None)
        ycat, ycat_t = _mix_fwd(z_mid, y2, xbc, dsk_row[l], g_ssd[l][None], g_v[l][None], g_mlp[l][None], w_s[l], b_st[l])
        if more:
            o, rows_b = _mm(ycat, w_o, NN, "out_proj", side=_Exchange(w_in_b[l + 1, r_scan:], "gather4"))
            w_in_rows = [rows_a[0], rows_b]
        else:
            o = _mm(ycat, w_o, NN, "out_proj")
        saved.append((stream, hx_t, z_xbc, dsilu, z_mid, z_dt, xbc, y2, hs, ycat_t, o, w_xbc, w_mid, w_dt, w_o))
        stream = _post_fwd(o, stream, g_post[l][None], mods[l])

    sq, d_stream = _loss_grad(stream, loss_target[0])
    loss = lax.psum(0.5 / d * sq[0, 0], ("x", "y", "c"))

    small = []
    dmods = []
    q_in = in_w // 4
    names = ["g_pre", "g_post", "conv_w", "conv_b", "dt_bias", "a_log", "d_skip", "g_ssd", "g_v", "w_s", "b_s", "g_mlp"]

    def pack(arrays):
        f = jnp.concatenate([a.reshape(-1) for a in arrays])
        padn = -(-f.shape[0] // (PACK_ROWS * LANES)) * (PACK_ROWS * LANES)
        return jnp.pad(f, (0, padn - f.shape[0])).reshape(padn // LANES, LANES)

    sum_in, sum_out = [None] * depth, [None] * depth
    swap_in, swap_out = [None] * depth, [None] * depth
    small_sum = [None] * depth
    parts_in = packed = None
    for l in reversed(range(depth)):
        x_in, hx_t, z_xbc, dsilu, z_mid, z_dt, xbc, y2, hs, ycat_t, o, w_xbc, w_mid, w_dt, w_o = saved[l]
        up = l + 1
        d_o, acc_post = _post_bwd(d_stream, o, g_post[l][None], mods[l])
        if packed is not None:
            d_ycat, gathered = _mm(d_o, w_o, NT, "d_ycat", side=_Exchange(packed, "gather8"))
            small_sum[up] = _sum_lead(gathered, "sum_small")
        else:
            d_ycat = _mm(d_o, w_o, NT, "d_ycat")
        g_out = _mm(ycat_t, d_o, NN, "dw_out", out_dtype=BF16)
        dz_mid, d_y, vec, d_ws, d_bs = _mix_bwd(z_mid, y2, xbc, d_ycat, dsk_row[l], g_ssd[l][None], g_v[l][None], g_mlp[l][None],
                                                w_s[l], w_st[l], b_st[l], ind_head, ind_group)
        d_xbc2, dz_dt, d_bias, d_alog, *got = _ssd_bwd(xbc, z_dt, dtb[l], alog[l], tri, tri_t, ind_t, ind_b, d_y, y2, hs, d, ctx_len,
                                                       side=_Exchange(parts_in, "scatter4") if parts_in is not None else None)
        if parts_in is not None:
            sum_in[up] = _sum_lead(got[0], "sum_w_in")
        dz_xbc, d_cw, d_cb = _conv_bwd(z_xbc, dsilu, d_xbc2, d_y, dsk_row[l], conv_w8[l], ctx_len)
        g_xbc, got_out = _mm(hx_t, dz_xbc, NN, "dw_xbc", out_dtype=BF16, side=_Exchange(g_out.reshape(4, 2 * d // 4, d), "scatter4"))
        sum_out[l] = _sum_lead(got_out, "sum_w_out")
        swaps = [_Exchange(sum_out[l], "swap")] + ([_Exchange(sum_in[up], "swap")] if parts_in is not None else [])
        g_mid, swap_out[l], *swapped = _mm(hx_t, dz_mid, NN, "dw_mid", out_dtype=BF16, side=swaps)
        if parts_in is not None:
            swap_in[up] = swapped[0]
        g_dt = _mm(hx_t, dz_dt, NN, "dw_dt", out_dtype=BF16)
        g_in = jnp.concatenate([g_xbc, g_dt[:, :heads], g_dt[:, LANES:LANES + heads], g_mid], axis=1)
        parts_in = jnp.moveaxis(g_in.reshape(d, 4, q_in), 1, 0)
        if l == 0:
            r_a = max(LANES, d // 3 // LANES * LANES)
            d_hx, got_a = _mm(dz_xbc, w_xbc, NT, "dhx_xbc", side=_Exchange(parts_in[:, :r_a], "scatter4"))
            d_hx, got_b = _mm(dz_mid, w_mid, NT, "dhx_mid", acc=d_hx, side=_Exchange(parts_in[:, r_a:], "scatter4"))
            sum_in[0] = _sum_lead(jnp.concatenate([got_a, got_b], axis=1), "sum_w_in")
        else:
            d_hx = _mm(dz_xbc, w_xbc, NT, "dhx_xbc")
            d_hx = _mm(dz_mid, w_mid, NT, "dhx_mid", acc=d_hx)
        d_hx = _mm(dz_dt, w_dt, NT, "dhx_dt", acc=d_hx)
        d_stream, acc_pre, *swapped = _pre_bwd(x_in, d_hx, d_stream, g_pre[l][None], mods[l],
                                               side=_Exchange(sum_in[0], "swap") if l == 0 else None)
        if l == 0:
            swap_in[0] = swapped[0]
        dmods.append(jnp.concatenate([acc_pre[:, 0], acc_pre[:, 1], acc_post[:, 0]], axis=1))
        small.append(dict(
            g_pre=acc_pre[0, 2] + acc_pre[1, 2], g_post=acc_post[0, 1] + acc_post[1, 1], conv_w=d_cw[:CONV_TAPS], conv_b=d_cb[0],
            dt_bias=d_bias[:, 0, :heads], a_log=d_alog[:, 0, :heads], d_skip=vec[3, :heads], g_ssd=vec[0], g_v=vec[1],
            w_s=d_ws, b_s=d_bs[:, :groups_mlp].T, g_mlp=vec[2]))
        packed = pack([small[-1][n] for n in names])
    small.reverse(), dmods.reverse()
    grad_x = d_stream[ctx_len:][None]

    weights = dict(c_ctx=c_ctx, w_ada=w_ada, b_ada=b_ada, g_pre=g_pre, g_post=g_post, w_in=w_in, conv_w=conv_w, conv_b=conv_b,
                   dt_bias=dt_bias, a_log=a_log, d_skip=d_skip, g_ssd=g_ssd, g_v=g_v, w_s=w_s, b_s=b_s, g_mlp=g_mlp, w_out=w_out)
    m_in = dict(c_ctx=m_c_ctx, w_ada=m_w_ada, b_ada=m_b_ada, g_pre=m_g_pre, g_post=m_g_post, w_in=m_w_in, conv_w=m_conv_w,
                conv_b=m_conv_b, dt_bias=m_dt_bias, a_log=m_a_log, d_skip=m_d_skip, g_ssd=m_g_ssd, g_v=m_g_v, w_s=m_w_s,
                b_s=m_b_s, g_mlp=m_g_mlp, w_out=m_w_out)
    v_in = dict(c_ctx=v_c_ctx, w_ada=v_w_ada, b_ada=v_b_ada, g_pre=v_g_pre, g_post=v_g_post, w_in=v_w_in, conv_w=v_conv_w,
                conv_b=v_conv_b, dt_bias=v_dt_bias, a_log=v_a_log, d_skip=v_d_skip, g_ssd=v_g_ssd, g_v=v_g_v, w_s=v_w_s,
                b_s=v_b_s, g_mlp=v_g_mlp, w_out=v_w_out)
    order = list(weights)
    results = {}

    def adamw_big(n, ga, gb, side):
        shp = weights[n].shape
        two = lambda a: a.reshape(-1, shp[-1])
        res = _adamw(two(weights[n]), ga, two(m_in[n]), two(v_in[n]), "adamw_" + n, g2=gb, side=side)
        results[n] = [r.reshape(shp) for r in res[:4]]
        return res[4]

    gathered = adamw_big("w_in", jnp.concatenate(sum_in, axis=0), jnp.concatenate(swap_in, axis=0), _Exchange(packed, "gather8"))
    small_sum[0] = _sum_lead(gathered, "sum_small")
    grads = {}
    for j, n in enumerate(names):
        off = sum(small[0][m].size for m in names[:j])
        grads[n] = jnp.stack([small_sum[l].reshape(-1)[off:off + small[l][n].size].reshape(small[l][n].shape) for l in range(depth)])
    grads["conv_w"] = lax.dynamic_slice_in_dim(grads["conv_w"], chip * (2 * d // 4), 2 * d // 4, axis=2)

    dm_all = adamw_big("w_out", jnp.concatenate(sum_out, axis=0), jnp.concatenate(swap_out, axis=0),
                       _Exchange(jnp.stack(dmods).reshape(depth * 2, 3 * d), "gather8")).reshape(8, depth, 2, 3 * d)
    dm_ctx = _sum_lead(dm_all[:, :, 0, :], "sum_dm_ctx")
    dm16 = jnp.concatenate([jnp.moveaxis(dm_all[:, :, 1, :], 0, 1), dm_ctx[:, None, :], jnp.zeros((depth, 7, 3 * d), F32)], axis=1)
    grads["b_ada"] = _rowsum(dm16)[:, 0, :]
    dm_loc = jnp.pad(lax.dynamic_slice_in_dim(dm16, chip * n_ada, n_ada, axis=2), ((0, 0), (0, LANES - 16), (0, 0)))
    c_t = jnp.pad(c16.T, ((0, 0), (0, LANES - 16)))
    g_w_ada, d_scc_part = _ada_bwd(c_t, dm_loc, w_ada)
    d_scc_all = adamw_big("w_ada", g_w_ada.reshape(depth * d, n_ada), None, _Exchange(d_scc_part, "gather8"))
    d_scc = _sum_lead(d_scc_all.reshape(4, 2, 8, d)[:, 0], "sum_dscc")
    grads["c_ctx"] = _cctx_grad(d_scc[0:1], c_ctx[None])[0]

    rest = [n for n in order if n not in results]
    outs = _adamw(pack([weights[n] for n in rest]), pack([grads[n] for n in rest]), pack([m_in[n] for n in rest]),
                  pack([v_in[n] for n in rest]), "adamw_small")
    off = 0
    for n in rest:
        size, shp = weights[n].size, weights[n].shape
        results[n] = [o_.reshape(-1)[off:off + size].reshape(shp) for o_ in outs]
        off += size

    return (loss, grad_x, *[results[n][0] for n in order], *[results[n][1] for n in order],
            *[results[n][2] for n in order], *[results[n][3] for n in order])


def _pad_rows8(a):
    return jnp.pad(a, [(0, 0)] * (a.ndim - 2) + [(0, 8 - a.shape[-2]), (0, 0)])
```

```python
import jax
import jax.numpy as jnp
from jax import lax
from jax.experimental import pallas as pl
from jax.experimental.pallas import tpu as pltpu

F32 = jnp.float32
BF16 = jnp.bfloat16
EPS = 1e-6
Q = 128
TB = 256
ROW = 64
HP = 64
LANES = 128
CONV_TAPS = 5
VMEM_LIMIT = 48 * 1024 * 1024
HI = lax.Precision.HIGHEST
SUM_BLOCK_BYTES = 4 * 1024 * 1024
ADAM_BLOCK_BYTES = 1024 * 1024
PACK_ROWS = 256
MESH = pl.DeviceIdType.MESH
ANY = pl.BlockSpec(memory_space=pl.ANY)

ADAM_LR, ADAM_B1, ADAM_B2, ADAM_EPS, ADAM_WD, ADAM_STEP = 0.001, 0.9, 0.999, 1e-08, 0.01, 10

NN = (((1,), (0,)), ((), ()))
NT = (((1,), (1,)), ((), ()))
TN = (((0,), (0,)), ((), ()))


def _dot(a, b, dims=NN, prec=None):
    return lax.dot_general(a, b, dims, precision=prec, preferred_element_type=F32)


def _params(*sem):
    if sem:
        return pltpu.CompilerParams(vmem_limit_bytes=VMEM_LIMIT, dimension_semantics=sem)
    return pltpu.CompilerParams(vmem_limit_bytes=VMEM_LIMIT)


def _tile(dim, cands):
    for t in cands:
        if dim % t == 0:
            return t
    return dim


def _sigmoid(x):
    return 1.0 / (1.0 + jnp.exp(-x))


def _softplus(x):
    e = jnp.exp(-jnp.abs(x))
    u = 1.0 + e
    um1 = u - 1.0
    l1p = jnp.where(um1 == 0.0, e, jnp.log(u) * (e / jnp.where(um1 == 0.0, 1.0, um1)))
    return jnp.maximum(x, 0.0) + l1p


def _rms(x):
    return lax.rsqrt(jnp.mean(x * x, axis=-1, keepdims=True) + EPS)


def _rms_bwd(x, r, t):
    return r * t - x * (r * r * r) * jnp.mean(x * t, axis=-1, keepdims=True)


def _mm(a, b, dims, name, acc=None, out_dtype=F32, side=None):
    (ca,), (cb,) = dims[0]
    m, k = a.shape[1 - ca], a.shape[ca]
    n = b.shape[1 - cb]
    tm = _tile(m, (1024, 768, 512, 384, 256, 128))
    tn = _tile(n, (1024, 512, 256, 128))
    tk = k if k <= 2048 else _tile(k, (2048, 768, 512, 384, 256, 128))
    nk = k // tk
    a_spec = pl.BlockSpec((tm, tk), lambda i, j, kk: (i, kk)) if ca == 1 else pl.BlockSpec((tk, tm), lambda i, j, kk: (kk, i))
    b_spec = pl.BlockSpec((tk, tn), lambda i, j, kk: (kk, j)) if cb == 0 else pl.BlockSpec((tn, tk), lambda i, j, kk: (j, kk))
    o_spec = pl.BlockSpec((tm, tn), lambda i, j, kk: (i, j))
    has_acc = acc is not None

    def body(*refs):
        if has_acc:
            a_ref, b_ref, c_ref, o_ref, acc_ref = refs
        else:
            a_ref, b_ref, o_ref, acc_ref = refs
        kk = pl.program_id(2)

        @pl.when(kk == 0)
        def _():
            acc_ref[...] = c_ref[...] if has_acc else jnp.zeros_like(acc_ref)

        acc_ref[...] += _dot(a_ref[...].astype(BF16), b_ref[...].astype(BF16), dims)

        @pl.when(kk == nk - 1)
        def _():
            o_ref[...] = acc_ref[...].astype(out_dtype)

    res = _hosted_call(
        body, side, name=name, grid=(m // tm, n // tn, nk),
        in_specs=[a_spec, b_spec] + ([o_spec] if has_acc else []),
        out_specs=[o_spec], out_shape=[jax.ShapeDtypeStruct((m, n), out_dtype)],
        scratch_shapes=[pltpu.VMEM((tm, tn), F32)], args=(a, b, acc) if has_acc else (a, b))
    return res[0] if side is None else res


def _which(i):
    return jnp.minimum(i, 1)


def _pre_fwd(x, g_pre, mods):
    t, d = x.shape

    def body(x_ref, g_ref, m_ref, o_ref, ot_ref):
        xb = x_ref[...]
        xn = xb * _rms(xb) * g_ref[...]
        hx = xn * (1.0 + m_ref[0, 1:2, :]) + m_ref[0, 0:1, :]
        o_ref[...] = hx.astype(BF16)
        ot_ref[...] = hx.T.astype(BF16)

    return pl.pallas_call(
        body, name="pre_fwd", grid=(t // TB,),
        in_specs=[pl.BlockSpec((TB, d), lambda i: (i, 0)), pl.BlockSpec((1, d), lambda i: (0, 0)),
                  pl.BlockSpec((1, 8, d), lambda i: (_which(i), 0, 0))],
        out_specs=[pl.BlockSpec((TB, d), lambda i: (i, 0)), pl.BlockSpec((d, TB), lambda i: (0, i))],
        out_shape=[jax.ShapeDtypeStruct((t, d), BF16), jax.ShapeDtypeStruct((d, t), BF16)], compiler_params=_params("arbitrary"),
    )(x, g_pre, mods)


def _pre_bwd(x, d_hx, d_up, g_pre, mods, side=None, latent_only=False):
    t, d = x.shape
    dx_rows = t - TB if latent_only else t
    dx_spec = pl.BlockSpec((TB, d), (lambda i: (jnp.maximum(i - 1, 0), 0)) if latent_only else (lambda i: (i, 0)))

    def body(x_ref, dh_ref, du_ref, g_ref, m_ref, dx_ref, acc_ref):
        i = pl.program_id(0)

        @pl.when(i <= 1)
        def _():
            acc_ref[...] = jnp.zeros_like(acc_ref)

        xb = x_ref[...]
        dh = dh_ref[...]
        r = _rms(xb)
        xr = xb * r
        d_xn = dh * (1.0 + m_ref[0, 1:2, :])
        dx_ref[...] = du_ref[...] + _rms_bwd(xb, r, d_xn * g_ref[...])
        acc_ref[0, 0:1, :] += jnp.sum(dh, axis=0, keepdims=True)
        acc_ref[0, 1:2, :] += jnp.sum(dh * (xr * g_ref[...]), axis=0, keepdims=True)
        acc_ref[0, 2:3, :] += jnp.sum(d_xn * xr, axis=0, keepdims=True)

    blk = pl.BlockSpec((TB, d), lambda i: (i, 0))
    return _hosted_call(
        body, side, name="pre_bwd", grid=(t // TB,),
        in_specs=[blk, blk, blk, pl.BlockSpec((1, d), lambda i: (0, 0)),
                  pl.BlockSpec((1, 8, d), lambda i: (_which(i), 0, 0))],
        out_specs=[dx_spec, pl.BlockSpec((1, 8, d), lambda i: (_which(i), 0, 0))],
        out_shape=[jax.ShapeDtypeStruct((dx_rows, d), F32), jax.ShapeDtypeStruct((2, 8, d), F32)],
        scratch_shapes=[], args=(x, d_hx, d_up, g_pre, mods))


def _post_fwd(o, x, g_post, mods):
    t, d = x.shape

    def body(o_ref, x_ref, g_ref, m_ref, y_ref):
        ob = o_ref[...]
        y_ref[...] = x_ref[...] + m_ref[0, 2:3, :] * (ob * _rms(ob) * g_ref[...])

    blk = pl.BlockSpec((TB, d), lambda i: (i, 0))
    return pl.pallas_call(
        body, name="post_fwd", grid=(t // TB,),
        in_specs=[blk, blk, pl.BlockSpec((1, d), lambda i: (0, 0)), pl.BlockSpec((1, 8, d), lambda i: (_which(i), 0, 0))],
        out_specs=blk, out_shape=jax.ShapeDtypeStruct((t, d), F32), compiler_params=_params("arbitrary"),
    )(o, x, g_post, mods)


def _post_bwd(d_y, o, g_post, mods):
    t, d = o.shape

    def body(dy_ref, o_ref, g_ref, m_ref, do_ref, acc_ref):
        i = pl.program_id(0)

        @pl.when(i <= 1)
        def _():
            acc_ref[...] = jnp.zeros_like(acc_ref)

        ob = o_ref[...]
        dy = dy_ref[...]
        r = _rms(ob)
        orr = ob * r
        d_out = dy * m_ref[0, 2:3, :]
        do_ref[...] = _rms_bwd(ob, r, d_out * g_ref[...]).astype(BF16)
        acc_ref[0, 0:1, :] += jnp.sum(dy * (orr * g_ref[...]), axis=0, keepdims=True)
        acc_ref[0, 1:2, :] += jnp.sum(d_out * orr, axis=0, keepdims=True)

    blk = pl.BlockSpec((TB, d), lambda i: (i, 0))
    return pl.pallas_call(
        body, name="post_bwd", grid=(t // TB,),
        in_specs=[blk, blk, pl.BlockSpec((1, d), lambda i: (0, 0)), pl.BlockSpec((1, 8, d), lambda i: (_which(i), 0, 0))],
        out_specs=[blk, pl.BlockSpec((1, 8, d), lambda i: (_which(i), 0, 0))],
        out_shape=[jax.ShapeDtypeStruct((t, d), BF16), jax.ShapeDtypeStruct((2, 8, d), F32)],
        compiler_params=_params("arbitrary"),
    )(d_y, o, g_post, mods)


def _loss_grad(xf, target):
    t, d = xf.shape

    def body(x_ref, t_ref, loss_ref, dx_ref):
        i = pl.program_id(0)

        @pl.when(i == 0)
        def _():
            loss_ref[...] = jnp.zeros_like(loss_ref)
            dx_ref[...] = jnp.zeros_like(dx_ref)

        @pl.when(i > 0)
        def _():
            err = x_ref[...] - t_ref[...]
            loss_ref[...] += jnp.sum(err * err).reshape(1, 1)
            dx_ref[...] = err * (1.0 / d)

    return pl.pallas_call(
        body, name="loss_grad", grid=(t // TB,),
        in_specs=[pl.BlockSpec((TB, d), lambda i: (i, 0)), pl.BlockSpec((TB, d), lambda i: (jnp.maximum(i - 1, 0), 0))],
        out_specs=[pl.BlockSpec((1, 1), lambda i: (0, 0)), pl.BlockSpec((TB, d), lambda i: (i, 0))],
        out_shape=[jax.ShapeDtypeStruct((1, 1), F32), jax.ShapeDtypeStruct((t, d), F32)],
        compiler_params=_params("arbitrary"),
    )(xf, target)


def _conv_terms(zb, pos, row_len):
    out = []
    for k in range(CONV_TAPS):
        o = k - CONV_TAPS // 2
        sh = zb if o == 0 else pltpu.roll(zb, (-o) % TB, 0)
        out.append(jnp.where((pos + o >= 0) & (pos + o < row_len), sh, 0.0))
    return out


def _row_pos(i, ctx_len):
    row_len = jnp.where(i == 0, ctx_len, ROW)
    pos = lax.broadcasted_iota(jnp.int32, (TB, 1), 0) & (row_len - 1)
    return pos, row_len


def _conv_fwd(z_xbc, conv_w8, conv_b, ctx_len):
    t, c = z_xbc.shape
    tc = _tile(c, (1024, 512, 256, 128))

    def body(z_ref, w_ref, b_ref, o_ref, ds_ref):
        pos, row_len = _row_pos(pl.program_id(1), ctx_len)
        terms = _conv_terms(z_ref[...], pos, row_len)
        pre = b_ref[...]
        for k in range(CONV_TAPS):
            pre = pre + terms[k] * w_ref[k:k + 1, :]
        sig = _sigmoid(pre)
        o_ref[...] = pre * sig
        ds_ref[...] = sig * (1.0 + pre * (1.0 - sig))

    blk = pl.BlockSpec((TB, tc), lambda j, i: (i, j))
    return pl.pallas_call(
        body, name="conv_fwd", grid=(c // tc, t // TB),
        in_specs=[blk, pl.BlockSpec((8, tc), lambda j, i: (0, j)), pl.BlockSpec((1, tc), lambda j, i: (0, j))],
        out_specs=[blk, blk], out_shape=[jax.ShapeDtypeStruct((t, c), F32)] * 2, compiler_params=_params("arbitrary", "arbitrary"),
    )(z_xbc, conv_w8, conv_b)


def _conv_bwd(z_xbc, dsilu, d_xbc2, d_y, d_skip_row, conv_w8, ctx_len):
    t, c = z_xbc.shape
    d = d_y.shape[1]
    tc = _tile(d, (1024, 512, 256, 128))
    nskip = d // tc

    def body(z_ref, dsl_ref, g2_ref, dy_ref, ds_ref, w_ref, dz_ref, dw_ref, db_ref):
        j, i = pl.program_id(0), pl.program_id(1)

        @pl.when(i == 0)
        def _():
            dw_ref[...] = jnp.zeros_like(dw_ref)
            db_ref[...] = jnp.zeros_like(db_ref)

        pos, row_len = _row_pos(i, ctx_len)
        skip = jnp.where(j < nskip, 1.0, 0.0) * ds_ref[...]
        d_pre = (g2_ref[0] + g2_ref[1] + dy_ref[...] * skip) * dsl_ref[...]
        db_ref[...] += jnp.sum(d_pre, axis=0, keepdims=True)
        zb = z_ref[...]
        dz = jnp.zeros_like(d_pre)
        for k in range(CONV_TAPS):
            o = k - CONV_TAPS // 2
            sh = d_pre if o == 0 else pltpu.roll(d_pre, o % TB, 0)
            sh = jnp.where((pos - o >= 0) & (pos - o < row_len), sh, 0.0)
            dw_ref[k:k + 1, :] += jnp.sum(sh * zb, axis=0, keepdims=True)
            dz = dz + sh * w_ref[k:k + 1, :]
        dz_ref[...] = dz.astype(BF16)

    jd = lambda j: jnp.minimum(j, nskip - 1)
    blk = pl.BlockSpec((TB, tc), lambda j, i: (i, j))
    return pl.pallas_call(
        body, name="conv_bwd", grid=(c // tc, t // TB),
        in_specs=[blk, blk, pl.BlockSpec((2, TB, tc), lambda j, i: (0, i, j)),
                  pl.BlockSpec((TB, tc), lambda j, i: (i, jd(j))), pl.BlockSpec((1, tc), lambda j, i: (0, jd(j))),
                  pl.BlockSpec((8, tc), lambda j, i: (0, j))],
        out_specs=[blk, pl.BlockSpec((8, tc), lambda j, i: (0, j)), pl.BlockSpec((1, tc), lambda j, i: (0, j))],
        out_shape=[jax.ShapeDtypeStruct((t, c), BF16), jax.ShapeDtypeStruct((8, c), F32), jax.ShapeDtypeStruct((1, c), F32)],
        compiler_params=_params("arbitrary", "arbitrary"),
    )(z_xbc, dsilu, d_xbc2, d_y, d_skip_row, conv_w8)


def _scan_chunk(dirn, s, nch, ncc):
    bwd = jnp.where(s < ncc, ncc - 1 - s, nch - 1 - (s - ncc))
    return jnp.where(dirn == 0, s, bwd)


def _ssd_decays(dt_ref, dtb_ref, alog_ref, tri):
    raw = dt_ref[...] + dtb_ref[0]
    dt = _softplus(raw)
    a_neg = -jnp.exp(alog_ref[0])
    a = dt * a_neg
    s = _dot(tri, a, NN, HI)
    stot = jnp.sum(a, axis=0, keepdims=True)
    return raw, dt, a_neg, s, stot, s.T


def _split(v):
    hi = v.astype(BF16)
    return hi, (v - hi.astype(F32)).astype(BF16)


def _expand(v, indt_ref):
    hi, lo = _split(v)
    return _dot(hi, indt_ref[...]) + _dot(lo, indt_ref[...])


def _head_sums(v, ind_ref):
    hi, lo = _split(v)
    return _dot(hi, ind_ref[...]) + _dot(lo, ind_ref[...])


def _ssd_fwd(xbc, z_dt, dtb, alog, tri, ind_t, d, ctx_len, side=None):
    t = xbc.shape[0]
    nch, ncc = t // Q, ctx_len // Q
    heads = d // HP
    groups = heads // 4
    gn = groups * LANES

    def body(xbc_ref, dt_ref, dtb_ref, alog_ref, tri_ref, indt_ref, y_ref, hs_ref, h_scr, xdb_scr, xde_scr, esx_scr):
        @pl.when(pl.program_id(1) == 0)
        def _():
            h_scr[...] = jnp.zeros_like(h_scr)

        tri = tri_ref[0]
        mask = tri > 0.0
        _, dt, _, s, stot, s_t = _ssd_decays(dt_ref, dtb_ref, alog_ref, tri)
        esx_scr[...] = _expand(jnp.exp(s), indt_ref)
        etot_x = _expand(jnp.broadcast_to(jnp.exp(stot), (8, LANES)), indt_ref)[0:1]
        xd = xbc_ref[:, :d] * _expand(dt, indt_ref)
        xdb_scr[...] = xd.astype(BF16)
        xde_scr[...] = (xd * _expand(jnp.exp(stot - s), indt_ref)).astype(BF16)
        left = lax.broadcasted_iota(jnp.int32, (Q, LANES), 1) < HP
        hs_ref[0, 0] = h_scr[...]
        for g in range(groups):
            b32 = xbc_ref[:, d + g * LANES:d + (g + 1) * LANES]
            bb = b32.astype(BF16)
            bbt = b32.T.astype(BF16)
            cb = xbc_ref[:, d + gn + g * LANES:d + gn + (g + 1) * LANES].astype(BF16)
            cbt = _dot(cb, bb, NT)
            gcols = slice(4 * g * HP, 4 * (g + 1) * HP)
            hg = h_scr[:, gcols]
            y_off = _dot(cb, hg.astype(BF16)) * esx_scr[:, gcols]
            h_scr[:, gcols] = hg * etot_x[:, gcols] + _dot(bbt, xde_scr[:, gcols])
            for j, pr in enumerate((2 * g, 2 * g + 1)):
                h0 = 2 * pr
                cols = slice(pr * LANES, (pr + 1) * LANES)
                xdb = xdb_scr[:, cols]
                res = []
                for h in (h0, h0 + 1):
                    lm = jnp.exp(jnp.where(mask, s[:, h:h + 1] - s_t[h:h + 1, :], -jnp.inf))
                    res.append(_dot((cbt * lm).astype(BF16), xdb))
                y_ref[0, :, cols] = jnp.where(left, res[0], res[1]) + y_off[:, j * LANES:(j + 1) * LANES]

    cidx = lambda dd, ss: _scan_chunk(dd, ss, nch, ncc)
    return _hosted_call(
        body, side, name="ssd_fwd", grid=(2, nch),
        in_specs=[pl.BlockSpec((Q, 2 * d), lambda dd, ss: (cidx(dd, ss), 0)),
                  pl.BlockSpec((Q, LANES), lambda dd, ss: (cidx(dd, ss), dd)),
                  pl.BlockSpec((1, 1, LANES), lambda dd, ss: (dd, 0, 0)),
                  pl.BlockSpec((1, 1, LANES), lambda dd, ss: (dd, 0, 0)),
                  pl.BlockSpec((1, Q, Q), lambda dd, ss: (dd, 0, 0)),
                  pl.BlockSpec((LANES, d), lambda dd, ss: (0, 0))],
        out_specs=[pl.BlockSpec((1, Q, d), lambda dd, ss: (dd, cidx(dd, ss), 0)),
                   pl.BlockSpec((1, 1, LANES, d), lambda dd, ss: (dd, cidx(dd, ss), 0, 0))],
        out_shape=[jax.ShapeDtypeStruct((2, t, d), F32), jax.ShapeDtypeStruct((2, nch, LANES, d), F32)],
        scratch_shapes=[pltpu.VMEM((LANES, d), F32), pltpu.VMEM((Q, d), BF16), pltpu.VMEM((Q, d), BF16), pltpu.VMEM((Q, d), F32)],
        args=(xbc, z_dt, dtb, alog, tri, ind_t))


def _ssd_bwd(xbc, z_dt, dtb, alog, tri, tri_t, ind_t, ind, d_y, y2, hs, d, ctx_len, side=None):
    t = xbc.shape[0]
    nch, ncc = t // Q, ctx_len // Q
    heads = d // HP
    groups = heads // 4
    gn = groups * LANES

    def body(xbc_ref, dt_ref, dtb_ref, alog_ref, tri_ref, trit_ref, indt_ref, ind_ref, dy_ref, y_ref, hs_ref,
             dx_ref, dzdt_ref, dbias_ref, dalog_ref, dh_scr, dtx_scr, ex_scr, xdb_scr, xde_scr, dyb_scr, dye_scr, dxd_scr, bdh_scr):
        @pl.when(pl.program_id(1) == 0)
        def _():
            dh_scr[...] = jnp.zeros_like(dh_scr)
            dbias_ref[...] = jnp.zeros_like(dbias_ref)
            dalog_ref[...] = jnp.zeros_like(dalog_ref)

        tri = tri_ref[0]
        mask = tri > 0.0
        mask_t = trit_ref[0] > 0.0
        raw, dt, a_neg, s, stot, s_t = _ssd_decays(dt_ref, dtb_ref, alog_ref, tri)
        etot = jnp.exp(stot)
        etot_x = _expand(jnp.broadcast_to(etot, (8, LANES)), indt_ref)[0:1]
        dtx_scr[...] = _expand(dt, indt_ref)
        ex_scr[...] = _expand(jnp.exp(stot - s), indt_ref)
        xd = xbc_ref[:, :d] * dtx_scr[...]
        xdb_scr[...] = xd.astype(BF16)
        xde_scr[...] = (xd * ex_scr[...]).astype(BF16)
        dyb_scr[...] = dy_ref[...].astype(BF16)
        dye_scr[...] = (dy_ref[...] * _dot(jnp.exp(s).astype(BF16), indt_ref[...])).astype(BF16)
        hd_cols = jnp.sum(dh_scr[...] * hs_ref[0, 0], axis=0, keepdims=True)
        left = lax.broadcasted_iota(jnp.int32, (Q, LANES), 1) < HP
        for g in range(groups):
            b32 = xbc_ref[:, d + g * LANES:d + (g + 1) * LANES]
            c32 = xbc_ref[:, d + gn + g * LANES:d + gn + (g + 1) * LANES]
            bb, cb = b32.astype(BF16), c32.astype(BF16)
            c_t = c32.T.astype(BF16)
            cbt = _dot(cb, bb, NT)
            cbt_t = _dot(bb, cb, NT)
            d_cbt = jnp.zeros((Q, Q), F32)
            gcols = slice(4 * g * HP, 4 * (g + 1) * HP)
            dyeb = dye_scr[:, gcols]
            dhg = dh_scr[:, gcols]
            dhb = dhg.astype(BF16)
            bdh_scr[:, gcols] = _dot(bb, dhb)
            d_c = _dot(dyeb, hs_ref[0, 0, :, gcols].astype(BF16), NT)
            d_b = _dot(xde_scr[:, gcols], dhb, NT)
            dh_scr[:, gcols] = dhg * etot_x[:, gcols] + _dot(c_t, dyeb)
            for pr in (2 * g, 2 * g + 1):
                h0 = 2 * pr
                cols = slice(pr * LANES, (pr + 1) * LANES)
                xdb = xdb_scr[:, cols]
                dyb = dyb_scr[:, cols]
                parts = []
                for hh, h in enumerate((h0, h0 + 1)):
                    mine = left if hh == 0 else jnp.logical_not(left)
                    diff = s[:, h:h + 1] - s_t[h:h + 1, :]
                    lm = jnp.exp(jnp.where(mask, diff, -jnp.inf))
                    lm_t = jnp.exp(jnp.where(mask_t, -diff, -jnp.inf))
                    gm = _dot(jnp.where(mine, dyb, jnp.zeros_like(dyb)), xdb, NT)
                    d_cbt = d_cbt + gm * lm
                    parts.append(_dot((cbt_t * lm_t).astype(BF16), dyb))
                dxd_scr[:, cols] = jnp.where(left, parts[0], parts[1])
            dx_ref[0, :, d + g * LANES:d + (g + 1) * LANES] = d_b + _dot(d_cbt.T.astype(BF16), cb)
            dx_ref[0, :, d + gn + g * LANES:d + gn + (g + 1) * LANES] = d_c + _dot(d_cbt.astype(BF16), bb)
        x = xbc_ref[:, :d]
        ebdh = ex_scr[...] * bdh_scr[...]
        d_xd = dxd_scr[...] + ebdh
        dx_ref[0, :, :d] = d_xd * dtx_scr[...]
        xe = x * dtx_scr[...] * ebdh
        d_s = _head_sums(dyb_scr[...].astype(F32) * y_ref[0] - xdb_scr[...].astype(F32) * dxd_scr[...] - xe, ind_ref)
        r_dx = _head_sums(d_xd * x, ind_ref)
        row8 = lax.broadcasted_iota(jnp.int32, (8, d), 0)
        tot = _head_sums(jnp.where(row8 == 0, jnp.sum(xe, axis=0, keepdims=True), jnp.where(row8 == 1, hd_cols, 0.0)), ind_ref)
        d_stot = tot[0:1] + etot * tot[1:2]
        d_a = _dot(trit_ref[0], d_s, NN, HI) + d_stot
        valid = lax.broadcasted_iota(jnp.int32, (Q, LANES), 1) < heads
        d_dt_tot = jnp.where(valid, d_a * a_neg + r_dx, 0.0)
        d_raw = d_dt_tot * _sigmoid(raw)
        dzdt_ref[...] = d_raw.astype(BF16)
        dbias_ref[0] += jnp.sum(d_raw, axis=0, keepdims=True)
        dalog_ref[0] += jnp.sum(jnp.where(valid, d_a * dt, 0.0), axis=0, keepdims=True) * a_neg

    cidx = lambda dd, ss: _scan_chunk(dd, nch - 1 - ss, nch, ncc)
    full = lambda shape: pltpu.VMEM(shape, F32)
    half = lambda shape: pltpu.VMEM(shape, BF16)
    return _hosted_call(
        body, side, name="ssd_bwd", grid=(2, nch),
        in_specs=[pl.BlockSpec((Q, 2 * d), lambda dd, ss: (cidx(dd, ss), 0)),
                  pl.BlockSpec((Q, LANES), lambda dd, ss: (cidx(dd, ss), dd)),
                  pl.BlockSpec((1, 1, LANES), lambda dd, ss: (dd, 0, 0)),
                  pl.BlockSpec((1, 1, LANES), lambda dd, ss: (dd, 0, 0)),
                  pl.BlockSpec((1, Q, Q), lambda dd, ss: (dd, 0, 0)),
                  pl.BlockSpec((1, Q, Q), lambda dd, ss: (dd, 0, 0)),
                  pl.BlockSpec((LANES, d), lambda dd, ss: (0, 0)),
                  pl.BlockSpec((d, LANES), lambda dd, ss: (0, 0)),
                  pl.BlockSpec((Q, d), lambda dd, ss: (cidx(dd, ss), 0)),
                  pl.BlockSpec((1, Q, d), lambda dd, ss: (dd, cidx(dd, ss), 0)),
                  pl.BlockSpec((1, 1, LANES, d), lambda dd, ss: (dd, cidx(dd, ss), 0, 0))],
        out_specs=[pl.BlockSpec((1, Q, 2 * d), lambda dd, ss: (dd, cidx(dd, ss), 0)),
                   pl.BlockSpec((Q, LANES), lambda dd, ss: (cidx(dd, ss), dd)),
                   pl.BlockSpec((1, 1, LANES), lambda dd, ss: (dd, 0, 0)),
                   pl.BlockSpec((1, 1, LANES), lambda dd, ss: (dd, 0, 0))],
        out_shape=[jax.ShapeDtypeStruct((2, t, 2 * d), F32), jax.ShapeDtypeStruct((t, 2 * LANES), BF16),
                   jax.ShapeDtypeStruct((2, 1, LANES), F32), jax.ShapeDtypeStruct((2, 1, LANES), F32)],
        scratch_shapes=[full((LANES, d)), full((Q, d)), full((Q, d)), half((Q, d)), half((Q, d)), half((Q, d)), half((Q, d)),
                        full((Q, d)), full((Q, d))],
        args=(xbc, z_dt, dtb, alog, tri, tri_t, ind_t, ind, d_y, y2, hs))


def _mix_common(zm_ref, y2_ref, xh_ref, dsk_ref, gv_ref, ws_ref, bst_ref, d):
    groups = d // LANES
    z_ssd, u, v, z_mlp = (zm_ref[:, k * d:(k + 1) * d] for k in range(4))
    y = y2_ref[0] + y2_ref[1] + dsk_ref[...] * xh_ref[...]
    sig_a = _sigmoid(z_ssd)
    ya_pre = y * (z_ssd * sig_a)
    r_v = _rms(v)
    vn = (v * r_v * gv_ref[...]).astype(BF16)
    sg = jnp.concatenate(
        [_dot(ws_ref[g].astype(BF16), vn[:, g * LANES:(g + 1) * LANES]) + bst_ref[:, g:g + 1] for g in range(groups)], axis=1)
    sig_m = _sigmoid(z_mlp)
    yb_pre = u * sg * (z_mlp * sig_m)
    return z_ssd, u, v, z_mlp, y, sig_a, ya_pre, r_v, vn, sg, sig_m, yb_pre


def _mix_fwd(z_mid, y2, xbc, dsk_row, g_ssd, g_v, g_mlp, w_s, b_st):
    t = z_mid.shape[0]
    d = z_mid.shape[1] // 4
    groups = d // LANES

    def body(zm_ref, y2_ref, xh_ref, dsk_ref, ga_ref, gv_ref, gm_ref, ws_ref, bst_ref, o_ref, ot_ref):
        (_, _, _, _, _, _, ya_pre, _, _, _, _, yb_pre) = _mix_common(zm_ref, y2_ref, xh_ref, dsk_ref, gv_ref, ws_ref, bst_ref, d)
        y_a = ya_pre * _rms(ya_pre) * ga_ref[...]
        y_b = yb_pre * _rms(yb_pre) * gm_ref[...]
        o_ref[:, :d] = y_a.astype(BF16)
        o_ref[:, d:] = y_b.astype(BF16)
        ot_ref[:d, :] = y_a.T.astype(BF16)
        ot_ref[d:, :] = y_b.T.astype(BF16)

    row = pl.BlockSpec((1, d), lambda i: (0, 0))
    return pl.pallas_call(
        body, name="mix_fwd", grid=(t // Q,),
        in_specs=[pl.BlockSpec((Q, 4 * d), lambda i: (i, 0)), pl.BlockSpec((2, Q, d), lambda i: (0, i, 0)),
                  pl.BlockSpec((Q, d), lambda i: (i, 0)), row, row, row, row,
                  pl.BlockSpec((groups, Q, Q), lambda i: (0, 0, 0)), pl.BlockSpec((Q, LANES), lambda i: (0, 0))],
        out_specs=[pl.BlockSpec((Q, 2 * d), lambda i: (i, 0)), pl.BlockSpec((2 * d, Q), lambda i: (0, i))],
        out_shape=[jax.ShapeDtypeStruct((t, 2 * d), BF16), jax.ShapeDtypeStruct((2 * d, t), BF16)], compiler_params=_params("arbitrary"),
    )(z_mid, y2, xbc, dsk_row, g_ssd, g_v, g_mlp, w_s, b_st)


def _mix_bwd(z_mid, y2, xbc, d_ycat, dsk_row, g_ssd, g_v, g_mlp, w_s, w_st, b_st, ind_head, ind_group):
    t = z_mid.shape[0]
    d = z_mid.shape[1] // 4
    groups = d // LANES
    nsteps = t // Q

    def body(zm_ref, y2_ref, xh_ref, dyc_ref, dsk_ref, ga_ref, gv_ref, gm_ref, ws_ref, wst_ref, bst_ref, ih_ref, ig_ref,
             dzm_ref, dy_ref, vec_ref, dws_ref, dbs_ref, dsk_acc, dsg_acc):
        i = pl.program_id(0)

        @pl.when(i == 0)
        def _():
            vec_ref[...] = jnp.zeros_like(vec_ref)
            dws_ref[...] = jnp.zeros_like(dws_ref)
            dsk_acc[...] = jnp.zeros_like(dsk_acc)
            dsg_acc[...] = jnp.zeros_like(dsg_acc)

        (z_ssd, u, v, z_mlp, y, sig_a, ya_pre, r_v, vn, sg, sig_m, yb_pre) = _mix_common(
            zm_ref, y2_ref, xh_ref, dsk_ref, gv_ref, ws_ref, bst_ref, d)
        d_ya = dyc_ref[:, :d]
        r_a = _rms(ya_pre)
        vec_ref[0:1, :] += jnp.sum(d_ya * (ya_pre * r_a), axis=0, keepdims=True)
        d_ya_pre = _rms_bwd(ya_pre, r_a, d_ya * ga_ref[...])
        d_y = d_ya_pre * (z_ssd * sig_a)
        dy_ref[...] = d_y
        dsk_acc[...] += jnp.sum(d_y * xh_ref[...], axis=0, keepdims=True)
        dzm_ref[:, 0:d] = (d_ya_pre * y * (sig_a * (1.0 + z_ssd * (1.0 - sig_a)))).astype(BF16)
        d_yb = dyc_ref[:, d:]
        r_b = _rms(yb_pre)
        vec_ref[2:3, :] += jnp.sum(d_yb * (yb_pre * r_b), axis=0, keepdims=True)
        d_yb_pre = _rms_bwd(yb_pre, r_b, d_yb * gm_ref[...])
        silu_m = z_mlp * sig_m
        dzm_ref[:, d:2 * d] = (d_yb_pre * sg * silu_m).astype(BF16)
        dzm_ref[:, 3 * d:4 * d] = (d_yb_pre * u * sg * (sig_m * (1.0 + z_mlp * (1.0 - sig_m)))).astype(BF16)
        d_sg = d_yb_pre * u * silu_m
        dsg_acc[...] += d_sg
        d_sgb = d_sg.astype(BF16)
        d_vn = []
        for g in range(groups):
            cols = slice(g * LANES, (g + 1) * LANES)
            dws_ref[g] += _dot(d_sgb[:, cols], vn[:, cols], NT)
            d_vn.append(_dot(wst_ref[g].astype(BF16), d_sgb[:, cols]))
        d_vn = jnp.concatenate(d_vn, axis=1)
        vec_ref[1:2, :] += jnp.sum(d_vn * (v * r_v), axis=0, keepdims=True)
        dzm_ref[:, 2 * d:3 * d] = _rms_bwd(v, r_v, d_vn * gv_ref[...]).astype(BF16)

        @pl.when(i == nsteps - 1)
        def _():
            vec_ref[3:4, 0:LANES] = _dot(dsk_acc[...], ih_ref[...], NN, HI)
            dbs_ref[...] = _dot(dsg_acc[...], ig_ref[...], NN, HI)

    row = pl.BlockSpec((1, d), lambda i: (0, 0))
    wsp = pl.BlockSpec((groups, Q, Q), lambda i: (0, 0, 0))
    ind = pl.BlockSpec((d, LANES), lambda i: (0, 0))
    return pl.pallas_call(
        body, name="mix_bwd", grid=(nsteps,),
        in_specs=[pl.BlockSpec((Q, 4 * d), lambda i: (i, 0)), pl.BlockSpec((2, Q, d), lambda i: (0, i, 0)),
                  pl.BlockSpec((Q, d), lambda i: (i, 0)), pl.BlockSpec((Q, 2 * d), lambda i: (i, 0)),
                  row, row, row, row, wsp, wsp, pl.BlockSpec((Q, LANES), lambda i: (0, 0)), ind, ind],
        out_specs=[pl.BlockSpec((Q, 4 * d), lambda i: (i, 0)), pl.BlockSpec((Q, d), lambda i: (i, 0)),
                   pl.BlockSpec((8, d), lambda i: (0, 0)), wsp, pl.BlockSpec((Q, LANES), lambda i: (0, 0))],
        out_shape=[jax.ShapeDtypeStruct((t, 4 * d), BF16), jax.ShapeDtypeStruct((t, d), F32),
                   jax.ShapeDtypeStruct((8, d), F32), jax.ShapeDtypeStruct((groups, Q, Q), F32),
                   jax.ShapeDtypeStruct((Q, LANES), F32)],
        scratch_shapes=[pltpu.VMEM((1, d), F32), pltpu.VMEM((Q, d), F32)],
        compiler_params=_params("arbitrary"),
    )(z_mid, y2, xbc, d_ycat, dsk_row, g_ssd, g_v, g_mlp, w_s, w_st, b_st, ind_head, ind_group)


def _ada_fwd(c16, w_ada, b_loc):
    depth, d, n = w_ada.shape
    tn = _tile(n, (512, 256, 128))

    def body(c_ref, w_ref, b_ref, o_ref):
        cv = c_ref[...]
        o_ref[0] = _dot(cv * _sigmoid(cv), w_ref[0], NN, HI) + b_ref[0]

    return pl.pallas_call(
        body, name="ada_fwd", grid=(depth, n // tn),
        in_specs=[pl.BlockSpec((16, d), lambda l, j: (0, 0)), pl.BlockSpec((1, d, tn), lambda l, j: (l, 0, j)),
                  pl.BlockSpec((1, 1, tn), lambda l, j: (l, 0, j))],
        out_specs=pl.BlockSpec((1, 16, tn), lambda l, j: (l, 0, j)),
        out_shape=jax.ShapeDtypeStruct((depth, 16, n), F32), compiler_params=_params("arbitrary", "arbitrary"),
    )(c16, w_ada, b_loc)


def _ada_bwd(c_t, dm_loc, w_ada):
    depth, d, n = w_ada.shape
    tn = _tile(n, (512, 256, 128))

    def body(s_ref, dm_ref, w_ref, gw_ref, dsc_ref):
        @pl.when((pl.program_id(0) == 0) & (pl.program_id(1) == 0))
        def _():
            dsc_ref[...] = jnp.zeros_like(dsc_ref)

        cv = s_ref[...]
        gw_ref[0] = _dot(cv * _sigmoid(cv), dm_ref[0], NN, HI)
        dsc_ref[...] += _dot(dm_ref[0, 8:16, :], w_ref[0], NT, HI)

    return pl.pallas_call(
        body, name="ada_bwd", grid=(depth, n // tn),
        in_specs=[pl.BlockSpec((d, LANES), lambda l, j: (0, 0)), pl.BlockSpec((1, LANES, tn), lambda l, j: (l, 0, j)),
                  pl.BlockSpec((1, d, tn), lambda l, j: (l, 0, j))],
        out_specs=[pl.BlockSpec((1, d, tn), lambda l, j: (l, 0, j)), pl.BlockSpec((8, d), lambda l, j: (0, 0))],
        out_shape=[jax.ShapeDtypeStruct((depth, d, n), F32), jax.ShapeDtypeStruct((8, d), F32)],
        compiler_params=_params("arbitrary", "arbitrary"),
    )(c_t, dm_loc, w_ada)


def _rowsum(x):
    depth, r, n = x.shape

    def body(x_ref, o_ref):
        o_ref[0] = jnp.sum(x_ref[0], axis=0, keepdims=True)

    return pl.pallas_call(
        body, name="rowsum", grid=(depth,),
        in_specs=[pl.BlockSpec((1, r, n), lambda l: (l, 0, 0))], out_specs=pl.BlockSpec((1, 1, n), lambda l: (l, 0, 0)),
        out_shape=jax.ShapeDtypeStruct((depth, 1, n), F32), compiler_params=_params("arbitrary"),
    )(x)


def _cctx_grad(d_scc, c_ctx_row):
    def body(g_ref, c_ref, o_ref):
        cv = c_ref[...]
        sig = _sigmoid(cv)
        o_ref[...] = g_ref[...] * (sig * (1.0 + cv * (1.0 - sig)))

    return pl.pallas_call(body, name="cctx_grad", out_shape=jax.ShapeDtypeStruct(c_ctx_row.shape, F32))(d_scc, c_ctx_row)


def _sum_lead(x, name):
    k, r, c = x.shape
    tr = _tile(r, [tt for tt in (1024, 512, 256, 128, 64, 32, 16, 8) if k * tt * c * x.dtype.itemsize <= SUM_BLOCK_BYTES])

    def body(x_ref, o_ref):
        acc = x_ref[0].astype(F32)
        for e in range(1, k):
            acc = acc + x_ref[e].astype(F32)
        o_ref[...] = acc

    return pl.pallas_call(
        body, name=name, grid=(r // tr,),
        in_specs=[pl.BlockSpec((k, tr, c), lambda i: (0, i, 0))], out_specs=pl.BlockSpec((tr, c), lambda i: (i, 0)),
        out_shape=jax.ShapeDtypeStruct((r, c), F32), compiler_params=_params("arbitrary"),
    )(x)


def _adamw(w, g, m, v, name, g2=None, side=None):
    r, c = w.shape
    tr = _tile(r, [tt for tt in (2048, 1024, 512, 256, 128, 64, 32, 16, 8) if tt * c * 4 <= ADAM_BLOCK_BYTES])
    two = g2 is not None
    bc1 = 1.0 - ADAM_B1 ** ADAM_STEP
    bc2 = 1.0 - ADAM_B2 ** ADAM_STEP

    def body(*refs):
        if two:
            w_ref, g_ref, g2_ref, m_ref, v_ref, go_ref, d_ref, mo_ref, vo_ref = refs
            gr = g_ref[...] + g2_ref[...]
        else:
            w_ref, g_ref, m_ref, v_ref, go_ref, d_ref, mo_ref, vo_ref = refs
            gr = g_ref[...]
        mn = ADAM_B1 * m_ref[...] + (1.0 - ADAM_B1) * gr
        vn = ADAM_B2 * v_ref[...] + (1.0 - ADAM_B2) * (gr * gr)
        go_ref[...] = gr
        mo_ref[...] = mn
        vo_ref[...] = vn
        d_ref[...] = -ADAM_LR * ((mn / bc1) / (jnp.sqrt(vn / bc2) + ADAM_EPS) + ADAM_WD * w_ref[...])

    blk = pl.BlockSpec((tr, c), lambda i: (i, 0))
    ins = (w, g, g2, m, v) if two else (w, g, m, v)
    return _hosted_call(body, side, name=name, grid=(r // tr,), in_specs=[blk] * len(ins), out_specs=[blk] * 4,
                        out_shape=[jax.ShapeDtypeStruct((r, c), F32)] * 4, scratch_shapes=[], args=ins)


def _flip(pos, k):
    x, y, c = pos
    return (x ^ ((k >> 2) & 1), y ^ ((k >> 1) & 1), c ^ (k & 1))


def _lin(pos):
    return 4 * pos[0] + 2 * pos[1] + pos[2]


def _chip(pos):
    return 2 * pos[0] + pos[1]


def _here():
    return (lax.axis_index("x"), lax.axis_index("y"), lax.axis_index("c"))


class _Exchange:
    def __init__(self, x, kind):
        self.x, self.kind = x, kind
        self.masks = {"gather4": (2, 4, 6), "scatter4": (2, 4, 6), "gather8": tuple(range(1, 8)), "swap": (1,)}[kind]
        self.slot = {"gather4": _chip, "scatter4": _chip, "gather8": _lin, "swap": None}[kind]
        lead = {"gather4": (4,), "scatter4": (), "gather8": (8,), "swap": ()}[kind]
        self.out_shape = jax.ShapeDtypeStruct(lead + x.shape, x.dtype)
        n = len(self.masks)
        self.scratch = [pltpu.SemaphoreType.DMA((n,)), pltpu.SemaphoreType.DMA((n,))] + ([] if kind == "swap" else [pltpu.SemaphoreType.DMA])

    def _copies(self, x_ref, o_ref, send, recv, *own, arrivals):
        me = _here()
        src = (lambda pos: x_ref.at[_chip(pos)]) if self.kind == "scatter4" else (lambda pos: x_ref)
        dst = (lambda pos: o_ref.at[self.slot(pos)]) if self.slot else (lambda pos: o_ref)
        local = [pltpu.make_async_copy(src(me), dst(me), own[0])] if own else []
        outs, ins = [], []
        for j, k in enumerate(self.masks):
            peer = _flip(me, k)
            sems = dict(send_sem=send.at[j], recv_sem=recv.at[j], device_id=peer, device_id_type=MESH)
            outs.append(pltpu.make_async_remote_copy(src_ref=src(peer), dst_ref=dst(me), **sems))
            if arrivals:
                ins.append(pltpu.make_async_remote_copy(src_ref=src(me), dst_ref=dst(peer), **sems))
        return local, outs, ins

    def start(self, *refs):
        local, outs, _ = self._copies(*refs, arrivals=False)
        for cp in local + outs:
            cp.start()

    def wait(self, *refs):
        local, outs, ins = self._copies(*refs, arrivals=True)
        for cp in ins:
            cp.wait_recv()
        for cp in outs:
            cp.wait_send()
        for cp in local:
            cp.wait()


def _exchange(x, kind, name):
    ex = _Exchange(x, kind)

    def body(*refs):
        ex.start(*refs)
        ex.wait(*refs)

    return pl.pallas_call(body, name=name, in_specs=[ANY], out_specs=ANY, out_shape=ex.out_shape, scratch_shapes=ex.scratch)(x)


def _hosted_call(body, side, *, name, grid, in_specs, out_specs, out_shape, scratch_shapes, args):
    sides = [] if side is None else list(side) if isinstance(side, (list, tuple)) else [side]
    n_in, n_out, ns = len(in_specs), len(out_specs), len(sides)
    params = _params(*(["arbitrary"] * len(grid)))
    if not sides:
        return pl.pallas_call(body, name=name, grid=grid, in_specs=in_specs, out_specs=out_specs, out_shape=out_shape,
                              scratch_shapes=scratch_shapes, compiler_params=params)(*args)
    n_sem = [len(s.scratch) for s in sides]

    def hosted(*refs):
        ins, xs = refs[:n_in], refs[n_in:n_in + ns]
        outs, os_ = refs[n_in + ns:n_in + ns + n_out], refs[n_in + ns + n_out:n_in + 2 * ns + n_out]
        rest = refs[n_in + 2 * ns + n_out:]
        scratch, sems = rest[:len(rest) - sum(n_sem)], list(rest[len(rest) - sum(n_sem):])
        per_side = [[sems.pop(0) for _ in range(k)] for k in n_sem]
        ids = [pl.program_id(a) for a in range(len(grid))]
        first, last = ids[0] == 0, ids[0] == grid[0] - 1
        for a in range(1, len(grid)):
            first, last = first & (ids[a] == 0), last & (ids[a] == grid[a] - 1)

        @pl.when(first)
        def _():
            for s, x_ref, o_ref, sm in zip(sides, xs, os_, per_side):
                s.start(x_ref, o_ref, *sm)

        body(*ins, *outs, *scratch)

        @pl.when(last)
        def _():
            for s, x_ref, o_ref, sm in zip(sides, xs, os_, per_side):
                s.wait(x_ref, o_ref, *sm)

    return pl.pallas_call(hosted, name=name + "_x_" + "_".join(s.kind for s in sides), grid=grid, in_specs=list(in_specs) + [ANY] * ns,
                          out_specs=list(out_specs) + [ANY] * ns, out_shape=list(out_shape) + [s.out_shape for s in sides],
                          scratch_shapes=list(scratch_shapes) + [sem for s in sides for sem in s.scratch],
                          compiler_params=params)(*args, *[s.x for s in sides])


def _pad_lanes(a, width):
    return jnp.pad(a, [(0, 0)] * (a.ndim - 1) + [(0, width - a.shape[-1])])


def kernel(x, c, ctx, c_ctx, w_ada, b_ada, g_pre, g_post, w_in, conv_w, conv_b, dt_bias, a_log, d_skip, g_ssd, g_v, w_s, b_s, g_mlp, w_out, loss_target, m_c_ctx, m_w_ada, m_b_ada, m_g_pre, m_g_post, m_w_in, m_conv_w, m_conv_b, m_dt_bias, m_a_log, m_d_skip, m_g_ssd, m_g_v, m_w_s, m_b_s, m_g_mlp, m_w_out, v_c_ctx, v_w_ada, v_b_ada, v_g_pre, v_g_post, v_w_in, v_conv_w, v_conv_b, v_dt_bias, v_a_log, v_d_skip, v_g_ssd, v_g_v, v_w_s, v_b_s, v_g_mlp, v_w_out):
    depth, d = g_pre.shape
    seq, ctx_len = x.shape[1], ctx.shape[1]
    heads = d // HP
    in_w = 6 * d + 2 * heads
    groups_mlp = d // LANES
    t = ctx_len + seq
    assert ctx_len == TB and seq % TB == 0 and TB % ROW == 0 and heads % 4 == 0 and heads <= LANES and d % LANES == 0
    assert w_in.shape == (depth, d, in_w // 4)

    xi, yi, ci = lax.axis_index("x"), lax.axis_index("y"), lax.axis_index("c")
    chip = 2 * xi + yi
    me = 4 * xi + 2 * yi + ci

    n_ada = 3 * d // 4
    c_all = _exchange(c, "gather8", "ag_c")[:, 0, :]
    c16 = jnp.concatenate([c_all, c_ctx[None, :], jnp.zeros((7, d), F32)], axis=0)
    b_loc = lax.dynamic_slice_in_dim(b_ada, chip * n_ada, n_ada, axis=1)[:, None, :]
    mods_loc = _exchange(_ada_fwd(c16, w_ada, b_loc).reshape(depth * 16, n_ada), "gather8", "ag_mods")
    mods_loc = mods_loc.reshape(4, 2, depth, 16, n_ada)[:, 0]
    mods_full = jnp.moveaxis(mods_loc, 0, 2).reshape(depth, 16, 3 * d)
    mods_x = lax.dynamic_index_in_dim(mods_full, me, axis=1, keepdims=False).reshape(depth, 3, d)
    mods_c = mods_full[:, 8, :].reshape(depth, 3, d)
    mods = _pad_rows8(jnp.stack([mods_c, mods_x], axis=1))

    w_in_b, w_out_b = w_in.astype(BF16), w_out.astype(BF16)

    def lay_out(w_in_rows):
        full = jnp.moveaxis(jnp.concatenate(w_in_rows, axis=1), 0, 1).reshape(d, in_w)
        w_dt_l = jnp.concatenate([_pad_lanes(full[:, 2 * d:2 * d + heads], LANES),
                                  _pad_lanes(full[:, 2 * d + heads:2 * d + 2 * heads], LANES)], axis=1)
        return full[:, :2 * d], full[:, 2 * d + 2 * heads:], w_dt_l

    w_in_rows = [_exchange(w_in_b[0], "gather4", "ag_w_in")]
    r_scan = 3 * d // 4
    conv_w_full = jnp.moveaxis(_exchange(conv_w, "gather4", "ag_conv_w"), 0, 2).reshape(depth, CONV_TAPS, 2 * d)
    conv_w8 = jnp.pad(conv_w_full, ((0, 0), (0, 8 - CONV_TAPS), (0, 0)))

    tri = jnp.stack([jnp.tril(jnp.ones((Q, Q), F32)), jnp.triu(jnp.ones((Q, Q), F32))])
    tri_t = jnp.swapaxes(tri, 1, 2)
    dtb = _pad_lanes(dt_bias, LANES)[:, :, None, :]
    alog = _pad_lanes(a_log, LANES)[:, :, None, :]
    dsk_row = jnp.repeat(d_skip, HP, axis=1)[:, None, :]
    w_st = jnp.swapaxes(w_s, 2, 3)
    b_st = _pad_lanes(jnp.swapaxes(b_s, 1, 2), LANES)
    chan = jnp.arange(d)
    ind_head = (chan[:, None] // HP == jnp.arange(LANES)[None, :]).astype(F32)
    ind_b, ind_t = ind_head.astype(BF16), ind_head.T.astype(BF16)
    ind_group = (chan[:, None] // LANES == jnp.arange(LANES)[None, :]).astype(F32)

    stream = jnp.concatenate([ctx[0], x[0]], axis=0)
    saved = []
    for l in range(depth):
        w_xbc, w_mid, w_dt = lay_out(w_in_rows)
        more = l + 1 < depth
        hx, hx_t = _pre_fwd(stream, g_pre[l][None], mods[l])
        z_xbc = _mm(hx, w_xbc, NN, "in_xbc")
        z_mid, w_out_all = _mm(hx, w_mid, NN, "in_mid", side=_Exchange(w_out_b[l], "gather4"))
        w_o = w_out_all.reshape(2 * d, d)
        z_dt = _mm(hx, w_dt, NN, "in_dt")
        xbc, dsilu = _conv_fwd(z_xbc, conv_w8[l], conv_b[l][None], ctx_len)
        y2, hs, *rows_a = _ssd_fwd(xbc, z_dt, dtb[l], alog[l], tri, ind_t, d, ctx_len,
                                   side=_Exchange(w_in_b[l + 1, :r_scan], "gather4") if more else None)
        ycat, ycat_t = _mix_fwd(z_mid, y2, xbc, dsk_row[l], g_ssd[l][None], g_v[l][None], g_mlp[l][None], w_s[l], b_st[l])
        if more:
            o, rows_b = _mm(ycat, w_o, NN, "out_proj", side=_Exchange(w_in_b[l + 1, r_scan:], "gather4"))
            w_in_rows = [rows_a[0], rows_b]
        else:
            o = _mm(ycat, w_o, NN, "out_proj")
        saved.append((stream, hx_t, z_xbc, dsilu, z_mid, z_dt, xbc, y2, hs, ycat_t, o, w_xbc, w_mid, w_dt, w_o))
        stream = _post_fwd(o, stream, g_post[l][None], mods[l])

    sq, d_stream = _loss_grad(stream, loss_target[0])
    loss = lax.psum(0.5 / d * sq[0, 0], ("x", "y", "c"))

    small = []
    dmods = []
    q_in = in_w // 4
    names = ["g_post", "conv_w", "conv_b", "dt_bias", "a_log", "d_skip", "g_ssd", "g_v", "w_s", "b_s", "g_mlp"]

    def rows_of(a):
        return -(-a.size // (8 * LANES)) * 8

    def pack(arrays):
        blocks = [jnp.pad(a.reshape(-1), (0, rows_of(a) * LANES - a.size)).reshape(rows_of(a), LANES) for a in arrays]
        rows = sum(b.shape[0] for b in blocks)
        return jnp.pad(jnp.concatenate(blocks, axis=0), ((0, -(-rows // PACK_ROWS) * PACK_ROWS - rows), (0, 0)))

    def unpack(block, likes):
        out, row = [], 0
        for a in likes:
            out.append(block[row:row + rows_of(a)].reshape(-1)[:a.size].reshape(a.shape))
            row += rows_of(a)
        return out

    def quarter_parts(g_xbc, g_mid, g_dt):
        segs = [(g_xbc, 0, 2 * d), (g_dt, 0, heads), (g_dt, LANES, heads), (g_mid, 0, 4 * d)]
        parts = []
        for qi in range(4):
            lo, hi, off, pieces = qi * q_in, (qi + 1) * q_in, 0, []
            for arr, start, width in segs:
                a, b = max(lo, off), min(hi, off + width)
                if a < b:
                    pieces.append(arr[:, start + a - off:start + b - off])
                off += width
            parts.append(pieces[0] if len(pieces) == 1 else jnp.concatenate(pieces, axis=1))
        return jnp.stack(parts)

    sum_in, sum_out = [None] * depth, [None] * depth
    swap_in, swap_out = [None] * depth, [None] * depth
    small_sum = [None] * depth
    parts_in = packed = None
    for l in reversed(range(depth)):
        x_in, hx_t, z_xbc, dsilu, z_mid, z_dt, xbc, y2, hs, ycat_t, o, w_xbc, w_mid, w_dt, w_o = saved[l]
        up = l + 1
        d_o, acc_post = _post_bwd(d_stream, o, g_post[l][None], mods[l])
        if packed is not None:
            d_ycat, gathered = _mm(d_o, w_o, NT, "d_ycat", side=_Exchange(packed, "gather8"))
            small_sum[up] = _sum_lead(gathered, "sum_small")
        else:
            d_ycat = _mm(d_o, w_o, NT, "d_ycat")
        g_out = _mm(ycat_t, d_o, NN, "dw_out", out_dtype=BF16)
        dz_mid, d_y, vec, d_ws, d_bs = _mix_bwd(z_mid, y2, xbc, d_ycat, dsk_row[l], g_ssd[l][None], g_v[l][None], g_mlp[l][None],
                                                w_s[l], w_st[l], b_st[l], ind_head, ind_group)
        d_xbc2, dz_dt, d_bias, d_alog, *got = _ssd_bwd(xbc, z_dt, dtb[l], alog[l], tri, tri_t, ind_t, ind_b, d_y, y2, hs, d, ctx_len,
                                                       side=_Exchange(parts_in, "scatter4") if parts_in is not None else None)
        if parts_in is not None:
            sum_in[up] = _sum_lead(got[0], "sum_w_in")
        dz_xbc, d_cw, d_cb = _conv_bwd(z_xbc, dsilu, d_xbc2, d_y, dsk_row[l], conv_w8[l], ctx_len)
        g_xbc, got_out = _mm(hx_t, dz_xbc, NN, "dw_xbc", out_dtype=BF16, side=_Exchange(g_out.reshape(4, 2 * d // 4, d), "scatter4"))
        sum_out[l] = _sum_lead(got_out, "sum_w_out")
        swaps = [_Exchange(sum_out[l], "swap")] + ([_Exchange(sum_in[up], "swap")] if parts_in is not None else [])
        g_mid, swap_out[l], *swapped = _mm(hx_t, dz_mid, NN, "dw_mid", out_dtype=BF16, side=swaps)
        if parts_in is not None:
            swap_in[up] = swapped[0]
        g_dt = _mm(hx_t, dz_dt, NN, "dw_dt", out_dtype=BF16)
        parts_in = quarter_parts(g_xbc, g_mid, g_dt)
        small.append(dict(
            g_post=acc_post[0, 1] + acc_post[1, 1], conv_w=d_cw[:CONV_TAPS], conv_b=d_cb[0],
            dt_bias=d_bias[:, 0, :heads], a_log=d_alog[:, 0, :heads], d_skip=vec[3, :heads], g_ssd=vec[0], g_v=vec[1],
            w_s=d_ws, b_s=d_bs[:, :groups_mlp].T, g_mlp=vec[2]))
        packed = pack([small[-1][n] for n in names])
        if l == 0:
            r_a = max(LANES, d // 3 // LANES * LANES)
            d_hx, got_a = _mm(dz_xbc, w_xbc, NT, "dhx_xbc", side=_Exchange(parts_in[:, :r_a], "scatter4"))
            d_hx, got_b = _mm(dz_mid, w_mid, NT, "dhx_mid", acc=d_hx, side=_Exchange(parts_in[:, r_a:], "scatter4"))
            sum_in[0] = _sum_lead(jnp.concatenate([got_a, got_b], axis=1), "sum_w_in")
        else:
            d_hx = _mm(dz_xbc, w_xbc, NT, "dhx_xbc")
            d_hx = _mm(dz_mid, w_mid, NT, "dhx_mid", acc=d_hx)
        d_hx = _mm(dz_dt, w_dt, NT, "dhx_dt", acc=d_hx)
        if l == 0:
            d_stream, acc_pre, swap_in[0], gathered = _pre_bwd(
                x_in, d_hx, d_stream, g_pre[l][None], mods[l], latent_only=True,
                side=[_Exchange(sum_in[0], "swap"), _Exchange(packed, "gather8")])
            small_sum[0] = _sum_lead(gathered, "sum_small")
        else:
            d_stream, acc_pre = _pre_bwd(x_in, d_hx, d_stream, g_pre[l][None], mods[l])
        dmods.append(jnp.concatenate([acc_pre[:, 0], acc_pre[:, 1], acc_post[:, 0], acc_pre[:, 2]], axis=1))
    small.reverse(), dmods.reverse()
    grad_x = d_stream[None]

    weights = dict(c_ctx=c_ctx, w_ada=w_ada, b_ada=b_ada, g_pre=g_pre, g_post=g_post, w_in=w_in, conv_w=conv_w, conv_b=conv_b,
                   dt_bias=dt_bias, a_log=a_log, d_skip=d_skip, g_ssd=g_ssd, g_v=g_v, w_s=w_s, b_s=b_s, g_mlp=g_mlp, w_out=w_out)
    m_in = dict(c_ctx=m_c_ctx, w_ada=m_w_ada, b_ada=m_b_ada, g_pre=m_g_pre, g_post=m_g_post, w_in=m_w_in, conv_w=m_conv_w,
                conv_b=m_conv_b, dt_bias=m_dt_bias, a_log=m_a_log, d_skip=m_d_skip, g_ssd=m_g_ssd, g_v=m_g_v, w_s=m_w_s,
                b_s=m_b_s, g_mlp=m_g_mlp, w_out=m_w_out)
    v_in = dict(c_ctx=v_c_ctx, w_ada=v_w_ada, b_ada=v_b_ada, g_pre=v_g_pre, g_post=v_g_post, w_in=v_w_in, conv_w=v_conv_w,
                conv_b=v_conv_b, dt_bias=v_dt_bias, a_log=v_a_log, d_skip=v_d_skip, g_ssd=v_g_ssd, g_v=v_g_v, w_s=v_w_s,
                b_s=v_b_s, g_mlp=v_g_mlp, w_out=v_w_out)
    order = list(weights)
    results = {}

    def adamw_big(n, ga, gb):
        shp = weights[n].shape
        two = lambda a: a.reshape(-1, shp[-1])
        results[n] = [r.reshape(shp) for r in _adamw(two(weights[n]), ga, two(m_in[n]), two(v_in[n]), "adamw_" + n, g2=gb)]

    adamw_big("w_in", jnp.concatenate(sum_in, axis=0), jnp.concatenate(swap_in, axis=0))
    adamw_big("w_out", jnp.concatenate(sum_out, axis=0), jnp.concatenate(swap_out, axis=0))
    per_layer = [unpack(small_sum[l], [small[l][n] for n in names]) for l in range(depth)]
    grads = {n: jnp.stack([per_layer[l][j] for l in range(depth)]) for j, n in enumerate(names)}
    grads["conv_w"] = lax.dynamic_slice_in_dim(grads["conv_w"], chip * (2 * d // 4), 2 * d // 4, axis=2)

    dm_all = _exchange(jnp.stack(dmods).reshape(depth * 2, 4 * d), "gather8", "ag_dmods").reshape(8, depth, 2, 4 * d)
    grads["g_pre"] = _sum_lead(jnp.moveaxis(dm_all[..., 3 * d:], 2, 1).reshape(16, depth, d), "sum_g_pre")
    dm_ctx = _sum_lead(dm_all[:, :, 0, :3 * d], "sum_dm_ctx")
    dm16 = jnp.concatenate([jnp.moveaxis(dm_all[:, :, 1, :3 * d], 0, 1), dm_ctx[:, None, :], jnp.zeros((depth, 7, 3 * d), F32)], axis=1)
    grads["b_ada"] = _rowsum(dm16)[:, 0, :]
    dm_loc = jnp.pad(lax.dynamic_slice_in_dim(dm16, chip * n_ada, n_ada, axis=2), ((0, 0), (0, LANES - 16), (0, 0)))
    c_t = jnp.pad(c16.T, ((0, 0), (0, LANES - 16)))
    g_w_ada, d_scc_part = _ada_bwd(c_t, dm_loc, w_ada)
    adamw_big("w_ada", g_w_ada.reshape(depth * d, n_ada), None)
    d_scc = _sum_lead(_exchange(d_scc_part, "gather8", "ag_dscc").reshape(4, 2, 8, d)[:, 0], "sum_dscc")
    grads["c_ctx"] = _cctx_grad(d_scc[0:1], c_ctx[None])[0]

    rest = [n for n in order if n not in results]
    outs = _adamw(pack([weights[n] for n in rest]), pack([grads[n] for n in rest]), pack([m_in[n] for n in rest]),
                  pack([v_in[n] for n in rest]), "adamw_small")
    for j, res in enumerate(zip(*[unpack(o_, [weights[n] for n in rest]) for o_ in outs])):
        results[rest[j]] = list(res)

    return (loss, grad_x, *[results[n][0] for n in order], *[results[n][1] for n in order],
            *[results[n][2] for n in order], *[results[n][3] for n in order])


def _pad_rows8(a):
    return jnp.pad(a, [(0, 0)] * (a.ndim - 2) + [(0, 8 - a.shape[-2]), (0, 0)])
```

```python
import jax
import jax.numpy as jnp
from jax import lax
from jax.experimental import pallas as pl
from jax.experimental.pallas import tpu as pltpu

F32 = jnp.float32
BF16 = jnp.bfloat16
EPS = 1e-6
Q = 128
TB = 256
ROW = 64
HP = 64
LANES = 128
CONV_TAPS = 5
VMEM_LIMIT = 48 * 1024 * 1024
HI = lax.Precision.HIGHEST
SUM_BLOCK_BYTES = 4 * 1024 * 1024
ADAM_BLOCK_BYTES = 1024 * 1024
PACK_ROWS = 256
MESH = pl.DeviceIdType.MESH
ANY = pl.BlockSpec(memory_space=pl.ANY)

ADAM_LR, ADAM_B1, ADAM_B2, ADAM_EPS, ADAM_WD, ADAM_STEP = 0.001, 0.9, 0.999, 1e-08, 0.01, 10

NN = (((1,), (0,)), ((), ()))
NT = (((1,), (1,)), ((), ()))
TN = (((0,), (0,)), ((), ()))


def _dot(a, b, dims=NN, prec=None):
    return lax.dot_general(a, b, dims, precision=prec, preferred_element_type=F32)


def _params(*sem):
    if sem:
        return pltpu.CompilerParams(vmem_limit_bytes=VMEM_LIMIT, dimension_semantics=sem)
    return pltpu.CompilerParams(vmem_limit_bytes=VMEM_LIMIT)


def _tile(dim, cands):
    for t in cands:
        if dim % t == 0:
            return t
    return dim


def _sigmoid(x):
    return 1.0 / (1.0 + jnp.exp(-x))


def _softplus(x):
    e = jnp.exp(-jnp.abs(x))
    u = 1.0 + e
    um1 = u - 1.0
    l1p = jnp.where(um1 == 0.0, e, jnp.log(u) * (e / jnp.where(um1 == 0.0, 1.0, um1)))
    return jnp.maximum(x, 0.0) + l1p


def _rms(x):
    return lax.rsqrt(jnp.mean(x * x, axis=-1, keepdims=True) + EPS)


def _rms_bwd(x, r, t):
    return r * t - x * (r * r * r) * jnp.mean(x * t, axis=-1, keepdims=True)


def _mm(a, b, dims, name, out_dtype=F32, side=None):
    (ca,), (cb,) = dims[0]
    m, k = a.shape[1 - ca], a.shape[ca]
    n = b.shape[1 - cb]
    tm = _tile(m, (1024, 768, 512, 384, 256, 128))
    tn = _tile(n, (1024, 512, 256, 128))
    tk = k if k <= 2048 else _tile(k, (2048, 1408, 768, 512, 384, 256, 128))
    nk = k // tk
    a_spec = pl.BlockSpec((tm, tk), lambda i, j, kk: (i, kk)) if ca == 1 else pl.BlockSpec((tk, tm), lambda i, j, kk: (kk, i))
    b_spec = pl.BlockSpec((tk, tn), lambda i, j, kk: (kk, j)) if cb == 0 else pl.BlockSpec((tn, tk), lambda i, j, kk: (j, kk))
    o_spec = pl.BlockSpec((tm, tn), lambda i, j, kk: (i, j))

    def body(a_ref, b_ref, o_ref, acc_ref):
        kk = pl.program_id(2)
        part = _dot(a_ref[...], b_ref[...], dims)
        if nk == 1:
            o_ref[...] = part.astype(out_dtype)
            return

        @pl.when(kk == 0)
        def _():
            acc_ref[...] = part

        @pl.when(kk > 0)
        def _():
            acc_ref[...] += part

        @pl.when(kk == nk - 1)
        def _():
            o_ref[...] = acc_ref[...].astype(out_dtype)

    res = _hosted_call(
        body, side, name=name, grid=(m // tm, n // tn, nk), in_specs=[a_spec, b_spec],
        out_specs=[o_spec], out_shape=[jax.ShapeDtypeStruct((m, n), out_dtype)],
        scratch_shapes=[pltpu.VMEM((tm, tn), F32)], args=(a, b))
    return res[0] if side is None else res


def _mm_sum_nt(pairs, name, side=None):
    m, n = pairs[0][0].shape[0], pairs[0][1].shape[0]
    tm = _tile(m, (1024, 768, 512, 384, 256, 128))
    tn = _tile(n, (1024, 512, 256, 128))
    tks = [a.shape[1] if a.shape[1] <= 1024 else _tile(a.shape[1], (1024, 512, 256, 128)) for a, _ in pairs]
    nks = [a.shape[1] // tk for (a, _), tk in zip(pairs, tks)]
    starts = [sum(nks[:i]) for i in range(len(pairs))]
    total = sum(nks)

    def body(*refs):
        o_ref, acc_ref = refs[-2], refs[-1]
        kk = pl.program_id(2)

        @pl.when(kk == 0)
        def _():
            acc_ref[...] = jnp.zeros_like(acc_ref)

        for i in range(len(pairs)):
            @pl.when((kk >= starts[i]) & (kk < starts[i] + nks[i]))
            def _(i=i):
                acc_ref[...] += _dot(refs[2 * i][...], refs[2 * i + 1][...], NT)

        @pl.when(kk == total - 1)
        def _():
            o_ref[...] = acc_ref[...]

    in_specs, args = [], []
    for (a, b), tk, nk, s0 in zip(pairs, tks, nks, starts):
        blk = lambda kk, s0=s0, nk=nk: jnp.clip(kk - s0, 0, nk - 1)
        in_specs += [pl.BlockSpec((tm, tk), lambda i, j, kk, blk=blk: (i, blk(kk))),
                     pl.BlockSpec((tn, tk), lambda i, j, kk, blk=blk: (j, blk(kk)))]
        args += [a, b]
    res = _hosted_call(body, side, name=name, grid=(m // tm, n // tn, total), in_specs=in_specs,
                       out_specs=[pl.BlockSpec((tm, tn), lambda i, j, kk: (i, j))], out_shape=[jax.ShapeDtypeStruct((m, n), F32)],
                       scratch_shapes=[pltpu.VMEM((tm, tn), F32)], args=tuple(args))
    return res[0] if side is None else res


def _which(i):
    return jnp.minimum(i, 1)


def _pre_fwd(x, g_pre, mods):
    t, d = x.shape

    def body(x_ref, g_ref, m_ref, o_ref, ot_ref):
        xb = x_ref[...]
        xn = xb * _rms(xb) * g_ref[...]
        hx = xn * (1.0 + m_ref[0, 1:2, :]) + m_ref[0, 0:1, :]
        o_ref[...] = hx.astype(BF16)
        ot_ref[...] = hx.T.astype(BF16)

    return pl.pallas_call(
        body, name="pre_fwd", grid=(t // TB,),
        in_specs=[pl.BlockSpec((TB, d), lambda i: (i, 0)), pl.BlockSpec((1, d), lambda i: (0, 0)),
                  pl.BlockSpec((1, 8, d), lambda i: (_which(i), 0, 0))],
        out_specs=[pl.BlockSpec((TB, d), lambda i: (i, 0)), pl.BlockSpec((d, TB), lambda i: (0, i))],
        out_shape=[jax.ShapeDtypeStruct((t, d), BF16), jax.ShapeDtypeStruct((d, t), BF16)], compiler_params=_params("arbitrary"),
    )(x, g_pre, mods)


def _pre_bwd(x, d_hx, d_up, g_pre, mods, side=None, latent_only=False):
    t, d = x.shape
    dx_rows = t - TB if latent_only else t
    dx_spec = pl.BlockSpec((TB, d), (lambda i: (jnp.maximum(i - 1, 0), 0)) if latent_only else (lambda i: (i, 0)))

    def body(x_ref, dh_ref, du_ref, g_ref, m_ref, dx_ref, acc_ref):
        i = pl.program_id(0)

        @pl.when(i <= 1)
        def _():
            acc_ref[...] = jnp.zeros_like(acc_ref)

        xb = x_ref[...]
        dh = dh_ref[...]
        r = _rms(xb)
        xr = xb * r
        d_xn = dh * (1.0 + m_ref[0, 1:2, :])
        dx_ref[...] = du_ref[...] + _rms_bwd(xb, r, d_xn * g_ref[...])
        acc_ref[0, 0:1, :] += jnp.sum(dh, axis=0, keepdims=True)
        acc_ref[0, 1:2, :] += jnp.sum(dh * (xr * g_ref[...]), axis=0, keepdims=True)
        acc_ref[0, 2:3, :] += jnp.sum(d_xn * xr, axis=0, keepdims=True)

    blk = pl.BlockSpec((TB, d), lambda i: (i, 0))
    return _hosted_call(
        body, side, name="pre_bwd", grid=(t // TB,),
        in_specs=[blk, blk, blk, pl.BlockSpec((1, d), lambda i: (0, 0)),
                  pl.BlockSpec((1, 8, d), lambda i: (_which(i), 0, 0))],
        out_specs=[dx_spec, pl.BlockSpec((1, 8, d), lambda i: (_which(i), 0, 0))],
        out_shape=[jax.ShapeDtypeStruct((dx_rows, d), F32), jax.ShapeDtypeStruct((2, 8, d), F32)],
        scratch_shapes=[], args=(x, d_hx, d_up, g_pre, mods))


def _post_fwd(o, x, g_post, mods):
    t, d = x.shape

    def body(o_ref, x_ref, g_ref, m_ref, y_ref):
        ob = o_ref[...]
        y_ref[...] = x_ref[...] + m_ref[0, 2:3, :] * (ob * _rms(ob) * g_ref[...])

    blk = pl.BlockSpec((TB, d), lambda i: (i, 0))
    return pl.pallas_call(
        body, name="post_fwd", grid=(t // TB,),
        in_specs=[blk, blk, pl.BlockSpec((1, d), lambda i: (0, 0)), pl.BlockSpec((1, 8, d), lambda i: (_which(i), 0, 0))],
        out_specs=blk, out_shape=jax.ShapeDtypeStruct((t, d), F32), compiler_params=_params("arbitrary"),
    )(o, x, g_post, mods)


def _post_bwd(d_y, o, g_post, mods):
    t, d = o.shape

    def body(dy_ref, o_ref, g_ref, m_ref, do_ref, acc_ref):
        i = pl.program_id(0)

        @pl.when(i <= 1)
        def _():
            acc_ref[...] = jnp.zeros_like(acc_ref)

        ob = o_ref[...]
        dy = dy_ref[...]
        r = _rms(ob)
        orr = ob * r
        d_out = dy * m_ref[0, 2:3, :]
        do_ref[...] = _rms_bwd(ob, r, d_out * g_ref[...]).astype(BF16)
        acc_ref[0, 0:1, :] += jnp.sum(dy * (orr * g_ref[...]), axis=0, keepdims=True)
        acc_ref[0, 1:2, :] += jnp.sum(d_out * orr, axis=0, keepdims=True)

    blk = pl.BlockSpec((TB, d), lambda i: (i, 0))
    return pl.pallas_call(
        body, name="post_bwd", grid=(t // TB,),
        in_specs=[blk, blk, pl.BlockSpec((1, d), lambda i: (0, 0)), pl.BlockSpec((1, 8, d), lambda i: (_which(i), 0, 0))],
        out_specs=[blk, pl.BlockSpec((1, 8, d), lambda i: (_which(i), 0, 0))],
        out_shape=[jax.ShapeDtypeStruct((t, d), BF16), jax.ShapeDtypeStruct((2, 8, d), F32)],
        compiler_params=_params("arbitrary"),
    )(d_y, o, g_post, mods)


def _loss_grad(xf, target):
    t, d = xf.shape

    def body(x_ref, t_ref, loss_ref, dx_ref):
        i = pl.program_id(0)

        @pl.when(i == 0)
        def _():
            loss_ref[...] = jnp.zeros_like(loss_ref)
            dx_ref[...] = jnp.zeros_like(dx_ref)

        @pl.when(i > 0)
        def _():
            err = x_ref[...] - t_ref[...]
            loss_ref[...] += jnp.sum(err * err).reshape(1, 1)
            dx_ref[...] = err * (1.0 / d)

    return pl.pallas_call(
        body, name="loss_grad", grid=(t // TB,),
        in_specs=[pl.BlockSpec((TB, d), lambda i: (i, 0)), pl.BlockSpec((TB, d), lambda i: (jnp.maximum(i - 1, 0), 0))],
        out_specs=[pl.BlockSpec((1, 1), lambda i: (0, 0)), pl.BlockSpec((TB, d), lambda i: (i, 0))],
        out_shape=[jax.ShapeDtypeStruct((1, 1), F32), jax.ShapeDtypeStruct((t, d), F32)],
        compiler_params=_params("arbitrary"),
    )(xf, target)


def _conv_terms(zb, pos, row_len):
    out = []
    for k in range(CONV_TAPS):
        o = k - CONV_TAPS // 2
        sh = zb if o == 0 else pltpu.roll(zb, (-o) % TB, 0)
        out.append(jnp.where((pos + o >= 0) & (pos + o < row_len), sh, 0.0))
    return out


def _row_pos(i, ctx_len):
    row_len = jnp.where(i == 0, ctx_len, ROW)
    pos = lax.broadcasted_iota(jnp.int32, (TB, 1), 0) & (row_len - 1)
    return pos, row_len


def _conv_fwd(z_xbc, conv_w8, conv_b, ctx_len):
    t, c = z_xbc.shape
    tc = _tile(c, (1024, 512, 256, 128))

    def body(z_ref, w_ref, b_ref, o_ref, ds_ref):
        pos, row_len = _row_pos(pl.program_id(1), ctx_len)
        terms = _conv_terms(z_ref[...], pos, row_len)
        pre = b_ref[...]
        for k in range(CONV_TAPS):
            pre = pre + terms[k] * w_ref[k:k + 1, :]
        sig = _sigmoid(pre)
        o_ref[...] = pre * sig
        ds_ref[...] = sig * (1.0 + pre * (1.0 - sig))

    blk = pl.BlockSpec((TB, tc), lambda j, i: (i, j))
    return pl.pallas_call(
        body, name="conv_fwd", grid=(c // tc, t // TB),
        in_specs=[blk, pl.BlockSpec((8, tc), lambda j, i: (0, j)), pl.BlockSpec((1, tc), lambda j, i: (0, j))],
        out_specs=[blk, blk], out_shape=[jax.ShapeDtypeStruct((t, c), F32)] * 2, compiler_params=_params("arbitrary", "arbitrary"),
    )(z_xbc, conv_w8, conv_b)


def _conv_bwd(z_xbc, dsilu, d_xbc2, d_y, d_skip_row, conv_w8, ctx_len):
    t, c = z_xbc.shape
    d = d_y.shape[1]
    tc = _tile(d, (1024, 512, 256, 128))
    nskip = d // tc

    def body(z_ref, dsl_ref, g2_ref, dy_ref, ds_ref, w_ref, dz_ref, dw_ref, db_ref):
        j, i = pl.program_id(0), pl.program_id(1)

        @pl.when(i == 0)
        def _():
            dw_ref[...] = jnp.zeros_like(dw_ref)
            db_ref[...] = jnp.zeros_like(db_ref)

        pos, row_len = _row_pos(i, ctx_len)
        skip = jnp.where(j < nskip, 1.0, 0.0) * ds_ref[...]
        d_pre = (g2_ref[0] + g2_ref[1] + dy_ref[...] * skip) * dsl_ref[...]
        db_ref[...] += jnp.sum(d_pre, axis=0, keepdims=True)
        zb = z_ref[...]
        dz = jnp.zeros_like(d_pre)
        for k in range(CONV_TAPS):
            o = k - CONV_TAPS // 2
            sh = d_pre if o == 0 else pltpu.roll(d_pre, o % TB, 0)
            sh = jnp.where((pos - o >= 0) & (pos - o < row_len), sh, 0.0)
            dw_ref[k:k + 1, :] += jnp.sum(sh * zb, axis=0, keepdims=True)
            dz = dz + sh * w_ref[k:k + 1, :]
        dz_ref[...] = dz.astype(BF16)

    jd = lambda j: jnp.minimum(j, nskip - 1)
    blk = pl.BlockSpec((TB, tc), lambda j, i: (i, j))
    return pl.pallas_call(
        body, name="conv_bwd", grid=(c // tc, t // TB),
        in_specs=[blk, blk, pl.BlockSpec((2, TB, tc), lambda j, i: (0, i, j)),
                  pl.BlockSpec((TB, tc), lambda j, i: (i, jd(j))), pl.BlockSpec((1, tc), lambda j, i: (0, jd(j))),
                  pl.BlockSpec((8, tc), lambda j, i: (0, j))],
        out_specs=[blk, pl.BlockSpec((8, tc), lambda j, i: (0, j)), pl.BlockSpec((1, tc), lambda j, i: (0, j))],
        out_shape=[jax.ShapeDtypeStruct((t, c), BF16), jax.ShapeDtypeStruct((8, c), F32), jax.ShapeDtypeStruct((1, c), F32)],
        compiler_params=_params("arbitrary", "arbitrary"),
    )(z_xbc, dsilu, d_xbc2, d_y, d_skip_row, conv_w8)


def _scan_chunk(dirn, s, nch, ncc):
    bwd = jnp.where(s < ncc, ncc - 1 - s, nch - 1 - (s - ncc))
    return jnp.where(dirn == 0, s, bwd)


def _ssd_decays(dt_ref, dtb_ref, alog_ref, tri):
    raw = dt_ref[...] + dtb_ref[0]
    dt = _softplus(raw)
    a_neg = -jnp.exp(alog_ref[0])
    a = dt * a_neg
    s = _dot(tri, a, NN, HI)
    stot = jnp.sum(a, axis=0, keepdims=True)
    return raw, dt, a_neg, s, stot, s.T


def _split(v):
    hi = v.astype(BF16)
    return hi, (v - hi.astype(F32)).astype(BF16)


def _expand(v, indt_ref):
    hi, lo = _split(v)
    return _dot(hi, indt_ref[...]) + _dot(lo, indt_ref[...])


def _head_sums(v, ind_ref):
    hi, lo = _split(v)
    return _dot(hi, ind_ref[...]) + _dot(lo, ind_ref[...])


def _ssd_fwd(xbc, z_dt, dtb, alog, tri, ind_t, d, ctx_len, side=None):
    t = xbc.shape[0]
    nch, ncc = t // Q, ctx_len // Q
    heads = d // HP
    groups = heads // 4
    gn = groups * LANES

    def body(xbc_ref, dt_ref, dtb_ref, alog_ref, tri_ref, indt_ref, y_ref, hs_ref, h_scr, xdb_scr, xde_scr, esx_scr):
        @pl.when(pl.program_id(1) == 0)
        def _():
            h_scr[...] = jnp.zeros_like(h_scr)

        tri = tri_ref[0]
        mask = tri > 0.0
        _, dt, _, s, stot, s_t = _ssd_decays(dt_ref, dtb_ref, alog_ref, tri)
        esx_scr[...] = _expand(jnp.exp(s), indt_ref)
        etot_x = _expand(jnp.broadcast_to(jnp.exp(stot), (8, LANES)), indt_ref)[0:1]
        xd = xbc_ref[:, :d] * _expand(dt, indt_ref)
        xdb_scr[...] = xd.astype(BF16)
        xde_scr[...] = (xd * _expand(jnp.exp(stot - s), indt_ref)).astype(BF16)
        left = lax.broadcasted_iota(jnp.int32, (Q, LANES), 1) < HP
        hs_ref[0, 0] = h_scr[...]
        for g in range(groups):
            b32 = xbc_ref[:, d + g * LANES:d + (g + 1) * LANES]
            bb = b32.astype(BF16)
            bbt = b32.T.astype(BF16)
            cb = xbc_ref[:, d + gn + g * LANES:d + gn + (g + 1) * LANES].astype(BF16)
            cbt = _dot(cb, bb, NT)
            gcols = slice(4 * g * HP, 4 * (g + 1) * HP)
            hg = h_scr[:, gcols]
            y_off = _dot(cb, hg.astype(BF16)) * esx_scr[:, gcols]
            h_scr[:, gcols] = hg * etot_x[:, gcols] + _dot(bbt, xde_scr[:, gcols])
            for j, pr in enumerate((2 * g, 2 * g + 1)):
                h0 = 2 * pr
                cols = slice(pr * LANES, (pr + 1) * LANES)
                xdb = xdb_scr[:, cols]
                res = []
                for h in (h0, h0 + 1):
                    lm = jnp.exp(jnp.where(mask, s[:, h:h + 1] - s_t[h:h + 1, :], -jnp.inf))
                    res.append(_dot((cbt * lm).astype(BF16), xdb))
                y_ref[0, :, cols] = jnp.where(left, res[0], res[1]) + y_off[:, j * LANES:(j + 1) * LANES]

    cidx = lambda dd, ss: _scan_chunk(dd, ss, nch, ncc)
    return _hosted_call(
        body, side, name="ssd_fwd", grid=(2, nch),
        in_specs=[pl.BlockSpec((Q, 2 * d), lambda dd, ss: (cidx(dd, ss), 0)),
                  pl.BlockSpec((Q, LANES), lambda dd, ss: (cidx(dd, ss), dd)),
                  pl.BlockSpec((1, 1, LANES), lambda dd, ss: (dd, 0, 0)),
                  pl.BlockSpec((1, 1, LANES), lambda dd, ss: (dd, 0, 0)),
                  pl.BlockSpec((1, Q, Q), lambda dd, ss: (dd, 0, 0)),
                  pl.BlockSpec((LANES, d), lambda dd, ss: (0, 0))],
        out_specs=[pl.BlockSpec((1, Q, d), lambda dd, ss: (dd, cidx(dd, ss), 0)),
                   pl.BlockSpec((1, 1, LANES, d), lambda dd, ss: (dd, cidx(dd, ss), 0, 0))],
        out_shape=[jax.ShapeDtypeStruct((2, t, d), F32), jax.ShapeDtypeStruct((2, nch, LANES, d), F32)],
        scratch_shapes=[pltpu.VMEM((LANES, d), F32), pltpu.VMEM((Q, d), BF16), pltpu.VMEM((Q, d), BF16), pltpu.VMEM((Q, d), F32)],
        args=(xbc, z_dt, dtb, alog, tri, ind_t))


def _ssd_bwd(xbc, z_dt, dtb, alog, tri, tri_t, ind_t, ind, d_y, y2, hs, d, ctx_len, side=None):
    t = xbc.shape[0]
    nch, ncc = t // Q, ctx_len // Q
    heads = d // HP
    groups = heads // 4
    gn = groups * LANES

    def body(xbc_ref, dt_ref, dtb_ref, alog_ref, tri_ref, trit_ref, indt_ref, ind_ref, dy_ref, y_ref, hs_ref,
             dx_ref, dzdt_ref, dbias_ref, dalog_ref, dh_scr, dtx_scr, ex_scr, xdb_scr, xde_scr, dyb_scr, dye_scr, dxd_scr, bdh_scr):
        @pl.when(pl.program_id(1) == 0)
        def _():
            dh_scr[...] = jnp.zeros_like(dh_scr)
            dbias_ref[...] = jnp.zeros_like(dbias_ref)
            dalog_ref[...] = jnp.zeros_like(dalog_ref)

        tri = tri_ref[0]
        mask = tri > 0.0
        mask_t = trit_ref[0] > 0.0
        raw, dt, a_neg, s, stot, s_t = _ssd_decays(dt_ref, dtb_ref, alog_ref, tri)
        etot = jnp.exp(stot)
        etot_x = _expand(jnp.broadcast_to(etot, (8, LANES)), indt_ref)[0:1]
        dtx_scr[...] = _expand(dt, indt_ref)
        ex_scr[...] = _expand(jnp.exp(stot - s), indt_ref)
        xd = xbc_ref[:, :d] * dtx_scr[...]
        xdb_scr[...] = xd.astype(BF16)
        xde_scr[...] = (xd * ex_scr[...]).astype(BF16)
        dyb_scr[...] = dy_ref[...].astype(BF16)
        dye_scr[...] = (dy_ref[...] * _dot(jnp.exp(s).astype(BF16), indt_ref[...])).astype(BF16)
        hd_cols = jnp.sum(dh_scr[...] * hs_ref[0, 0], axis=0, keepdims=True)
        left = lax.broadcasted_iota(jnp.int32, (Q, LANES), 1) < HP
        for g in range(groups):
            b32 = xbc_ref[:, d + g * LANES:d + (g + 1) * LANES]
            c32 = xbc_ref[:, d + gn + g * LANES:d + gn + (g + 1) * LANES]
            bb, cb = b32.astype(BF16), c32.astype(BF16)
            c_t = c32.T.astype(BF16)
            cbt = _dot(cb, bb, NT)
            cbt_t = _dot(bb, cb, NT)
            d_cbt = jnp.zeros((Q, Q), F32)
            gcols = slice(4 * g * HP, 4 * (g + 1) * HP)
            dyeb = dye_scr[:, gcols]
            dhg = dh_scr[:, gcols]
            dhb = dhg.astype(BF16)
            bdh_scr[:, gcols] = _dot(bb, dhb)
            d_c = _dot(dyeb, hs_ref[0, 0, :, gcols].astype(BF16), NT)
            d_b = _dot(xde_scr[:, gcols], dhb, NT)
            dh_scr[:, gcols] = dhg * etot_x[:, gcols] + _dot(c_t, dyeb)
            for pr in (2 * g, 2 * g + 1):
                h0 = 2 * pr
                cols = slice(pr * LANES, (pr + 1) * LANES)
                xdb = xdb_scr[:, cols]
                dyb = dyb_scr[:, cols]
                parts = []
                for hh, h in enumerate((h0, h0 + 1)):
                    mine = left if hh == 0 else jnp.logical_not(left)
                    diff = s[:, h:h + 1] - s_t[h:h + 1, :]
                    lm = jnp.exp(jnp.where(mask, diff, -jnp.inf))
                    lm_t = jnp.exp(jnp.where(mask_t, -diff, -jnp.inf))
                    gm = _dot(jnp.where(mine, dyb, jnp.zeros_like(dyb)), xdb, NT)
                    d_cbt = d_cbt + gm * lm
                    parts.append(_dot((cbt_t * lm_t).astype(BF16), dyb))
                dxd_scr[:, cols] = jnp.where(left, parts[0], parts[1])
            dx_ref[0, :, d + g * LANES:d + (g + 1) * LANES] = d_b + _dot(d_cbt.T.astype(BF16), cb)
            dx_ref[0, :, d + gn + g * LANES:d + gn + (g + 1) * LANES] = d_c + _dot(d_cbt.astype(BF16), bb)
        x = xbc_ref[:, :d]
        ebdh = ex_scr[...] * bdh_scr[...]
        d_xd = dxd_scr[...] + ebdh
        dx_ref[0, :, :d] = d_xd * dtx_scr[...]
        xe = x * dtx_scr[...] * ebdh
        d_s = _head_sums(dyb_scr[...].astype(F32) * y_ref[0] - xdb_scr[...].astype(F32) * dxd_scr[...] - xe, ind_ref)
        r_dx = _head_sums(d_xd * x, ind_ref)
        row8 = lax.broadcasted_iota(jnp.int32, (8, d), 0)
        tot = _head_sums(jnp.where(row8 == 0, jnp.sum(xe, axis=0, keepdims=True), jnp.where(row8 == 1, hd_cols, 0.0)), ind_ref)
        d_stot = tot[0:1] + etot * tot[1:2]
        d_a = _dot(trit_ref[0], d_s, NN, HI) + d_stot
        valid = lax.broadcasted_iota(jnp.int32, (Q, LANES), 1) < heads
        d_dt_tot = jnp.where(valid, d_a * a_neg + r_dx, 0.0)
        d_raw = d_dt_tot * _sigmoid(raw)
        dzdt_ref[...] = d_raw.astype(BF16)
        dbias_ref[0] += jnp.sum(d_raw, axis=0, keepdims=True)
        dalog_ref[0] += jnp.sum(jnp.where(valid, d_a * dt, 0.0), axis=0, keepdims=True) * a_neg

    cidx = lambda dd, ss: _scan_chunk(dd, nch - 1 - ss, nch, ncc)
    full = lambda shape: pltpu.VMEM(shape, F32)
    half = lambda shape: pltpu.VMEM(shape, BF16)
    return _hosted_call(
        body, side, name="ssd_bwd", grid=(2, nch),
        in_specs=[pl.BlockSpec((Q, 2 * d), lambda dd, ss: (cidx(dd, ss), 0)),
                  pl.BlockSpec((Q, LANES), lambda dd, ss: (cidx(dd, ss), dd)),
                  pl.BlockSpec((1, 1, LANES), lambda dd, ss: (dd, 0, 0)),
                  pl.BlockSpec((1, 1, LANES), lambda dd, ss: (dd, 0, 0)),
                  pl.BlockSpec((1, Q, Q), lambda dd, ss: (dd, 0, 0)),
                  pl.BlockSpec((1, Q, Q), lambda dd, ss: (dd, 0, 0)),
                  pl.BlockSpec((LANES, d), lambda dd, ss: (0, 0)),
                  pl.BlockSpec((d, LANES), lambda dd, ss: (0, 0)),
                  pl.BlockSpec((Q, d), lambda dd, ss: (cidx(dd, ss), 0)),
                  pl.BlockSpec((1, Q, d), lambda dd, ss: (dd, cidx(dd, ss), 0)),
                  pl.BlockSpec((1, 1, LANES, d), lambda dd, ss: (dd, cidx(dd, ss), 0, 0))],
        out_specs=[pl.BlockSpec((1, Q, 2 * d), lambda dd, ss: (dd, cidx(dd, ss), 0)),
                   pl.BlockSpec((Q, LANES), lambda dd, ss: (cidx(dd, ss), dd)),
                   pl.BlockSpec((1, 1, LANES), lambda dd, ss: (dd, 0, 0)),
                   pl.BlockSpec((1, 1, LANES), lambda dd, ss: (dd, 0, 0))],
        out_shape=[jax.ShapeDtypeStruct((2, t, 2 * d), F32), jax.ShapeDtypeStruct((t, 2 * LANES), BF16),
                   jax.ShapeDtypeStruct((2, 1, LANES), F32), jax.ShapeDtypeStruct((2, 1, LANES), F32)],
        scratch_shapes=[full((LANES, d)), full((Q, d)), full((Q, d)), half((Q, d)), half((Q, d)), half((Q, d)), half((Q, d)),
                        full((Q, d)), full((Q, d))],
        args=(xbc, z_dt, dtb, alog, tri, tri_t, ind_t, ind, d_y, y2, hs))


def _mix_common(zm_ref, y2_ref, xh_ref, dsk_ref, gv_ref, ws_ref, bst_ref, d):
    groups = d // LANES
    z_ssd, u, v, z_mlp = (zm_ref[:, k * d:(k + 1) * d] for k in range(4))
    y = y2_ref[0] + y2_ref[1] + dsk_ref[...] * xh_ref[...]
    sig_a = _sigmoid(z_ssd)
    ya_pre = y * (z_ssd * sig_a)
    r_v = _rms(v)
    vn = (v * r_v * gv_ref[...]).astype(BF16)
    sg = jnp.concatenate(
        [_dot(ws_ref[g].astype(BF16), vn[:, g * LANES:(g + 1) * LANES]) + bst_ref[:, g:g + 1] for g in range(groups)], axis=1)
    sig_m = _sigmoid(z_mlp)
    yb_pre = u * sg * (z_mlp * sig_m)
    return z_ssd, u, v, z_mlp, y, sig_a, ya_pre, r_v, vn, sg, sig_m, yb_pre


def _mix_fwd(z_mid, y2, xbc, dsk_row, g_ssd, g_v, g_mlp, w_s, b_st):
    t = z_mid.shape[0]
    d = z_mid.shape[1] // 4
    groups = d // LANES

    def body(zm_ref, y2_ref, xh_ref, dsk_ref, ga_ref, gv_ref, gm_ref, ws_ref, bst_ref, o_ref, ot_ref):
        (_, _, _, _, _, _, ya_pre, _, _, _, _, yb_pre) = _mix_common(zm_ref, y2_ref, xh_ref, dsk_ref, gv_ref, ws_ref, bst_ref, d)
        y_a = ya_pre * _rms(ya_pre) * ga_ref[...]
        y_b = yb_pre * _rms(yb_pre) * gm_ref[...]
        o_ref[:, :d] = y_a.astype(BF16)
        o_ref[:, d:] = y_b.astype(BF16)
        ot_ref[:d, :] = y_a.T.astype(BF16)
        ot_ref[d:, :] = y_b.T.astype(BF16)

    row = pl.BlockSpec((1, d), lambda i: (0, 0))
    return pl.pallas_call(
        body, name="mix_fwd", grid=(t // Q,),
        in_specs=[pl.BlockSpec((Q, 4 * d), lambda i: (i, 0)), pl.BlockSpec((2, Q, d), lambda i: (0, i, 0)),
                  pl.BlockSpec((Q, d), lambda i: (i, 0)), row, row, row, row,
                  pl.BlockSpec((groups, Q, Q), lambda i: (0, 0, 0)), pl.BlockSpec((Q, LANES), lambda i: (0, 0))],
        out_specs=[pl.BlockSpec((Q, 2 * d), lambda i: (i, 0)), pl.BlockSpec((2 * d, Q), lambda i: (0, i))],
        out_shape=[jax.ShapeDtypeStruct((t, 2 * d), BF16), jax.ShapeDtypeStruct((2 * d, t), BF16)], compiler_params=_params("arbitrary"),
    )(z_mid, y2, xbc, dsk_row, g_ssd, g_v, g_mlp, w_s, b_st)


def _mix_bwd(z_mid, y2, xbc, d_ycat, dsk_row, g_ssd, g_v, g_mlp, w_s, w_st, b_st, ind_head, ind_group):
    t = z_mid.shape[0]
    d = z_mid.shape[1] // 4
    groups = d // LANES
    nsteps = t // Q

    def body(zm_ref, y2_ref, xh_ref, dyc_ref, dsk_ref, ga_ref, gv_ref, gm_ref, ws_ref, wst_ref, bst_ref, ih_ref, ig_ref,
             dzm_ref, dy_ref, vec_ref, dws_ref, dbs_ref, dsk_acc, dsg_acc):
        i = pl.program_id(0)

        @pl.when(i == 0)
        def _():
            vec_ref[...] = jnp.zeros_like(vec_ref)
            dws_ref[...] = jnp.zeros_like(dws_ref)
            dsk_acc[...] = jnp.zeros_like(dsk_acc)
            dsg_acc[...] = jnp.zeros_like(dsg_acc)

        (z_ssd, u, v, z_mlp, y, sig_a, ya_pre, r_v, vn, sg, sig_m, yb_pre) = _mix_common(
            zm_ref, y2_ref, xh_ref, dsk_ref, gv_ref, ws_ref, bst_ref, d)
        d_ya = dyc_ref[:, :d]
        r_a = _rms(ya_pre)
        vec_ref[0:1, :] += jnp.sum(d_ya * (ya_pre * r_a), axis=0, keepdims=True)
        d_ya_pre = _rms_bwd(ya_pre, r_a, d_ya * ga_ref[...])
        d_y = d_ya_pre * (z_ssd * sig_a)
        dy_ref[...] = d_y
        dsk_acc[...] += jnp.sum(d_y * xh_ref[...], axis=0, keepdims=True)
        dzm_ref[:, 0:d] = (d_ya_pre * y * (sig_a * (1.0 + z_ssd * (1.0 - sig_a)))).astype(BF16)
        d_yb = dyc_ref[:, d:]
        r_b = _rms(yb_pre)
        vec_ref[2:3, :] += jnp.sum(d_yb * (yb_pre * r_b), axis=0, keepdims=True)
        d_yb_pre = _rms_bwd(yb_pre, r_b, d_yb * gm_ref[...])
        silu_m = z_mlp * sig_m
        dzm_ref[:, d:2 * d] = (d_yb_pre * sg * silu_m).astype(BF16)
        dzm_ref[:, 3 * d:4 * d] = (d_yb_pre * u * sg * (sig_m * (1.0 + z_mlp * (1.0 - sig_m)))).astype(BF16)
        d_sg = d_yb_pre * u * silu_m
        dsg_acc[...] += d_sg
        d_sgb = d_sg.astype(BF16)
        d_vn = []
        for g in range(groups):
            cols = slice(g * LANES, (g + 1) * LANES)
            dws_ref[g] += _dot(d_sgb[:, cols], vn[:, cols], NT)
            d_vn.append(_dot(wst_ref[g].astype(BF16), d_sgb[:, cols]))
        d_vn = jnp.concatenate(d_vn, axis=1)
        vec_ref[1:2, :] += jnp.sum(d_vn * (v * r_v), axis=0, keepdims=True)
        dzm_ref[:, 2 * d:3 * d] = _rms_bwd(v, r_v, d_vn * gv_ref[...]).astype(BF16)

        @pl.when(i == nsteps - 1)
        def _():
            vec_ref[3:4, 0:LANES] = _dot(dsk_acc[...], ih_ref[...], NN, HI)
            dbs_ref[...] = _dot(dsg_acc[...], ig_ref[...], NN, HI)

    row = pl.BlockSpec((1, d), lambda i: (0, 0))
    wsp = pl.BlockSpec((groups, Q, Q), lambda i: (0, 0, 0))
    ind = pl.BlockSpec((d, LANES), lambda i: (0, 0))
    return pl.pallas_call(
        body, name="mix_bwd", grid=(nsteps,),
        in_specs=[pl.BlockSpec((Q, 4 * d), lambda i: (i, 0)), pl.BlockSpec((2, Q, d), lambda i: (0, i, 0)),
                  pl.BlockSpec((Q, d), lambda i: (i, 0)), pl.BlockSpec((Q, 2 * d), lambda i: (i, 0)),
                  row, row, row, row, wsp, wsp, pl.BlockSpec((Q, LANES), lambda i: (0, 0)), ind, ind],
        out_specs=[pl.BlockSpec((Q, 4 * d), lambda i: (i, 0)), pl.BlockSpec((Q, d), lambda i: (i, 0)),
                   pl.BlockSpec((8, d), lambda i: (0, 0)), wsp, pl.BlockSpec((Q, LANES), lambda i: (0, 0))],
        out_shape=[jax.ShapeDtypeStruct((t, 4 * d), BF16), jax.ShapeDtypeStruct((t, d), F32),
                   jax.ShapeDtypeStruct((8, d), F32), jax.ShapeDtypeStruct((groups, Q, Q), F32),
                   jax.ShapeDtypeStruct((Q, LANES), F32)],
        scratch_shapes=[pltpu.VMEM((1, d), F32), pltpu.VMEM((Q, d), F32)],
        compiler_params=_params("arbitrary"),
    )(z_mid, y2, xbc, d_ycat, dsk_row, g_ssd, g_v, g_mlp, w_s, w_st, b_st, ind_head, ind_group)


def _ada_fwd(c16, w_ada, b_loc):
    depth, d, n = w_ada.shape
    tn = _tile(n, (512, 256, 128))

    def body(c_ref, w_ref, b_ref, o_ref):
        cv = c_ref[...]
        o_ref[0] = _dot(cv * _sigmoid(cv), w_ref[0], NN, HI) + b_ref[0]

    return pl.pallas_call(
        body, name="ada_fwd", grid=(depth, n // tn),
        in_specs=[pl.BlockSpec((16, d), lambda l, j: (0, 0)), pl.BlockSpec((1, d, tn), lambda l, j: (l, 0, j)),
                  pl.BlockSpec((1, 1, tn), lambda l, j: (l, 0, j))],
        out_specs=pl.BlockSpec((1, 16, tn), lambda l, j: (l, 0, j)),
        out_shape=jax.ShapeDtypeStruct((depth, 16, n), F32), compiler_params=_params("arbitrary", "arbitrary"),
    )(c16, w_ada, b_loc)


def _ada_bwd(c_t, dm_loc, w_ada):
    depth, d, n = w_ada.shape
    tn = _tile(n, (512, 256, 128))

    def body(s_ref, dm_ref, w_ref, gw_ref, dsc_ref):
        @pl.when((pl.program_id(0) == 0) & (pl.program_id(1) == 0))
        def _():
            dsc_ref[...] = jnp.zeros_like(dsc_ref)

        cv = s_ref[...]
        gw_ref[0] = _dot(cv * _sigmoid(cv), dm_ref[0], NN, HI)
        dsc_ref[...] += _dot(dm_ref[0, 8:16, :], w_ref[0], NT, HI)

    return pl.pallas_call(
        body, name="ada_bwd", grid=(depth, n // tn),
        in_specs=[pl.BlockSpec((d, LANES), lambda l, j: (0, 0)), pl.BlockSpec((1, LANES, tn), lambda l, j: (l, 0, j)),
                  pl.BlockSpec((1, d, tn), lambda l, j: (l, 0, j))],
        out_specs=[pl.BlockSpec((1, d, tn), lambda l, j: (l, 0, j)), pl.BlockSpec((8, d), lambda l, j: (0, 0))],
        out_shape=[jax.ShapeDtypeStruct((depth, d, n), F32), jax.ShapeDtypeStruct((8, d), F32)],
        compiler_params=_params("arbitrary", "arbitrary"),
    )(c_t, dm_loc, w_ada)


def _rowsum(x):
    depth, r, n = x.shape

    def body(x_ref, o_ref):
        o_ref[0] = jnp.sum(x_ref[0], axis=0, keepdims=True)

    return pl.pallas_call(
        body, name="rowsum", grid=(depth,),
        in_specs=[pl.BlockSpec((1, r, n), lambda l: (l, 0, 0))], out_specs=pl.BlockSpec((1, 1, n), lambda l: (l, 0, 0)),
        out_shape=jax.ShapeDtypeStruct((depth, 1, n), F32), compiler_params=_params("arbitrary"),
    )(x)


def _cctx_grad(d_scc, c_ctx_row):
    def body(g_ref, c_ref, o_ref):
        cv = c_ref[...]
        sig = _sigmoid(cv)
        o_ref[...] = g_ref[...] * (sig * (1.0 + cv * (1.0 - sig)))

    return pl.pallas_call(body, name="cctx_grad", out_shape=jax.ShapeDtypeStruct(c_ctx_row.shape, F32))(d_scc, c_ctx_row)


def _sum_lead(x, name):
    k, r, c = x.shape
    tr = _tile(r, [tt for tt in (1024, 512, 256, 128, 64, 32, 16, 8) if k * tt * c * x.dtype.itemsize <= SUM_BLOCK_BYTES])

    def body(x_ref, o_ref):
        acc = x_ref[0].astype(F32)
        for e in range(1, k):
            acc = acc + x_ref[e].astype(F32)
        o_ref[...] = acc

    return pl.pallas_call(
        body, name=name, grid=(r // tr,),
        in_specs=[pl.BlockSpec((k, tr, c), lambda i: (0, i, 0))], out_specs=pl.BlockSpec((tr, c), lambda i: (i, 0)),
        out_shape=jax.ShapeDtypeStruct((r, c), F32), compiler_params=_params("arbitrary"),
    )(x)


def _adamw(w, g, m, v, name, g2=None, side=None):
    r, c = w.shape
    tr = _tile(r, [tt for tt in (2048, 1024, 512, 256, 128, 64, 32, 16, 8) if tt * c * 4 <= ADAM_BLOCK_BYTES])
    two = g2 is not None
    bc1 = 1.0 - ADAM_B1 ** ADAM_STEP
    bc2 = 1.0 - ADAM_B2 ** ADAM_STEP

    def body(*refs):
        if two:
            w_ref, g_ref, g2_ref, m_ref, v_ref, go_ref, d_ref, mo_ref, vo_ref = refs
            gr = g_ref[...] + g2_ref[...]
        else:
            w_ref, g_ref, m_ref, v_ref, go_ref, d_ref, mo_ref, vo_ref = refs
            gr = g_ref[...]
        mn = ADAM_B1 * m_ref[...] + (1.0 - ADAM_B1) * gr
        vn = ADAM_B2 * v_ref[...] + (1.0 - ADAM_B2) * (gr * gr)
        go_ref[...] = gr
        mo_ref[...] = mn
        vo_ref[...] = vn
        d_ref[...] = -ADAM_LR * ((mn / bc1) / (jnp.sqrt(vn / bc2) + ADAM_EPS) + ADAM_WD * w_ref[...])

    blk = pl.BlockSpec((tr, c), lambda i: (i, 0))
    ins = (w, g, g2, m, v) if two else (w, g, m, v)
    return _hosted_call(body, side, name=name, grid=(r // tr,), in_specs=[blk] * len(ins), out_specs=[blk] * 4,
                        out_shape=[jax.ShapeDtypeStruct((r, c), F32)] * 4, scratch_shapes=[], args=ins)


def _flip(pos, k):
    x, y, c = pos
    return (x ^ ((k >> 2) & 1), y ^ ((k >> 1) & 1), c ^ (k & 1))


def _lin(pos):
    return 4 * pos[0] + 2 * pos[1] + pos[2]


def _chip(pos):
    return 2 * pos[0] + pos[1]


def _here():
    return (lax.axis_index("x"), lax.axis_index("y"), lax.axis_index("c"))


class _Exchange:
    def __init__(self, x, kind):
        self.x, self.kind = x, kind
        self.masks = {"gather4": (2, 4, 6), "scatter4": (2, 4, 6), "gather8": tuple(range(1, 8)), "swap": (1,)}[kind]
        self.slot = {"gather4": _chip, "scatter4": _chip, "gather8": _lin, "swap": None}[kind]
        lead = {"gather4": (4,), "scatter4": (), "gather8": (8,), "swap": ()}[kind]
        self.out_shape = jax.ShapeDtypeStruct(lead + x.shape, x.dtype)
        n = len(self.masks)
        self.scratch = [pltpu.SemaphoreType.DMA((n,)), pltpu.SemaphoreType.DMA((n,))] + ([] if kind == "swap" else [pltpu.SemaphoreType.DMA])

    def _copies(self, x_ref, o_ref, send, recv, *own, arrivals):
        me = _here()
        src = (lambda pos: x_ref.at[_chip(pos)]) if self.kind == "scatter4" else (lambda pos: x_ref)
        dst = (lambda pos: o_ref.at[self.slot(pos)]) if self.slot else (lambda pos: o_ref)
        local = [pltpu.make_async_copy(src(me), dst(me), own[0])] if own else []
        outs, ins = [], []
        for j, k in enumerate(self.masks):
            peer = _flip(me, k)
            sems = dict(send_sem=send.at[j], recv_sem=recv.at[j], device_id=peer, device_id_type=MESH)
            outs.append(pltpu.make_async_remote_copy(src_ref=src(peer), dst_ref=dst(me), **sems))
            if arrivals:
                ins.append(pltpu.make_async_remote_copy(src_ref=src(me), dst_ref=dst(peer), **sems))
        return local, outs, ins

    def start(self, *refs):
        local, outs, _ = self._copies(*refs, arrivals=False)
        for cp in local + outs:
            cp.start()

    def wait(self, *refs):
        local, outs, ins = self._copies(*refs, arrivals=True)
        for cp in ins:
            cp.wait_recv()
        for cp in outs:
            cp.wait_send()
        for cp in local:
            cp.wait()


def _exchange(x, kind, name):
    ex = _Exchange(x, kind)

    def body(*refs):
        ex.start(*refs)
        ex.wait(*refs)

    return pl.pallas_call(body, name=name, in_specs=[ANY], out_specs=ANY, out_shape=ex.out_shape, scratch_shapes=ex.scratch)(x)


def _gather4_two_level(x, name):
    half = x.shape[0] // 2

    def body(x_ref, o_ref, send1, recv1, send2, recv2, own):
        me = _here()
        sibling = _flip(me, 1)
        mine, theirs = pl.ds(me[2] * half, half), pl.ds((1 - me[2]) * half, half)
        peers = [_flip(me, k) for k in (2, 4, 6)]

        def over_ici(j, src_chip, rows, src=None):
            dst = o_ref.at[_chip(src_chip), rows]
            return pltpu.make_async_remote_copy(src_ref=dst if src is None else src, dst_ref=dst, send_sem=send1.at[j],
                                                recv_sem=recv1.at[j], device_id=peers[j], device_id_type=MESH)

        def over_d2d(j, rows):
            blk = o_ref.at[_chip(peers[j]), rows]
            return pltpu.make_async_remote_copy(src_ref=blk, dst_ref=blk, send_sem=send2.at[j], recv_sem=recv2.at[j],
                                                device_id=sibling, device_id_type=MESH)

        local = pltpu.make_async_copy(x_ref, o_ref.at[_chip(me)], own)
        local.start()
        sent = [over_ici(j, me, mine, src=x_ref.at[mine]) for j in range(3)]
        for cp in sent:
            cp.start()
        passed = [over_d2d(j, mine) for j in range(3)]
        for j in range(3):
            over_ici(j, peers[j], mine).wait_recv()
            passed[j].start()
        for j in range(3):
            over_d2d(j, theirs).wait_recv()
        for cp in sent + passed:
            cp.wait_send()
        local.wait()

    sems = pltpu.SemaphoreType.DMA((3,))
    return pl.pallas_call(body, name=name, in_specs=[ANY], out_specs=ANY, out_shape=jax.ShapeDtypeStruct((4,) + x.shape, x.dtype),
                          scratch_shapes=[sems, sems, sems, sems, pltpu.SemaphoreType.DMA])(x)


def _hosted_call(body, side, *, name, grid, in_specs, out_specs, out_shape, scratch_shapes, args):
    sides = [] if side is None else list(side) if isinstance(side, (list, tuple)) else [side]
    n_in, n_out, ns = len(in_specs), len(out_specs), len(sides)
    params = _params(*(["arbitrary"] * len(grid)))
    if not sides:
        return pl.pallas_call(body, name=name, grid=grid, in_specs=in_specs, out_specs=out_specs, out_shape=out_shape,
                              scratch_shapes=scratch_shapes, compiler_params=params)(*args)
    n_sem = [len(s.scratch) for s in sides]

    def hosted(*refs):
        ins, xs = refs[:n_in], refs[n_in:n_in + ns]
        outs, os_ = refs[n_in + ns:n_in + ns + n_out], refs[n_in + ns + n_out:n_in + 2 * ns + n_out]
        rest = refs[n_in + 2 * ns + n_out:]
        scratch, sems = rest[:len(rest) - sum(n_sem)], list(rest[len(rest) - sum(n_sem):])
        per_side = [[sems.pop(0) for _ in range(k)] for k in n_sem]
        ids = [pl.program_id(a) for a in range(len(grid))]
        first, last = ids[0] == 0, ids[0] == grid[0] - 1
        for a in range(1, len(grid)):
            first, last = first & (ids[a] == 0), last & (ids[a] == grid[a] - 1)

        @pl.when(first)
        def _():
            for s, x_ref, o_ref, sm in zip(sides, xs, os_, per_side):
                s.start(x_ref, o_ref, *sm)

        body(*ins, *outs, *scratch)

        @pl.when(last)
        def _():
            for s, x_ref, o_ref, sm in zip(sides, xs, os_, per_side):
                s.wait(x_ref, o_ref, *sm)

    return pl.pallas_call(hosted, name=name + "_x_" + "_".join(s.kind for s in sides), grid=grid, in_specs=list(in_specs) + [ANY] * ns,
                          out_specs=list(out_specs) + [ANY] * ns, out_shape=list(out_shape) + [s.out_shape for s in sides],
                          scratch_shapes=list(scratch_shapes) + [sem for s in sides for sem in s.scratch],
                          compiler_params=params)(*args, *[s.x for s in sides])


def _pad_lanes(a, width):
    return jnp.pad(a, [(0, 0)] * (a.ndim - 1) + [(0, width - a.shape[-1])])


def kernel(x, c, ctx, c_ctx, w_ada, b_ada, g_pre, g_post, w_in, conv_w, conv_b, dt_bias, a_log, d_skip, g_ssd, g_v, w_s, b_s, g_mlp, w_out, loss_target, m_c_ctx, m_w_ada, m_b_ada, m_g_pre, m_g_post, m_w_in, m_conv_w, m_conv_b, m_dt_bias, m_a_log, m_d_skip, m_g_ssd, m_g_v, m_w_s, m_b_s, m_g_mlp, m_w_out, v_c_ctx, v_w_ada, v_b_ada, v_g_pre, v_g_post, v_w_in, v_conv_w, v_conv_b, v_dt_bias, v_a_log, v_d_skip, v_g_ssd, v_g_v, v_w_s, v_b_s, v_g_mlp, v_w_out):
    depth, d = g_pre.shape
    seq, ctx_len = x.shape[1], ctx.shape[1]
    heads = d // HP
    in_w = 6 * d + 2 * heads
    groups_mlp = d // LANES
    t = ctx_len + seq
    assert ctx_len == TB and seq % TB == 0 and TB % ROW == 0 and heads % 4 == 0 and heads <= LANES and d % LANES == 0
    assert w_in.shape == (depth, d, in_w // 4)

    xi, yi, ci = lax.axis_index("x"), lax.axis_index("y"), lax.axis_index("c")
    chip = 2 * xi + yi
    me = 4 * xi + 2 * yi + ci

    n_ada = 3 * d // 4
    c_all = _exchange(c, "gather8", "ag_c")[:, 0, :]
    c16 = jnp.concatenate([c_all, c_ctx[None, :], jnp.zeros((7, d), F32)], axis=0)
    b_loc = lax.dynamic_slice_in_dim(b_ada, chip * n_ada, n_ada, axis=1)[:, None, :]
    mods_loc = _exchange(_ada_fwd(c16, w_ada, b_loc).reshape(depth * 16, n_ada), "gather8", "ag_mods")
    mods_loc = mods_loc.reshape(4, 2, depth, 16, n_ada)[:, 0]
    mods_full = jnp.moveaxis(mods_loc, 0, 2).reshape(depth, 16, 3 * d)
    mods_x = lax.dynamic_index_in_dim(mods_full, me, axis=1, keepdims=False).reshape(depth, 3, d)
    mods_c = mods_full[:, 8, :].reshape(depth, 3, d)
    mods = _pad_rows8(jnp.stack([mods_c, mods_x], axis=1))

    w_in_b, w_out_b = w_in.astype(BF16), w_out.astype(BF16)

    def lay_out(w_in_rows):
        full = jnp.moveaxis(jnp.concatenate(w_in_rows, axis=1), 0, 1).reshape(d, in_w)
        w_dt_l = jnp.concatenate([_pad_lanes(full[:, 2 * d:2 * d + heads], LANES),
                                  _pad_lanes(full[:, 2 * d + heads:2 * d + 2 * heads], LANES)], axis=1)
        return full[:, :2 * d], full[:, 2 * d + 2 * heads:], w_dt_l

    w_in_rows = [_gather4_two_level(w_in_b[0], "ag_w_in")]
    r_scan = 3 * d // 4
    conv_w_full = jnp.moveaxis(_exchange(conv_w, "gather4", "ag_conv_w"), 0, 2).reshape(depth, CONV_TAPS, 2 * d)
    conv_w8 = jnp.pad(conv_w_full, ((0, 0), (0, 8 - CONV_TAPS), (0, 0)))

    tri = jnp.stack([jnp.tril(jnp.ones((Q, Q), F32)), jnp.triu(jnp.ones((Q, Q), F32))])
    tri_t = jnp.swapaxes(tri, 1, 2)
    dtb = _pad_lanes(dt_bias, LANES)[:, :, None, :]
    alog = _pad_lanes(a_log, LANES)[:, :, None, :]
    dsk_row = jnp.repeat(d_skip, HP, axis=1)[:, None, :]
    w_st = jnp.swapaxes(w_s, 2, 3)
    b_st = _pad_lanes(jnp.swapaxes(b_s, 1, 2), LANES)
    chan = jnp.arange(d)
    ind_head = (chan[:, None] // HP == jnp.arange(LANES)[None, :]).astype(F32)
    ind_b, ind_t = ind_head.astype(BF16), ind_head.T.astype(BF16)
    ind_group = (chan[:, None] // LANES == jnp.arange(LANES)[None, :]).astype(F32)

    stream = jnp.concatenate([ctx[0], x[0]], axis=0)
    saved = []
    for l in range(depth):
        w_xbc, w_mid, w_dt = lay_out(w_in_rows)
        more = l + 1 < depth
        hx, hx_t = _pre_fwd(stream, g_pre[l][None], mods[l])
        z_xbc = _mm(hx, w_xbc, NN, "in_xbc")
        z_mid, w_out_all = _mm(hx, w_mid, NN, "in_mid", side=_Exchange(w_out_b[l], "gather4"))
        w_o = w_out_all.reshape(2 * d, d)
        z_dt = _mm(hx, w_dt, NN, "in_dt")
        xbc, dsilu = _conv_fwd(z_xbc, conv_w8[l], conv_b[l][None], ctx_len)
        y2, hs, *rows_a = _ssd_fwd(xbc, z_dt, dtb[l], alog[l], tri, ind_t, d, ctx_len,
                                   side=_Exchange(w_in_b[l + 1, :r_scan], "gather4") if more else None)
        ycat, ycat_t = _mix_fwd(z_mid, y2, xbc, dsk_row[l], g_ssd[l][None], g_v[l][None], g_mlp[l][None], w_s[l], b_st[l])
        if more:
            o, rows_b = _mm(ycat, w_o, NN, "out_proj", side=_Exchange(w_in_b[l + 1, r_scan:], "gather4"))
            w_in_rows = [rows_a[0], rows_b]
        else:
            o = _mm(ycat, w_o, NN, "out_proj")
        saved.append((stream, hx_t, z_xbc, dsilu, z_mid, z_dt, xbc, y2, hs, ycat_t, o, w_xbc, w_mid, w_dt, w_o))
        stream = _post_fwd(o, stream, g_post[l][None], mods[l])

    sq, d_stream = _loss_grad(stream, loss_target[0])
    loss = lax.psum(0.5 / d * sq[0, 0], ("x", "y", "c"))

    small = []
    dmods = []
    q_in = in_w // 4
    names = ["g_post", "conv_w", "conv_b", "dt_bias", "a_log", "d_skip", "g_ssd", "g_v", "w_s", "b_s", "g_mlp"]

    def rows_of(a):
        return -(-a.size // (8 * LANES)) * 8

    def pack(arrays):
        blocks = [jnp.pad(a.reshape(-1), (0, rows_of(a) * LANES - a.size)).reshape(rows_of(a), LANES) for a in arrays]
        rows = sum(b.shape[0] for b in blocks)
        return jnp.pad(jnp.concatenate(blocks, axis=0), ((0, -(-rows // PACK_ROWS) * PACK_ROWS - rows), (0, 0)))

    def unpack(block, likes):
        out, row = [], 0
        for a in likes:
            out.append(block[row:row + rows_of(a)].reshape(-1)[:a.size].reshape(a.shape))
            row += rows_of(a)
        return out

    def quarter_parts(g_xbc, g_mid, g_dt):
        segs = [(g_xbc, 0, 2 * d), (g_dt, 0, heads), (g_dt, LANES, heads), (g_mid, 0, 4 * d)]
        parts = []
        for qi in range(4):
            lo, hi, off, pieces = qi * q_in, (qi + 1) * q_in, 0, []
            for arr, start, width in segs:
                a, b = max(lo, off), min(hi, off + width)
                if a < b:
                    pieces.append(arr[:, start + a - off:start + b - off])
                off += width
            parts.append(pieces[0] if len(pieces) == 1 else jnp.concatenate(pieces, axis=1))
        return jnp.stack(parts)

    sum_in, sum_out = [None] * depth, [None] * depth
    swap_in, swap_out = [None] * depth, [None] * depth
    small_sum = [None] * depth
    parts_in = packed = None
    for l in reversed(range(depth)):
        x_in, hx_t, z_xbc, dsilu, z_mid, z_dt, xbc, y2, hs, ycat_t, o, w_xbc, w_mid, w_dt, w_o = saved[l]
        up = l + 1
        d_o, acc_post = _post_bwd(d_stream, o, g_post[l][None], mods[l])
        if packed is not None:
            d_ycat, gathered = _mm(d_o, w_o, NT, "d_ycat", side=_Exchange(packed, "gather8"))
            small_sum[up] = _sum_lead(gathered, "sum_small")
        else:
            d_ycat = _mm(d_o, w_o, NT, "d_ycat")
        g_out = _mm(ycat_t, d_o, NN, "dw_out", out_dtype=BF16)
        dz_mid, d_y, vec, d_ws, d_bs = _mix_bwd(z_mid, y2, xbc, d_ycat, dsk_row[l], g_ssd[l][None], g_v[l][None], g_mlp[l][None],
                                                w_s[l], w_st[l], b_st[l], ind_head, ind_group)
        d_xbc2, dz_dt, d_bias, d_alog, *got = _ssd_bwd(xbc, z_dt, dtb[l], alog[l], tri, tri_t, ind_t, ind_b, d_y, y2, hs, d, ctx_len,
                                                       side=_Exchange(parts_in, "scatter4") if parts_in is not None else None)
        if parts_in is not None:
            sum_in[up] = _sum_lead(got[0], "sum_w_in")
        dz_xbc, d_cw, d_cb = _conv_bwd(z_xbc, dsilu, d_xbc2, d_y, dsk_row[l], conv_w8[l], ctx_len)
        g_xbc, got_out = _mm(hx_t, dz_xbc, NN, "dw_xbc", out_dtype=BF16, side=_Exchange(g_out.reshape(4, 2 * d // 4, d), "scatter4"))
        sum_out[l] = _sum_lead(got_out, "sum_w_out")
        swaps = [_Exchange(sum_out[l], "swap")] + ([_Exchange(sum_in[up], "swap")] if parts_in is not None else [])
        g_mid, swap_out[l], *swapped = _mm(hx_t, dz_mid, NN, "dw_mid", out_dtype=BF16, side=swaps)
        if parts_in is not None:
            swap_in[up] = swapped[0]
        g_dt = _mm(hx_t, dz_dt, NN, "dw_dt", out_dtype=BF16)
        parts_in = quarter_parts(g_xbc, g_mid, g_dt)
        small.append(dict(
            g_post=acc_post[0, 1] + acc_post[1, 1], conv_w=d_cw[:CONV_TAPS], conv_b=d_cb[0],
            dt_bias=d_bias[:, 0, :heads], a_log=d_alog[:, 0, :heads], d_skip=vec[3, :heads], g_ssd=vec[0], g_v=vec[1],
            w_s=d_ws, b_s=d_bs[:, :groups_mlp].T, g_mlp=vec[2]))
        packed = pack([small[-1][n] for n in names])
        dz_w = [(dz_xbc, w_xbc), (dz_mid, w_mid), (dz_dt, w_dt)]
        if l == 0:
            d_hx, got = _mm_sum_nt(dz_w, "dhx", side=_Exchange(parts_in, "scatter4"))
            sum_in[0] = _sum_lead(got, "sum_w_in")
        else:
            d_hx = _mm_sum_nt(dz_w, "dhx")
        if l == 0:
            d_stream, acc_pre, swap_in[0], gathered = _pre_bwd(
                x_in, d_hx, d_stream, g_pre[l][None], mods[l], latent_only=True,
                side=[_Exchange(sum_in[0], "swap"), _Exchange(packed, "gather8")])
            small_sum[0] = _sum_lead(gathered, "sum_small")
        else:
            d_stream, acc_pre = _pre_bwd(x_in, d_hx, d_stream, g_pre[l][None], mods[l])
        dmods.append(jnp.concatenate([acc_pre[:, 0], acc_pre[:, 1], acc_post[:, 0], acc_pre[:, 2]], axis=1))
    small.reverse(), dmods.reverse()
    grad_x = d_stream[None]

    weights = dict(c_ctx=c_ctx, w_ada=w_ada, b_ada=b_ada, g_pre=g_pre, g_post=g_post, w_in=w_in, conv_w=conv_w, conv_b=conv_b,
                   dt_bias=dt_bias, a_log=a_log, d_skip=d_skip, g_ssd=g_ssd, g_v=g_v, w_s=w_s, b_s=b_s, g_mlp=g_mlp, w_out=w_out)
    m_in = dict(c_ctx=m_c_ctx, w_ada=m_w_ada, b_ada=m_b_ada, g_pre=m_g_pre, g_post=m_g_post, w_in=m_w_in, conv_w=m_conv_w,
                conv_b=m_conv_b, dt_bias=m_dt_bias, a_log=m_a_log, d_skip=m_d_skip, g_ssd=m_g_ssd, g_v=m_g_v, w_s=m_w_s,
                b_s=m_b_s, g_mlp=m_g_mlp, w_out=m_w_out)
    v_in = dict(c_ctx=v_c_ctx, w_ada=v_w_ada, b_ada=v_b_ada, g_pre=v_g_pre, g_post=v_g_post, w_in=v_w_in, conv_w=v_conv_w,
                conv_b=v_conv_b, dt_bias=v_dt_bias, a_log=v_a_log, d_skip=v_d_skip, g_ssd=v_g_ssd, g_v=v_g_v, w_s=v_w_s,
                b_s=v_b_s, g_mlp=v_g_mlp, w_out=v_w_out)
    order = list(weights)
    results = {}

    def adamw_big(n, ga, gb):
        shp = weights[n].shape
        two = lambda a: a.reshape(-1, shp[-1])
        results[n] = [r.reshape(shp) for r in _adamw(two(weights[n]), ga, two(m_in[n]), two(v_in[n]), "adamw_" + n, g2=gb)]

    adamw_big("w_in", jnp.concatenate(sum_in, axis=0), jnp.concatenate(swap_in, axis=0))
    adamw_big("w_out", jnp.concatenate(sum_out, axis=0), jnp.concatenate(swap_out, axis=0))
    per_layer = [unpack(small_sum[l], [small[l][n] for n in names]) for l in range(depth)]
    grads = {n: jnp.stack([per_layer[l][j] for l in range(depth)]) for j, n in enumerate(names)}
    grads["conv_w"] = lax.dynamic_slice_in_dim(grads["conv_w"], chip * (2 * d // 4), 2 * d // 4, axis=2)

    dm_all = _exchange(jnp.stack(dmods).reshape(depth * 2, 4 * d), "gather8", "ag_dmods").reshape(8, depth, 2, 4 * d)
    grads["g_pre"] = _sum_lead(jnp.moveaxis(dm_all[..., 3 * d:], 2, 1).reshape(16, depth, d), "sum_g_pre")
    dm_ctx = _sum_lead(dm_all[:, :, 0, :3 * d], "sum_dm_ctx")
    dm16 = jnp.concatenate([jnp.moveaxis(dm_all[:, :, 1, :3 * d], 0, 1), dm_ctx[:, None, :], jnp.zeros((depth, 7, 3 * d), F32)], axis=1)
    grads["b_ada"] = _rowsum(dm16)[:, 0, :]
    dm_loc = jnp.pad(lax.dynamic_slice_in_dim(dm16, chip * n_ada, n_ada, axis=2), ((0, 0), (0, LANES - 16), (0, 0)))
    c_t = jnp.pad(c16.T, ((0, 0), (0, LANES - 16)))
    g_w_ada, d_scc_part = _ada_bwd(c_t, dm_loc, w_ada)
    adamw_big("w_ada", g_w_ada.reshape(depth * d, n_ada), None)
    d_scc = _sum_lead(_exchange(d_scc_part, "gather8", "ag_dscc").reshape(4, 2, 8, d)[:, 0], "sum_dscc")
    grads["c_ctx"] = _cctx_grad(d_scc[0:1], c_ctx[None])[0]

    rest = [n for n in order if n not in results]
    outs = _adamw(pack([weights[n] for n in rest]), pack([grads[n] for n in rest]), pack([m_in[n] for n in rest]),
                  pack([v_in[n] for n in rest]), "adamw_small")
    for j, res in enumerate(zip(*[unpack(o_, [weights[n] for n in rest]) for o_ in outs])):
        results[rest[j]] = list(res)

    return (loss, grad_x, *[results[n][0] for n in order], *[results[n][1] for n in order],
            *[results[n][2] for n in order], *[results[n][3] for n in order])


def _pad_rows8(a):
    return jnp.pad(a, [(0, 0)] * (a.ndim - 2) + [(0, 8 - a.shape[-2]), (0, 0)])
```

```python
import jax
import jax.numpy as jnp
from jax import lax
from jax.experimental import pallas as pl
from jax.experimental.pallas import tpu as pltpu

F32 = jnp.float32
BF16 = jnp.bfloat16
EPS = 1e-6
Q = 128
TB = 256
ROW = 64
HP = 64
LANES = 128
CONV_TAPS = 5
VMEM_LIMIT = 48 * 1024 * 1024
HI = lax.Precision.HIGHEST
SUM_BLOCK_BYTES = 4 * 1024 * 1024
ADAM_BLOCK_BYTES = 1024 * 1024
PACK_ROWS = 256
MESH = pl.DeviceIdType.MESH
ANY = pl.BlockSpec(memory_space=pl.ANY)

ADAM_LR, ADAM_B1, ADAM_B2, ADAM_EPS, ADAM_WD, ADAM_STEP = 0.001, 0.9, 0.999, 1e-08, 0.01, 10

NN = (((1,), (0,)), ((), ()))
NT = (((1,), (1,)), ((), ()))
TN = (((0,), (0,)), ((), ()))


def _dot(a, b, dims=NN, prec=None):
    return lax.dot_general(a, b, dims, precision=prec, preferred_element_type=F32)


def _params(*sem):
    if sem:
        return pltpu.CompilerParams(vmem_limit_bytes=VMEM_LIMIT, dimension_semantics=sem)
    return pltpu.CompilerParams(vmem_limit_bytes=VMEM_LIMIT)


def _tile(dim, cands):
    for t in cands:
        if dim % t == 0:
            return t
    return dim


def _sigmoid(x):
    return 1.0 / (1.0 + jnp.exp(-x))


def _softplus(x):
    e = jnp.exp(-jnp.abs(x))
    u = 1.0 + e
    um1 = u - 1.0
    l1p = jnp.where(um1 == 0.0, e, jnp.log(u) * (e / jnp.where(um1 == 0.0, 1.0, um1)))
    return jnp.maximum(x, 0.0) + l1p


def _rms(x):
    return lax.rsqrt(jnp.mean(x * x, axis=-1, keepdims=True) + EPS)


def _rms_bwd(x, r, t):
    return r * t - x * (r * r * r) * jnp.mean(x * t, axis=-1, keepdims=True)


def _mm(a, b, dims, name, acc=None, out_dtype=F32, side=None):
    (ca,), (cb,) = dims[0]
    m, k = a.shape[1 - ca], a.shape[ca]
    n = b.shape[1 - cb]
    tm = _tile(m, (1024, 768, 512, 384, 256, 128))
    tn = _tile(n, (1024, 512, 256, 128))
    tk = k if k <= 2048 else _tile(k, (2048, 768, 512, 384, 256, 128))
    nk = k // tk
    a_spec = pl.BlockSpec((tm, tk), lambda i, j, kk: (i, kk)) if ca == 1 else pl.BlockSpec((tk, tm), lambda i, j, kk: (kk, i))
    b_spec = pl.BlockSpec((tk, tn), lambda i, j, kk: (kk, j)) if cb == 0 else pl.BlockSpec((tn, tk), lambda i, j, kk: (j, kk))
    o_spec = pl.BlockSpec((tm, tn), lambda i, j, kk: (i, j))
    has_acc = acc is not None

    def body(*refs):
        if has_acc:
            a_ref, b_ref, c_ref, o_ref, acc_ref = refs
        else:
            a_ref, b_ref, o_ref, acc_ref = refs
        kk = pl.program_id(2)

        @pl.when(kk == 0)
        def _():
            acc_ref[...] = c_ref[...] if has_acc else jnp.zeros_like(acc_ref)

        acc_ref[...] += _dot(a_ref[...], b_ref[...], dims)

        @pl.when(kk == nk - 1)
        def _():
            o_ref[...] = acc_ref[...].astype(out_dtype)

    res = _hosted_call(
        body, side, name=name, grid=(m // tm, n // tn, nk),
        in_specs=[a_spec, b_spec] + ([o_spec] if has_acc else []),
        out_specs=[o_spec], out_shape=[jax.ShapeDtypeStruct((m, n), out_dtype)],
        scratch_shapes=[pltpu.VMEM((tm, tn), F32)], args=(a, b, acc) if has_acc else (a, b))
    return res[0] if side is None else res


def _which(i):
    return jnp.minimum(i, 1)


def _pre_fwd(x, g_pre, mods):
    t, d = x.shape

    def body(x_ref, g_ref, m_ref, o_ref, ot_ref):
        xb = x_ref[...]
        xn = xb * _rms(xb) * g_ref[...]
        hx = xn * (1.0 + m_ref[0, 1:2, :]) + m_ref[0, 0:1, :]
        o_ref[...] = hx.astype(BF16)
        ot_ref[...] = hx.T.astype(BF16)

    return pl.pallas_call(
        body, name="pre_fwd", grid=(t // TB,),
        in_specs=[pl.BlockSpec((TB, d), lambda i: (i, 0)), pl.BlockSpec((1, d), lambda i: (0, 0)),
                  pl.BlockSpec((1, 8, d), lambda i: (_which(i), 0, 0))],
        out_specs=[pl.BlockSpec((TB, d), lambda i: (i, 0)), pl.BlockSpec((d, TB), lambda i: (0, i))],
        out_shape=[jax.ShapeDtypeStruct((t, d), BF16), jax.ShapeDtypeStruct((d, t), BF16)], compiler_params=_params("arbitrary"),
    )(x, g_pre, mods)


def _pre_bwd(x, d_hx, d_up, g_pre, mods, side=None, latent_only=False):
    t, d = x.shape
    dx_rows = t - TB if latent_only else t
    dx_spec = pl.BlockSpec((TB, d), (lambda i: (jnp.maximum(i - 1, 0), 0)) if latent_only else (lambda i: (i, 0)))

    def body(x_ref, dh_ref, du_ref, g_ref, m_ref, dx_ref, acc_ref):
        i = pl.program_id(0)

        @pl.when(i <= 1)
        def _():
            acc_ref[...] = jnp.zeros_like(acc_ref)

        xb = x_ref[...]
        dh = dh_ref[...]
        r = _rms(xb)
        xr = xb * r
        d_xn = dh * (1.0 + m_ref[0, 1:2, :])
        dx_ref[...] = du_ref[...] + _rms_bwd(xb, r, d_xn * g_ref[...])
        acc_ref[0, 0:1, :] += jnp.sum(dh, axis=0, keepdims=True)
        acc_ref[0, 1:2, :] += jnp.sum(dh * (xr * g_ref[...]), axis=0, keepdims=True)
        acc_ref[0, 2:3, :] += jnp.sum(d_xn * xr, axis=0, keepdims=True)

    blk = pl.BlockSpec((TB, d), lambda i: (i, 0))
    return _hosted_call(
        body, side, name="pre_bwd", grid=(t // TB,),
        in_specs=[blk, blk, blk, pl.BlockSpec((1, d), lambda i: (0, 0)),
                  pl.BlockSpec((1, 8, d), lambda i: (_which(i), 0, 0))],
        out_specs=[dx_spec, pl.BlockSpec((1, 8, d), lambda i: (_which(i), 0, 0))],
        out_shape=[jax.ShapeDtypeStruct((dx_rows, d), F32), jax.ShapeDtypeStruct((2, 8, d), F32)],
        scratch_shapes=[], args=(x, d_hx, d_up, g_pre, mods))


def _post_fwd(o, x, g_post, mods):
    t, d = x.shape

    def body(o_ref, x_ref, g_ref, m_ref, y_ref):
        ob = o_ref[...]
        y_ref[...] = x_ref[...] + m_ref[0, 2:3, :] * (ob * _rms(ob) * g_ref[...])

    blk = pl.BlockSpec((TB, d), lambda i: (i, 0))
    return pl.pallas_call(
        body, name="post_fwd", grid=(t // TB,),
        in_specs=[blk, blk, pl.BlockSpec((1, d), lambda i: (0, 0)), pl.BlockSpec((1, 8, d), lambda i: (_which(i), 0, 0))],
        out_specs=blk, out_shape=jax.ShapeDtypeStruct((t, d), F32), compiler_params=_params("arbitrary"),
    )(o, x, g_post, mods)


def _post_bwd(d_y, o, g_post, mods):
    t, d = o.shape

    def body(dy_ref, o_ref, g_ref, m_ref, do_ref, acc_ref):
        i = pl.program_id(0)

        @pl.when(i <= 1)
        def _():
            acc_ref[...] = jnp.zeros_like(acc_ref)

        ob = o_ref[...]
        dy = dy_ref[...]
        r = _rms(ob)
        orr = ob * r
        d_out = dy * m_ref[0, 2:3, :]
        do_ref[...] = _rms_bwd(ob, r, d_out * g_ref[...]).astype(BF16)
        acc_ref[0, 0:1, :] += jnp.sum(dy * (orr * g_ref[...]), axis=0, keepdims=True)
        acc_ref[0, 1:2, :] += jnp.sum(d_out * orr, axis=0, keepdims=True)

    blk = pl.BlockSpec((TB, d), lambda i: (i, 0))
    return pl.pallas_call(
        body, name="post_bwd", grid=(t // TB,),
        in_specs=[blk, blk, pl.BlockSpec((1, d), lambda i: (0, 0)), pl.BlockSpec((1, 8, d), lambda i: (_which(i), 0, 0))],
        out_specs=[blk, pl.BlockSpec((1, 8, d), lambda i: (_which(i), 0, 0))],
        out_shape=[jax.ShapeDtypeStruct((t, d), BF16), jax.ShapeDtypeStruct((2, 8, d), F32)],
        compiler_params=_params("arbitrary"),
    )(d_y, o, g_post, mods)


def _loss_grad(xf, target):
    t, d = xf.shape

    def body(x_ref, t_ref, loss_ref, dx_ref):
        i = pl.program_id(0)

        @pl.when(i == 0)
        def _():
            loss_ref[...] = jnp.zeros_like(loss_ref)
            dx_ref[...] = jnp.zeros_like(dx_ref)

        @pl.when(i > 0)
        def _():
            err = x_ref[...] - t_ref[...]
            loss_ref[...] += jnp.sum(err * err).reshape(1, 1)
            dx_ref[...] = err * (1.0 / d)

    return pl.pallas_call(
        body, name="loss_grad", grid=(t // TB,),
        in_specs=[pl.BlockSpec((TB, d), lambda i: (i, 0)), pl.BlockSpec((TB, d), lambda i: (jnp.maximum(i - 1, 0), 0))],
        out_specs=[pl.BlockSpec((1, 1), lambda i: (0, 0)), pl.BlockSpec((TB, d), lambda i: (i, 0))],
        out_shape=[jax.ShapeDtypeStruct((1, 1), F32), jax.ShapeDtypeStruct((t, d), F32)],
        compiler_params=_params("arbitrary"),
    )(xf, target)


def _conv_terms(zb, pos, row_len):
    out = []
    for k in range(CONV_TAPS):
        o = k - CONV_TAPS // 2
        sh = zb if o == 0 else pltpu.roll(zb, (-o) % TB, 0)
        out.append(jnp.where((pos + o >= 0) & (pos + o < row_len), sh, 0.0))
    return out


def _row_pos(i, ctx_len):
    row_len = jnp.where(i == 0, ctx_len, ROW)
    pos = lax.broadcasted_iota(jnp.int32, (TB, 1), 0) & (row_len - 1)
    return pos, row_len


def _conv_fwd(z_xbc, conv_w8, conv_b, ctx_len):
    t, c = z_xbc.shape
    tc = _tile(c, (1024, 512, 256, 128))

    def body(z_ref, w_ref, b_ref, o_ref, ds_ref):
        pos, row_len = _row_pos(pl.program_id(1), ctx_len)
        terms = _conv_terms(z_ref[...], pos, row_len)
        pre = b_ref[...]
        for k in range(CONV_TAPS):
            pre = pre + terms[k] * w_ref[k:k + 1, :]
        sig = _sigmoid(pre)
        o_ref[...] = pre * sig
        ds_ref[...] = sig * (1.0 + pre * (1.0 - sig))

    blk = pl.BlockSpec((TB, tc), lambda j, i: (i, j))
    return pl.pallas_call(
        body, name="conv_fwd", grid=(c // tc, t // TB),
        in_specs=[blk, pl.BlockSpec((8, tc), lambda j, i: (0, j)), pl.BlockSpec((1, tc), lambda j, i: (0, j))],
        out_specs=[blk, blk], out_shape=[jax.ShapeDtypeStruct((t, c), F32)] * 2, compiler_params=_params("arbitrary", "arbitrary"),
    )(z_xbc, conv_w8, conv_b)


def _conv_bwd(z_xbc, dsilu, d_xbc2, d_y, d_skip_row, conv_w8, ctx_len):
    t, c = z_xbc.shape
    d = d_y.shape[1]
    tc = _tile(d, (1024, 512, 256, 128))
    nskip = d // tc

    def body(z_ref, dsl_ref, g2_ref, dy_ref, ds_ref, w_ref, dz_ref, dw_ref, db_ref):
        j, i = pl.program_id(0), pl.program_id(1)

        @pl.when(i == 0)
        def _():
            dw_ref[...] = jnp.zeros_like(dw_ref)
            db_ref[...] = jnp.zeros_like(db_ref)

        pos, row_len = _row_pos(i, ctx_len)
        skip = jnp.where(j < nskip, 1.0, 0.0) * ds_ref[...]
        d_pre = (g2_ref[0] + g2_ref[1] + dy_ref[...] * skip) * dsl_ref[...]
        db_ref[...] += jnp.sum(d_pre, axis=0, keepdims=True)
        zb = z_ref[...]
        dz = jnp.zeros_like(d_pre)
        for k in range(CONV_TAPS):
            o = k - CONV_TAPS // 2
            sh = d_pre if o == 0 else pltpu.roll(d_pre, o % TB, 0)
            sh = jnp.where((pos - o >= 0) & (pos - o < row_len), sh, 0.0)
            dw_ref[k:k + 1, :] += jnp.sum(sh * zb, axis=0, keepdims=True)
            dz = dz + sh * w_ref[k:k + 1, :]
        dz_ref[...] = dz.astype(BF16)

    jd = lambda j: jnp.minimum(j, nskip - 1)
    blk = pl.BlockSpec((TB, tc), lambda j, i: (i, j))
    return pl.pallas_call(
        body, name="conv_bwd", grid=(c // tc, t // TB),
        in_specs=[blk, blk, pl.BlockSpec((2, TB, tc), lambda j, i: (0, i, j)),
                  pl.BlockSpec((TB, tc), lambda j, i: (i, jd(j))), pl.BlockSpec((1, tc), lambda j, i: (0, jd(j))),
                  pl.BlockSpec((8, tc), lambda j, i: (0, j))],
        out_specs=[blk, pl.BlockSpec((8, tc), lambda j, i: (0, j)), pl.BlockSpec((1, tc), lambda j, i: (0, j))],
        out_shape=[jax.ShapeDtypeStruct((t, c), BF16), jax.ShapeDtypeStruct((8, c), F32), jax.ShapeDtypeStruct((1, c), F32)],
        compiler_params=_params("arbitrary", "arbitrary"),
    )(z_xbc, dsilu, d_xbc2, d_y, d_skip_row, conv_w8)


def _scan_chunk(dirn, s, nch, ncc):
    bwd = jnp.where(s < ncc, ncc - 1 - s, nch - 1 - (s - ncc))
    return jnp.where(dirn == 0, s, bwd)


def _ssd_decays(dt_ref, dtb_ref, alog_ref, tri):
    raw = dt_ref[...] + dtb_ref[0]
    dt = _softplus(raw)
    a_neg = -jnp.exp(alog_ref[0])
    a = dt * a_neg
    s = _dot(tri, a, NN, HI)
    stot = jnp.sum(a, axis=0, keepdims=True)
    return raw, dt, a_neg, s, stot, s.T


def _split(v):
    hi = v.astype(BF16)
    return hi, (v - hi.astype(F32)).astype(BF16)


def _expand(v, indt_ref):
    hi, lo = _split(v)
    return _dot(hi, indt_ref[...]) + _dot(lo, indt_ref[...])


def _head_sums(v, ind_ref):
    hi, lo = _split(v)
    return _dot(hi, ind_ref[...]) + _dot(lo, ind_ref[...])


def _ssd_fwd(xbc, z_dt, dtb, alog, tri, ind_t, d, ctx_len, side=None):
    t = xbc.shape[0]
    nch, ncc = t // Q, ctx_len // Q
    heads = d // HP
    groups = heads // 4
    gn = groups * LANES

    def body(xbc_ref, dt_ref, dtb_ref, alog_ref, tri_ref, indt_ref, y_ref, hs_ref, h_scr, xdb_scr, xde_scr, esx_scr):
        @pl.when(pl.program_id(1) == 0)
        def _():
            h_scr[...] = jnp.zeros_like(h_scr)

        tri = tri_ref[0]
        mask = tri > 0.0
        _, dt, _, s, stot, s_t = _ssd_decays(dt_ref, dtb_ref, alog_ref, tri)
        esx_scr[...] = _expand(jnp.exp(s), indt_ref)
        etot_x = _expand(jnp.broadcast_to(jnp.exp(stot), (8, LANES)), indt_ref)[0:1]
        xd = xbc_ref[:, :d] * _expand(dt, indt_ref)
        xdb_scr[...] = xd.astype(BF16)
        xde_scr[...] = (xd * _expand(jnp.exp(stot - s), indt_ref)).astype(BF16)
        left = lax.broadcasted_iota(jnp.int32, (Q, LANES), 1) < HP
        hs_ref[0, 0] = h_scr[...]
        for g in range(groups):
            b32 = xbc_ref[:, d + g * LANES:d + (g + 1) * LANES]
            bb = b32.astype(BF16)
            bbt = b32.T.astype(BF16)
            cb = xbc_ref[:, d + gn + g * LANES:d + gn + (g + 1) * LANES].astype(BF16)
            cbt = _dot(cb, bb, NT)
            gcols = slice(4 * g * HP, 4 * (g + 1) * HP)
            hg = h_scr[:, gcols]
            y_off = _dot(cb, hg.astype(BF16)) * esx_scr[:, gcols]
            h_scr[:, gcols] = hg * etot_x[:, gcols] + _dot(bbt, xde_scr[:, gcols])
            for j, pr in enumerate((2 * g, 2 * g + 1)):
                h0 = 2 * pr
                cols = slice(pr * LANES, (pr + 1) * LANES)
                xdb = xdb_scr[:, cols]
                res = []
                for h in (h0, h0 + 1):
                    lm = jnp.exp(jnp.where(mask, s[:, h:h + 1] - s_t[h:h + 1, :], -jnp.inf))
                    res.append(_dot((cbt * lm).astype(BF16), xdb))
                y_ref[0, :, cols] = jnp.where(left, res[0], res[1]) + y_off[:, j * LANES:(j + 1) * LANES]

    cidx = lambda dd, ss: _scan_chunk(dd, ss, nch, ncc)
    return _hosted_call(
        body, side, name="ssd_fwd", grid=(2, nch),
        in_specs=[pl.BlockSpec((Q, 2 * d), lambda dd, ss: (cidx(dd, ss), 0)),
                  pl.BlockSpec((Q, LANES), lambda dd, ss: (cidx(dd, ss), dd)),
                  pl.BlockSpec((1, 1, LANES), lambda dd, ss: (dd, 0, 0)),
                  pl.BlockSpec((1, 1, LANES), lambda dd, ss: (dd, 0, 0)),
                  pl.BlockSpec((1, Q, Q), lambda dd, ss: (dd, 0, 0)),
                  pl.BlockSpec((LANES, d), lambda dd, ss: (0, 0))],
        out_specs=[pl.BlockSpec((1, Q, d), lambda dd, ss: (dd, cidx(dd, ss), 0)),
                   pl.BlockSpec((1, 1, LANES, d), lambda dd, ss: (dd, cidx(dd, ss), 0, 0))],
        out_shape=[jax.ShapeDtypeStruct((2, t, d), F32), jax.ShapeDtypeStruct((2, nch, LANES, d), F32)],
        scratch_shapes=[pltpu.VMEM((LANES, d), F32), pltpu.VMEM((Q, d), BF16), pltpu.VMEM((Q, d), BF16), pltpu.VMEM((Q, d), F32)],
        args=(xbc, z_dt, dtb, alog, tri, ind_t))


def _ssd_bwd(xbc, z_dt, dtb, alog, tri, tri_t, ind_t, ind, d_y, y2, hs, d, ctx_len, side=None):
    t = xbc.shape[0]
    nch, ncc = t // Q, ctx_len // Q
    heads = d // HP
    groups = heads // 4
    gn = groups * LANES

    def body(xbc_ref, dt_ref, dtb_ref, alog_ref, tri_ref, trit_ref, indt_ref, ind_ref, dy_ref, y_ref, hs_ref,
             dx_ref, dzdt_ref, dbias_ref, dalog_ref, dh_scr, dtx_scr, ex_scr, xdb_scr, xde_scr, dyb_scr, dye_scr, dxd_scr, bdh_scr):
        @pl.when(pl.program_id(1) == 0)
        def _():
            dh_scr[...] = jnp.zeros_like(dh_scr)
            dbias_ref[...] = jnp.zeros_like(dbias_ref)
            dalog_ref[...] = jnp.zeros_like(dalog_ref)

        tri = tri_ref[0]
        mask = tri > 0.0
        mask_t = trit_ref[0] > 0.0
        raw, dt, a_neg, s, stot, s_t = _ssd_decays(dt_ref, dtb_ref, alog_ref, tri)
        etot = jnp.exp(stot)
        etot_x = _expand(jnp.broadcast_to(etot, (8, LANES)), indt_ref)[0:1]
        dtx_scr[...] = _expand(dt, indt_ref)
        ex_scr[...] = _expand(jnp.exp(stot - s), indt_ref)
        xd = xbc_ref[:, :d] * dtx_scr[...]
        xdb_scr[...] = xd.astype(BF16)
        xde_scr[...] = (xd * ex_scr[...]).astype(BF16)
        dyb_scr[...] = dy_ref[...].astype(BF16)
        dye_scr[...] = (dy_ref[...] * _dot(jnp.exp(s).astype(BF16), indt_ref[...])).astype(BF16)
        hd_cols = jnp.sum(dh_scr[...] * hs_ref[0, 0], axis=0, keepdims=True)
        left = lax.broadcasted_iota(jnp.int32, (Q, LANES), 1) < HP
        for g in range(groups):
            b32 = xbc_ref[:, d + g * LANES:d + (g + 1) * LANES]
            c32 = xbc_ref[:, d + gn + g * LANES:d + gn + (g + 1) * LANES]
            bb, cb = b32.astype(BF16), c32.astype(BF16)
            c_t = c32.T.astype(BF16)
            cbt = _dot(cb, bb, NT)
            cbt_t = _dot(bb, cb, NT)
            d_cbt = jnp.zeros((Q, Q), F32)
            gcols = slice(4 * g * HP, 4 * (g + 1) * HP)
            dyeb = dye_scr[:, gcols]
            dhg = dh_scr[:, gcols]
            dhb = dhg.astype(BF16)
            bdh_scr[:, gcols] = _dot(bb, dhb)
            d_c = _dot(dyeb, hs_ref[0, 0, :, gcols].astype(BF16), NT)
            d_b = _dot(xde_scr[:, gcols], dhb, NT)
            dh_scr[:, gcols] = dhg * etot_x[:, gcols] + _dot(c_t, dyeb)
            for pr in (2 * g, 2 * g + 1):
                h0 = 2 * pr
                cols = slice(pr * LANES, (pr + 1) * LANES)
                xdb = xdb_scr[:, cols]
                dyb = dyb_scr[:, cols]
                parts = []
                for hh, h in enumerate((h0, h0 + 1)):
                    mine = left if hh == 0 else jnp.logical_not(left)
                    diff = s[:, h:h + 1] - s_t[h:h + 1, :]
                    lm = jnp.exp(jnp.where(mask, diff, -jnp.inf))
                    lm_t = jnp.exp(jnp.where(mask_t, -diff, -jnp.inf))
                    gm = _dot(jnp.where(mine, dyb, jnp.zeros_like(dyb)), xdb, NT)
                    d_cbt = d_cbt + gm * lm
                    parts.append(_dot((cbt_t * lm_t).astype(BF16), dyb))
                dxd_scr[:, cols] = jnp.where(left, parts[0], parts[1])
            dx_ref[0, :, d + g * LANES:d + (g + 1) * LANES] = d_b + _dot(d_cbt.T.astype(BF16), cb)
            dx_ref[0, :, d + gn + g * LANES:d + gn + (g + 1) * LANES] = d_c + _dot(d_cbt.astype(BF16), bb)
        x = xbc_ref[:, :d]
        ebdh = ex_scr[...] * bdh_scr[...]
        d_xd = dxd_scr[...] + ebdh
        dx_ref[0, :, :d] = d_xd * dtx_scr[...]
        xe = x * dtx_scr[...] * ebdh
        d_s = _head_sums(dyb_scr[...].astype(F32) * y_ref[0] - xdb_scr[...].astype(F32) * dxd_scr[...] - xe, ind_ref)
        r_dx = _head_sums(d_xd * x, ind_ref)
        row8 = lax.broadcasted_iota(jnp.int32, (8, d), 0)
        tot = _head_sums(jnp.where(row8 == 0, jnp.sum(xe, axis=0, keepdims=True), jnp.where(row8 == 1, hd_cols, 0.0)), ind_ref)
        d_stot = tot[0:1] + etot * tot[1:2]
        d_a = _dot(trit_ref[0], d_s, NN, HI) + d_stot
        valid = lax.broadcasted_iota(jnp.int32, (Q, LANES), 1) < heads
        d_dt_tot = jnp.where(valid, d_a * a_neg + r_dx, 0.0)
        d_raw = d_dt_tot * _sigmoid(raw)
        dzdt_ref[...] = d_raw.astype(BF16)
        dbias_ref[0] += jnp.sum(d_raw, axis=0, keepdims=True)
        dalog_ref[0] += jnp.sum(jnp.where(valid, d_a * dt, 0.0), axis=0, keepdims=True) * a_neg

    cidx = lambda dd, ss: _scan_chunk(dd, nch - 1 - ss, nch, ncc)
    full = lambda shape: pltpu.VMEM(shape, F32)
    half = lambda shape: pltpu.VMEM(shape, BF16)
    return _hosted_call(
        body, side, name="ssd_bwd", grid=(2, nch),
        in_specs=[pl.BlockSpec((Q, 2 * d), lambda dd, ss: (cidx(dd, ss), 0)),
                  pl.BlockSpec((Q, LANES), lambda dd, ss: (cidx(dd, ss), dd)),
                  pl.BlockSpec((1, 1, LANES), lambda dd, ss: (dd, 0, 0)),
                  pl.BlockSpec((1, 1, LANES), lambda dd, ss: (dd, 0, 0)),
                  pl.BlockSpec((1, Q, Q), lambda dd, ss: (dd, 0, 0)),
                  pl.BlockSpec((1, Q, Q), lambda dd, ss: (dd, 0, 0)),
                  pl.BlockSpec((LANES, d), lambda dd, ss: (0, 0)),
                  pl.BlockSpec((d, LANES), lambda dd, ss: (0, 0)),
                  pl.BlockSpec((Q, d), lambda dd, ss: (cidx(dd, ss), 0)),
                  pl.BlockSpec((1, Q, d), lambda dd, ss: (dd, cidx(dd, ss), 0)),
                  pl.BlockSpec((1, 1, LANES, d), lambda dd, ss: (dd, cidx(dd, ss), 0, 0))],
        out_specs=[pl.BlockSpec((1, Q, 2 * d), lambda dd, ss: (dd, cidx(dd, ss), 0)),
                   pl.BlockSpec((Q, LANES), lambda dd, ss: (cidx(dd, ss), dd)),
                   pl.BlockSpec((1, 1, LANES), lambda dd, ss: (dd, 0, 0)),
                   pl.BlockSpec((1, 1, LANES), lambda dd, ss: (dd, 0, 0))],
        out_shape=[jax.ShapeDtypeStruct((2, t, 2 * d), F32), jax.ShapeDtypeStruct((t, 2 * LANES), BF16),
                   jax.ShapeDtypeStruct((2, 1, LANES), F32), jax.ShapeDtypeStruct((2, 1, LANES), F32)],
        scratch_shapes=[full((LANES, d)), full((Q, d)), full((Q, d)), half((Q, d)), half((Q, d)), half((Q, d)), half((Q, d)),
                        full((Q, d)), full((Q, d))],
        args=(xbc, z_dt, dtb, alog, tri, tri_t, ind_t, ind, d_y, y2, hs))


def _mix_common(zm_ref, y2_ref, xh_ref, dsk_ref, gv_ref, ws_ref, bst_ref, d):
    groups = d // LANES
    z_ssd, u, v, z_mlp = (zm_ref[:, k * d:(k + 1) * d].astype(F32) for k in range(4))
    y = y2_ref[0] + y2_ref[1] + dsk_ref[...] * xh_ref[...]
    sig_a = _sigmoid(z_ssd)
    ya_pre = y * (z_ssd * sig_a)
    r_v = _rms(v)
    vn = (v * r_v * gv_ref[...]).astype(BF16)
    sg = jnp.concatenate(
        [_dot(ws_ref[g].astype(BF16), vn[:, g * LANES:(g + 1) * LANES]) + bst_ref[:, g:g + 1] for g in range(groups)], axis=1)
    sig_m = _sigmoid(z_mlp)
    yb_pre = u * sg * (z_mlp * sig_m)
    return z_ssd, u, v, z_mlp, y, sig_a, ya_pre, r_v, vn, sg, sig_m, yb_pre


def _mix_fwd(z_mid, y2, xbc, dsk_row, g_ssd, g_v, g_mlp, w_s, b_st):
    t = z_mid.shape[0]
    d = z_mid.shape[1] // 4
    groups = d // LANES

    def body(zm_ref, y2_ref, xh_ref, dsk_ref, ga_ref, gv_ref, gm_ref, ws_ref, bst_ref, o_ref, ot_ref):
        (_, _, _, _, _, _, ya_pre, _, _, _, _, yb_pre) = _mix_common(zm_ref, y2_ref, xh_ref, dsk_ref, gv_ref, ws_ref, bst_ref, d)
        y_a = ya_pre * _rms(ya_pre) * ga_ref[...]
        y_b = yb_pre * _rms(yb_pre) * gm_ref[...]
        o_ref[:, :d] = y_a.astype(BF16)
        o_ref[:, d:] = y_b.astype(BF16)
        ot_ref[:d, :] = y_a.T.astype(BF16)
        ot_ref[d:, :] = y_b.T.astype(BF16)

    row = pl.BlockSpec((1, d), lambda i: (0, 0))
    return pl.pallas_call(
        body, name="mix_fwd", grid=(t // Q,),
        in_specs=[pl.BlockSpec((Q, 4 * d), lambda i: (i, 0)), pl.BlockSpec((2, Q, d), lambda i: (0, i, 0)),
                  pl.BlockSpec((Q, d), lambda i: (i, 0)), row, row, row, row,
                  pl.BlockSpec((groups, Q, Q), lambda i: (0, 0, 0)), pl.BlockSpec((Q, LANES), lambda i: (0, 0))],
        out_specs=[pl.BlockSpec((Q, 2 * d), lambda i: (i, 0)), pl.BlockSpec((2 * d, Q), lambda i: (0, i))],
        out_shape=[jax.ShapeDtypeStruct((t, 2 * d), BF16), jax.ShapeDtypeStruct((2 * d, t), BF16)], compiler_params=_params("arbitrary"),
    )(z_mid, y2, xbc, dsk_row, g_ssd, g_v, g_mlp, w_s, b_st)


def _mix_bwd(z_mid, y2, xbc, d_ycat, dsk_row, g_ssd, g_v, g_mlp, w_s, w_st, b_st, ind_head, ind_group):
    t = z_mid.shape[0]
    d = z_mid.shape[1] // 4
    groups = d // LANES
    nsteps = t // Q

    def body(zm_ref, y2_ref, xh_ref, dyc_ref, dsk_ref, ga_ref, gv_ref, gm_ref, ws_ref, wst_ref, bst_ref, ih_ref, ig_ref,
             dzm_ref, dy_ref, vec_ref, dws_ref, dbs_ref, dsk_acc, dsg_acc):
        i = pl.program_id(0)

        @pl.when(i == 0)
        def _():
            vec_ref[...] = jnp.zeros_like(vec_ref)
            dws_ref[...] = jnp.zeros_like(dws_ref)
            dsk_acc[...] = jnp.zeros_like(dsk_acc)
            dsg_acc[...] = jnp.zeros_like(dsg_acc)

        (z_ssd, u, v, z_mlp, y, sig_a, ya_pre, r_v, vn, sg, sig_m, yb_pre) = _mix_common(
            zm_ref, y2_ref, xh_ref, dsk_ref, gv_ref, ws_ref, bst_ref, d)
        d_ya = dyc_ref[:, :d]
        r_a = _rms(ya_pre)
        vec_ref[0:1, :] += jnp.sum(d_ya * (ya_pre * r_a), axis=0, keepdims=True)
        d_ya_pre = _rms_bwd(ya_pre, r_a, d_ya * ga_ref[...])
        d_y = d_ya_pre * (z_ssd * sig_a)
        dy_ref[...] = d_y
        dsk_acc[...] += jnp.sum(d_y * xh_ref[...], axis=0, keepdims=True)
        dzm_ref[:, 0:d] = (d_ya_pre * y * (sig_a * (1.0 + z_ssd * (1.0 - sig_a)))).astype(BF16)
        d_yb = dyc_ref[:, d:]
        r_b = _rms(yb_pre)
        vec_ref[2:3, :] += jnp.sum(d_yb * (yb_pre * r_b), axis=0, keepdims=True)
        d_yb_pre = _rms_bwd(yb_pre, r_b, d_yb * gm_ref[...])
        silu_m = z_mlp * sig_m
        dzm_ref[:, d:2 * d] = (d_yb_pre * sg * silu_m).astype(BF16)
        dzm_ref[:, 3 * d:4 * d] = (d_yb_pre * u * sg * (sig_m * (1.0 + z_mlp * (1.0 - sig_m)))).astype(BF16)
        d_sg = d_yb_pre * u * silu_m
        dsg_acc[...] += d_sg
        d_sgb = d_sg.astype(BF16)
        d_vn = []
        for g in range(groups):
            cols = slice(g * LANES, (g + 1) * LANES)
            dws_ref[g] += _dot(d_sgb[:, cols], vn[:, cols], NT)
            d_vn.append(_dot(wst_ref[g].astype(BF16), d_sgb[:, cols]))
        d_vn = jnp.concatenate(d_vn, axis=1)
        vec_ref[1:2, :] += jnp.sum(d_vn * (v * r_v), axis=0, keepdims=True)
        dzm_ref[:, 2 * d:3 * d] = _rms_bwd(v, r_v, d_vn * gv_ref[...]).astype(BF16)

        @pl.when(i == nsteps - 1)
        def _():
            vec_ref[3:4, 0:LANES] = _dot(dsk_acc[...], ih_ref[...], NN, HI)
            dbs_ref[...] = _dot(dsg_acc[...], ig_ref[...], NN, HI)

    row = pl.BlockSpec((1, d), lambda i: (0, 0))
    wsp = pl.BlockSpec((groups, Q, Q), lambda i: (0, 0, 0))
    ind = pl.BlockSpec((d, LANES), lambda i: (0, 0))
    return pl.pallas_call(
        body, name="mix_bwd", grid=(nsteps,),
        in_specs=[pl.BlockSpec((Q, 4 * d), lambda i: (i, 0)), pl.BlockSpec((2, Q, d), lambda i: (0, i, 0)),
                  pl.BlockSpec((Q, d), lambda i: (i, 0)), pl.BlockSpec((Q, 2 * d), lambda i: (i, 0)),
                  row, row, row, row, wsp, wsp, pl.BlockSpec((Q, LANES), lambda i: (0, 0)), ind, ind],
        out_specs=[pl.BlockSpec((Q, 4 * d), lambda i: (i, 0)), pl.BlockSpec((Q, d), lambda i: (i, 0)),
                   pl.BlockSpec((8, d), lambda i: (0, 0)), wsp, pl.BlockSpec((Q, LANES), lambda i: (0, 0))],
        out_shape=[jax.ShapeDtypeStruct((t, 4 * d), BF16), jax.ShapeDtypeStruct((t, d), F32),
                   jax.ShapeDtypeStruct((8, d), F32), jax.ShapeDtypeStruct((groups, Q, Q), F32),
                   jax.ShapeDtypeStruct((Q, LANES), F32)],
        scratch_shapes=[pltpu.VMEM((1, d), F32), pltpu.VMEM((Q, d), F32)],
        compiler_params=_params("arbitrary"),
    )(z_mid, y2, xbc, d_ycat, dsk_row, g_ssd, g_v, g_mlp, w_s, w_st, b_st, ind_head, ind_group)


def _ada_fwd(c16, w_ada, b_loc):
    depth, d, n = w_ada.shape
    tn = _tile(n, (512, 256, 128))

    def body(c_ref, w_ref, b_ref, o_ref):
        cv = c_ref[...]
        o_ref[0] = _dot(cv * _sigmoid(cv), w_ref[0], NN, HI) + b_ref[0]

    return pl.pallas_call(
        body, name="ada_fwd", grid=(depth, n // tn),
        in_specs=[pl.BlockSpec((16, d), lambda l, j: (0, 0)), pl.BlockSpec((1, d, tn), lambda l, j: (l, 0, j)),
                  pl.BlockSpec((1, 1, tn), lambda l, j: (l, 0, j))],
        out_specs=pl.BlockSpec((1, 16, tn), lambda l, j: (l, 0, j)),
        out_shape=jax.ShapeDtypeStruct((depth, 16, n), F32), compiler_params=_params("arbitrary", "arbitrary"),
    )(c16, w_ada, b_loc)


def _ada_bwd(c_t, dm_loc, w_ada):
    depth, d, n = w_ada.shape
    tn = _tile(n, (512, 256, 128))

    def body(s_ref, dm_ref, w_ref, gw_ref, dsc_ref):
        @pl.when((pl.program_id(0) == 0) & (pl.program_id(1) == 0))
        def _():
            dsc_ref[...] = jnp.zeros_like(dsc_ref)

        cv = s_ref[...]
        gw_ref[0] = _dot(cv * _sigmoid(cv), dm_ref[0], NN, HI)
        dsc_ref[...] += _dot(dm_ref[0, 8:16, :], w_ref[0], NT, HI)

    return pl.pallas_call(
        body, name="ada_bwd", grid=(depth, n // tn),
        in_specs=[pl.BlockSpec((d, LANES), lambda l, j: (0, 0)), pl.BlockSpec((1, LANES, tn), lambda l, j: (l, 0, j)),
                  pl.BlockSpec((1, d, tn), lambda l, j: (l, 0, j))],
        out_specs=[pl.BlockSpec((1, d, tn), lambda l, j: (l, 0, j)), pl.BlockSpec((8, d), lambda l, j: (0, 0))],
        out_shape=[jax.ShapeDtypeStruct((depth, d, n), F32), jax.ShapeDtypeStruct((8, d), F32)],
        compiler_params=_params("arbitrary", "arbitrary"),
    )(c_t, dm_loc, w_ada)


def _rowsum(x):
    depth, r, n = x.shape

    def body(x_ref, o_ref):
        o_ref[0] = jnp.sum(x_ref[0], axis=0, keepdims=True)

    return pl.pallas_call(
        body, name="rowsum", grid=(depth,),
        in_specs=[pl.BlockSpec((1, r, n), lambda l: (l, 0, 0))], out_specs=pl.BlockSpec((1, 1, n), lambda l: (l, 0, 0)),
        out_shape=jax.ShapeDtypeStruct((depth, 1, n), F32), compiler_params=_params("arbitrary"),
    )(x)


def _cctx_grad(d_scc, c_ctx_row):
    def body(g_ref, c_ref, o_ref):
        cv = c_ref[...]
        sig = _sigmoid(cv)
        o_ref[...] = g_ref[...] * (sig * (1.0 + cv * (1.0 - sig)))

    return pl.pallas_call(body, name="cctx_grad", out_shape=jax.ShapeDtypeStruct(c_ctx_row.shape, F32))(d_scc, c_ctx_row)


def _sum_lead(x, name):
    k, r, c = x.shape
    tr = _tile(r, [tt for tt in (1024, 512, 256, 128, 64, 32, 16, 8) if k * tt * c * x.dtype.itemsize <= SUM_BLOCK_BYTES])

    def body(x_ref, o_ref):
        acc = x_ref[0].astype(F32)
        for e in range(1, k):
            acc = acc + x_ref[e].astype(F32)
        o_ref[...] = acc

    return pl.pallas_call(
        body, name=name, grid=(r // tr,),
        in_specs=[pl.BlockSpec((k, tr, c), lambda i: (0, i, 0))], out_specs=pl.BlockSpec((tr, c), lambda i: (i, 0)),
        out_shape=jax.ShapeDtypeStruct((r, c), F32), compiler_params=_params("arbitrary"),
    )(x)


def _adamw(w, g, m, v, name, g2=None, side=None):
    r, c = w.shape
    tr = _tile(r, [tt for tt in (2048, 1024, 512, 256, 128, 64, 32, 16, 8) if tt * c * 4 <= ADAM_BLOCK_BYTES])
    two = g2 is not None
    bc1 = 1.0 - ADAM_B1 ** ADAM_STEP
    bc2 = 1.0 - ADAM_B2 ** ADAM_STEP

    def body(*refs):
        if two:
            w_ref, g_ref, g2_ref, m_ref, v_ref, go_ref, d_ref, mo_ref, vo_ref = refs
            gr = g_ref[...] + g2_ref[...]
        else:
            w_ref, g_ref, m_ref, v_ref, go_ref, d_ref, mo_ref, vo_ref = refs
            gr = g_ref[...]
        mn = ADAM_B1 * m_ref[...] + (1.0 - ADAM_B1) * gr
        vn = ADAM_B2 * v_ref[...] + (1.0 - ADAM_B2) * (gr * gr)
        go_ref[...] = gr
        mo_ref[...] = mn
        vo_ref[...] = vn
        d_ref[...] = -ADAM_LR * ((mn / bc1) / (jnp.sqrt(vn / bc2) + ADAM_EPS) + ADAM_WD * w_ref[...])

    blk = pl.BlockSpec((tr, c), lambda i: (i, 0))
    ins = (w, g, g2, m, v) if two else (w, g, m, v)
    return _hosted_call(body, side, name=name, grid=(r // tr,), in_specs=[blk] * len(ins), out_specs=[blk] * 4,
                        out_shape=[jax.ShapeDtypeStruct((r, c), F32)] * 4, scratch_shapes=[], args=ins)


def _flip(pos, k):
    x, y, c = pos
    return (x ^ ((k >> 2) & 1), y ^ ((k >> 1) & 1), c ^ (k & 1))


def _lin(pos):
    return 4 * pos[0] + 2 * pos[1] + pos[2]


def _chip(pos):
    return 2 * pos[0] + pos[1]


def _here():
    return (lax.axis_index("x"), lax.axis_index("y"), lax.axis_index("c"))


class _Exchange:
    def __init__(self, x, kind):
        self.x, self.kind = x, kind
        self.masks = {"gather4": (2, 4, 6), "scatter4": (2, 4, 6), "gather8": tuple(range(1, 8)), "swap": (1,)}[kind]
        self.slot = {"gather4": _chip, "scatter4": _chip, "gather8": _lin, "swap": None}[kind]
        lead = {"gather4": (4,), "scatter4": (), "gather8": (8,), "swap": ()}[kind]
        self.out_shape = jax.ShapeDtypeStruct(lead + x.shape, x.dtype)
        n = len(self.masks)
        self.scratch = [pltpu.SemaphoreType.DMA((n,)), pltpu.SemaphoreType.DMA((n,))] + ([] if kind == "swap" else [pltpu.SemaphoreType.DMA])

    def _copies(self, x_ref, o_ref, send, recv, *own, arrivals):
        me = _here()
        src = (lambda pos: x_ref.at[_chip(pos)]) if self.kind == "scatter4" else (lambda pos: x_ref)
        dst = (lambda pos: o_ref.at[self.slot(pos)]) if self.slot else (lambda pos: o_ref)
        local = [pltpu.make_async_copy(src(me), dst(me), own[0])] if own else []
        outs, ins = [], []
        for j, k in enumerate(self.masks):
            peer = _flip(me, k)
            sems = dict(send_sem=send.at[j], recv_sem=recv.at[j], device_id=peer, device_id_type=MESH)
            outs.append(pltpu.make_async_remote_copy(src_ref=src(peer), dst_ref=dst(me), **sems))
            if arrivals:
                ins.append(pltpu.make_async_remote_copy(src_ref=src(me), dst_ref=dst(peer), **sems))
        return local, outs, ins

    def start(self, *refs):
        local, outs, _ = self._copies(*refs, arrivals=False)
        for cp in local + outs:
            cp.start()

    def wait(self, *refs):
        local, outs, ins = self._copies(*refs, arrivals=True)
        for cp in ins:
            cp.wait_recv()
        for cp in outs:
            cp.wait_send()
        for cp in local:
            cp.wait()


def _exchange(x, kind, name):
    ex = _Exchange(x, kind)

    def body(*refs):
        ex.start(*refs)
        ex.wait(*refs)

    return pl.pallas_call(body, name=name, in_specs=[ANY], out_specs=ANY, out_shape=ex.out_shape, scratch_shapes=ex.scratch)(x)


def _gather4_two_level(x, name):
    half = x.shape[0] // 2

    def body(x_ref, o_ref, send1, recv1, send2, recv2, own):
        me = _here()
        sibling = _flip(me, 1)
        mine, theirs = pl.ds(me[2] * half, half), pl.ds((1 - me[2]) * half, half)
        peers = [_flip(me, k) for k in (2, 4, 6)]

        def over_ici(j, src_chip, rows, src=None):
            dst = o_ref.at[_chip(src_chip), rows]
            return pltpu.make_async_remote_copy(src_ref=dst if src is None else src, dst_ref=dst, send_sem=send1.at[j],
                                                recv_sem=recv1.at[j], device_id=peers[j], device_id_type=MESH)

        def over_d2d(j, rows):
            blk = o_ref.at[_chip(peers[j]), rows]
            return pltpu.make_async_remote_copy(src_ref=blk, dst_ref=blk, send_sem=send2.at[j], recv_sem=recv2.at[j],
                                                device_id=sibling, device_id_type=MESH)

        local = pltpu.make_async_copy(x_ref, o_ref.at[_chip(me)], own)
        local.start()
        sent = [over_ici(j, me, mine, src=x_ref.at[mine]) for j in range(3)]
        for cp in sent:
            cp.start()
        passed = [over_d2d(j, mine) for j in range(3)]
        for j in range(3):
            over_ici(j, peers[j], mine).wait_recv()
            passed[j].start()
        for j in range(3):
            over_d2d(j, theirs).wait_recv()
        for cp in sent + passed:
            cp.wait_send()
        local.wait()

    sems = pltpu.SemaphoreType.DMA((3,))
    return pl.pallas_call(body, name=name, in_specs=[ANY], out_specs=ANY, out_shape=jax.ShapeDtypeStruct((4,) + x.shape, x.dtype),
                          scratch_shapes=[sems, sems, sems, sems, pltpu.SemaphoreType.DMA])(x)


def _hosted_call(body, side, *, name, grid, in_specs, out_specs, out_shape, scratch_shapes, args):
    sides = [] if side is None else list(side) if isinstance(side, (list, tuple)) else [side]
    n_in, n_out, ns = len(in_specs), len(out_specs), len(sides)
    params = _params(*(["arbitrary"] * len(grid)))
    if not sides:
        return pl.pallas_call(body, name=name, grid=grid, in_specs=in_specs, out_specs=out_specs, out_shape=out_shape,
                              scratch_shapes=scratch_shapes, compiler_params=params)(*args)
    n_sem = [len(s.scratch) for s in sides]

    def hosted(*refs):
        ins, xs = refs[:n_in], refs[n_in:n_in + ns]
        outs, os_ = refs[n_in + ns:n_in + ns + n_out], refs[n_in + ns + n_out:n_in + 2 * ns + n_out]
        rest = refs[n_in + 2 * ns + n_out:]
        scratch, sems = rest[:len(rest) - sum(n_sem)], list(rest[len(rest) - sum(n_sem):])
        per_side = [[sems.pop(0) for _ in range(k)] for k in n_sem]
        ids = [pl.program_id(a) for a in range(len(grid))]
        first, last = ids[0] == 0, ids[0] == grid[0] - 1
        for a in range(1, len(grid)):
            first, last = first & (ids[a] == 0), last & (ids[a] == grid[a] - 1)

        @pl.when(first)
        def _():
            for s, x_ref, o_ref, sm in zip(sides, xs, os_, per_side):
                s.start(x_ref, o_ref, *sm)

        body(*ins, *outs, *scratch)

        @pl.when(last)
        def _():
            for s, x_ref, o_ref, sm in zip(sides, xs, os_, per_side):
                s.wait(x_ref, o_ref, *sm)

    return pl.pallas_call(hosted, name=name + "_x_" + "_".join(s.kind for s in sides), grid=grid, in_specs=list(in_specs) + [ANY] * ns,
                          out_specs=list(out_specs) + [ANY] * ns, out_shape=list(out_shape) + [s.out_shape for s in sides],
                          scratch_shapes=list(scratch_shapes) + [sem for s in sides for sem in s.scratch],
                          compiler_params=params)(*args, *[s.x for s in sides])


def _pad_lanes(a, width):
    return jnp.pad(a, [(0, 0)] * (a.ndim - 1) + [(0, width - a.shape[-1])])


def kernel(x, c, ctx, c_ctx, w_ada, b_ada, g_pre, g_post, w_in, conv_w, conv_b, dt_bias, a_log, d_skip, g_ssd, g_v, w_s, b_s, g_mlp, w_out, loss_target, m_c_ctx, m_w_ada, m_b_ada, m_g_pre, m_g_post, m_w_in, m_conv_w, m_conv_b, m_dt_bias, m_a_log, m_d_skip, m_g_ssd, m_g_v, m_w_s, m_b_s, m_g_mlp, m_w_out, v_c_ctx, v_w_ada, v_b_ada, v_g_pre, v_g_post, v_w_in, v_conv_w, v_conv_b, v_dt_bias, v_a_log, v_d_skip, v_g_ssd, v_g_v, v_w_s, v_b_s, v_g_mlp, v_w_out):
    depth, d = g_pre.shape
    seq, ctx_len = x.shape[1], ctx.shape[1]
    heads = d // HP
    in_w = 6 * d + 2 * heads
    groups_mlp = d // LANES
    t = ctx_len + seq
    assert ctx_len == TB and seq % TB == 0 and TB % ROW == 0 and heads % 4 == 0 and heads <= LANES and d % LANES == 0
    assert w_in.shape == (depth, d, in_w // 4)

    xi, yi, ci = lax.axis_index("x"), lax.axis_index("y"), lax.axis_index("c")
    chip = 2 * xi + yi
    me = 4 * xi + 2 * yi + ci

    n_ada = 3 * d // 4
    c_all = _exchange(c, "gather8", "ag_c")[:, 0, :]
    c16 = jnp.concatenate([c_all, c_ctx[None, :], jnp.zeros((7, d), F32)], axis=0)
    b_loc = lax.dynamic_slice_in_dim(b_ada, chip * n_ada, n_ada, axis=1)[:, None, :]
    mods_loc = _exchange(_ada_fwd(c16, w_ada, b_loc).reshape(depth * 16, n_ada), "gather8", "ag_mods")
    mods_loc = mods_loc.reshape(4, 2, depth, 16, n_ada)[:, 0]
    mods_full = jnp.moveaxis(mods_loc, 0, 2).reshape(depth, 16, 3 * d)
    mods_x = lax.dynamic_index_in_dim(mods_full, me, axis=1, keepdims=False).reshape(depth, 3, d)
    mods_c = mods_full[:, 8, :].reshape(depth, 3, d)
    mods = _pad_rows8(jnp.stack([mods_c, mods_x], axis=1))

    w_in_b, w_out_b = w_in.astype(BF16), w_out.astype(BF16)

    def lay_out(w_in_rows):
        full = jnp.moveaxis(jnp.concatenate(w_in_rows, axis=1), 0, 1).reshape(d, in_w)
        w_dt_l = jnp.concatenate([_pad_lanes(full[:, 2 * d:2 * d + heads], LANES),
                                  _pad_lanes(full[:, 2 * d + heads:2 * d + 2 * heads], LANES)], axis=1)
        return full[:, :2 * d], full[:, 2 * d + 2 * heads:], w_dt_l

    w_in_rows = [_gather4_two_level(w_in_b[0], "ag_w_in")]
    r_scan = 3 * d // 4
    conv_w_full = jnp.moveaxis(_exchange(conv_w, "gather4", "ag_conv_w"), 0, 2).reshape(depth, CONV_TAPS, 2 * d)
    conv_w8 = jnp.pad(conv_w_full, ((0, 0), (0, 8 - CONV_TAPS), (0, 0)))

    tri = jnp.stack([jnp.tril(jnp.ones((Q, Q), F32)), jnp.triu(jnp.ones((Q, Q), F32))])
    tri_t = jnp.swapaxes(tri, 1, 2)
    dtb = _pad_lanes(dt_bias, LANES)[:, :, None, :]
    alog = _pad_lanes(a_log, LANES)[:, :, None, :]
    dsk_row = jnp.repeat(d_skip, HP, axis=1)[:, None, :]
    w_st = jnp.swapaxes(w_s, 2, 3)
    b_st = _pad_lanes(jnp.swapaxes(b_s, 1, 2), LANES)
    chan = jnp.arange(d)
    ind_head = (chan[:, None] // HP == jnp.arange(LANES)[None, :]).astype(F32)
    ind_b, ind_t = ind_head.astype(BF16), ind_head.T.astype(BF16)
    ind_group = (chan[:, None] // LANES == jnp.arange(LANES)[None, :]).astype(F32)

    stream = jnp.concatenate([ctx[0], x[0]], axis=0)
    saved = []
    for l in range(depth):
        w_xbc, w_mid, w_dt = lay_out(w_in_rows)
        more = l + 1 < depth
        hx, hx_t = _pre_fwd(stream, g_pre[l][None], mods[l])
        z_xbc = _mm(hx, w_xbc, NN, "in_xbc")
        z_mid, w_out_all = _mm(hx, w_mid, NN, "in_mid", out_dtype=BF16, side=_Exchange(w_out_b[l], "gather4"))
        w_o = w_out_all.reshape(2 * d, d)
        z_dt = _mm(hx, w_dt, NN, "in_dt")
        xbc, dsilu = _conv_fwd(z_xbc, conv_w8[l], conv_b[l][None], ctx_len)
        y2, hs, *rows_a = _ssd_fwd(xbc, z_dt, dtb[l], alog[l], tri, ind_t, d, ctx_len,
                                   side=_Exchange(w_in_b[l + 1, :r_scan], "gather4") if more else None)
        ycat, ycat_t = _mix_fwd(z_mid, y2, xbc, dsk_row[l], g_ssd[l][None], g_v[l][None], g_mlp[l][None], w_s[l], b_st[l])
        if more:
            o, rows_b = _mm(ycat, w_o, NN, "out_proj", side=_Exchange(w_in_b[l + 1, r_scan:], "gather4"))
            w_in_rows = [rows_a[0], rows_b]
        else:
            o = _mm(ycat, w_o, NN, "out_proj")
        saved.append((stream, hx_t, z_xbc, dsilu, z_mid, z_dt, xbc, y2, hs, ycat_t, o, w_xbc, w_mid, w_dt, w_o))
        stream = _post_fwd(o, stream, g_post[l][None], mods[l])

    sq, d_stream = _loss_grad(stream, loss_target[0])
    loss = lax.psum(0.5 / d * sq[0, 0], ("x", "y", "c"))

    small = []
    dmods = []
    q_in = in_w // 4
    names = ["g_post", "conv_w", "conv_b", "dt_bias", "a_log", "d_skip", "g_ssd", "g_v", "w_s", "b_s", "g_mlp"]

    def rows_of(a):
        return -(-a.size // (8 * LANES)) * 8

    def pack(arrays):
        blocks = [jnp.pad(a.reshape(-1), (0, rows_of(a) * LANES - a.size)).reshape(rows_of(a), LANES) for a in arrays]
        rows = sum(b.shape[0] for b in blocks)
        return jnp.pad(jnp.concatenate(blocks, axis=0), ((0, -(-rows // PACK_ROWS) * PACK_ROWS - rows), (0, 0)))

    def unpack(block, likes):
        out, row = [], 0
        for a in likes:
            out.append(block[row:row + rows_of(a)].reshape(-1)[:a.size].reshape(a.shape))
            row += rows_of(a)
        return out

    def quarter_parts(g_xbc, g_mid, g_dt):
        segs = [(g_xbc, 0, 2 * d), (g_dt, 0, heads), (g_dt, LANES, heads), (g_mid, 0, 4 * d)]
        parts = []
        for qi in range(4):
            lo, hi, off, pieces = qi * q_in, (qi + 1) * q_in, 0, []
            for arr, start, width in segs:
                a, b = max(lo, off), min(hi, off + width)
                if a < b:
                    pieces.append(arr[:, start + a - off:start + b - off])
                off += width
            parts.append(pieces[0] if len(pieces) == 1 else jnp.concatenate(pieces, axis=1))
        return jnp.stack(parts)

    sum_in, sum_out = [None] * depth, [None] * depth
    swap_in, swap_out = [None] * depth, [None] * depth
    small_sum = [None] * depth
    parts_in = packed = None
    for l in reversed(range(depth)):
        x_in, hx_t, z_xbc, dsilu, z_mid, z_dt, xbc, y2, hs, ycat_t, o, w_xbc, w_mid, w_dt, w_o = saved[l]
        up = l + 1
        d_o, acc_post = _post_bwd(d_stream, o, g_post[l][None], mods[l])
        if packed is not None:
            d_ycat, gathered = _mm(d_o, w_o, NT, "d_ycat", side=_Exchange(packed, "gather8"))
            small_sum[up] = _sum_lead(gathered, "sum_small")
        else:
            d_ycat = _mm(d_o, w_o, NT, "d_ycat")
        g_out = _mm(ycat_t, d_o, NN, "dw_out", out_dtype=BF16)
        dz_mid, d_y, vec, d_ws, d_bs = _mix_bwd(z_mid, y2, xbc, d_ycat, dsk_row[l], g_ssd[l][None], g_v[l][None], g_mlp[l][None],
                                                w_s[l], w_st[l], b_st[l], ind_head, ind_group)
        d_xbc2, dz_dt, d_bias, d_alog, *got = _ssd_bwd(xbc, z_dt, dtb[l], alog[l], tri, tri_t, ind_t, ind_b, d_y, y2, hs, d, ctx_len,
                                                       side=_Exchange(parts_in, "scatter4") if parts_in is not None else None)
        if parts_in is not None:
            sum_in[up] = _sum_lead(got[0], "sum_w_in")
        dz_xbc, d_cw, d_cb = _conv_bwd(z_xbc, dsilu, d_xbc2, d_y, dsk_row[l], conv_w8[l], ctx_len)
        g_xbc, got_out = _mm(hx_t, dz_xbc, NN, "dw_xbc", out_dtype=BF16, side=_Exchange(g_out.reshape(4, 2 * d // 4, d), "scatter4"))
        sum_out[l] = _sum_lead(got_out, "sum_w_out")
        swaps = [_Exchange(sum_out[l], "swap")] + ([_Exchange(sum_in[up], "swap")] if parts_in is not None else [])
        g_mid, swap_out[l], *swapped = _mm(hx_t, dz_mid, NN, "dw_mid", out_dtype=BF16, side=swaps)
        if parts_in is not None:
            swap_in[up] = swapped[0]
        g_dt = _mm(hx_t, dz_dt, NN, "dw_dt", out_dtype=BF16)
        parts_in = quarter_parts(g_xbc, g_mid, g_dt)
        small.append(dict(
            g_post=acc_post[0, 1] + acc_post[1, 1], conv_w=d_cw[:CONV_TAPS], conv_b=d_cb[0],
            dt_bias=d_bias[:, 0, :heads], a_log=d_alog[:, 0, :heads], d_skip=vec[3, :heads], g_ssd=vec[0], g_v=vec[1],
            w_s=d_ws, b_s=d_bs[:, :groups_mlp].T, g_mlp=vec[2]))
        packed = pack([small[-1][n] for n in names])
        if l == 0:
            r_a = max(LANES, d // 3 // LANES * LANES)
            d_hx, got_a = _mm(dz_xbc, w_xbc, NT, "dhx_xbc", side=_Exchange(parts_in[:, :r_a], "scatter4"))
            d_hx, got_b = _mm(dz_mid, w_mid, NT, "dhx_mid", acc=d_hx, side=_Exchange(parts_in[:, r_a:], "scatter4"))
            sum_in[0] = _sum_lead(jnp.concatenate([got_a, got_b], axis=1), "sum_w_in")
        else:
            d_hx = _mm(dz_xbc, w_xbc, NT, "dhx_xbc")
            d_hx = _mm(dz_mid, w_mid, NT, "dhx_mid", acc=d_hx)
        d_hx = _mm(dz_dt, w_dt, NT, "dhx_dt", acc=d_hx)
        if l == 0:
            d_stream, acc_pre, swap_in[0], gathered = _pre_bwd(
                x_in, d_hx, d_stream, g_pre[l][None], mods[l], latent_only=True,
                side=[_Exchange(sum_in[0], "swap"), _Exchange(packed, "gather8")])
            small_sum[0] = _sum_lead(gathered, "sum_small")
        else:
            d_stream, acc_pre = _pre_bwd(x_in, d_hx, d_stream, g_pre[l][None], mods[l])
        dmods.append(jnp.concatenate([acc_pre[:, 0], acc_pre[:, 1], acc_post[:, 0], acc_pre[:, 2]], axis=1))
    small.reverse(), dmods.reverse()
    grad_x = d_stream[None]

    weights = dict(c_ctx=c_ctx, w_ada=w_ada, b_ada=b_ada, g_pre=g_pre, g_post=g_post, w_in=w_in, conv_w=conv_w, conv_b=conv_b,
                   dt_bias=dt_bias, a_log=a_log, d_skip=d_skip, g_ssd=g_ssd, g_v=g_v, w_s=w_s, b_s=b_s, g_mlp=g_mlp, w_out=w_out)
    m_in = dict(c_ctx=m_c_ctx, w_ada=m_w_ada, b_ada=m_b_ada, g_pre=m_g_pre, g_post=m_g_post, w_in=m_w_in, conv_w=m_conv_w,
                conv_b=m_conv_b, dt_bias=m_dt_bias, a_log=m_a_log, d_skip=m_d_skip, g_ssd=m_g_ssd, g_v=m_g_v, w_s=m_w_s,
                b_s=m_b_s, g_mlp=m_g_mlp, w_out=m_w_out)
    v_in = dict(c_ctx=v_c_ctx, w_ada=v_w_ada, b_ada=v_b_ada, g_pre=v_g_pre, g_post=v_g_post, w_in=v_w_in, conv_w=v_conv_w,
                conv_b=v_conv_b, dt_bias=v_dt_bias, a_log=v_a_log, d_skip=v_d_skip, g_ssd=v_g_ssd, g_v=v_g_v, w_s=v_w_s,
                b_s=v_b_s, g_mlp=v_g_mlp, w_out=v_w_out)
    order = list(weights)
    results = {}

    def adamw_big(n, ga, gb):
        shp = weights[n].shape
        two = lambda a: a.reshape(-1, shp[-1])
        results[n] = [r.reshape(shp) for r in _adamw(two(weights[n]), ga, two(m_in[n]), two(v_in[n]), "adamw_" + n, g2=gb)]

    adamw_big("w_in", jnp.concatenate(sum_in, axis=0), jnp.concatenate(swap_in, axis=0))
    adamw_big("w_out", jnp.concatenate(sum_out, axis=0), jnp.concatenate(swap_out, axis=0))
    per_layer = [unpack(small_sum[l], [small[l][n] for n in names]) for l in range(depth)]
    grads = {n: jnp.stack([per_layer[l][j] for l in range(depth)]) for j, n in enumerate(names)}
    grads["conv_w"] = lax.dynamic_slice_in_dim(grads["conv_w"], chip * (2 * d // 4), 2 * d // 4, axis=2)

    dm_all = _exchange(jnp.stack(dmods).reshape(depth * 2, 4 * d), "gather8", "ag_dmods").reshape(8, depth, 2, 4 * d)
    grads["g_pre"] = _sum_lead(jnp.moveaxis(dm_all[..., 3 * d:], 2, 1).reshape(16, depth, d), "sum_g_pre")
    dm_ctx = _sum_lead(dm_all[:, :, 0, :3 * d], "sum_dm_ctx")
    dm16 = jnp.concatenate([jnp.moveaxis(dm_all[:, :, 1, :3 * d], 0, 1), dm_ctx[:, None, :], jnp.zeros((depth, 7, 3 * d), F32)], axis=1)
    grads["b_ada"] = _rowsum(dm16)[:, 0, :]
    dm_loc = jnp.pad(lax.dynamic_slice_in_dim(dm16, chip * n_ada, n_ada, axis=2), ((0, 0), (0, LANES - 16), (0, 0)))
    c_t = jnp.pad(c16.T, ((0, 0), (0, LANES - 16)))
    g_w_ada, d_scc_part = _ada_bwd(c_t, dm_loc, w_ada)
    adamw_big("w_ada", g_w_ada.reshape(depth * d, n_ada), None)
    d_scc = _sum_lead(_exchange(d_scc_part, "gather8", "ag_dscc").reshape(4, 2, 8, d)[:, 0], "sum_dscc")
    grads["c_ctx"] = _cctx_grad(d_scc[0:1], c_ctx[None])[0]

    rest = [n for n in order if n not in results]
    outs = _adamw(pack([weights[n] for n in rest]), pack([grads[n] for n in rest]), pack([m_in[n] for n in rest]),
                  pack([v_in[n] for n in rest]), "adamw_small")
    for j, res in enumerate(zip(*[unpack(o_, [weights[n] for n in rest]) for o_ in outs])):
        results[rest[j]] = list(res)

    return (loss, grad_x, *[results[n][0] for n in order], *[results[n][1] for n in order],
            *[results[n][2] for n in order], *[results[n][3] for n in order])


def _pad_rows8(a):
    return jnp.pad(a, [(0, 0)] * (a.ndim - 2) + [(0, 8 - a.shape[-2]), (0, 0)])
```

```python
import jax
import jax.numpy as jnp
from jax import lax
from jax.experimental import pallas as pl
from jax.experimental.pallas import tpu as pltpu

F32 = jnp.float32
BF16 = jnp.bfloat16
EPS = 1e-6
Q = 128
TB = 256
ROW = 64
HP = 64
LANES = 128
CONV_TAPS = 5
VMEM_LIMIT = 48 * 1024 * 1024
HI = lax.Precision.HIGHEST
SUM_BLOCK_BYTES = 4 * 1024 * 1024
ADAM_BLOCK_BYTES = 1024 * 1024
PACK_ROWS = 256
MESH = pl.DeviceIdType.MESH
ANY = pl.BlockSpec(memory_space=pl.ANY)

ADAM_LR, ADAM_B1, ADAM_B2, ADAM_EPS, ADAM_WD, ADAM_STEP = 0.001, 0.9, 0.999, 1e-08, 0.01, 10

NN = (((1,), (0,)), ((), ()))
NT = (((1,), (1,)), ((), ()))
TN = (((0,), (0,)), ((), ()))


def _dot(a, b, dims=NN, prec=None):
    return lax.dot_general(a, b, dims, precision=prec, preferred_element_type=F32)


def _params(*sem):
    if sem:
        return pltpu.CompilerParams(vmem_limit_bytes=VMEM_LIMIT, dimension_semantics=sem)
    return pltpu.CompilerParams(vmem_limit_bytes=VMEM_LIMIT)


def _tile(dim, cands):
    for t in cands:
        if dim % t == 0:
            return t
    return dim


def _sigmoid(x):
    return 1.0 / (1.0 + jnp.exp(-x))


def _softplus(x):
    e = jnp.exp(-jnp.abs(x))
    u = 1.0 + e
    um1 = u - 1.0
    l1p = jnp.where(um1 == 0.0, e, jnp.log(u) * (e / jnp.where(um1 == 0.0, 1.0, um1)))
    return jnp.maximum(x, 0.0) + l1p


def _rms(x):
    return lax.rsqrt(jnp.mean(x * x, axis=-1, keepdims=True) + EPS)


def _rms_bwd(x, r, t):
    return r * t - x * (r * r * r) * jnp.mean(x * t, axis=-1, keepdims=True)


def _mm(a, b, dims, name, acc=None, out_dtype=F32, side=None):
    (ca,), (cb,) = dims[0]
    m, k = a.shape[1 - ca], a.shape[ca]
    n = b.shape[1 - cb]
    tk = k if k <= 2048 else _tile(k, (2048, 768, 512, 384, 256, 128))
    nk = k // tk
    tm = _tile(m, (2048, 1024, 768, 512, 384, 256, 128) if nk > 1 else (1024, 768, 512, 384, 256, 128))
    tn = _tile(n, (1024, 512, 256, 128))
    a_spec = pl.BlockSpec((tm, tk), lambda i, j, kk: (i, kk)) if ca == 1 else pl.BlockSpec((tk, tm), lambda i, j, kk: (kk, i))
    b_spec = pl.BlockSpec((tk, tn), lambda i, j, kk: (kk, j)) if cb == 0 else pl.BlockSpec((tn, tk), lambda i, j, kk: (j, kk))
    o_spec = pl.BlockSpec((tm, tn), lambda i, j, kk: (i, j))
    has_acc = acc is not None

    def body(*refs):
        if has_acc:
            a_ref, b_ref, c_ref, o_ref, acc_ref = refs
        else:
            a_ref, b_ref, o_ref, acc_ref = refs
        kk = pl.program_id(2)

        @pl.when(kk == 0)
        def _():
            acc_ref[...] = c_ref[...] if has_acc else jnp.zeros_like(acc_ref)

        acc_ref[...] += _dot(a_ref[...], b_ref[...], dims)

        @pl.when(kk == nk - 1)
        def _():
            o_ref[...] = acc_ref[...].astype(out_dtype)

    res = _hosted_call(
        body, side, name=name, grid=(m // tm, n // tn, nk),
        in_specs=[a_spec, b_spec] + ([o_spec] if has_acc else []),
        out_specs=[o_spec], out_shape=[jax.ShapeDtypeStruct((m, n), out_dtype)],
        scratch_shapes=[pltpu.VMEM((tm, tn), F32)], args=(a, b, acc) if has_acc else (a, b))
    return res[0] if side is None else res


def _which(i):
    return jnp.minimum(i, 1)


def _pre_fwd(x, g_pre, mods):
    t, d = x.shape

    def body(x_ref, g_ref, m_ref, o_ref, ot_ref):
        xb = x_ref[...]
        xn = xb * _rms(xb) * g_ref[...]
        hx = xn * (1.0 + m_ref[0, 1:2, :]) + m_ref[0, 0:1, :]
        o_ref[...] = hx.astype(BF16)
        ot_ref[...] = hx.T.astype(BF16)

    return pl.pallas_call(
        body, name="pre_fwd", grid=(t // TB,),
        in_specs=[pl.BlockSpec((TB, d), lambda i: (i, 0)), pl.BlockSpec((1, d), lambda i: (0, 0)),
                  pl.BlockSpec((1, 8, d), lambda i: (_which(i), 0, 0))],
        out_specs=[pl.BlockSpec((TB, d), lambda i: (i, 0)), pl.BlockSpec((d, TB), lambda i: (0, i))],
        out_shape=[jax.ShapeDtypeStruct((t, d), BF16), jax.ShapeDtypeStruct((d, t), BF16)], compiler_params=_params("arbitrary"),
    )(x, g_pre, mods)


def _pre_bwd(x, d_hx, d_up, g_pre, mods, side=None, latent_only=False):
    t, d = x.shape
    dx_rows = t - TB if latent_only else t
    dx_spec = pl.BlockSpec((TB, d), (lambda i: (jnp.maximum(i - 1, 0), 0)) if latent_only else (lambda i: (i, 0)))

    def body(x_ref, dh_ref, du_ref, g_ref, m_ref, dx_ref, acc_ref):
        i = pl.program_id(0)

        @pl.when(i <= 1)
        def _():
            acc_ref[...] = jnp.zeros_like(acc_ref)

        xb = x_ref[...]
        dh = dh_ref[...]
        r = _rms(xb)
        xr = xb * r
        d_xn = dh * (1.0 + m_ref[0, 1:2, :])
        dx_ref[...] = du_ref[...] + _rms_bwd(xb, r, d_xn * g_ref[...])
        acc_ref[0, 0:1, :] += jnp.sum(dh, axis=0, keepdims=True)
        acc_ref[0, 1:2, :] += jnp.sum(dh * (xr * g_ref[...]), axis=0, keepdims=True)
        acc_ref[0, 2:3, :] += jnp.sum(d_xn * xr, axis=0, keepdims=True)

    blk = pl.BlockSpec((TB, d), lambda i: (i, 0))
    return _hosted_call(
        body, side, name="pre_bwd", grid=(t // TB,),
        in_specs=[blk, blk, blk, pl.BlockSpec((1, d), lambda i: (0, 0)),
                  pl.BlockSpec((1, 8, d), lambda i: (_which(i), 0, 0))],
        out_specs=[dx_spec, pl.BlockSpec((1, 8, d), lambda i: (_which(i), 0, 0))],
        out_shape=[jax.ShapeDtypeStruct((dx_rows, d), F32), jax.ShapeDtypeStruct((2, 8, d), F32)],
        scratch_shapes=[], args=(x, d_hx, d_up, g_pre, mods))


def _post_fwd(o, x, g_post, mods):
    t, d = x.shape

    def body(o_ref, x_ref, g_ref, m_ref, y_ref):
        ob = o_ref[...]
        y_ref[...] = x_ref[...] + m_ref[0, 2:3, :] * (ob * _rms(ob) * g_ref[...])

    blk = pl.BlockSpec((TB, d), lambda i: (i, 0))
    return pl.pallas_call(
        body, name="post_fwd", grid=(t // TB,),
        in_specs=[blk, blk, pl.BlockSpec((1, d), lambda i: (0, 0)), pl.BlockSpec((1, 8, d), lambda i: (_which(i), 0, 0))],
        out_specs=blk, out_shape=jax.ShapeDtypeStruct((t, d), F32), compiler_params=_params("arbitrary"),
    )(o, x, g_post, mods)


def _post_bwd(d_y, o, g_post, mods):
    t, d = o.shape

    def body(dy_ref, o_ref, g_ref, m_ref, do_ref, acc_ref):
        i = pl.program_id(0)

        @pl.when(i <= 1)
        def _():
            acc_ref[...] = jnp.zeros_like(acc_ref)

        ob = o_ref[...]
        dy = dy_ref[...]
        r = _rms(ob)
        orr = ob * r
        d_out = dy * m_ref[0, 2:3, :]
        do_ref[...] = _rms_bwd(ob, r, d_out * g_ref[...]).astype(BF16)
        acc_ref[0, 0:1, :] += jnp.sum(dy * (orr * g_ref[...]), axis=0, keepdims=True)
        acc_ref[0, 1:2, :] += jnp.sum(d_out * orr, axis=0, keepdims=True)

    blk = pl.BlockSpec((TB, d), lambda i: (i, 0))
    return pl.pallas_call(
        body, name="post_bwd", grid=(t // TB,),
        in_specs=[blk, blk, pl.BlockSpec((1, d), lambda i: (0, 0)), pl.BlockSpec((1, 8, d), lambda i: (_which(i), 0, 0))],
        out_specs=[blk, pl.BlockSpec((1, 8, d), lambda i: (_which(i), 0, 0))],
        out_shape=[jax.ShapeDtypeStruct((t, d), BF16), jax.ShapeDtypeStruct((2, 8, d), F32)],
        compiler_params=_params("arbitrary"),
    )(d_y, o, g_post, mods)


def _loss_grad(xf, target):
    t, d = xf.shape

    def body(x_ref, t_ref, loss_ref, dx_ref):
        i = pl.program_id(0)

        @pl.when(i == 0)
        def _():
            loss_ref[...] = jnp.zeros_like(loss_ref)
            dx_ref[...] = jnp.zeros_like(dx_ref)

        @pl.when(i > 0)
        def _():
            err = x_ref[...] - t_ref[...]
            loss_ref[...] += jnp.sum(err * err).reshape(1, 1)
            dx_ref[...] = err * (1.0 / d)

    return pl.pallas_call(
        body, name="loss_grad", grid=(t // TB,),
        in_specs=[pl.BlockSpec((TB, d), lambda i: (i, 0)), pl.BlockSpec((TB, d), lambda i: (jnp.maximum(i - 1, 0), 0))],
        out_specs=[pl.BlockSpec((1, 1), lambda i: (0, 0)), pl.BlockSpec((TB, d), lambda i: (i, 0))],
        out_shape=[jax.ShapeDtypeStruct((1, 1), F32), jax.ShapeDtypeStruct((t, d), F32)],
        compiler_params=_params("arbitrary"),
    )(xf, target)


def _conv_terms(zb, pos, row_len):
    out = []
    for k in range(CONV_TAPS):
        o = k - CONV_TAPS // 2
        sh = zb if o == 0 else pltpu.roll(zb, (-o) % TB, 0)
        out.append(jnp.where((pos + o >= 0) & (pos + o < row_len), sh, 0.0))
    return out


def _row_pos(i, ctx_len):
    row_len = jnp.where(i == 0, ctx_len, ROW)
    pos = lax.broadcasted_iota(jnp.int32, (TB, 1), 0) & (row_len - 1)
    return pos, row_len


def _conv_fwd(z_xbc, conv_w8, conv_b, ctx_len):
    t, c = z_xbc.shape
    tc = _tile(c, (1024, 512, 256, 128))

    def body(z_ref, w_ref, b_ref, o_ref, ds_ref):
        pos, row_len = _row_pos(pl.program_id(1), ctx_len)
        terms = _conv_terms(z_ref[...], pos, row_len)
        pre = b_ref[...]
        for k in range(CONV_TAPS):
            pre = pre + terms[k] * w_ref[k:k + 1, :]
        sig = _sigmoid(pre)
        o_ref[...] = pre * sig
        ds_ref[...] = sig * (1.0 + pre * (1.0 - sig))

    blk = pl.BlockSpec((TB, tc), lambda j, i: (i, j))
    return pl.pallas_call(
        body, name="conv_fwd", grid=(c // tc, t // TB),
        in_specs=[blk, pl.BlockSpec((8, tc), lambda j, i: (0, j)), pl.BlockSpec((1, tc), lambda j, i: (0, j))],
        out_specs=[blk, blk], out_shape=[jax.ShapeDtypeStruct((t, c), F32)] * 2, compiler_params=_params("arbitrary", "arbitrary"),
    )(z_xbc, conv_w8, conv_b)


def _conv_bwd(z_xbc, dsilu, d_xbc2, d_y, d_skip_row, conv_w8, ctx_len):
    t, c = z_xbc.shape
    d = d_y.shape[1]
    tc = _tile(d, (1024, 512, 256, 128))
    nskip = d // tc

    def body(z_ref, dsl_ref, g2_ref, dy_ref, ds_ref, w_ref, dz_ref, dw_ref, db_ref):
        j, i = pl.program_id(0), pl.program_id(1)

        @pl.when(i == 0)
        def _():
            dw_ref[...] = jnp.zeros_like(dw_ref)
            db_ref[...] = jnp.zeros_like(db_ref)

        pos, row_len = _row_pos(i, ctx_len)
        skip = jnp.where(j < nskip, 1.0, 0.0) * ds_ref[...]
        d_pre = (g2_ref[0] + g2_ref[1] + dy_ref[...] * skip) * dsl_ref[...]
        db_ref[...] += jnp.sum(d_pre, axis=0, keepdims=True)
        zb = z_ref[...]
        dz = jnp.zeros_like(d_pre)
        for k in range(CONV_TAPS):
            o = k - CONV_TAPS // 2
            sh = d_pre if o == 0 else pltpu.roll(d_pre, o % TB, 0)
            sh = jnp.where((pos - o >= 0) & (pos - o < row_len), sh, 0.0)
            dw_ref[k:k + 1, :] += jnp.sum(sh * zb, axis=0, keepdims=True)
            dz = dz + sh * w_ref[k:k + 1, :]
        dz_ref[...] = dz.astype(BF16)

    jd = lambda j: jnp.minimum(j, nskip - 1)
    blk = pl.BlockSpec((TB, tc), lambda j, i: (i, j))
    return pl.pallas_call(
        body, name="conv_bwd", grid=(c // tc, t // TB),
        in_specs=[blk, blk, pl.BlockSpec((2, TB, tc), lambda j, i: (0, i, j)),
                  pl.BlockSpec((TB, tc), lambda j, i: (i, jd(j))), pl.BlockSpec((1, tc), lambda j, i: (0, jd(j))),
                  pl.BlockSpec((8, tc), lambda j, i: (0, j))],
        out_specs=[blk, pl.BlockSpec((8, tc), lambda j, i: (0, j)), pl.BlockSpec((1, tc), lambda j, i: (0, j))],
        out_shape=[jax.ShapeDtypeStruct((t, c), BF16), jax.ShapeDtypeStruct((8, c), F32), jax.ShapeDtypeStruct((1, c), F32)],
        compiler_params=_params("arbitrary", "arbitrary"),
    )(z_xbc, dsilu, d_xbc2, d_y, d_skip_row, conv_w8)


def _scan_chunk(dirn, s, nch, ncc):
    bwd = jnp.where(s < ncc, ncc - 1 - s, nch - 1 - (s - ncc))
    return jnp.where(dirn == 0, s, bwd)


def _ssd_decays(dt_ref, dtb_ref, alog_ref, tri):
    raw = dt_ref[...] + dtb_ref[0]
    dt = _softplus(raw)
    a_neg = -jnp.exp(alog_ref[0])
    a = dt * a_neg
    s = _dot(tri, a, NN, HI)
    stot = jnp.sum(a, axis=0, keepdims=True)
    return raw, dt, a_neg, s, stot, s.T


def _split(v):
    hi = v.astype(BF16)
    return hi, (v - hi.astype(F32)).astype(BF16)


def _expand(v, indt_ref):
    hi, lo = _split(v)
    return _dot(hi, indt_ref[...]) + _dot(lo, indt_ref[...])


def _head_sums(v, ind_ref):
    hi, lo = _split(v)
    return _dot(hi, ind_ref[...]) + _dot(lo, ind_ref[...])


def _ssd_fwd(xbc, z_dt, dtb, alog, tri, ind_t, d, ctx_len, side=None):
    t = xbc.shape[0]
    nch, ncc = t // Q, ctx_len // Q
    heads = d // HP
    groups = heads // 4
    gn = groups * LANES

    def body(xbc_ref, dt_ref, dtb_ref, alog_ref, tri_ref, indt_ref, y_ref, hs_ref, h_scr, xdb_scr, xde_scr, esx_scr):
        @pl.when(pl.program_id(1) == 0)
        def _():
            h_scr[...] = jnp.zeros_like(h_scr)

        tri = tri_ref[0]
        mask = tri > 0.0
        _, dt, _, s, stot, s_t = _ssd_decays(dt_ref, dtb_ref, alog_ref, tri)
        esx_scr[...] = _expand(jnp.exp(s), indt_ref)
        etot_x = _expand(jnp.broadcast_to(jnp.exp(stot), (8, LANES)), indt_ref)[0:1]
        xd = xbc_ref[:, :d] * _expand(dt, indt_ref)
        xdb_scr[...] = xd.astype(BF16)
        xde_scr[...] = (xd * _expand(jnp.exp(stot - s), indt_ref)).astype(BF16)
        left = lax.broadcasted_iota(jnp.int32, (Q, LANES), 1) < HP
        hs_ref[0, 0] = h_scr[...]
        for g in range(groups):
            b32 = xbc_ref[:, d + g * LANES:d + (g + 1) * LANES]
            bb = b32.astype(BF16)
            bbt = b32.T.astype(BF16)
            cb = xbc_ref[:, d + gn + g * LANES:d + gn + (g + 1) * LANES].astype(BF16)
            cbt = _dot(cb, bb, NT)
            gcols = slice(4 * g * HP, 4 * (g + 1) * HP)
            hg = h_scr[:, gcols]
            y_off = _dot(cb, hg.astype(BF16)) * esx_scr[:, gcols]
            h_scr[:, gcols] = hg * etot_x[:, gcols] + _dot(bbt, xde_scr[:, gcols])
            for j, pr in enumerate((2 * g, 2 * g + 1)):
                h0 = 2 * pr
                cols = slice(pr * LANES, (pr + 1) * LANES)
                xdb = xdb_scr[:, cols]
                res = []
                for h in (h0, h0 + 1):
                    lm = jnp.exp(jnp.where(mask, s[:, h:h + 1] - s_t[h:h + 1, :], -jnp.inf))
                    res.append(_dot((cbt * lm).astype(BF16), xdb))
                y_ref[0, :, cols] = jnp.where(left, res[0], res[1]) + y_off[:, j * LANES:(j + 1) * LANES]

    cidx = lambda dd, ss: _scan_chunk(dd, ss, nch, ncc)
    return _hosted_call(
        body, side, name="ssd_fwd", grid=(2, nch),
        in_specs=[pl.BlockSpec((Q, 2 * d), lambda dd, ss: (cidx(dd, ss), 0)),
                  pl.BlockSpec((Q, LANES), lambda dd, ss: (cidx(dd, ss), dd)),
                  pl.BlockSpec((1, 1, LANES), lambda dd, ss: (dd, 0, 0)),
                  pl.BlockSpec((1, 1, LANES), lambda dd, ss: (dd, 0, 0)),
                  pl.BlockSpec((1, Q, Q), lambda dd, ss: (dd, 0, 0)),
                  pl.BlockSpec((LANES, d), lambda dd, ss: (0, 0))],
        out_specs=[pl.BlockSpec((1, Q, d), lambda dd, ss: (dd, cidx(dd, ss), 0)),
                   pl.BlockSpec((1, 1, LANES, d), lambda dd, ss: (dd, cidx(dd, ss), 0, 0))],
        out_shape=[jax.ShapeDtypeStruct((2, t, d), F32), jax.ShapeDtypeStruct((2, nch, LANES, d), F32)],
        scratch_shapes=[pltpu.VMEM((LANES, d), F32), pltpu.VMEM((Q, d), BF16), pltpu.VMEM((Q, d), BF16), pltpu.VMEM((Q, d), F32)],
        args=(xbc, z_dt, dtb, alog, tri, ind_t))


def _ssd_bwd(xbc, z_dt, dtb, alog, tri, tri_t, ind_t, ind, d_y, y2, hs, d, ctx_len, side=None):
    t = xbc.shape[0]
    nch, ncc = t // Q, ctx_len // Q
    heads = d // HP
    groups = heads // 4
    gn = groups * LANES

    def body(xbc_ref, dt_ref, dtb_ref, alog_ref, tri_ref, trit_ref, indt_ref, ind_ref, dy_ref, y_ref, hs_ref,
             dx_ref, dzdt_ref, dbias_ref, dalog_ref, dh_scr, dtx_scr, ex_scr, xdb_scr, xde_scr, dyb_scr, dye_scr, dxd_scr, bdh_scr):
        @pl.when(pl.program_id(1) == 0)
        def _():
            dh_scr[...] = jnp.zeros_like(dh_scr)
            dbias_ref[...] = jnp.zeros_like(dbias_ref)
            dalog_ref[...] = jnp.zeros_like(dalog_ref)

        tri = tri_ref[0]
        mask = tri > 0.0
        mask_t = trit_ref[0] > 0.0
        raw, dt, a_neg, s, stot, s_t = _ssd_decays(dt_ref, dtb_ref, alog_ref, tri)
        etot = jnp.exp(stot)
        etot_x = _expand(jnp.broadcast_to(etot, (8, LANES)), indt_ref)[0:1]
        dtx_scr[...] = _expand(dt, indt_ref)
        ex_scr[...] = _expand(jnp.exp(stot - s), indt_ref)
        xd = xbc_ref[:, :d] * dtx_scr[...]
        xdb_scr[...] = xd.astype(BF16)
        xde_scr[...] = (xd * ex_scr[...]).astype(BF16)
        dyb_scr[...] = dy_ref[...].astype(BF16)
        dye_scr[...] = (dy_ref[...] * _dot(jnp.exp(s).astype(BF16), indt_ref[...])).astype(BF16)
        hd_cols = jnp.sum(dh_scr[...] * hs_ref[0, 0], axis=0, keepdims=True)
        left = lax.broadcasted_iota(jnp.int32, (Q, LANES), 1) < HP
        for g in range(groups):
            b32 = xbc_ref[:, d + g * LANES:d + (g + 1) * LANES]
            c32 = xbc_ref[:, d + gn + g * LANES:d + gn + (g + 1) * LANES]
            bb, cb = b32.astype(BF16), c32.astype(BF16)
            c_t = c32.T.astype(BF16)
            cbt = _dot(cb, bb, NT)
            cbt_t = _dot(bb, cb, NT)
            d_cbt = jnp.zeros((Q, Q), F32)
            gcols = slice(4 * g * HP, 4 * (g + 1) * HP)
            dyeb = dye_scr[:, gcols]
            dhg = dh_scr[:, gcols]
            dhb = dhg.astype(BF16)
            bdh_scr[:, gcols] = _dot(bb, dhb)
            d_c = _dot(dyeb, hs_ref[0, 0, :, gcols].astype(BF16), NT)
            d_b = _dot(xde_scr[:, gcols], dhb, NT)
            dh_scr[:, gcols] = dhg * etot_x[:, gcols] + _dot(c_t, dyeb)
            for pr in (2 * g, 2 * g + 1):
                h0 = 2 * pr
                cols = slice(pr * LANES, (pr + 1) * LANES)
                xdb = xdb_scr[:, cols]
                dyb = dyb_scr[:, cols]
                parts = []
                for hh, h in enumerate((h0, h0 + 1)):
                    mine = left if hh == 0 else jnp.logical_not(left)
                    diff = s[:, h:h + 1] - s_t[h:h + 1, :]
                    lm = jnp.exp(jnp.where(mask, diff, -jnp.inf))
                    lm_t = jnp.exp(jnp.where(mask_t, -diff, -jnp.inf))
                    gm = _dot(jnp.where(mine, dyb, jnp.zeros_like(dyb)), xdb, NT)
                    d_cbt = d_cbt + gm * lm
                    parts.append(_dot((cbt_t * lm_t).astype(BF16), dyb))
                dxd_scr[:, cols] = jnp.where(left, parts[0], parts[1])
            dx_ref[0, :, d + g * LANES:d + (g + 1) * LANES] = d_b + _dot(d_cbt.T.astype(BF16), cb)
            dx_ref[0, :, d + gn + g * LANES:d + gn + (g + 1) * LANES] = d_c + _dot(d_cbt.astype(BF16), bb)
        x = xbc_ref[:, :d]
        ebdh = ex_scr[...] * bdh_scr[...]
        d_xd = dxd_scr[...] + ebdh
        dx_ref[0, :, :d] = d_xd * dtx_scr[...]
        xe = x * dtx_scr[...] * ebdh
        d_s = _head_sums(dyb_scr[...].astype(F32) * y_ref[0] - xdb_scr[...].astype(F32) * dxd_scr[...] - xe, ind_ref)
        r_dx = _head_sums(d_xd * x, ind_ref)
        row8 = lax.broadcasted_iota(jnp.int32, (8, d), 0)
        tot = _head_sums(jnp.where(row8 == 0, jnp.sum(xe, axis=0, keepdims=True), jnp.where(row8 == 1, hd_cols, 0.0)), ind_ref)
        d_stot = tot[0:1] + etot * tot[1:2]
        d_a = _dot(trit_ref[0], d_s, NN, HI) + d_stot
        valid = lax.broadcasted_iota(jnp.int32, (Q, LANES), 1) < heads
        d_dt_tot = jnp.where(valid, d_a * a_neg + r_dx, 0.0)
        d_raw = d_dt_tot * _sigmoid(raw)
        dzdt_ref[...] = d_raw.astype(BF16)
        dbias_ref[0] += jnp.sum(d_raw, axis=0, keepdims=True)
        dalog_ref[0] += jnp.sum(jnp.where(valid, d_a * dt, 0.0), axis=0, keepdims=True) * a_neg

    cidx = lambda dd, ss: _scan_chunk(dd, nch - 1 - ss, nch, ncc)
    full = lambda shape: pltpu.VMEM(shape, F32)
    half = lambda shape: pltpu.VMEM(shape, BF16)
    return _hosted_call(
        body, side, name="ssd_bwd", grid=(2, nch),
        in_specs=[pl.BlockSpec((Q, 2 * d), lambda dd, ss: (cidx(dd, ss), 0)),
                  pl.BlockSpec((Q, LANES), lambda dd, ss: (cidx(dd, ss), dd)),
                  pl.BlockSpec((1, 1, LANES), lambda dd, ss: (dd, 0, 0)),
                  pl.BlockSpec((1, 1, LANES), lambda dd, ss: (dd, 0, 0)),
                  pl.BlockSpec((1, Q, Q), lambda dd, ss: (dd, 0, 0)),
                  pl.BlockSpec((1, Q, Q), lambda dd, ss: (dd, 0, 0)),
                  pl.BlockSpec((LANES, d), lambda dd, ss: (0, 0)),
                  pl.BlockSpec((d, LANES), lambda dd, ss: (0, 0)),
                  pl.BlockSpec((Q, d), lambda dd, ss: (cidx(dd, ss), 0)),
                  pl.BlockSpec((1, Q, d), lambda dd, ss: (dd, cidx(dd, ss), 0)),
                  pl.BlockSpec((1, 1, LANES, d), lambda dd, ss: (dd, cidx(dd, ss), 0, 0))],
        out_specs=[pl.BlockSpec((1, Q, 2 * d), lambda dd, ss: (dd, cidx(dd, ss), 0)),
                   pl.BlockSpec((Q, LANES), lambda dd, ss: (cidx(dd, ss), dd)),
                   pl.BlockSpec((1, 1, LANES), lambda dd, ss: (dd, 0, 0)),
                   pl.BlockSpec((1, 1, LANES), lambda dd, ss: (dd, 0, 0))],
        out_shape=[jax.ShapeDtypeStruct((2, t, 2 * d), F32), jax.ShapeDtypeStruct((t, 2 * LANES), BF16),
                   jax.ShapeDtypeStruct((2, 1, LANES), F32), jax.ShapeDtypeStruct((2, 1, LANES), F32)],
        scratch_shapes=[full((LANES, d)), full((Q, d)), full((Q, d)), half((Q, d)), half((Q, d)), half((Q, d)), half((Q, d)),
                        full((Q, d)), full((Q, d))],
        args=(xbc, z_dt, dtb, alog, tri, tri_t, ind_t, ind, d_y, y2, hs))


def _mix_common(zm_ref, y2_ref, xh_ref, dsk_ref, gv_ref, ws_ref, bst_ref, d):
    groups = d // LANES
    z_ssd, u, v, z_mlp = (zm_ref[:, k * d:(k + 1) * d].astype(F32) for k in range(4))
    y = y2_ref[0] + y2_ref[1] + dsk_ref[...] * xh_ref[...]
    sig_a = _sigmoid(z_ssd)
    ya_pre = y * (z_ssd * sig_a)
    r_v = _rms(v)
    vn = (v * r_v * gv_ref[...]).astype(BF16)
    sg = jnp.concatenate(
        [_dot(ws_ref[g].astype(BF16), vn[:, g * LANES:(g + 1) * LANES]) + bst_ref[:, g:g + 1] for g in range(groups)], axis=1)
    sig_m = _sigmoid(z_mlp)
    yb_pre = u * sg * (z_mlp * sig_m)
    return z_ssd, u, v, z_mlp, y, sig_a, ya_pre, r_v, vn, sg, sig_m, yb_pre


def _mix_fwd(z_mid, y2, xbc, dsk_row, g_ssd, g_v, g_mlp, w_s, b_st):
    t = z_mid.shape[0]
    d = z_mid.shape[1] // 4
    groups = d // LANES

    def body(zm_ref, y2_ref, xh_ref, dsk_ref, ga_ref, gv_ref, gm_ref, ws_ref, bst_ref, o_ref, ot_ref):
        (_, _, _, _, _, _, ya_pre, _, _, _, _, yb_pre) = _mix_common(zm_ref, y2_ref, xh_ref, dsk_ref, gv_ref, ws_ref, bst_ref, d)
        y_a = ya_pre * _rms(ya_pre) * ga_ref[...]
        y_b = yb_pre * _rms(yb_pre) * gm_ref[...]
        o_ref[:, :d] = y_a.astype(BF16)
        o_ref[:, d:] = y_b.astype(BF16)
        ot_ref[:d, :] = y_a.T.astype(BF16)
        ot_ref[d:, :] = y_b.T.astype(BF16)

    row = pl.BlockSpec((1, d), lambda i: (0, 0))
    return pl.pallas_call(
        body, name="mix_fwd", grid=(t // Q,),
        in_specs=[pl.BlockSpec((Q, 4 * d), lambda i: (i, 0)), pl.BlockSpec((2, Q, d), lambda i: (0, i, 0)),
                  pl.BlockSpec((Q, d), lambda i: (i, 0)), row, row, row, row,
                  pl.BlockSpec((groups, Q, Q), lambda i: (0, 0, 0)), pl.BlockSpec((Q, LANES), lambda i: (0, 0))],
        out_specs=[pl.BlockSpec((Q, 2 * d), lambda i: (i, 0)), pl.BlockSpec((2 * d, Q), lambda i: (0, i))],
        out_shape=[jax.ShapeDtypeStruct((t, 2 * d), BF16), jax.ShapeDtypeStruct((2 * d, t), BF16)], compiler_params=_params("arbitrary"),
    )(z_mid, y2, xbc, dsk_row, g_ssd, g_v, g_mlp, w_s, b_st)


def _mix_bwd(z_mid, y2, xbc, d_ycat, dsk_row, g_ssd, g_v, g_mlp, w_s, w_st, b_st, ind_head, ind_group):
    t = z_mid.shape[0]
    d = z_mid.shape[1] // 4
    groups = d // LANES
    nsteps = t // Q

    def body(zm_ref, y2_ref, xh_ref, dyc_ref, dsk_ref, ga_ref, gv_ref, gm_ref, ws_ref, wst_ref, bst_ref, ih_ref, ig_ref,
             dzm_ref, dy_ref, vec_ref, dws_ref, dbs_ref, dsk_acc, dsg_acc):
        i = pl.program_id(0)

        @pl.when(i == 0)
        def _():
            vec_ref[...] = jnp.zeros_like(vec_ref)
            dws_ref[...] = jnp.zeros_like(dws_ref)
            dsk_acc[...] = jnp.zeros_like(dsk_acc)
            dsg_acc[...] = jnp.zeros_like(dsg_acc)

        (z_ssd, u, v, z_mlp, y, sig_a, ya_pre, r_v, vn, sg, sig_m, yb_pre) = _mix_common(
            zm_ref, y2_ref, xh_ref, dsk_ref, gv_ref, ws_ref, bst_ref, d)
        d_ya = dyc_ref[:, :d].astype(F32)
        r_a = _rms(ya_pre)
        vec_ref[0:1, :] += jnp.sum(d_ya * (ya_pre * r_a), axis=0, keepdims=True)
        d_ya_pre = _rms_bwd(ya_pre, r_a, d_ya * ga_ref[...])
        d_y = d_ya_pre * (z_ssd * sig_a)
        dy_ref[...] = d_y
        dsk_acc[...] += jnp.sum(d_y * xh_ref[...], axis=0, keepdims=True)
        dzm_ref[:, 0:d] = (d_ya_pre * y * (sig_a * (1.0 + z_ssd * (1.0 - sig_a)))).astype(BF16)
        d_yb = dyc_ref[:, d:].astype(F32)
        r_b = _rms(yb_pre)
        vec_ref[2:3, :] += jnp.sum(d_yb * (yb_pre * r_b), axis=0, keepdims=True)
        d_yb_pre = _rms_bwd(yb_pre, r_b, d_yb * gm_ref[...])
        silu_m = z_mlp * sig_m
        dzm_ref[:, d:2 * d] = (d_yb_pre * sg * silu_m).astype(BF16)
        dzm_ref[:, 3 * d:4 * d] = (d_yb_pre * u * sg * (sig_m * (1.0 + z_mlp * (1.0 - sig_m)))).astype(BF16)
        d_sg = d_yb_pre * u * silu_m
        dsg_acc[...] += d_sg
        d_sgb = d_sg.astype(BF16)
        d_vn = []
        for g in range(groups):
            cols = slice(g * LANES, (g + 1) * LANES)
            dws_ref[g] += _dot(d_sgb[:, cols], vn[:, cols], NT)
            d_vn.append(_dot(wst_ref[g].astype(BF16), d_sgb[:, cols]))
        d_vn = jnp.concatenate(d_vn, axis=1)
        vec_ref[1:2, :] += jnp.sum(d_vn * (v * r_v), axis=0, keepdims=True)
        dzm_ref[:, 2 * d:3 * d] = _rms_bwd(v, r_v, d_vn * gv_ref[...]).astype(BF16)

        @pl.when(i == nsteps - 1)
        def _():
            vec_ref[3:4, 0:LANES] = _dot(dsk_acc[...], ih_ref[...], NN, HI)
            dbs_ref[...] = _dot(dsg_acc[...], ig_ref[...], NN, HI)

    row = pl.BlockSpec((1, d), lambda i: (0, 0))
    wsp = pl.BlockSpec((groups, Q, Q), lambda i: (0, 0, 0))
    ind = pl.BlockSpec((d, LANES), lambda i: (0, 0))
    return pl.pallas_call(
        body, name="mix_bwd", grid=(nsteps,),
        in_specs=[pl.BlockSpec((Q, 4 * d), lambda i: (i, 0)), pl.BlockSpec((2, Q, d), lambda i: (0, i, 0)),
                  pl.BlockSpec((Q, d), lambda i: (i, 0)), pl.BlockSpec((Q, 2 * d), lambda i: (i, 0)),
                  row, row, row, row, wsp, wsp, pl.BlockSpec((Q, LANES), lambda i: (0, 0)), ind, ind],
        out_specs=[pl.BlockSpec((Q, 4 * d), lambda i: (i, 0)), pl.BlockSpec((Q, d), lambda i: (i, 0)),
                   pl.BlockSpec((8, d), lambda i: (0, 0)), wsp, pl.BlockSpec((Q, LANES), lambda i: (0, 0))],
        out_shape=[jax.ShapeDtypeStruct((t, 4 * d), BF16), jax.ShapeDtypeStruct((t, d), F32),
                   jax.ShapeDtypeStruct((8, d), F32), jax.ShapeDtypeStruct((groups, Q, Q), F32),
                   jax.ShapeDtypeStruct((Q, LANES), F32)],
        scratch_shapes=[pltpu.VMEM((1, d), F32), pltpu.VMEM((Q, d), F32)],
        compiler_params=_params("arbitrary"),
    )(z_mid, y2, xbc, d_ycat, dsk_row, g_ssd, g_v, g_mlp, w_s, w_st, b_st, ind_head, ind_group)


def _ada_fwd(c16, w_ada, b_loc):
    depth, d, n = w_ada.shape
    tn = _tile(n, (512, 256, 128))

    def body(c_ref, w_ref, b_ref, o_ref):
        cv = c_ref[...]
        o_ref[0] = _dot(cv * _sigmoid(cv), w_ref[0], NN, HI) + b_ref[0]

    return pl.pallas_call(
        body, name="ada_fwd", grid=(depth, n // tn),
        in_specs=[pl.BlockSpec((16, d), lambda l, j: (0, 0)), pl.BlockSpec((1, d, tn), lambda l, j: (l, 0, j)),
                  pl.BlockSpec((1, 1, tn), lambda l, j: (l, 0, j))],
        out_specs=pl.BlockSpec((1, 16, tn), lambda l, j: (l, 0, j)),
        out_shape=jax.ShapeDtypeStruct((depth, 16, n), F32), compiler_params=_params("arbitrary", "arbitrary"),
    )(c16, w_ada, b_loc)


def _ada_bwd(c_t, dm_loc, w_ada):
    depth, d, n = w_ada.shape
    tn = _tile(n, (512, 256, 128))

    def body(s_ref, dm_ref, w_ref, gw_ref, dsc_ref):
        @pl.when((pl.program_id(0) == 0) & (pl.program_id(1) == 0))
        def _():
            dsc_ref[...] = jnp.zeros_like(dsc_ref)

        cv = s_ref[...]
        gw_ref[0] = _dot(cv * _sigmoid(cv), dm_ref[0], NN, HI)
        dsc_ref[...] += _dot(dm_ref[0, 8:16, :], w_ref[0], NT, HI)

    return pl.pallas_call(
        body, name="ada_bwd", grid=(depth, n // tn),
        in_specs=[pl.BlockSpec((d, LANES), lambda l, j: (0, 0)), pl.BlockSpec((1, LANES, tn), lambda l, j: (l, 0, j)),
                  pl.BlockSpec((1, d, tn), lambda l, j: (l, 0, j))],
        out_specs=[pl.BlockSpec((1, d, tn), lambda l, j: (l, 0, j)), pl.BlockSpec((8, d), lambda l, j: (0, 0))],
        out_shape=[jax.ShapeDtypeStruct((depth, d, n), F32), jax.ShapeDtypeStruct((8, d), F32)],
        compiler_params=_params("arbitrary", "arbitrary"),
    )(c_t, dm_loc, w_ada)


def _rowsum(x):
    depth, r, n = x.shape

    def body(x_ref, o_ref):
        o_ref[0] = jnp.sum(x_ref[0], axis=0, keepdims=True)

    return pl.pallas_call(
        body, name="rowsum", grid=(depth,),
        in_specs=[pl.BlockSpec((1, r, n), lambda l: (l, 0, 0))], out_specs=pl.BlockSpec((1, 1, n), lambda l: (l, 0, 0)),
        out_shape=jax.ShapeDtypeStruct((depth, 1, n), F32), compiler_params=_params("arbitrary"),
    )(x)


def _cctx_grad(d_scc, c_ctx_row):
    def body(g_ref, c_ref, o_ref):
        cv = c_ref[...]
        sig = _sigmoid(cv)
        o_ref[...] = g_ref[...] * (sig * (1.0 + cv * (1.0 - sig)))

    return pl.pallas_call(body, name="cctx_grad", out_shape=jax.ShapeDtypeStruct(c_ctx_row.shape, F32))(d_scc, c_ctx_row)


def _sum_lead(x, name):
    k, r, c = x.shape
    tr = _tile(r, [tt for tt in (1024, 512, 256, 128, 64, 32, 16, 8) if k * tt * c * x.dtype.itemsize <= SUM_BLOCK_BYTES])

    def body(x_ref, o_ref):
        acc = x_ref[0].astype(F32)
        for e in range(1, k):
            acc = acc + x_ref[e].astype(F32)
        o_ref[...] = acc

    return pl.pallas_call(
        body, name=name, grid=(r // tr,),
        in_specs=[pl.BlockSpec((k, tr, c), lambda i: (0, i, 0))], out_specs=pl.BlockSpec((tr, c), lambda i: (i, 0)),
        out_shape=jax.ShapeDtypeStruct((r, c), F32), compiler_params=_params("arbitrary"),
    )(x)


def _adamw(w, g, m, v, name, g2=None, side=None):
    r, c = w.shape
    tr = _tile(r, [tt for tt in (2048, 1024, 512, 256, 128, 64, 32, 16, 8) if tt * c * 4 <= ADAM_BLOCK_BYTES])
    two = g2 is not None
    bc1 = 1.0 - ADAM_B1 ** ADAM_STEP
    bc2 = 1.0 - ADAM_B2 ** ADAM_STEP

    def body(*refs):
        if two:
            w_ref, g_ref, g2_ref, m_ref, v_ref, go_ref, d_ref, mo_ref, vo_ref = refs
            gr = g_ref[...] + g2_ref[...]
        else:
            w_ref, g_ref, m_ref, v_ref, go_ref, d_ref, mo_ref, vo_ref = refs
            gr = g_ref[...]
        mn = ADAM_B1 * m_ref[...] + (1.0 - ADAM_B1) * gr
        vn = ADAM_B2 * v_ref[...] + (1.0 - ADAM_B2) * (gr * gr)
        go_ref[...] = gr
        mo_ref[...] = mn
        vo_ref[...] = vn
        d_ref[...] = -ADAM_LR * ((mn / bc1) / (jnp.sqrt(vn / bc2) + ADAM_EPS) + ADAM_WD * w_ref[...])

    blk = pl.BlockSpec((tr, c), lambda i: (i, 0))
    ins = (w, g, g2, m, v) if two else (w, g, m, v)
    return _hosted_call(body, side, name=name, grid=(r // tr,), in_specs=[blk] * len(ins), out_specs=[blk] * 4,
                        out_shape=[jax.ShapeDtypeStruct((r, c), F32)] * 4, scratch_shapes=[], args=ins)


def _flip(pos, k):
    x, y, c = pos
    return (x ^ ((k >> 2) & 1), y ^ ((k >> 1) & 1), c ^ (k & 1))


def _lin(pos):
    return 4 * pos[0] + 2 * pos[1] + pos[2]


def _chip(pos):
    return 2 * pos[0] + pos[1]


def _here():
    return (lax.axis_index("x"), lax.axis_index("y"), lax.axis_index("c"))


class _Exchange:
    def __init__(self, x, kind):
        self.x, self.kind = x, kind
        self.masks = {"gather4": (2, 4, 6), "scatter4": (2, 4, 6), "gather8": tuple(range(1, 8)), "swap": (1,)}[kind]
        self.slot = {"gather4": _chip, "scatter4": _chip, "gather8": _lin, "swap": None}[kind]
        lead = {"gather4": (4,), "scatter4": (), "gather8": (8,), "swap": ()}[kind]
        self.out_shape = jax.ShapeDtypeStruct(lead + x.shape, x.dtype)
        n = len(self.masks)
        self.scratch = [pltpu.SemaphoreType.DMA((n,)), pltpu.SemaphoreType.DMA((n,))] + ([] if kind == "swap" else [pltpu.SemaphoreType.DMA])

    def _copies(self, x_ref, o_ref, send, recv, *own, arrivals):
        me = _here()
        src = (lambda pos: x_ref.at[_chip(pos)]) if self.kind == "scatter4" else (lambda pos: x_ref)
        dst = (lambda pos: o_ref.at[self.slot(pos)]) if self.slot else (lambda pos: o_ref)
        local = [pltpu.make_async_copy(src(me), dst(me), own[0])] if own else []
        outs, ins = [], []
        for j, k in enumerate(self.masks):
            peer = _flip(me, k)
            sems = dict(send_sem=send.at[j], recv_sem=recv.at[j], device_id=peer, device_id_type=MESH)
            outs.append(pltpu.make_async_remote_copy(src_ref=src(peer), dst_ref=dst(me), **sems))
            if arrivals:
                ins.append(pltpu.make_async_remote_copy(src_ref=src(me), dst_ref=dst(peer), **sems))
        return local, outs, ins

    def start(self, *refs):
        local, outs, _ = self._copies(*refs, arrivals=False)
        for cp in local + outs:
            cp.start()

    def wait(self, *refs):
        local, outs, ins = self._copies(*refs, arrivals=True)
        for cp in ins:
            cp.wait_recv()
        for cp in outs:
            cp.wait_send()
        for cp in local:
            cp.wait()


def _exchange(x, kind, name):
    ex = _Exchange(x, kind)

    def body(*refs):
        ex.start(*refs)
        ex.wait(*refs)

    return pl.pallas_call(body, name=name, in_specs=[ANY], out_specs=ANY, out_shape=ex.out_shape, scratch_shapes=ex.scratch)(x)


def _gather4_two_level(x, name):
    half = x.shape[0] // 2

    def body(x_ref, o_ref, send1, recv1, send2, recv2, own):
        me = _here()
        sibling = _flip(me, 1)
        mine, theirs = pl.ds(me[2] * half, half), pl.ds((1 - me[2]) * half, half)
        peers = [_flip(me, k) for k in (2, 4, 6)]

        def over_ici(j, src_chip, rows, src=None):
            dst = o_ref.at[_chip(src_chip), rows]
            return pltpu.make_async_remote_copy(src_ref=dst if src is None else src, dst_ref=dst, send_sem=send1.at[j],
                                                recv_sem=recv1.at[j], device_id=peers[j], device_id_type=MESH)

        def over_d2d(j, rows):
            blk = o_ref.at[_chip(peers[j]), rows]
            return pltpu.make_async_remote_copy(src_ref=blk, dst_ref=blk, send_sem=send2.at[j], recv_sem=recv2.at[j],
                                                device_id=sibling, device_id_type=MESH)

        local = pltpu.make_async_copy(x_ref, o_ref.at[_chip(me)], own)
        local.start()
        sent = [over_ici(j, me, mine, src=x_ref.at[mine]) for j in range(3)]
        for cp in sent:
            cp.start()
        passed = [over_d2d(j, mine) for j in range(3)]
        for j in range(3):
            over_ici(j, peers[j], mine).wait_recv()
            passed[j].start()
        for j in range(3):
            over_d2d(j, theirs).wait_recv()
        for cp in sent + passed:
            cp.wait_send()
        local.wait()

    sems = pltpu.SemaphoreType.DMA((3,))
    return pl.pallas_call(body, name=name, in_specs=[ANY], out_specs=ANY, out_shape=jax.ShapeDtypeStruct((4,) + x.shape, x.dtype),
                          scratch_shapes=[sems, sems, sems, sems, pltpu.SemaphoreType.DMA])(x)


def _hosted_call(body, side, *, name, grid, in_specs, out_specs, out_shape, scratch_shapes, args):
    sides = [] if side is None else list(side) if isinstance(side, (list, tuple)) else [side]
    n_in, n_out, ns = len(in_specs), len(out_specs), len(sides)
    params = _params(*(["arbitrary"] * len(grid)))
    if not sides:
        return pl.pallas_call(body, name=name, grid=grid, in_specs=in_specs, out_specs=out_specs, out_shape=out_shape,
                              scratch_shapes=scratch_shapes, compiler_params=params)(*args)
    n_sem = [len(s.scratch) for s in sides]

    def hosted(*refs):
        ins, xs = refs[:n_in], refs[n_in:n_in + ns]
        outs, os_ = refs[n_in + ns:n_in + ns + n_out], refs[n_in + ns + n_out:n_in + 2 * ns + n_out]
        rest = refs[n_in + 2 * ns + n_out:]
        scratch, sems = rest[:len(rest) - sum(n_sem)], list(rest[len(rest) - sum(n_sem):])
        per_side = [[sems.pop(0) for _ in range(k)] for k in n_sem]
        ids = [pl.program_id(a) for a in range(len(grid))]
        first, last = ids[0] == 0, ids[0] == grid[0] - 1
        for a in range(1, len(grid)):
            first, last = first & (ids[a] == 0), last & (ids[a] == grid[a] - 1)

        @pl.when(first)
        def _():
            for s, x_ref, o_ref, sm in zip(sides, xs, os_, per_side):
                s.start(x_ref, o_ref, *sm)

        body(*ins, *outs, *scratch)

        @pl.when(last)
        def _():
            for s, x_ref, o_ref, sm in zip(sides, xs, os_, per_side):
                s.wait(x_ref, o_ref, *sm)

    return pl.pallas_call(hosted, name=name + "_x_" + "_".join(s.kind for s in sides), grid=grid, in_specs=list(in_specs) + [ANY] * ns,
                          out_specs=list(out_specs) + [ANY] * ns, out_shape=list(out_shape) + [s.out_shape for s in sides],
                          scratch_shapes=list(scratch_shapes) + [sem for s in sides for sem in s.scratch],
                          compiler_params=params)(*args, *[s.x for s in sides])


def _pad_lanes(a, width):
    return jnp.pad(a, [(0, 0)] * (a.ndim - 1) + [(0, width - a.shape[-1])])


def kernel(x, c, ctx, c_ctx, w_ada, b_ada, g_pre, g_post, w_in, conv_w, conv_b, dt_bias, a_log, d_skip, g_ssd, g_v, w_s, b_s, g_mlp, w_out, loss_target, m_c_ctx, m_w_ada, m_b_ada, m_g_pre, m_g_post, m_w_in, m_conv_w, m_conv_b, m_dt_bias, m_a_log, m_d_skip, m_g_ssd, m_g_v, m_w_s, m_b_s, m_g_mlp, m_w_out, v_c_ctx, v_w_ada, v_b_ada, v_g_pre, v_g_post, v_w_in, v_conv_w, v_conv_b, v_dt_bias, v_a_log, v_d_skip, v_g_ssd, v_g_v, v_w_s, v_b_s, v_g_mlp, v_w_out):
    depth, d = g_pre.shape
    seq, ctx_len = x.shape[1], ctx.shape[1]
    heads = d // HP
    in_w = 6 * d + 2 * heads
    groups_mlp = d // LANES
    t = ctx_len + seq
    assert ctx_len == TB and seq % TB == 0 and TB % ROW == 0 and heads % 4 == 0 and heads <= LANES and d % LANES == 0
    assert w_in.shape == (depth, d, in_w // 4)

    xi, yi, ci = lax.axis_index("x"), lax.axis_index("y"), lax.axis_index("c")
    chip = 2 * xi + yi
    me = 4 * xi + 2 * yi + ci

    n_ada = 3 * d // 4
    c_all = _exchange(c, "gather8", "ag_c")[:, 0, :]
    c16 = jnp.concatenate([c_all, c_ctx[None, :], jnp.zeros((7, d), F32)], axis=0)
    b_loc = lax.dynamic_slice_in_dim(b_ada, chip * n_ada, n_ada, axis=1)[:, None, :]
    mods_loc = _exchange(_ada_fwd(c16, w_ada, b_loc).reshape(depth * 16, n_ada), "gather8", "ag_mods")
    mods_loc = mods_loc.reshape(4, 2, depth, 16, n_ada)[:, 0]
    mods_full = jnp.moveaxis(mods_loc, 0, 2).reshape(depth, 16, 3 * d)
    mods_x = lax.dynamic_index_in_dim(mods_full, me, axis=1, keepdims=False).reshape(depth, 3, d)
    mods_c = mods_full[:, 8, :].reshape(depth, 3, d)
    mods = _pad_rows8(jnp.stack([mods_c, mods_x], axis=1))

    w_in_b, w_out_b = w_in.astype(BF16), w_out.astype(BF16)

    def lay_out(w_in_rows):
        full = jnp.moveaxis(jnp.concatenate(w_in_rows, axis=1), 0, 1).reshape(d, in_w)
        w_dt_l = jnp.concatenate([_pad_lanes(full[:, 2 * d:2 * d + heads], LANES),
                                  _pad_lanes(full[:, 2 * d + heads:2 * d + 2 * heads], LANES)], axis=1)
        return full[:, :2 * d], full[:, 2 * d + 2 * heads:], w_dt_l

    w_in_rows = [_gather4_two_level(w_in_b[0], "ag_w_in")]
    r_scan = 3 * d // 4
    conv_w_full = jnp.moveaxis(_exchange(conv_w, "gather4", "ag_conv_w"), 0, 2).reshape(depth, CONV_TAPS, 2 * d)
    conv_w8 = jnp.pad(conv_w_full, ((0, 0), (0, 8 - CONV_TAPS), (0, 0)))

    tri = jnp.stack([jnp.tril(jnp.ones((Q, Q), F32)), jnp.triu(jnp.ones((Q, Q), F32))])
    tri_t = jnp.swapaxes(tri, 1, 2)
    dtb = _pad_lanes(dt_bias, LANES)[:, :, None, :]
    alog = _pad_lanes(a_log, LANES)[:, :, None, :]
    dsk_row = jnp.repeat(d_skip, HP, axis=1)[:, None, :]
    w_st = jnp.swapaxes(w_s, 2, 3)
    b_st = _pad_lanes(jnp.swapaxes(b_s, 1, 2), LANES)
    chan = jnp.arange(d)
    ind_head = (chan[:, None] // HP == jnp.arange(LANES)[None, :]).astype(F32)
    ind_b, ind_t = ind_head.astype(BF16), ind_head.T.astype(BF16)
    ind_group = (chan[:, None] // LANES == jnp.arange(LANES)[None, :]).astype(F32)

    stream = jnp.concatenate([ctx[0], x[0]], axis=0)
    saved = []
    for l in range(depth):
        w_xbc, w_mid, w_dt = lay_out(w_in_rows)
        more = l + 1 < depth
        hx, hx_t = _pre_fwd(stream, g_pre[l][None], mods[l])
        z_xbc = _mm(hx, w_xbc, NN, "in_xbc")
        z_mid, w_out_all = _mm(hx, w_mid, NN, "in_mid", out_dtype=BF16, side=_Exchange(w_out_b[l], "gather4"))
        w_o = w_out_all.reshape(2 * d, d)
        z_dt = _mm(hx, w_dt, NN, "in_dt")
        xbc, dsilu = _conv_fwd(z_xbc, conv_w8[l], conv_b[l][None], ctx_len)
        y2, hs, *rows_a = _ssd_fwd(xbc, z_dt, dtb[l], alog[l], tri, ind_t, d, ctx_len,
                                   side=_Exchange(w_in_b[l + 1, :r_scan], "gather4") if more else None)
        ycat, ycat_t = _mix_fwd(z_mid, y2, xbc, dsk_row[l], g_ssd[l][None], g_v[l][None], g_mlp[l][None], w_s[l], b_st[l])
        if more:
            o, rows_b = _mm(ycat, w_o, NN, "out_proj", side=_Exchange(w_in_b[l + 1, r_scan:], "gather4"))
            w_in_rows = [rows_a[0], rows_b]
        else:
            o = _mm(ycat, w_o, NN, "out_proj")
        saved.append((stream, hx_t, z_xbc, dsilu, z_mid, z_dt, xbc, y2, hs, ycat_t, o, w_xbc, w_mid, w_dt, w_o))
        stream = _post_fwd(o, stream, g_post[l][None], mods[l])

    sq, d_stream = _loss_grad(stream, loss_target[0])
    loss = lax.psum(0.5 / d * sq[0, 0], ("x", "y", "c"))

    small = []
    dmods = []
    q_in = in_w // 4
    names = ["g_post", "conv_w", "conv_b", "dt_bias", "a_log", "d_skip", "g_ssd", "g_v", "w_s", "b_s", "g_mlp"]

    def rows_of(a):
        return -(-a.size // (8 * LANES)) * 8

    def pack(arrays):
        blocks = [jnp.pad(a.reshape(-1), (0, rows_of(a) * LANES - a.size)).reshape(rows_of(a), LANES) for a in arrays]
        rows = sum(b.shape[0] for b in blocks)
        return jnp.pad(jnp.concatenate(blocks, axis=0), ((0, -(-rows // PACK_ROWS) * PACK_ROWS - rows), (0, 0)))

    def unpack(block, likes):
        out, row = [], 0
        for a in likes:
            out.append(block[row:row + rows_of(a)].reshape(-1)[:a.size].reshape(a.shape))
            row += rows_of(a)
        return out

    def quarter_parts(g_xbc, g_mid, g_dt):
        segs = [(g_xbc, 0, 2 * d), (g_dt, 0, heads), (g_dt, LANES, heads), (g_mid, 0, 4 * d)]
        parts = []
        for qi in range(4):
            lo, hi, off, pieces = qi * q_in, (qi + 1) * q_in, 0, []
            for arr, start, width in segs:
                a, b = max(lo, off), min(hi, off + width)
                if a < b:
                    pieces.append(arr[:, start + a - off:start + b - off])
                off += width
            parts.append(pieces[0] if len(pieces) == 1 else jnp.concatenate(pieces, axis=1))
        return jnp.stack(parts)

    sum_in, sum_out = [None] * depth, [None] * depth
    swap_in, swap_out = [None] * depth, [None] * depth
    small_sum = [None] * depth
    parts_in = packed = None
    for l in reversed(range(depth)):
        x_in, hx_t, z_xbc, dsilu, z_mid, z_dt, xbc, y2, hs, ycat_t, o, w_xbc, w_mid, w_dt, w_o = saved[l]
        up = l + 1
        d_o, acc_post = _post_bwd(d_stream, o, g_post[l][None], mods[l])
        if packed is not None:
            d_ycat, gathered = _mm(d_o, w_o, NT, "d_ycat", out_dtype=BF16, side=_Exchange(packed, "gather8"))
            small_sum[up] = _sum_lead(gathered, "sum_small")
        else:
            d_ycat = _mm(d_o, w_o, NT, "d_ycat", out_dtype=BF16)
        g_out = _mm(ycat_t, d_o, NN, "dw_out", out_dtype=BF16)
        dz_mid, d_y, vec, d_ws, d_bs = _mix_bwd(z_mid, y2, xbc, d_ycat, dsk_row[l], g_ssd[l][None], g_v[l][None], g_mlp[l][None],
                                                w_s[l], w_st[l], b_st[l], ind_head, ind_group)
        d_xbc2, dz_dt, d_bias, d_alog, *got = _ssd_bwd(xbc, z_dt, dtb[l], alog[l], tri, tri_t, ind_t, ind_b, d_y, y2, hs, d, ctx_len,
                                                       side=_Exchange(parts_in, "scatter4") if parts_in is not None else None)
        if parts_in is not None:
            sum_in[up] = _sum_lead(got[0], "sum_w_in")
        dz_xbc, d_cw, d_cb = _conv_bwd(z_xbc, dsilu, d_xbc2, d_y, dsk_row[l], conv_w8[l], ctx_len)
        g_xbc, got_out = _mm(hx_t, dz_xbc, NN, "dw_xbc", out_dtype=BF16, side=_Exchange(g_out.reshape(4, 2 * d // 4, d), "scatter4"))
        sum_out[l] = _sum_lead(got_out, "sum_w_out")
        swaps = [_Exchange(sum_out[l], "swap")] + ([_Exchange(sum_in[up], "swap")] if parts_in is not None else [])
        g_mid, swap_out[l], *swapped = _mm(hx_t, dz_mid, NN, "dw_mid", out_dtype=BF16, side=swaps)
        if parts_in is not None:
            swap_in[up] = swapped[0]
        g_dt = _mm(hx_t, dz_dt, NN, "dw_dt", out_dtype=BF16)
        parts_in = quarter_parts(g_xbc, g_mid, g_dt)
        small.append(dict(
            g_post=acc_post[0, 1] + acc_post[1, 1], conv_w=d_cw[:CONV_TAPS], conv_b=d_cb[0],
            dt_bias=d_bias[:, 0, :heads], a_log=d_alog[:, 0, :heads], d_skip=vec[3, :heads], g_ssd=vec[0], g_v=vec[1],
            w_s=d_ws, b_s=d_bs[:, :groups_mlp].T, g_mlp=vec[2]))
        packed = pack([small[-1][n] for n in names])
        if l == 0:
            r_a = max(LANES, d // 3 // LANES * LANES)
            d_hx, got_a = _mm(dz_xbc, w_xbc, NT, "dhx_xbc", side=_Exchange(parts_in[:, :r_a], "scatter4"))
            d_hx, got_b = _mm(dz_mid, w_mid, NT, "dhx_mid", acc=d_hx, side=_Exchange(parts_in[:, r_a:], "scatter4"))
            sum_in[0] = _sum_lead(jnp.concatenate([got_a, got_b], axis=1), "sum_w_in")
        else:
            d_hx = _mm(dz_xbc, w_xbc, NT, "dhx_xbc")
            d_hx = _mm(dz_mid, w_mid, NT, "dhx_mid", acc=d_hx)
        d_hx = _mm(dz_dt, w_dt, NT, "dhx_dt", acc=d_hx)
        if l == 0:
            d_stream, acc_pre, swap_in[0], gathered = _pre_bwd(
                x_in, d_hx, d_stream, g_pre[l][None], mods[l], latent_only=True,
                side=[_Exchange(sum_in[0], "swap"), _Exchange(packed, "gather8")])
            small_sum[0] = _sum_lead(gathered, "sum_small")
        else:
            d_stream, acc_pre = _pre_bwd(x_in, d_hx, d_stream, g_pre[l][None], mods[l])
        dmods.append(jnp.concatenate([acc_pre[:, 0], acc_pre[:, 1], acc_post[:, 0], acc_pre[:, 2]], axis=1))
    small.reverse(), dmods.reverse()
    grad_x = d_stream[None]

    weights = dict(c_ctx=c_ctx, w_ada=w_ada, b_ada=b_ada, g_pre=g_pre, g_post=g_post, w_in=w_in, conv_w=conv_w, conv_b=conv_b,
                   dt_bias=dt_bias, a_log=a_log, d_skip=d_skip, g_ssd=g_ssd, g_v=g_v, w_s=w_s, b_s=b_s, g_mlp=g_mlp, w_out=w_out)
    m_in = dict(c_ctx=m_c_ctx, w_ada=m_w_ada, b_ada=m_b_ada, g_pre=m_g_pre, g_post=m_g_post, w_in=m_w_in, conv_w=m_conv_w,
                conv_b=m_conv_b, dt_bias=m_dt_bias, a_log=m_a_log, d_skip=m_d_skip, g_ssd=m_g_ssd, g_v=m_g_v, w_s=m_w_s,
                b_s=m_b_s, g_mlp=m_g_mlp, w_out=m_w_out)
    v_in = dict(c_ctx=v_c_ctx, w_ada=v_w_ada, b_ada=v_b_ada, g_pre=v_g_pre, g_post=v_g_post, w_in=v_w_in, conv_w=v_conv_w,
                conv_b=v_conv_b, dt_bias=v_dt_bias, a_log=v_a_log, d_skip=v_d_skip, g_ssd=v_g_ssd, g_v=v_g_v, w_s=v_w_s,
                b_s=v_b_s, g_mlp=v_g_mlp, w_out=v_w_out)
    order = list(weights)
    results = {}

    def adamw_big(n, ga, gb):
        shp = weights[n].shape
        two = lambda a: a.reshape(-1, shp[-1])
        results[n] = [r.reshape(shp) for r in _adamw(two(weights[n]), ga, two(m_in[n]), two(v_in[n]), "adamw_" + n, g2=gb)]

    adamw_big("w_in", jnp.concatenate(sum_in, axis=0), jnp.concatenate(swap_in, axis=0))
    adamw_big("w_out", jnp.concatenate(sum_out, axis=0), jnp.concatenate(swap_out, axis=0))
    per_layer = [unpack(small_sum[l], [small[l][n] for n in names]) for l in range(depth)]
    grads = {n: jnp.stack([per_layer[l][j] for l in range(depth)]) for j, n in enumerate(names)}
    grads["conv_w"] = lax.dynamic_slice_in_dim(grads["conv_w"], chip * (2 * d // 4), 2 * d // 4, axis=2)

    dm_all = _exchange(jnp.stack(dmods).reshape(depth * 2, 4 * d), "gather8", "ag_dmods").reshape(8, depth, 2, 4 * d)
    grads["g_pre"] = _sum_lead(jnp.moveaxis(dm_all[..., 3 * d:], 2, 1).reshape(16, depth, d), "sum_g_pre")
    dm_ctx = _sum_lead(dm_all[:, :, 0, :3 * d], "sum_dm_ctx")
    dm16 = jnp.concatenate([jnp.moveaxis(dm_all[:, :, 1, :3 * d], 0, 1), dm_ctx[:, None, :], jnp.zeros((depth, 7, 3 * d), F32)], axis=1)
    grads["b_ada"] = _rowsum(dm16)[:, 0, :]
    dm_loc = jnp.pad(lax.dynamic_slice_in_dim(dm16, chip * n_ada, n_ada, axis=2), ((0, 0), (0, LANES - 16), (0, 0)))
    c_t = jnp.pad(c16.T, ((0, 0), (0, LANES - 16)))
    g_w_ada, d_scc_part = _ada_bwd(c_t, dm_loc, w_ada)
    adamw_big("w_ada", g_w_ada.reshape(depth * d, n_ada), None)
    d_scc = _sum_lead(_exchange(d_scc_part, "gather8", "ag_dscc").reshape(4, 2, 8, d)[:, 0], "sum_dscc")
    grads["c_ctx"] = _cctx_grad(d_scc[0:1], c_ctx[None])[0]

    rest = [n for n in order if n not in results]
    outs = _adamw(pack([weights[n] for n in rest]), pack([grads[n] for n in rest]), pack([m_in[n] for n in rest]),
                  pack([v_in[n] for n in rest]), "adamw_small")
    for j, res in enumerate(zip(*[unpack(o_, [weights[n] for n in rest]) for o_ in outs])):
        results[rest[j]] = list(res)

    return (loss, grad_x, *[results[n][0] for n in order], *[results[n][1] for n in order],
            *[results[n][2] for n in order], *[results[n][3] for n in order])


def _pad_rows8(a):
    return jnp.pad(a, [(0, 0)] * (a.ndim - 2) + [(0, 8 - a.shape[-2]), (0, 0)])
```

```python
import jax
import jax.numpy as jnp
from jax import lax
from jax.experimental import pallas as pl
from jax.experimental.pallas import tpu as pltpu

F32 = jnp.float32
BF16 = jnp.bfloat16
EPS = 1e-6
Q = 128
TB = 256
ROW = 64
HP = 64
LANES = 128
CONV_TAPS = 5
VMEM_LIMIT = 48 * 1024 * 1024
HI = lax.Precision.HIGHEST
SUM_BLOCK_BYTES = 4 * 1024 * 1024
ADAM_BLOCK_BYTES = 1024 * 1024
PACK_ROWS = 256
MESH = pl.DeviceIdType.MESH
ANY = pl.BlockSpec(memory_space=pl.ANY)

ADAM_LR, ADAM_B1, ADAM_B2, ADAM_EPS, ADAM_WD, ADAM_STEP = 0.001, 0.9, 0.999, 1e-08, 0.01, 10

NN = (((1,), (0,)), ((), ()))
NT = (((1,), (1,)), ((), ()))
TN = (((0,), (0,)), ((), ()))


def _dot(a, b, dims=NN, prec=None):
    return lax.dot_general(a, b, dims, precision=prec, preferred_element_type=F32)


def _params(*sem):
    if sem:
        return pltpu.CompilerParams(vmem_limit_bytes=VMEM_LIMIT, dimension_semantics=sem)
    return pltpu.CompilerParams(vmem_limit_bytes=VMEM_LIMIT)


def _tile(dim, cands):
    for t in cands:
        if dim % t == 0:
            return t
    return dim


def _sigmoid(x):
    return 1.0 / (1.0 + jnp.exp(-x))


def _softplus(x):
    e = jnp.exp(-jnp.abs(x))
    u = 1.0 + e
    um1 = u - 1.0
    l1p = jnp.where(um1 == 0.0, e, jnp.log(u) * (e / jnp.where(um1 == 0.0, 1.0, um1)))
    return jnp.maximum(x, 0.0) + l1p


def _rms(x):
    return lax.rsqrt(jnp.mean(x * x, axis=-1, keepdims=True) + EPS)


def _rms_bwd(x, r, t):
    return r * t - x * (r * r * r) * jnp.mean(x * t, axis=-1, keepdims=True)


def _mm(a, b, dims, name, acc=None, out_dtype=F32, side=None):
    (ca,), (cb,) = dims[0]
    m, k = a.shape[1 - ca], a.shape[ca]
    n = b.shape[1 - cb]
    tk = k if k <= 2048 else _tile(k, (2048, 768, 512, 384, 256, 128))
    nk = k // tk
    tm = _tile(m, (2048, 1024, 768, 512, 384, 256, 128) if nk > 1 else (1024, 768, 512, 384, 256, 128))
    tn = _tile(n, (1024, 512, 256, 128))
    a_spec = pl.BlockSpec((tm, tk), lambda i, j, kk: (i, kk)) if ca == 1 else pl.BlockSpec((tk, tm), lambda i, j, kk: (kk, i))
    b_spec = pl.BlockSpec((tk, tn), lambda i, j, kk: (kk, j)) if cb == 0 else pl.BlockSpec((tn, tk), lambda i, j, kk: (j, kk))
    o_spec = pl.BlockSpec((tm, tn), lambda i, j, kk: (i, j))
    has_acc = acc is not None

    def body(*refs):
        if has_acc:
            a_ref, b_ref, c_ref, o_ref, acc_ref = refs
        else:
            a_ref, b_ref, o_ref, acc_ref = refs
        kk = pl.program_id(2)

        @pl.when(kk == 0)
        def _():
            acc_ref[...] = c_ref[...] if has_acc else jnp.zeros_like(acc_ref)

        acc_ref[...] += _dot(a_ref[...], b_ref[...], dims)

        @pl.when(kk == nk - 1)
        def _():
            o_ref[...] = acc_ref[...].astype(out_dtype)

    res = _hosted_call(
        body, side, name=name, grid=(m // tm, n // tn, nk),
        in_specs=[a_spec, b_spec] + ([o_spec] if has_acc else []),
        out_specs=[o_spec], out_shape=[jax.ShapeDtypeStruct((m, n), out_dtype)],
        scratch_shapes=[pltpu.VMEM((tm, tn), F32)], args=(a, b, acc) if has_acc else (a, b))
    return res[0] if side is None else res


def _which(i):
    return jnp.minimum(i, 1)


def _pre_fwd(x, g_pre, mods):
    t, d = x.shape

    def body(x_ref, g_ref, m_ref, o_ref, ot_ref):
        xb = x_ref[...]
        xn = xb * _rms(xb) * g_ref[...]
        hx = xn * (1.0 + m_ref[0, 1:2, :]) + m_ref[0, 0:1, :]
        o_ref[...] = hx.astype(BF16)
        ot_ref[...] = hx.T.astype(BF16)

    return pl.pallas_call(
        body, name="pre_fwd", grid=(t // TB,),
        in_specs=[pl.BlockSpec((TB, d), lambda i: (i, 0)), pl.BlockSpec((1, d), lambda i: (0, 0)),
                  pl.BlockSpec((1, 8, d), lambda i: (_which(i), 0, 0))],
        out_specs=[pl.BlockSpec((TB, d), lambda i: (i, 0)), pl.BlockSpec((d, TB), lambda i: (0, i))],
        out_shape=[jax.ShapeDtypeStruct((t, d), BF16), jax.ShapeDtypeStruct((d, t), BF16)], compiler_params=_params("arbitrary"),
    )(x, g_pre, mods)


def _pre_bwd(x, d_hx, d_up, g_pre, mods, side=None, latent_only=False):
    t, d = x.shape
    dx_rows = t - TB if latent_only else t
    dx_spec = pl.BlockSpec((TB, d), (lambda i: (jnp.maximum(i - 1, 0), 0)) if latent_only else (lambda i: (i, 0)))

    def body(x_ref, dh_ref, du_ref, g_ref, m_ref, dx_ref, acc_ref):
        i = pl.program_id(0)

        @pl.when(i <= 1)
        def _():
            acc_ref[...] = jnp.zeros_like(acc_ref)

        xb = x_ref[...]
        dh = dh_ref[...]
        r = _rms(xb)
        xr = xb * r
        d_xn = dh * (1.0 + m_ref[0, 1:2, :])
        dx_ref[...] = du_ref[...] + _rms_bwd(xb, r, d_xn * g_ref[...])
        acc_ref[0, 0:1, :] += jnp.sum(dh, axis=0, keepdims=True)
        acc_ref[0, 1:2, :] += jnp.sum(dh * (xr * g_ref[...]), axis=0, keepdims=True)
        acc_ref[0, 2:3, :] += jnp.sum(d_xn * xr, axis=0, keepdims=True)

    blk = pl.BlockSpec((TB, d), lambda i: (i, 0))
    return _hosted_call(
        body, side, name="pre_bwd", grid=(t // TB,),
        in_specs=[blk, blk, blk, pl.BlockSpec((1, d), lambda i: (0, 0)),
                  pl.BlockSpec((1, 8, d), lambda i: (_which(i), 0, 0))],
        out_specs=[dx_spec, pl.BlockSpec((1, 8, d), lambda i: (_which(i), 0, 0))],
        out_shape=[jax.ShapeDtypeStruct((dx_rows, d), F32), jax.ShapeDtypeStruct((2, 8, d), F32)],
        scratch_shapes=[], args=(x, d_hx, d_up, g_pre, mods))


def _post_fwd(o, x, g_post, mods):
    t, d = x.shape

    def body(o_ref, x_ref, g_ref, m_ref, y_ref):
        ob = o_ref[...]
        y_ref[...] = x_ref[...] + m_ref[0, 2:3, :] * (ob * _rms(ob) * g_ref[...])

    blk = pl.BlockSpec((TB, d), lambda i: (i, 0))
    return pl.pallas_call(
        body, name="post_fwd", grid=(t // TB,),
        in_specs=[blk, blk, pl.BlockSpec((1, d), lambda i: (0, 0)), pl.BlockSpec((1, 8, d), lambda i: (_which(i), 0, 0))],
        out_specs=blk, out_shape=jax.ShapeDtypeStruct((t, d), F32), compiler_params=_params("arbitrary"),
    )(o, x, g_post, mods)


def _post_bwd(d_y, o, g_post, mods):
    t, d = o.shape

    def body(dy_ref, o_ref, g_ref, m_ref, do_ref, acc_ref):
        i = pl.program_id(0)

        @pl.when(i <= 1)
        def _():
            acc_ref[...] = jnp.zeros_like(acc_ref)

        ob = o_ref[...]
        dy = dy_ref[...]
        r = _rms(ob)
        orr = ob * r
        d_out = dy * m_ref[0, 2:3, :]
        do_ref[...] = _rms_bwd(ob, r, d_out * g_ref[...]).astype(BF16)
        acc_ref[0, 0:1, :] += jnp.sum(dy * (orr * g_ref[...]), axis=0, keepdims=True)
        acc_ref[0, 1:2, :] += jnp.sum(d_out * orr, axis=0, keepdims=True)

    blk = pl.BlockSpec((TB, d), lambda i: (i, 0))
    return pl.pallas_call(
        body, name="post_bwd", grid=(t // TB,),
        in_specs=[blk, blk, pl.BlockSpec((1, d), lambda i: (0, 0)), pl.BlockSpec((1, 8, d), lambda i: (_which(i), 0, 0))],
        out_specs=[blk, pl.BlockSpec((1, 8, d), lambda i: (_which(i), 0, 0))],
        out_shape=[jax.ShapeDtypeStruct((t, d), BF16), jax.ShapeDtypeStruct((2, 8, d), F32)],
        compiler_params=_params("arbitrary"),
    )(d_y, o, g_post, mods)


def _loss_grad(xf, target):
    t, d = xf.shape

    def body(x_ref, t_ref, loss_ref, dx_ref):
        i = pl.program_id(0)

        @pl.when(i == 0)
        def _():
            loss_ref[...] = jnp.zeros_like(loss_ref)
            dx_ref[...] = jnp.zeros_like(dx_ref)

        @pl.when(i > 0)
        def _():
            err = x_ref[...] - t_ref[...]
            loss_ref[...] += jnp.sum(err * err).reshape(1, 1)
            dx_ref[...] = err * (1.0 / d)

    return pl.pallas_call(
        body, name="loss_grad", grid=(t // TB,),
        in_specs=[pl.BlockSpec((TB, d), lambda i: (i, 0)), pl.BlockSpec((TB, d), lambda i: (jnp.maximum(i - 1, 0), 0))],
        out_specs=[pl.BlockSpec((1, 1), lambda i: (0, 0)), pl.BlockSpec((TB, d), lambda i: (i, 0))],
        out_shape=[jax.ShapeDtypeStruct((1, 1), F32), jax.ShapeDtypeStruct((t, d), F32)],
        compiler_params=_params("arbitrary"),
    )(xf, target)


def _conv_terms(zb, pos, row_len):
    out = []
    for k in range(CONV_TAPS):
        o = k - CONV_TAPS // 2
        sh = zb if o == 0 else pltpu.roll(zb, (-o) % TB, 0)
        out.append(jnp.where((pos + o >= 0) & (pos + o < row_len), sh, 0.0))
    return out


def _row_pos(i, ctx_len):
    row_len = jnp.where(i == 0, ctx_len, ROW)
    pos = lax.broadcasted_iota(jnp.int32, (TB, 1), 0) & (row_len - 1)
    return pos, row_len


def _conv_fwd(z_xbc, conv_w8, conv_b, ctx_len):
    t, c = z_xbc.shape
    tc = _tile(c, (2048, 1024, 512, 256, 128))

    def body(z_ref, w_ref, b_ref, o_ref, ds_ref):
        pos, row_len = _row_pos(pl.program_id(1), ctx_len)
        terms = _conv_terms(z_ref[...], pos, row_len)
        pre = b_ref[...]
        for k in range(CONV_TAPS):
            pre = pre + terms[k] * w_ref[k:k + 1, :]
        sig = _sigmoid(pre)
        o_ref[...] = pre * sig
        ds_ref[...] = sig * (1.0 + pre * (1.0 - sig))

    blk = pl.BlockSpec((TB, tc), lambda j, i: (i, j))
    return pl.pallas_call(
        body, name="conv_fwd", grid=(c // tc, t // TB),
        in_specs=[blk, pl.BlockSpec((8, tc), lambda j, i: (0, j)), pl.BlockSpec((1, tc), lambda j, i: (0, j))],
        out_specs=[blk, blk], out_shape=[jax.ShapeDtypeStruct((t, c), F32)] * 2, compiler_params=_params("arbitrary", "arbitrary"),
    )(z_xbc, conv_w8, conv_b)


def _conv_bwd(z_xbc, dsilu, d_xbc2, d_y, d_skip_row, conv_w8, ctx_len):
    t, c = z_xbc.shape
    d = d_y.shape[1]
    tc = _tile(d, (2048, 1024, 512, 256, 128))
    nskip = d // tc

    def body(z_ref, dsl_ref, g2_ref, dy_ref, ds_ref, w_ref, dz_ref, dw_ref, db_ref):
        j, i = pl.program_id(0), pl.program_id(1)

        @pl.when(i == 0)
        def _():
            dw_ref[...] = jnp.zeros_like(dw_ref)
            db_ref[...] = jnp.zeros_like(db_ref)

        pos, row_len = _row_pos(i, ctx_len)
        skip = jnp.where(j < nskip, 1.0, 0.0) * ds_ref[...]
        d_pre = (g2_ref[0] + g2_ref[1] + dy_ref[...] * skip) * dsl_ref[...]
        db_ref[...] += jnp.sum(d_pre, axis=0, keepdims=True)
        zb = z_ref[...]
        dz = jnp.zeros_like(d_pre)
        for k in range(CONV_TAPS):
            o = k - CONV_TAPS // 2
            sh = d_pre if o == 0 else pltpu.roll(d_pre, o % TB, 0)
            sh = jnp.where((pos - o >= 0) & (pos - o < row_len), sh, 0.0)
            dw_ref[k:k + 1, :] += jnp.sum(sh * zb, axis=0, keepdims=True)
            dz = dz + sh * w_ref[k:k + 1, :]
        dz_ref[...] = dz.astype(BF16)

    jd = lambda j: jnp.minimum(j, nskip - 1)
    blk = pl.BlockSpec((TB, tc), lambda j, i: (i, j))
    return pl.pallas_call(
        body, name="conv_bwd", grid=(c // tc, t // TB),
        in_specs=[blk, blk, pl.BlockSpec((2, TB, tc), lambda j, i: (0, i, j)),
                  pl.BlockSpec((TB, tc), lambda j, i: (i, jd(j))), pl.BlockSpec((1, tc), lambda j, i: (0, jd(j))),
                  pl.BlockSpec((8, tc), lambda j, i: (0, j))],
        out_specs=[blk, pl.BlockSpec((8, tc), lambda j, i: (0, j)), pl.BlockSpec((1, tc), lambda j, i: (0, j))],
        out_shape=[jax.ShapeDtypeStruct((t, c), BF16), jax.ShapeDtypeStruct((8, c), F32), jax.ShapeDtypeStruct((1, c), F32)],
        compiler_params=_params("arbitrary", "arbitrary"),
    )(z_xbc, dsilu, d_xbc2, d_y, d_skip_row, conv_w8)


def _scan_chunk(dirn, s, nch, ncc):
    bwd = jnp.where(s < ncc, ncc - 1 - s, nch - 1 - (s - ncc))
    return jnp.where(dirn == 0, s, bwd)


def _ssd_decays(dt_ref, dtb_ref, alog_ref, tri):
    raw = dt_ref[...] + dtb_ref[0]
    dt = _softplus(raw)
    a_neg = -jnp.exp(alog_ref[0])
    a = dt * a_neg
    s = _dot(tri, a, NN, HI)
    stot = jnp.sum(a, axis=0, keepdims=True)
    return raw, dt, a_neg, s, stot, s.T


def _split(v):
    hi = v.astype(BF16)
    return hi, (v - hi.astype(F32)).astype(BF16)


def _expand(v, indt_ref):
    hi, lo = _split(v)
    return _dot(hi, indt_ref[...]) + _dot(lo, indt_ref[...])


def _head_sums(v, ind_ref):
    hi, lo = _split(v)
    return _dot(hi, ind_ref[...]) + _dot(lo, ind_ref[...])


def _ssd_fwd(xbc, z_dt, dtb, alog, tri, ind_t, d, ctx_len, side=None):
    t = xbc.shape[0]
    nch, ncc = t // Q, ctx_len // Q
    heads = d // HP
    groups = heads // 4
    gn = groups * LANES

    def body(xbc_ref, dt_ref, dtb_ref, alog_ref, tri_ref, indt_ref, y_ref, hs_ref, h_scr, xdb_scr, xde_scr, esx_scr):
        @pl.when(pl.program_id(1) == 0)
        def _():
            h_scr[...] = jnp.zeros_like(h_scr)

        tri = tri_ref[0]
        mask = tri > 0.0
        _, dt, _, s, stot, s_t = _ssd_decays(dt_ref, dtb_ref, alog_ref, tri)
        esx_scr[...] = _expand(jnp.exp(s), indt_ref)
        etot_x = _expand(jnp.broadcast_to(jnp.exp(stot), (8, LANES)), indt_ref)[0:1]
        xd = xbc_ref[:, :d] * _expand(dt, indt_ref)
        xdb_scr[...] = xd.astype(BF16)
        xde_scr[...] = (xd * _expand(jnp.exp(stot - s), indt_ref)).astype(BF16)
        left = lax.broadcasted_iota(jnp.int32, (Q, LANES), 1) < HP
        hs_ref[0, 0] = h_scr[...]
        for g in range(groups):
            b32 = xbc_ref[:, d + g * LANES:d + (g + 1) * LANES]
            bb = b32.astype(BF16)
            bbt = b32.T.astype(BF16)
            cb = xbc_ref[:, d + gn + g * LANES:d + gn + (g + 1) * LANES].astype(BF16)
            cbt = _dot(cb, bb, NT)
            gcols = slice(4 * g * HP, 4 * (g + 1) * HP)
            hg = h_scr[:, gcols]
            y_off = _dot(cb, hg.astype(BF16)) * esx_scr[:, gcols]
            h_scr[:, gcols] = hg * etot_x[:, gcols] + _dot(bbt, xde_scr[:, gcols])
            for j, pr in enumerate((2 * g, 2 * g + 1)):
                h0 = 2 * pr
                cols = slice(pr * LANES, (pr + 1) * LANES)
                xdb = xdb_scr[:, cols]
                res = []
                for h in (h0, h0 + 1):
                    lm = jnp.exp(jnp.where(mask, s[:, h:h + 1] - s_t[h:h + 1, :], -jnp.inf))
                    res.append(_dot((cbt * lm).astype(BF16), xdb))
                y_ref[0, :, cols] = jnp.where(left, res[0], res[1]) + y_off[:, j * LANES:(j + 1) * LANES]

    cidx = lambda dd, ss: _scan_chunk(dd, ss, nch, ncc)
    return _hosted_call(
        body, side, name="ssd_fwd", grid=(2, nch),
        in_specs=[pl.BlockSpec((Q, 2 * d), lambda dd, ss: (cidx(dd, ss), 0)),
                  pl.BlockSpec((Q, LANES), lambda dd, ss: (cidx(dd, ss), dd)),
                  pl.BlockSpec((1, 1, LANES), lambda dd, ss: (dd, 0, 0)),
                  pl.BlockSpec((1, 1, LANES), lambda dd, ss: (dd, 0, 0)),
                  pl.BlockSpec((1, Q, Q), lambda dd, ss: (dd, 0, 0)),
                  pl.BlockSpec((LANES, d), lambda dd, ss: (0, 0))],
        out_specs=[pl.BlockSpec((1, Q, d), lambda dd, ss: (dd, cidx(dd, ss), 0)),
                   pl.BlockSpec((1, 1, LANES, d), lambda dd, ss: (dd, cidx(dd, ss), 0, 0))],
        out_shape=[jax.ShapeDtypeStruct((2, t, d), F32), jax.ShapeDtypeStruct((2, nch, LANES, d), F32)],
        scratch_shapes=[pltpu.VMEM((LANES, d), F32), pltpu.VMEM((Q, d), BF16), pltpu.VMEM((Q, d), BF16), pltpu.VMEM((Q, d), F32)],
        args=(xbc, z_dt, dtb, alog, tri, ind_t))


def _ssd_bwd(xbc, z_dt, dtb, alog, tri, tri_t, ind_t, ind, d_y, y2, hs, d, ctx_len, side=None):
    t = xbc.shape[0]
    nch, ncc = t // Q, ctx_len // Q
    heads = d // HP
    groups = heads // 4
    gn = groups * LANES

    def body(xbc_ref, dt_ref, dtb_ref, alog_ref, tri_ref, trit_ref, indt_ref, ind_ref, dy_ref, y_ref, hs_ref,
             dx_ref, dzdt_ref, dbias_ref, dalog_ref, dh_scr, dtx_scr, ex_scr, xdb_scr, xde_scr, dyb_scr, dye_scr, dxd_scr, bdh_scr):
        @pl.when(pl.program_id(1) == 0)
        def _():
            dh_scr[...] = jnp.zeros_like(dh_scr)
            dbias_ref[...] = jnp.zeros_like(dbias_ref)
            dalog_ref[...] = jnp.zeros_like(dalog_ref)

        tri = tri_ref[0]
        mask = tri > 0.0
        mask_t = trit_ref[0] > 0.0
        raw, dt, a_neg, s, stot, s_t = _ssd_decays(dt_ref, dtb_ref, alog_ref, tri)
        etot = jnp.exp(stot)
        etot_x = _expand(jnp.broadcast_to(etot, (8, LANES)), indt_ref)[0:1]
        dtx_scr[...] = _expand(dt, indt_ref)
        ex_scr[...] = _expand(jnp.exp(stot - s), indt_ref)
        xd = xbc_ref[:, :d] * dtx_scr[...]
        xdb_scr[...] = xd.astype(BF16)
        xde_scr[...] = (xd * ex_scr[...]).astype(BF16)
        dyb_scr[...] = dy_ref[...].astype(BF16)
        dye_scr[...] = (dy_ref[...] * _dot(jnp.exp(s).astype(BF16), indt_ref[...])).astype(BF16)
        hd_cols = jnp.sum(dh_scr[...] * hs_ref[0, 0], axis=0, keepdims=True)
        left = lax.broadcasted_iota(jnp.int32, (Q, LANES), 1) < HP
        for g in range(groups):
            b32 = xbc_ref[:, d + g * LANES:d + (g + 1) * LANES]
            c32 = xbc_ref[:, d + gn + g * LANES:d + gn + (g + 1) * LANES]
            bb, cb = b32.astype(BF16), c32.astype(BF16)
            c_t = c32.T.astype(BF16)
            cbt = _dot(cb, bb, NT)
            cbt_t = _dot(bb, cb, NT)
            d_cbt = jnp.zeros((Q, Q), F32)
            gcols = slice(4 * g * HP, 4 * (g + 1) * HP)
            dyeb = dye_scr[:, gcols]
            dhg = dh_scr[:, gcols]
            dhb = dhg.astype(BF16)
            bdh_scr[:, gcols] = _dot(bb, dhb)
            d_c = _dot(dyeb, hs_ref[0, 0, :, gcols].astype(BF16), NT)
            d_b = _dot(xde_scr[:, gcols], dhb, NT)
            dh_scr[:, gcols] = dhg * etot_x[:, gcols] + _dot(c_t, dyeb)
            for pr in (2 * g, 2 * g + 1):
                h0 = 2 * pr
                cols = slice(pr * LANES, (pr + 1) * LANES)
                xdb = xdb_scr[:, cols]
                dyb = dyb_scr[:, cols]
                parts = []
                for hh, h in enumerate((h0, h0 + 1)):
                    mine = left if hh == 0 else jnp.logical_not(left)
                    diff = s[:, h:h + 1] - s_t[h:h + 1, :]
                    lm = jnp.exp(jnp.where(mask, diff, -jnp.inf))
                    lm_t = jnp.exp(jnp.where(mask_t, -diff, -jnp.inf))
                    gm = _dot(jnp.where(mine, dyb, jnp.zeros_like(dyb)), xdb, NT)
                    d_cbt = d_cbt + gm * lm
                    parts.append(_dot((cbt_t * lm_t).astype(BF16), dyb))
                dxd_scr[:, cols] = jnp.where(left, parts[0], parts[1])
            dx_ref[0, :, d + g * LANES:d + (g + 1) * LANES] = d_b + _dot(d_cbt.T.astype(BF16), cb)
            dx_ref[0, :, d + gn + g * LANES:d + gn + (g + 1) * LANES] = d_c + _dot(d_cbt.astype(BF16), bb)
        x = xbc_ref[:, :d]
        ebdh = ex_scr[...] * bdh_scr[...]
        d_xd = dxd_scr[...] + ebdh
        dx_ref[0, :, :d] = d_xd * dtx_scr[...]
        xe = x * dtx_scr[...] * ebdh
        d_s = _head_sums(dyb_scr[...].astype(F32) * y_ref[0] - xdb_scr[...].astype(F32) * dxd_scr[...] - xe, ind_ref)
        r_dx = _head_sums(d_xd * x, ind_ref)
        row8 = lax.broadcasted_iota(jnp.int32, (8, d), 0)
        tot = _head_sums(jnp.where(row8 == 0, jnp.sum(xe, axis=0, keepdims=True), jnp.where(row8 == 1, hd_cols, 0.0)), ind_ref)
        d_stot = tot[0:1] + etot * tot[1:2]
        d_a = _dot(trit_ref[0], d_s, NN, HI) + d_stot
        valid = lax.broadcasted_iota(jnp.int32, (Q, LANES), 1) < heads
        d_dt_tot = jnp.where(valid, d_a * a_neg + r_dx, 0.0)
        d_raw = d_dt_tot * _sigmoid(raw)
        dzdt_ref[...] = d_raw.astype(BF16)
        dbias_ref[0] += jnp.sum(d_raw, axis=0, keepdims=True)
        dalog_ref[0] += jnp.sum(jnp.where(valid, d_a * dt, 0.0), axis=0, keepdims=True) * a_neg

    cidx = lambda dd, ss: _scan_chunk(dd, nch - 1 - ss, nch, ncc)
    full = lambda shape: pltpu.VMEM(shape, F32)
    half = lambda shape: pltpu.VMEM(shape, BF16)
    return _hosted_call(
        body, side, name="ssd_bwd", grid=(2, nch),
        in_specs=[pl.BlockSpec((Q, 2 * d), lambda dd, ss: (cidx(dd, ss), 0)),
                  pl.BlockSpec((Q, LANES), lambda dd, ss: (cidx(dd, ss), dd)),
                  pl.BlockSpec((1, 1, LANES), lambda dd, ss: (dd, 0, 0)),
                  pl.BlockSpec((1, 1, LANES), lambda dd, ss: (dd, 0, 0)),
                  pl.BlockSpec((1, Q, Q), lambda dd, ss: (dd, 0, 0)),
                  pl.BlockSpec((1, Q, Q), lambda dd, ss: (dd, 0, 0)),
                  pl.BlockSpec((LANES, d), lambda dd, ss: (0, 0)),
                  pl.BlockSpec((d, LANES), lambda dd, ss: (0, 0)),
                  pl.BlockSpec((Q, d), lambda dd, ss: (cidx(dd, ss), 0)),
                  pl.BlockSpec((1, Q, d), lambda dd, ss: (dd, cidx(dd, ss), 0)),
                  pl.BlockSpec((1, 1, LANES, d), lambda dd, ss: (dd, cidx(dd, ss), 0, 0))],
        out_specs=[pl.BlockSpec((1, Q, 2 * d), lambda dd, ss: (dd, cidx(dd, ss), 0)),
                   pl.BlockSpec((Q, LANES), lambda dd, ss: (cidx(dd, ss), dd)),
                   pl.BlockSpec((1, 1, LANES), lambda dd, ss: (dd, 0, 0)),
                   pl.BlockSpec((1, 1, LANES), lambda dd, ss: (dd, 0, 0))],
        out_shape=[jax.ShapeDtypeStruct((2, t, 2 * d), F32), jax.ShapeDtypeStruct((t, 2 * LANES), BF16),
                   jax.ShapeDtypeStruct((2, 1, LANES), F32), jax.ShapeDtypeStruct((2, 1, LANES), F32)],
        scratch_shapes=[full((LANES, d)), full((Q, d)), full((Q, d)), half((Q, d)), half((Q, d)), half((Q, d)), half((Q, d)),
                        full((Q, d)), full((Q, d))],
        args=(xbc, z_dt, dtb, alog, tri, tri_t, ind_t, ind, d_y, y2, hs))


def _mix_common(zm_ref, y2_ref, xh_ref, dsk_ref, gv_ref, ws_ref, bst_ref, d):
    groups = d // LANES
    z_ssd, u, v, z_mlp = (zm_ref[:, k * d:(k + 1) * d].astype(F32) for k in range(4))
    y = y2_ref[0] + y2_ref[1] + dsk_ref[...] * xh_ref[...]
    sig_a = _sigmoid(z_ssd)
    ya_pre = y * (z_ssd * sig_a)
    r_v = _rms(v)
    vn = (v * r_v * gv_ref[...]).astype(BF16)
    sg = jnp.concatenate(
        [_dot(ws_ref[g].astype(BF16), vn[:, g * LANES:(g + 1) * LANES]) + bst_ref[:, g:g + 1] for g in range(groups)], axis=1)
    sig_m = _sigmoid(z_mlp)
    yb_pre = u * sg * (z_mlp * sig_m)
    return z_ssd, u, v, z_mlp, y, sig_a, ya_pre, r_v, vn, sg, sig_m, yb_pre


def _mix_fwd(z_mid, y2, xbc, dsk_row, g_ssd, g_v, g_mlp, w_s, b_st):
    t = z_mid.shape[0]
    d = z_mid.shape[1] // 4
    groups = d // LANES

    def body(zm_ref, y2_ref, xh_ref, dsk_ref, ga_ref, gv_ref, gm_ref, ws_ref, bst_ref, o_ref, ot_ref):
        (_, _, _, _, _, _, ya_pre, _, _, _, _, yb_pre) = _mix_common(zm_ref, y2_ref, xh_ref, dsk_ref, gv_ref, ws_ref, bst_ref, d)
        y_a = ya_pre * _rms(ya_pre) * ga_ref[...]
        y_b = yb_pre * _rms(yb_pre) * gm_ref[...]
        o_ref[:, :d] = y_a.astype(BF16)
        o_ref[:, d:] = y_b.astype(BF16)
        ot_ref[:d, :] = y_a.T.astype(BF16)
        ot_ref[d:, :] = y_b.T.astype(BF16)

    row = pl.BlockSpec((1, d), lambda i: (0, 0))
    return pl.pallas_call(
        body, name="mix_fwd", grid=(t // Q,),
        in_specs=[pl.BlockSpec((Q, 4 * d), lambda i: (i, 0)), pl.BlockSpec((2, Q, d), lambda i: (0, i, 0)),
                  pl.BlockSpec((Q, d), lambda i: (i, 0)), row, row, row, row,
                  pl.BlockSpec((groups, Q, Q), lambda i: (0, 0, 0)), pl.BlockSpec((Q, LANES), lambda i: (0, 0))],
        out_specs=[pl.BlockSpec((Q, 2 * d), lambda i: (i, 0)), pl.BlockSpec((2 * d, Q), lambda i: (0, i))],
        out_shape=[jax.ShapeDtypeStruct((t, 2 * d), BF16), jax.ShapeDtypeStruct((2 * d, t), BF16)], compiler_params=_params("arbitrary"),
    )(z_mid, y2, xbc, dsk_row, g_ssd, g_v, g_mlp, w_s, b_st)


def _mix_bwd(z_mid, y2, xbc, d_ycat, dsk_row, g_ssd, g_v, g_mlp, w_s, w_st, b_st, ind_head, ind_group):
    t = z_mid.shape[0]
    d = z_mid.shape[1] // 4
    groups = d // LANES
    nsteps = t // Q

    def body(zm_ref, y2_ref, xh_ref, dyc_ref, dsk_ref, ga_ref, gv_ref, gm_ref, ws_ref, wst_ref, bst_ref, ih_ref, ig_ref,
             dzm_ref, dy_ref, vec_ref, dws_ref, dbs_ref, dsk_acc, dsg_acc):
        i = pl.program_id(0)

        @pl.when(i == 0)
        def _():
            vec_ref[...] = jnp.zeros_like(vec_ref)
            dws_ref[...] = jnp.zeros_like(dws_ref)
            dsk_acc[...] = jnp.zeros_like(dsk_acc)
            dsg_acc[...] = jnp.zeros_like(dsg_acc)

        (z_ssd, u, v, z_mlp, y, sig_a, ya_pre, r_v, vn, sg, sig_m, yb_pre) = _mix_common(
            zm_ref, y2_ref, xh_ref, dsk_ref, gv_ref, ws_ref, bst_ref, d)
        d_ya = dyc_ref[:, :d].astype(F32)
        r_a = _rms(ya_pre)
        vec_ref[0:1, :] += jnp.sum(d_ya * (ya_pre * r_a), axis=0, keepdims=True)
        d_ya_pre = _rms_bwd(ya_pre, r_a, d_ya * ga_ref[...])
        d_y = d_ya_pre * (z_ssd * sig_a)
        dy_ref[...] = d_y
        dsk_acc[...] += jnp.sum(d_y * xh_ref[...], axis=0, keepdims=True)
        dzm_ref[:, 0:d] = (d_ya_pre * y * (sig_a * (1.0 + z_ssd * (1.0 - sig_a)))).astype(BF16)
        d_yb = dyc_ref[:, d:].astype(F32)
        r_b = _rms(yb_pre)
        vec_ref[2:3, :] += jnp.sum(d_yb * (yb_pre * r_b), axis=0, keepdims=True)
        d_yb_pre = _rms_bwd(yb_pre, r_b, d_yb * gm_ref[...])
        silu_m = z_mlp * sig_m
        dzm_ref[:, d:2 * d] = (d_yb_pre * sg * silu_m).astype(BF16)
        dzm_ref[:, 3 * d:4 * d] = (d_yb_pre * u * sg * (sig_m * (1.0 + z_mlp * (1.0 - sig_m)))).astype(BF16)
        d_sg = d_yb_pre * u * silu_m
        dsg_acc[...] += d_sg
        d_sgb = d_sg.astype(BF16)
        d_vn = []
        for g in range(groups):
            cols = slice(g * LANES, (g + 1) * LANES)
            dws_ref[g] += _dot(d_sgb[:, cols], vn[:, cols], NT)
            d_vn.append(_dot(wst_ref[g].astype(BF16), d_sgb[:, cols]))
        d_vn = jnp.concatenate(d_vn, axis=1)
        vec_ref[1:2, :] += jnp.sum(d_vn * (v * r_v), axis=0, keepdims=True)
        dzm_ref[:, 2 * d:3 * d] = _rms_bwd(v, r_v, d_vn * gv_ref[...]).astype(BF16)

        @pl.when(i == nsteps - 1)
        def _():
            vec_ref[3:4, 0:LANES] = _dot(dsk_acc[...], ih_ref[...], NN, HI)
            dbs_ref[...] = _dot(dsg_acc[...], ig_ref[...], NN, HI)

    row = pl.BlockSpec((1, d), lambda i: (0, 0))
    wsp = pl.BlockSpec((groups, Q, Q), lambda i: (0, 0, 0))
    ind = pl.BlockSpec((d, LANES), lambda i: (0, 0))
    return pl.pallas_call(
        body, name="mix_bwd", grid=(nsteps,),
        in_specs=[pl.BlockSpec((Q, 4 * d), lambda i: (i, 0)), pl.BlockSpec((2, Q, d), lambda i: (0, i, 0)),
                  pl.BlockSpec((Q, d), lambda i: (i, 0)), pl.BlockSpec((Q, 2 * d), lambda i: (i, 0)),
                  row, row, row, row, wsp, wsp, pl.BlockSpec((Q, LANES), lambda i: (0, 0)), ind, ind],
        out_specs=[pl.BlockSpec((Q, 4 * d), lambda i: (i, 0)), pl.BlockSpec((Q, d), lambda i: (i, 0)),
                   pl.BlockSpec((8, d), lambda i: (0, 0)), wsp, pl.BlockSpec((Q, LANES), lambda i: (0, 0))],
        out_shape=[jax.ShapeDtypeStruct((t, 4 * d), BF16), jax.ShapeDtypeStruct((t, d), F32),
                   jax.ShapeDtypeStruct((8, d), F32), jax.ShapeDtypeStruct((groups, Q, Q), F32),
                   jax.ShapeDtypeStruct((Q, LANES), F32)],
        scratch_shapes=[pltpu.VMEM((1, d), F32), pltpu.VMEM((Q, d), F32)],
        compiler_params=_params("arbitrary"),
    )(z_mid, y2, xbc, d_ycat, dsk_row, g_ssd, g_v, g_mlp, w_s, w_st, b_st, ind_head, ind_group)


def _ada_fwd(c16, w_ada, b_loc):
    depth, d, n = w_ada.shape
    tn = _tile(n, (512, 256, 128))

    def body(c_ref, w_ref, b_ref, o_ref):
        cv = c_ref[...]
        o_ref[0] = _dot(cv * _sigmoid(cv), w_ref[0], NN, HI) + b_ref[0]

    return pl.pallas_call(
        body, name="ada_fwd", grid=(depth, n // tn),
        in_specs=[pl.BlockSpec((16, d), lambda l, j: (0, 0)), pl.BlockSpec((1, d, tn), lambda l, j: (l, 0, j)),
                  pl.BlockSpec((1, 1, tn), lambda l, j: (l, 0, j))],
        out_specs=pl.BlockSpec((1, 16, tn), lambda l, j: (l, 0, j)),
        out_shape=jax.ShapeDtypeStruct((depth, 16, n), F32), compiler_params=_params("arbitrary", "arbitrary"),
    )(c16, w_ada, b_loc)


def _ada_bwd(c_t, dm_loc, w_ada):
    depth, d, n = w_ada.shape
    tn = _tile(n, (512, 256, 128))

    def body(s_ref, dm_ref, w_ref, gw_ref, dsc_ref):
        @pl.when((pl.program_id(0) == 0) & (pl.program_id(1) == 0))
        def _():
            dsc_ref[...] = jnp.zeros_like(dsc_ref)

        cv = s_ref[...]
        gw_ref[0] = _dot(cv * _sigmoid(cv), dm_ref[0], NN, HI)
        dsc_ref[...] += _dot(dm_ref[0, 8:16, :], w_ref[0], NT, HI)

    return pl.pallas_call(
        body, name="ada_bwd", grid=(depth, n // tn),
        in_specs=[pl.BlockSpec((d, LANES), lambda l, j: (0, 0)), pl.BlockSpec((1, LANES, tn), lambda l, j: (l, 0, j)),
                  pl.BlockSpec((1, d, tn), lambda l, j: (l, 0, j))],
        out_specs=[pl.BlockSpec((1, d, tn), lambda l, j: (l, 0, j)), pl.BlockSpec((8, d), lambda l, j: (0, 0))],
        out_shape=[jax.ShapeDtypeStruct((depth, d, n), F32), jax.ShapeDtypeStruct((8, d), F32)],
        compiler_params=_params("arbitrary", "arbitrary"),
    )(c_t, dm_loc, w_ada)


def _rowsum(x):
    depth, r, n = x.shape

    def body(x_ref, o_ref):
        o_ref[0] = jnp.sum(x_ref[0], axis=0, keepdims=True)

    return pl.pallas_call(
        body, name="rowsum", grid=(depth,),
        in_specs=[pl.BlockSpec((1, r, n), lambda l: (l, 0, 0))], out_specs=pl.BlockSpec((1, 1, n), lambda l: (l, 0, 0)),
        out_shape=jax.ShapeDtypeStruct((depth, 1, n), F32), compiler_params=_params("arbitrary"),
    )(x)


def _cctx_grad(d_scc, c_ctx_row):
    def body(g_ref, c_ref, o_ref):
        cv = c_ref[...]
        sig = _sigmoid(cv)
        o_ref[...] = g_ref[...] * (sig * (1.0 + cv * (1.0 - sig)))

    return pl.pallas_call(body, name="cctx_grad", out_shape=jax.ShapeDtypeStruct(c_ctx_row.shape, F32))(d_scc, c_ctx_row)


def _sum_lead(x, name):
    k, r, c = x.shape
    tr = _tile(r, [tt for tt in (1024, 512, 256, 128, 64, 32, 16, 8) if k * tt * c * x.dtype.itemsize <= SUM_BLOCK_BYTES])

    def body(x_ref, o_ref):
        acc = x_ref[0].astype(F32)
        for e in range(1, k):
            acc = acc + x_ref[e].astype(F32)
        o_ref[...] = acc

    return pl.pallas_call(
        body, name=name, grid=(r // tr,),
        in_specs=[pl.BlockSpec((k, tr, c), lambda i: (0, i, 0))], out_specs=pl.BlockSpec((tr, c), lambda i: (i, 0)),
        out_shape=jax.ShapeDtypeStruct((r, c), F32), compiler_params=_params("arbitrary"),
    )(x)


def _adamw(w, g, m, v, name, g2=None, side=None):
    r, c = w.shape
    tr = _tile(r, [tt for tt in (2048, 1024, 512, 256, 128, 64, 32, 16, 8) if tt * c * 4 <= ADAM_BLOCK_BYTES])
    two = g2 is not None
    bc1 = 1.0 - ADAM_B1 ** ADAM_STEP
    bc2 = 1.0 - ADAM_B2 ** ADAM_STEP

    def body(*refs):
        if two:
            w_ref, g_ref, g2_ref, m_ref, v_ref, go_ref, d_ref, mo_ref, vo_ref = refs
            gr = g_ref[...] + g2_ref[...]
        else:
            w_ref, g_ref, m_ref, v_ref, go_ref, d_ref, mo_ref, vo_ref = refs
            gr = g_ref[...]
        mn = ADAM_B1 * m_ref[...] + (1.0 - ADAM_B1) * gr
        vn = ADAM_B2 * v_ref[...] + (1.0 - ADAM_B2) * (gr * gr)
        go_ref[...] = gr
        mo_ref[...] = mn
        vo_ref[...] = vn
        d_ref[...] = -ADAM_LR * ((mn / bc1) / (jnp.sqrt(vn / bc2) + ADAM_EPS) + ADAM_WD * w_ref[...])

    blk = pl.BlockSpec((tr, c), lambda i: (i, 0))
    ins = (w, g, g2, m, v) if two else (w, g, m, v)
    return _hosted_call(body, side, name=name, grid=(r // tr,), in_specs=[blk] * len(ins), out_specs=[blk] * 4,
                        out_shape=[jax.ShapeDtypeStruct((r, c), F32)] * 4, scratch_shapes=[], args=ins)


def _flip(pos, k):
    x, y, c = pos
    return (x ^ ((k >> 2) & 1), y ^ ((k >> 1) & 1), c ^ (k & 1))


def _lin(pos):
    return 4 * pos[0] + 2 * pos[1] + pos[2]


def _chip(pos):
    return 2 * pos[0] + pos[1]


def _here():
    return (lax.axis_index("x"), lax.axis_index("y"), lax.axis_index("c"))


class _Exchange:
    def __init__(self, x, kind):
        self.x, self.kind = x, kind
        self.masks = {"gather4": (2, 4, 6), "scatter4": (2, 4, 6), "gather8": tuple(range(1, 8)), "swap": (1,)}[kind]
        self.slot = {"gather4": _chip, "scatter4": _chip, "gather8": _lin, "swap": None}[kind]
        lead = {"gather4": (4,), "scatter4": (), "gather8": (8,), "swap": ()}[kind]
        self.out_shape = jax.ShapeDtypeStruct(lead + x.shape, x.dtype)
        n = len(self.masks)
        self.scratch = [pltpu.SemaphoreType.DMA((n,)), pltpu.SemaphoreType.DMA((n,))] + ([] if kind == "swap" else [pltpu.SemaphoreType.DMA])

    def _copies(self, x_ref, o_ref, send, recv, *own, arrivals):
        me = _here()
        src = (lambda pos: x_ref.at[_chip(pos)]) if self.kind == "scatter4" else (lambda pos: x_ref)
        dst = (lambda pos: o_ref.at[self.slot(pos)]) if self.slot else (lambda pos: o_ref)
        local = [pltpu.make_async_copy(src(me), dst(me), own[0])] if own else []
        outs, ins = [], []
        for j, k in enumerate(self.masks):
            peer = _flip(me, k)
            sems = dict(send_sem=send.at[j], recv_sem=recv.at[j], device_id=peer, device_id_type=MESH)
            outs.append(pltpu.make_async_remote_copy(src_ref=src(peer), dst_ref=dst(me), **sems))
            if arrivals:
                ins.append(pltpu.make_async_remote_copy(src_ref=src(me), dst_ref=dst(peer), **sems))
        return local, outs, ins

    def start(self, *refs):
        local, outs, _ = self._copies(*refs, arrivals=False)
        for cp in local + outs:
            cp.start()

    def wait(self, *refs):
        local, outs, ins = self._copies(*refs, arrivals=True)
        for cp in ins:
            cp.wait_recv()
        for cp in outs:
            cp.wait_send()
        for cp in local:
            cp.wait()


def _exchange(x, kind, name):
    ex = _Exchange(x, kind)

    def body(*refs):
        ex.start(*refs)
        ex.wait(*refs)

    return pl.pallas_call(body, name=name, in_specs=[ANY], out_specs=ANY, out_shape=ex.out_shape, scratch_shapes=ex.scratch)(x)


def _gather4_two_level(x, name):
    half = x.shape[0] // 2

    def body(x_ref, o_ref, send1, recv1, send2, recv2, own):
        me = _here()
        sibling = _flip(me, 1)
        mine, theirs = pl.ds(me[2] * half, half), pl.ds((1 - me[2]) * half, half)
        peers = [_flip(me, k) for k in (2, 4, 6)]

        def over_ici(j, src_chip, rows, src=None):
            dst = o_ref.at[_chip(src_chip), rows]
            return pltpu.make_async_remote_copy(src_ref=dst if src is None else src, dst_ref=dst, send_sem=send1.at[j],
                                                recv_sem=recv1.at[j], device_id=peers[j], device_id_type=MESH)

        def over_d2d(j, rows):
            blk = o_ref.at[_chip(peers[j]), rows]
            return pltpu.make_async_remote_copy(src_ref=blk, dst_ref=blk, send_sem=send2.at[j], recv_sem=recv2.at[j],
                                                device_id=sibling, device_id_type=MESH)

        local = pltpu.make_async_copy(x_ref, o_ref.at[_chip(me)], own)
        local.start()
        sent = [over_ici(j, me, mine, src=x_ref.at[mine]) for j in range(3)]
        for cp in sent:
            cp.start()
        passed = [over_d2d(j, mine) for j in range(3)]
        for j in range(3):
            over_ici(j, peers[j], mine).wait_recv()
            passed[j].start()
        for j in range(3):
            over_d2d(j, theirs).wait_recv()
        for cp in sent + passed:
            cp.wait_send()
        local.wait()

    sems = pltpu.SemaphoreType.DMA((3,))
    return pl.pallas_call(body, name=name, in_specs=[ANY], out_specs=ANY, out_shape=jax.ShapeDtypeStruct((4,) + x.shape, x.dtype),
                          scratch_shapes=[sems, sems, sems, sems, pltpu.SemaphoreType.DMA])(x)


def _hosted_call(body, side, *, name, grid, in_specs, out_specs, out_shape, scratch_shapes, args):
    sides = [] if side is None else list(side) if isinstance(side, (list, tuple)) else [side]
    n_in, n_out, ns = len(in_specs), len(out_specs), len(sides)
    params = _params(*(["arbitrary"] * len(grid)))
    if not sides:
        return pl.pallas_call(body, name=name, grid=grid, in_specs=in_specs, out_specs=out_specs, out_shape=out_shape,
                              scratch_shapes=scratch_shapes, compiler_params=params)(*args)
    n_sem = [len(s.scratch) for s in sides]

    def hosted(*refs):
        ins, xs = refs[:n_in], refs[n_in:n_in + ns]
        outs, os_ = refs[n_in + ns:n_in + ns + n_out], refs[n_in + ns + n_out:n_in + 2 * ns + n_out]
        rest = refs[n_in + 2 * ns + n_out:]
        scratch, sems = rest[:len(rest) - sum(n_sem)], list(rest[len(rest) - sum(n_sem):])
        per_side = [[sems.pop(0) for _ in range(k)] for k in n_sem]
        ids = [pl.program_id(a) for a in range(len(grid))]
        first, last = ids[0] == 0, ids[0] == grid[0] - 1
        for a in range(1, len(grid)):
            first, last = first & (ids[a] == 0), last & (ids[a] == grid[a] - 1)

        @pl.when(first)
        def _():
            for s, x_ref, o_ref, sm in zip(sides, xs, os_, per_side):
                s.start(x_ref, o_ref, *sm)

        body(*ins, *outs, *scratch)

        @pl.when(last)
        def _():
            for s, x_ref, o_ref, sm in zip(sides, xs, os_, per_side):
                s.wait(x_ref, o_ref, *sm)

    return pl.pallas_call(hosted, name=name + "_x_" + "_".join(s.kind for s in sides), grid=grid, in_specs=list(in_specs) + [ANY] * ns,
                          out_specs=list(out_specs) + [ANY] * ns, out_shape=list(out_shape) + [s.out_shape for s in sides],
                          scratch_shapes=list(scratch_shapes) + [sem for s in sides for sem in s.scratch],
                          compiler_params=params)(*args, *[s.x for s in sides])


def _pad_lanes(a, width):
    return jnp.pad(a, [(0, 0)] * (a.ndim - 1) + [(0, width - a.shape[-1])])


def kernel(x, c, ctx, c_ctx, w_ada, b_ada, g_pre, g_post, w_in, conv_w, conv_b, dt_bias, a_log, d_skip, g_ssd, g_v, w_s, b_s, g_mlp, w_out, loss_target, m_c_ctx, m_w_ada, m_b_ada, m_g_pre, m_g_post, m_w_in, m_conv_w, m_conv_b, m_dt_bias, m_a_log, m_d_skip, m_g_ssd, m_g_v, m_w_s, m_b_s, m_g_mlp, m_w_out, v_c_ctx, v_w_ada, v_b_ada, v_g_pre, v_g_post, v_w_in, v_conv_w, v_conv_b, v_dt_bias, v_a_log, v_d_skip, v_g_ssd, v_g_v, v_w_s, v_b_s, v_g_mlp, v_w_out):
    depth, d = g_pre.shape
    seq, ctx_len = x.shape[1], ctx.shape[1]
    heads = d // HP
    in_w = 6 * d + 2 * heads
    groups_mlp = d // LANES
    t = ctx_len + seq
    assert ctx_len == TB and seq % TB == 0 and TB % ROW == 0 and heads % 4 == 0 and heads <= LANES and d % LANES == 0
    assert w_in.shape == (depth, d, in_w // 4)

    xi, yi, ci = lax.axis_index("x"), lax.axis_index("y"), lax.axis_index("c")
    chip = 2 * xi + yi
    me = 4 * xi + 2 * yi + ci

    n_ada = 3 * d // 4
    c_all = _exchange(c, "gather8", "ag_c")[:, 0, :]
    c16 = jnp.concatenate([c_all, c_ctx[None, :], jnp.zeros((7, d), F32)], axis=0)
    b_loc = lax.dynamic_slice_in_dim(b_ada, chip * n_ada, n_ada, axis=1)[:, None, :]
    mods_loc = _exchange(_ada_fwd(c16, w_ada, b_loc).reshape(depth * 16, n_ada), "gather8", "ag_mods")
    mods_loc = mods_loc.reshape(4, 2, depth, 16, n_ada)[:, 0]
    mods_full = jnp.moveaxis(mods_loc, 0, 2).reshape(depth, 16, 3 * d)
    mods_x = lax.dynamic_index_in_dim(mods_full, me, axis=1, keepdims=False).reshape(depth, 3, d)
    mods_c = mods_full[:, 8, :].reshape(depth, 3, d)
    mods = _pad_rows8(jnp.stack([mods_c, mods_x], axis=1))

    w_in_b, w_out_b = w_in.astype(BF16), w_out.astype(BF16)

    def lay_out(w_in_rows):
        full = jnp.moveaxis(jnp.concatenate(w_in_rows, axis=1), 0, 1).reshape(d, in_w)
        w_dt_l = jnp.concatenate([_pad_lanes(full[:, 2 * d:2 * d + heads], LANES),
                                  _pad_lanes(full[:, 2 * d + heads:2 * d + 2 * heads], LANES)], axis=1)
        return full[:, :2 * d], full[:, 2 * d + 2 * heads:], w_dt_l

    w_in_rows = [_gather4_two_level(w_in_b[0], "ag_w_in")]
    r_scan = 3 * d // 4
    conv_w_full = jnp.moveaxis(_exchange(conv_w, "gather4", "ag_conv_w"), 0, 2).reshape(depth, CONV_TAPS, 2 * d)
    conv_w8 = jnp.pad(conv_w_full, ((0, 0), (0, 8 - CONV_TAPS), (0, 0)))

    tri = jnp.stack([jnp.tril(jnp.ones((Q, Q), F32)), jnp.triu(jnp.ones((Q, Q), F32))])
    tri_t = jnp.swapaxes(tri, 1, 2)
    dtb = _pad_lanes(dt_bias, LANES)[:, :, None, :]
    alog = _pad_lanes(a_log, LANES)[:, :, None, :]
    dsk_row = jnp.repeat(d_skip, HP, axis=1)[:, None, :]
    w_st = jnp.swapaxes(w_s, 2, 3)
    b_st = _pad_lanes(jnp.swapaxes(b_s, 1, 2), LANES)
    chan = jnp.arange(d)
    ind_head = (chan[:, None] // HP == jnp.arange(LANES)[None, :]).astype(F32)
    ind_b, ind_t = ind_head.astype(BF16), ind_head.T.astype(BF16)
    ind_group = (chan[:, None] // LANES == jnp.arange(LANES)[None, :]).astype(F32)

    stream = jnp.concatenate([ctx[0], x[0]], axis=0)
    saved = []
    for l in range(depth):
        w_xbc, w_mid, w_dt = lay_out(w_in_rows)
        more = l + 1 < depth
        hx, hx_t = _pre_fwd(stream, g_pre[l][None], mods[l])
        z_xbc = _mm(hx, w_xbc, NN, "in_xbc")
        z_mid, w_out_all = _mm(hx, w_mid, NN, "in_mid", out_dtype=BF16, side=_Exchange(w_out_b[l], "gather4"))
        w_o = w_out_all.reshape(2 * d, d)
        z_dt = _mm(hx, w_dt, NN, "in_dt")
        xbc, dsilu = _conv_fwd(z_xbc, conv_w8[l], conv_b[l][None], ctx_len)
        y2, hs, *rows_a = _ssd_fwd(xbc, z_dt, dtb[l], alog[l], tri, ind_t, d, ctx_len,
                                   side=_Exchange(w_in_b[l + 1, :r_scan], "gather4") if more else None)
        ycat, ycat_t = _mix_fwd(z_mid, y2, xbc, dsk_row[l], g_ssd[l][None], g_v[l][None], g_mlp[l][None], w_s[l], b_st[l])
        if more:
            o, rows_b = _mm(ycat, w_o, NN, "out_proj", side=_Exchange(w_in_b[l + 1, r_scan:], "gather4"))
            w_in_rows = [rows_a[0], rows_b]
        else:
            o = _mm(ycat, w_o, NN, "out_proj")
        saved.append((stream, hx_t, z_xbc, dsilu, z_mid, z_dt, xbc, y2, hs, ycat_t, o, w_xbc, w_mid, w_dt, w_o))
        stream = _post_fwd(o, stream, g_post[l][None], mods[l])

    sq, d_stream = _loss_grad(stream, loss_target[0])
    loss = lax.psum(0.5 / d * sq[0, 0], ("x", "y", "c"))

    small = []
    dmods = []
    q_in = in_w // 4
    names = ["g_post", "conv_w", "conv_b", "dt_bias", "a_log", "d_skip", "g_ssd", "g_v", "w_s", "b_s", "g_mlp"]

    def rows_of(a):
        return -(-a.size // (8 * LANES)) * 8

    def pack(arrays):
        blocks = [jnp.pad(a.reshape(-1), (0, rows_of(a) * LANES - a.size)).reshape(rows_of(a), LANES) for a in arrays]
        rows = sum(b.shape[0] for b in blocks)
        return jnp.pad(jnp.concatenate(blocks, axis=0), ((0, -(-rows // PACK_ROWS) * PACK_ROWS - rows), (0, 0)))

    def unpack(block, likes):
        out, row = [], 0
        for a in likes:
            out.append(block[row:row + rows_of(a)].reshape(-1)[:a.size].reshape(a.shape))
            row += rows_of(a)
        return out

    def quarter_parts(g_xbc, g_mid, g_dt):
        segs = [(g_xbc, 0, 2 * d), (g_dt, 0, heads), (g_dt, LANES, heads), (g_mid, 0, 4 * d)]
        parts = []
        for qi in range(4):
            lo, hi, off, pieces = qi * q_in, (qi + 1) * q_in, 0, []
            for arr, start, width in segs:
                a, b = max(lo, off), min(hi, off + width)
                if a < b:
                    pieces.append(arr[:, start + a - off:start + b - off])
                off += width
            parts.append(pieces[0] if len(pieces) == 1 else jnp.concatenate(pieces, axis=1))
        return jnp.stack(parts)

    sum_in, sum_out = [None] * depth, [None] * depth
    swap_in, swap_out = [None] * depth, [None] * depth
    small_sum = [None] * depth
    parts_in = packed = None
    for l in reversed(range(depth)):
        x_in, hx_t, z_xbc, dsilu, z_mid, z_dt, xbc, y2, hs, ycat_t, o, w_xbc, w_mid, w_dt, w_o = saved[l]
        up = l + 1
        d_o, acc_post = _post_bwd(d_stream, o, g_post[l][None], mods[l])
        if packed is not None:
            d_ycat, gathered = _mm(d_o, w_o, NT, "d_ycat", out_dtype=BF16, side=_Exchange(packed, "gather8"))
            small_sum[up] = _sum_lead(gathered, "sum_small")
        else:
            d_ycat = _mm(d_o, w_o, NT, "d_ycat", out_dtype=BF16)
        g_out = _mm(ycat_t, d_o, NN, "dw_out", out_dtype=BF16)
        dz_mid, d_y, vec, d_ws, d_bs = _mix_bwd(z_mid, y2, xbc, d_ycat, dsk_row[l], g_ssd[l][None], g_v[l][None], g_mlp[l][None],
                                                w_s[l], w_st[l], b_st[l], ind_head, ind_group)
        d_xbc2, dz_dt, d_bias, d_alog, *got = _ssd_bwd(xbc, z_dt, dtb[l], alog[l], tri, tri_t, ind_t, ind_b, d_y, y2, hs, d, ctx_len,
                                                       side=_Exchange(parts_in, "scatter4") if parts_in is not None else None)
        if parts_in is not None:
            sum_in[up] = _sum_lead(got[0], "sum_w_in")
        dz_xbc, d_cw, d_cb = _conv_bwd(z_xbc, dsilu, d_xbc2, d_y, dsk_row[l], conv_w8[l], ctx_len)
        g_xbc, got_out = _mm(hx_t, dz_xbc, NN, "dw_xbc", out_dtype=BF16, side=_Exchange(g_out.reshape(4, 2 * d // 4, d), "scatter4"))
        sum_out[l] = _sum_lead(got_out, "sum_w_out")
        swaps = [_Exchange(sum_out[l], "swap")] + ([_Exchange(sum_in[up], "swap")] if parts_in is not None else [])
        g_mid, swap_out[l], *swapped = _mm(hx_t, dz_mid, NN, "dw_mid", out_dtype=BF16, side=swaps)
        if parts_in is not None:
            swap_in[up] = swapped[0]
        g_dt = _mm(hx_t, dz_dt, NN, "dw_dt", out_dtype=BF16)
        parts_in = quarter_parts(g_xbc, g_mid, g_dt)
        small.append(dict(
            g_post=acc_post[0, 1] + acc_post[1, 1], conv_w=d_cw[:CONV_TAPS], conv_b=d_cb[0],
            dt_bias=d_bias[:, 0, :heads], a_log=d_alog[:, 0, :heads], d_skip=vec[3, :heads], g_ssd=vec[0], g_v=vec[1],
            w_s=d_ws, b_s=d_bs[:, :groups_mlp].T, g_mlp=vec[2]))
        packed = pack([small[-1][n] for n in names])
        if l == 0:
            r_a = max(LANES, d // 3 // LANES * LANES)
            d_hx, got_a = _mm(dz_xbc, w_xbc, NT, "dhx_xbc", side=_Exchange(parts_in[:, :r_a], "scatter4"))
            d_hx, got_b = _mm(dz_mid, w_mid, NT, "dhx_mid", acc=d_hx, side=_Exchange(parts_in[:, r_a:], "scatter4"))
            sum_in[0] = _sum_lead(jnp.concatenate([got_a, got_b], axis=1), "sum_w_in")
        else:
            d_hx = _mm(dz_xbc, w_xbc, NT, "dhx_xbc")
            d_hx = _mm(dz_mid, w_mid, NT, "dhx_mid", acc=d_hx)
        d_hx = _mm(dz_dt, w_dt, NT, "dhx_dt", acc=d_hx)
        if l == 0:
            d_stream, acc_pre, swap_in[0], gathered = _pre_bwd(
                x_in, d_hx, d_stream, g_pre[l][None], mods[l], latent_only=True,
                side=[_Exchange(sum_in[0], "swap"), _Exchange(packed, "gather8")])
            small_sum[0] = _sum_lead(gathered, "sum_small")
        else:
            d_stream, acc_pre = _pre_bwd(x_in, d_hx, d_stream, g_pre[l][None], mods[l])
        dmods.append(jnp.concatenate([acc_pre[:, 0], acc_pre[:, 1], acc_post[:, 0], acc_pre[:, 2]], axis=1))
    small.reverse(), dmods.reverse()
    grad_x = d_stream[None]

    weights = dict(c_ctx=c_ctx, w_ada=w_ada, b_ada=b_ada, g_pre=g_pre, g_post=g_post, w_in=w_in, conv_w=conv_w, conv_b=conv_b,
                   dt_bias=dt_bias, a_log=a_log, d_skip=d_skip, g_ssd=g_ssd, g_v=g_v, w_s=w_s, b_s=b_s, g_mlp=g_mlp, w_out=w_out)
    m_in = dict(c_ctx=m_c_ctx, w_ada=m_w_ada, b_ada=m_b_ada, g_pre=m_g_pre, g_post=m_g_post, w_in=m_w_in, conv_w=m_conv_w,
                conv_b=m_conv_b, dt_bias=m_dt_bias, a_log=m_a_log, d_skip=m_d_skip, g_ssd=m_g_ssd, g_v=m_g_v, w_s=m_w_s,
                b_s=m_b_s, g_mlp=m_g_mlp, w_out=m_w_out)
    v_in = dict(c_ctx=v_c_ctx, w_ada=v_w_ada, b_ada=v_b_ada, g_pre=v_g_pre, g_post=v_g_post, w_in=v_w_in, conv_w=v_conv_w,
                conv_b=v_conv_b, dt_bias=v_dt_bias, a_log=v_a_log, d_skip=v_d_skip, g_ssd=v_g_ssd, g_v=v_g_v, w_s=v_w_s,
                b_s=v_b_s, g_mlp=v_g_mlp, w_out=v_w_out)
    order = list(weights)
    results = {}

    def adamw_big(n, ga, gb):
        shp = weights[n].shape
        two = lambda a: a.reshape(-1, shp[-1])
        results[n] = [r.reshape(shp) for r in _adamw(two(weights[n]), ga, two(m_in[n]), two(v_in[n]), "adamw_" + n, g2=gb)]

    adamw_big("w_in", jnp.concatenate(sum_in, axis=0), jnp.concatenate(swap_in, axis=0))
    adamw_big("w_out", jnp.concatenate(sum_out, axis=0), jnp.concatenate(swap_out, axis=0))
    per_layer = [unpack(small_sum[l], [small[l][n] for n in names]) for l in range(depth)]
    grads = {n: jnp.stack([per_layer[l][j] for l in range(depth)]) for j, n in enumerate(names)}
    grads["conv_w"] = lax.dynamic_slice_in_dim(grads["conv_w"], chip * (2 * d // 4), 2 * d // 4, axis=2)

    dm_all = _exchange(jnp.stack(dmods).reshape(depth * 2, 4 * d), "gather8", "ag_dmods").reshape(8, depth, 2, 4 * d)
    grads["g_pre"] = _sum_lead(jnp.moveaxis(dm_all[..., 3 * d:], 2, 1).reshape(16, depth, d), "sum_g_pre")
    dm_ctx = _sum_lead(dm_all[:, :, 0, :3 * d], "sum_dm_ctx")
    dm16 = jnp.concatenate([jnp.moveaxis(dm_all[:, :, 1, :3 * d], 0, 1), dm_ctx[:, None, :], jnp.zeros((depth, 7, 3 * d), F32)], axis=1)
    grads["b_ada"] = _rowsum(dm16)[:, 0, :]
    dm_loc = jnp.pad(lax.dynamic_slice_in_dim(dm16, chip * n_ada, n_ada, axis=2), ((0, 0), (0, LANES - 16), (0, 0)))
    c_t = jnp.pad(c16.T, ((0, 0), (0, LANES - 16)))
    g_w_ada, d_scc_part = _ada_bwd(c_t, dm_loc, w_ada)
    adamw_big("w_ada", g_w_ada.reshape(depth * d, n_ada), None)
    d_scc = _sum_lead(_exchange(d_scc_part, "gather8", "ag_dscc").reshape(4, 2, 8, d)[:, 0], "sum_dscc")
    grads["c_ctx"] = _cctx_grad(d_scc[0:1], c_ctx[None])[0]

    rest = [n for n in order if n not in results]
    outs = _adamw(pack([weights[n] for n in rest]), pack([grads[n] for n in rest]), pack([m_in[n] for n in rest]),
                  pack([v_in[n] for n in rest]), "adamw_small")
    for j, res in enumerate(zip(*[unpack(o_, [weights[n] for n in rest]) for o_ in outs])):
        results[rest[j]] = list(res)

    return (loss, grad_x, *[results[n][0] for n in order], *[results[n][1] for n in order],
            *[results[n][2] for n in order], *[results[n][3] for n in order])


def _pad_rows8(a):
    return jnp.pad(a, [(0, 0)] * (a.ndim - 2) + [(0, 8 - a.shape[-2]), (0, 0)])
```

```python
import jax
import jax.numpy as jnp
from jax import lax
from jax.experimental import pallas as pl
from jax.experimental.pallas import tpu as pltpu

F32 = jnp.float32
BF16 = jnp.bfloat16
EPS = 1e-6
Q = 128
TB = 256
ROW = 64
HP = 64
LANES = 128
CONV_TAPS = 5
VMEM_LIMIT = 48 * 1024 * 1024
HI = lax.Precision.HIGHEST
SUM_BLOCK_BYTES = 4 * 1024 * 1024
ADAM_BLOCK_BYTES = 1024 * 1024
PACK_ROWS = 256
MESH = pl.DeviceIdType.MESH
ANY = pl.BlockSpec(memory_space=pl.ANY)

ADAM_LR, ADAM_B1, ADAM_B2, ADAM_EPS, ADAM_WD, ADAM_STEP = 0.001, 0.9, 0.999, 1e-08, 0.01, 10

NN = (((1,), (0,)), ((), ()))
NT = (((1,), (1,)), ((), ()))
TN = (((0,), (0,)), ((), ()))


def _dot(a, b, dims=NN, prec=None):
    return lax.dot_general(a, b, dims, precision=prec, preferred_element_type=F32)


def _params(*sem):
    if sem:
        return pltpu.CompilerParams(vmem_limit_bytes=VMEM_LIMIT, dimension_semantics=sem)
    return pltpu.CompilerParams(vmem_limit_bytes=VMEM_LIMIT)


def _tile(dim, cands):
    for t in cands:
        if dim % t == 0:
            return t
    return dim


def _sigmoid(x):
    return 1.0 / (1.0 + jnp.exp(-x))


def _softplus(x):
    e = jnp.exp(-jnp.abs(x))
    u = 1.0 + e
    um1 = u - 1.0
    l1p = jnp.where(um1 == 0.0, e, jnp.log(u) * (e / jnp.where(um1 == 0.0, 1.0, um1)))
    return jnp.maximum(x, 0.0) + l1p


def _rms(x):
    return lax.rsqrt(jnp.mean(x * x, axis=-1, keepdims=True) + EPS)


def _rms_bwd(x, r, t):
    return r * t - x * (r * r * r) * jnp.mean(x * t, axis=-1, keepdims=True)


def _mm(a, b, dims, name, acc=None, out_dtype=F32, side=None):
    (ca,), (cb,) = dims[0]
    m, k = a.shape[1 - ca], a.shape[ca]
    n = b.shape[1 - cb]
    tk = k if k <= 2048 else _tile(k, (2048, 768, 512, 384, 256, 128))
    nk = k // tk
    tm = _tile(m, (2048, 1024, 768, 512, 384, 256, 128) if nk > 1 else (1024, 768, 512, 384, 256, 128))
    tn = _tile(n, (1024, 512, 256, 128))
    a_spec = pl.BlockSpec((tm, tk), lambda i, j, kk: (i, kk)) if ca == 1 else pl.BlockSpec((tk, tm), lambda i, j, kk: (kk, i))
    b_spec = pl.BlockSpec((tk, tn), lambda i, j, kk: (kk, j)) if cb == 0 else pl.BlockSpec((tn, tk), lambda i, j, kk: (j, kk))
    o_spec = pl.BlockSpec((tm, tn), lambda i, j, kk: (i, j))
    has_acc = acc is not None

    def body(*refs):
        if has_acc:
            a_ref, b_ref, c_ref, o_ref, acc_ref = refs
        else:
            a_ref, b_ref, o_ref, acc_ref = refs
        kk = pl.program_id(2)

        @pl.when(kk == 0)
        def _():
            acc_ref[...] = c_ref[...] if has_acc else jnp.zeros_like(acc_ref)

        acc_ref[...] += _dot(a_ref[...], b_ref[...], dims)

        @pl.when(kk == nk - 1)
        def _():
            o_ref[...] = acc_ref[...].astype(out_dtype)

    res = _hosted_call(
        body, side, name=name, grid=(m // tm, n // tn, nk),
        in_specs=[a_spec, b_spec] + ([o_spec] if has_acc else []),
        out_specs=[o_spec], out_shape=[jax.ShapeDtypeStruct((m, n), out_dtype)],
        scratch_shapes=[pltpu.VMEM((tm, tn), F32)], args=(a, b, acc) if has_acc else (a, b))
    return res[0] if side is None else res


def _which(i):
    return jnp.minimum(i, 1)


def _pre_fwd(x, g_pre, mods):
    t, d = x.shape

    def body(x_ref, g_ref, m_ref, o_ref, ot_ref):
        xb = x_ref[...]
        xn = xb * _rms(xb) * g_ref[...]
        hx = xn * (1.0 + m_ref[0, 1:2, :]) + m_ref[0, 0:1, :]
        o_ref[...] = hx.astype(BF16)
        ot_ref[...] = hx.T.astype(BF16)

    return pl.pallas_call(
        body, name="pre_fwd", grid=(t // TB,),
        in_specs=[pl.BlockSpec((TB, d), lambda i: (i, 0)), pl.BlockSpec((1, d), lambda i: (0, 0)),
                  pl.BlockSpec((1, 8, d), lambda i: (_which(i), 0, 0))],
        out_specs=[pl.BlockSpec((TB, d), lambda i: (i, 0)), pl.BlockSpec((d, TB), lambda i: (0, i))],
        out_shape=[jax.ShapeDtypeStruct((t, d), BF16), jax.ShapeDtypeStruct((d, t), BF16)], compiler_params=_params("arbitrary"),
    )(x, g_pre, mods)


def _pre_bwd(x, d_hx, d_up, g_pre, mods, side=None, latent_only=False):
    t, d = x.shape
    dx_rows = t - TB if latent_only else t
    dx_spec = pl.BlockSpec((TB, d), (lambda i: (jnp.maximum(i - 1, 0), 0)) if latent_only else (lambda i: (i, 0)))

    def body(x_ref, dh_ref, du_ref, g_ref, m_ref, dx_ref, acc_ref):
        i = pl.program_id(0)

        @pl.when(i <= 1)
        def _():
            acc_ref[...] = jnp.zeros_like(acc_ref)

        xb = x_ref[...]
        dh = dh_ref[...]
        r = _rms(xb)
        xr = xb * r
        d_xn = dh * (1.0 + m_ref[0, 1:2, :])
        dx_ref[...] = du_ref[...] + _rms_bwd(xb, r, d_xn * g_ref[...])
        acc_ref[0, 0:1, :] += jnp.sum(dh, axis=0, keepdims=True)
        acc_ref[0, 1:2, :] += jnp.sum(dh * (xr * g_ref[...]), axis=0, keepdims=True)
        acc_ref[0, 2:3, :] += jnp.sum(d_xn * xr, axis=0, keepdims=True)

    blk = pl.BlockSpec((TB, d), lambda i: (i, 0))
    return _hosted_call(
        body, side, name="pre_bwd", grid=(t // TB,),
        in_specs=[blk, blk, blk, pl.BlockSpec((1, d), lambda i: (0, 0)),
                  pl.BlockSpec((1, 8, d), lambda i: (_which(i), 0, 0))],
        out_specs=[dx_spec, pl.BlockSpec((1, 8, d), lambda i: (_which(i), 0, 0))],
        out_shape=[jax.ShapeDtypeStruct((dx_rows, d), F32), jax.ShapeDtypeStruct((2, 8, d), F32)],
        scratch_shapes=[], args=(x, d_hx, d_up, g_pre, mods))


def _post_fwd(o, x, g_post, mods):
    t, d = x.shape

    def body(o_ref, x_ref, g_ref, m_ref, y_ref):
        ob = o_ref[...]
        y_ref[...] = x_ref[...] + m_ref[0, 2:3, :] * (ob * _rms(ob) * g_ref[...])

    blk = pl.BlockSpec((TB, d), lambda i: (i, 0))
    return pl.pallas_call(
        body, name="post_fwd", grid=(t // TB,),
        in_specs=[blk, blk, pl.BlockSpec((1, d), lambda i: (0, 0)), pl.BlockSpec((1, 8, d), lambda i: (_which(i), 0, 0))],
        out_specs=blk, out_shape=jax.ShapeDtypeStruct((t, d), F32), compiler_params=_params("arbitrary"),
    )(o, x, g_post, mods)


def _post_bwd(d_y, o, g_post, mods):
    t, d = o.shape

    def body(dy_ref, o_ref, g_ref, m_ref, do_ref, acc_ref):
        i = pl.program_id(0)

        @pl.when(i <= 1)
        def _():
            acc_ref[...] = jnp.zeros_like(acc_ref)

        ob = o_ref[...]
        dy = dy_ref[...]
        r = _rms(ob)
        orr = ob * r
        d_out = dy * m_ref[0, 2:3, :]
        do_ref[...] = _rms_bwd(ob, r, d_out * g_ref[...]).astype(BF16)
        acc_ref[0, 0:1, :] += jnp.sum(dy * (orr * g_ref[...]), axis=0, keepdims=True)
        acc_ref[0, 1:2, :] += jnp.sum(d_out * orr, axis=0, keepdims=True)

    blk = pl.BlockSpec((TB, d), lambda i: (i, 0))
    return pl.pallas_call(
        body, name="post_bwd", grid=(t // TB,),
        in_specs=[blk, blk, pl.BlockSpec((1, d), lambda i: (0, 0)), pl.BlockSpec((1, 8, d), lambda i: (_which(i), 0, 0))],
        out_specs=[blk, pl.BlockSpec((1, 8, d), lambda i: (_which(i), 0, 0))],
        out_shape=[jax.ShapeDtypeStruct((t, d), BF16), jax.ShapeDtypeStruct((2, 8, d), F32)],
        compiler_params=_params("arbitrary"),
    )(d_y, o, g_post, mods)


def _loss_grad(xf, target):
    t, d = xf.shape

    def body(x_ref, t_ref, loss_ref, dx_ref):
        i = pl.program_id(0)

        @pl.when(i == 0)
        def _():
            loss_ref[...] = jnp.zeros_like(loss_ref)
            dx_ref[...] = jnp.zeros_like(dx_ref)

        @pl.when(i > 0)
        def _():
            err = x_ref[...] - t_ref[...]
            loss_ref[...] += jnp.sum(err * err).reshape(1, 1)
            dx_ref[...] = err * (1.0 / d)

    return pl.pallas_call(
        body, name="loss_grad", grid=(t // TB,),
        in_specs=[pl.BlockSpec((TB, d), lambda i: (i, 0)), pl.BlockSpec((TB, d), lambda i: (jnp.maximum(i - 1, 0), 0))],
        out_specs=[pl.BlockSpec((1, 1), lambda i: (0, 0)), pl.BlockSpec((TB, d), lambda i: (i, 0))],
        out_shape=[jax.ShapeDtypeStruct((1, 1), F32), jax.ShapeDtypeStruct((t, d), F32)],
        compiler_params=_params("arbitrary"),
    )(xf, target)


def _conv_terms(zb, pos, row_len):
    out = []
    for k in range(CONV_TAPS):
        o = k - CONV_TAPS // 2
        sh = zb if o == 0 else pltpu.roll(zb, (-o) % TB, 0)
        out.append(jnp.where((pos + o >= 0) & (pos + o < row_len), sh, 0.0))
    return out


def _row_pos(i, ctx_len):
    row_len = jnp.where(i == 0, ctx_len, ROW)
    pos = lax.broadcasted_iota(jnp.int32, (TB, 1), 0) & (row_len - 1)
    return pos, row_len


def _conv_fwd(z_xbc, conv_w8, conv_b, ctx_len):
    t, c = z_xbc.shape
    tc = _tile(c, (2048, 1024, 512, 256, 128))

    def body(z_ref, w_ref, b_ref, o_ref, ds_ref):
        pos, row_len = _row_pos(pl.program_id(1), ctx_len)
        terms = _conv_terms(z_ref[...], pos, row_len)
        pre = b_ref[...]
        for k in range(CONV_TAPS):
            pre = pre + terms[k] * w_ref[k:k + 1, :]
        sig = _sigmoid(pre)
        o_ref[...] = pre * sig
        ds_ref[...] = sig * (1.0 + pre * (1.0 - sig))

    blk = pl.BlockSpec((TB, tc), lambda j, i: (i, j))
    return pl.pallas_call(
        body, name="conv_fwd", grid=(c // tc, t // TB),
        in_specs=[blk, pl.BlockSpec((8, tc), lambda j, i: (0, j)), pl.BlockSpec((1, tc), lambda j, i: (0, j))],
        out_specs=[blk, blk], out_shape=[jax.ShapeDtypeStruct((t, c), F32)] * 2, compiler_params=_params("arbitrary", "arbitrary"),
    )(z_xbc, conv_w8, conv_b)


def _conv_bwd(z_xbc, dsilu, d_xbc2, d_y, d_skip_row, conv_w8, ctx_len):
    t, c = z_xbc.shape
    d = d_y.shape[1]
    tc = _tile(d, (2048, 1024, 512, 256, 128))
    nskip = d // tc

    def body(z_ref, dsl_ref, g2_ref, dy_ref, ds_ref, w_ref, dz_ref, dw_ref, db_ref):
        j, i = pl.program_id(0), pl.program_id(1)

        @pl.when(i == 0)
        def _():
            dw_ref[...] = jnp.zeros_like(dw_ref)
            db_ref[...] = jnp.zeros_like(db_ref)

        pos, row_len = _row_pos(i, ctx_len)
        skip = jnp.where(j < nskip, 1.0, 0.0) * ds_ref[...]
        d_pre = (g2_ref[0] + g2_ref[1] + dy_ref[...] * skip) * dsl_ref[...]
        db_ref[...] += jnp.sum(d_pre, axis=0, keepdims=True)
        zb = z_ref[...]
        dz = jnp.zeros_like(d_pre)
        for k in range(CONV_TAPS):
            o = k - CONV_TAPS // 2
            sh = d_pre if o == 0 else pltpu.roll(d_pre, o % TB, 0)
            sh = jnp.where((pos - o >= 0) & (pos - o < row_len), sh, 0.0)
            dw_ref[k:k + 1, :] += jnp.sum(sh * zb, axis=0, keepdims=True)
            dz = dz + sh * w_ref[k:k + 1, :]
        dz_ref[...] = dz.astype(BF16)

    jd = lambda j: jnp.minimum(j, nskip - 1)
    blk = pl.BlockSpec((TB, tc), lambda j, i: (i, j))
    return pl.pallas_call(
        body, name="conv_bwd", grid=(c // tc, t // TB),
        in_specs=[blk, blk, pl.BlockSpec((2, TB, tc), lambda j, i: (0, i, j)),
                  pl.BlockSpec((TB, tc), lambda j, i: (i, jd(j))), pl.BlockSpec((1, tc), lambda j, i: (0, jd(j))),
                  pl.BlockSpec((8, tc), lambda j, i: (0, j))],
        out_specs=[blk, pl.BlockSpec((8, tc), lambda j, i: (0, j)), pl.BlockSpec((1, tc), lambda j, i: (0, j))],
        out_shape=[jax.ShapeDtypeStruct((t, c), BF16), jax.ShapeDtypeStruct((8, c), F32), jax.ShapeDtypeStruct((1, c), F32)],
        compiler_params=_params("arbitrary", "arbitrary"),
    )(z_xbc, dsilu, d_xbc2, d_y, d_skip_row, conv_w8)


def _scan_chunk(dirn, s, nch, ncc):
    bwd = jnp.where(s < ncc, ncc - 1 - s, nch - 1 - (s - ncc))
    return jnp.where(dirn == 0, s, bwd)


def _ssd_decays(dt_ref, dtb_ref, alog_ref, tri):
    raw = dt_ref[...] + dtb_ref[0]
    dt = _softplus(raw)
    a_neg = -jnp.exp(alog_ref[0])
    a = dt * a_neg
    s = _dot(tri, a, NN, HI)
    stot = jnp.sum(a, axis=0, keepdims=True)
    return raw, dt, a_neg, s, stot, s.T


def _split(v):
    hi = v.astype(BF16)
    return hi, (v - hi.astype(F32)).astype(BF16)


def _expand(v, indt_ref):
    hi, lo = _split(v)
    return _dot(hi, indt_ref[...]) + _dot(lo, indt_ref[...])


def _head_sums(v, ind_ref):
    hi, lo = _split(v)
    return _dot(hi, ind_ref[...]) + _dot(lo, ind_ref[...])


def _ssd_fwd(xbc, z_dt, dtb, alog, tri, ind_t, d, ctx_len, side=None):
    t = xbc.shape[0]
    nch, ncc = t // Q, ctx_len // Q
    heads = d // HP
    groups = heads // 4
    gn = groups * LANES

    def body(xbc_ref, dt_ref, dtb_ref, alog_ref, tri_ref, indt_ref, y_ref, hs_ref, h_scr, xdb_scr, xde_scr, esx_scr):
        @pl.when(pl.program_id(1) == 0)
        def _():
            h_scr[...] = jnp.zeros_like(h_scr)

        tri = tri_ref[0]
        mask = tri > 0.0
        _, dt, _, s, stot, s_t = _ssd_decays(dt_ref, dtb_ref, alog_ref, tri)
        esx_scr[...] = _expand(jnp.exp(s), indt_ref)
        etot_x = _expand(jnp.broadcast_to(jnp.exp(stot), (8, LANES)), indt_ref)[0:1]
        xd = xbc_ref[:, :d] * _expand(dt, indt_ref)
        xdb_scr[...] = xd.astype(BF16)
        xde_scr[...] = (xd * _expand(jnp.exp(stot - s), indt_ref)).astype(BF16)
        left = lax.broadcasted_iota(jnp.int32, (Q, LANES), 1) < HP
        hs_ref[0, 0] = h_scr[...]
        for g in range(groups):
            b32 = xbc_ref[:, d + g * LANES:d + (g + 1) * LANES]
            bb = b32.astype(BF16)
            bbt = b32.T.astype(BF16)
            cb = xbc_ref[:, d + gn + g * LANES:d + gn + (g + 1) * LANES].astype(BF16)
            cbt = _dot(cb, bb, NT)
            gcols = slice(4 * g * HP, 4 * (g + 1) * HP)
            hg = h_scr[:, gcols]
            y_off = _dot(cb, hg.astype(BF16)) * esx_scr[:, gcols]
            h_scr[:, gcols] = hg * etot_x[:, gcols] + _dot(bbt, xde_scr[:, gcols])
            for j, pr in enumerate((2 * g, 2 * g + 1)):
                h0 = 2 * pr
                cols = slice(pr * LANES, (pr + 1) * LANES)
                xdb = xdb_scr[:, cols]
                res = []
                for h in (h0, h0 + 1):
                    lm = jnp.exp(jnp.where(mask, s[:, h:h + 1] - s_t[h:h + 1, :], -jnp.inf))
                    res.append(_dot((cbt * lm).astype(BF16), xdb))
                y_ref[0, :, cols] = jnp.where(left, res[0], res[1]) + y_off[:, j * LANES:(j + 1) * LANES]

    cidx = lambda dd, ss: _scan_chunk(dd, ss, nch, ncc)
    return _hosted_call(
        body, side, name="ssd_fwd", grid=(2, nch),
        in_specs=[pl.BlockSpec((Q, 2 * d), lambda dd, ss: (cidx(dd, ss), 0)),
                  pl.BlockSpec((Q, LANES), lambda dd, ss: (cidx(dd, ss), dd)),
                  pl.BlockSpec((1, 1, LANES), lambda dd, ss: (dd, 0, 0)),
                  pl.BlockSpec((1, 1, LANES), lambda dd, ss: (dd, 0, 0)),
                  pl.BlockSpec((1, Q, Q), lambda dd, ss: (dd, 0, 0)),
                  pl.BlockSpec((LANES, d), lambda dd, ss: (0, 0))],
        out_specs=[pl.BlockSpec((1, Q, d), lambda dd, ss: (dd, cidx(dd, ss), 0)),
                   pl.BlockSpec((1, 1, LANES, d), lambda dd, ss: (dd, cidx(dd, ss), 0, 0))],
        out_shape=[jax.ShapeDtypeStruct((2, t, d), F32), jax.ShapeDtypeStruct((2, nch, LANES, d), F32)],
        scratch_shapes=[pltpu.VMEM((LANES, d), F32), pltpu.VMEM((Q, d), BF16), pltpu.VMEM((Q, d), BF16), pltpu.VMEM((Q, d), F32)],
        args=(xbc, z_dt, dtb, alog, tri, ind_t))


def _ssd_bwd(xbc, z_dt, dtb, alog, tri, tri_t, ind_t, ind, d_y, y2, hs, d, ctx_len, side=None):
    t = xbc.shape[0]
    nch, ncc = t // Q, ctx_len // Q
    heads = d // HP
    groups = heads // 4
    gn = groups * LANES

    def body(xbc_ref, dt_ref, dtb_ref, alog_ref, tri_ref, trit_ref, indt_ref, ind_ref, dy_ref, y_ref, hs_ref,
             dx_ref, dzdt_ref, dbias_ref, dalog_ref, dh_scr, dtx_scr, ex_scr, xdb_scr, xde_scr, dyb_scr, dye_scr, dxd_scr, bdh_scr):
        @pl.when(pl.program_id(1) == 0)
        def _():
            dh_scr[...] = jnp.zeros_like(dh_scr)
            dbias_ref[...] = jnp.zeros_like(dbias_ref)
            dalog_ref[...] = jnp.zeros_like(dalog_ref)

        tri = tri_ref[0]
        mask = tri > 0.0
        mask_t = trit_ref[0] > 0.0
        raw, dt, a_neg, s, stot, s_t = _ssd_decays(dt_ref, dtb_ref, alog_ref, tri)
        etot = jnp.exp(stot)
        etot_x = _expand(jnp.broadcast_to(etot, (8, LANES)), indt_ref)[0:1]
        dtx_scr[...] = _expand(dt, indt_ref)
        ex_scr[...] = _expand(jnp.exp(stot - s), indt_ref)
        xd = xbc_ref[:, :d] * dtx_scr[...]
        xdb_scr[...] = xd.astype(BF16)
        xde_scr[...] = (xd * ex_scr[...]).astype(BF16)
        dyb_scr[...] = dy_ref[...].astype(BF16)
        dye_scr[...] = (dy_ref[...] * _dot(jnp.exp(s).astype(BF16), indt_ref[...])).astype(BF16)
        hd_cols = jnp.sum(dh_scr[...] * hs_ref[0, 0], axis=0, keepdims=True)
        left = lax.broadcasted_iota(jnp.int32, (Q, LANES), 1) < HP
        for g in range(groups):
            b32 = xbc_ref[:, d + g * LANES:d + (g + 1) * LANES]
            c32 = xbc_ref[:, d + gn + g * LANES:d + gn + (g + 1) * LANES]
            bb, cb = b32.astype(BF16), c32.astype(BF16)
            c_t = c32.T.astype(BF16)
            cbt = _dot(cb, bb, NT)
            cbt_t = _dot(bb, cb, NT)
            d_cbt = jnp.zeros((Q, Q), F32)
            gcols = slice(4 * g * HP, 4 * (g + 1) * HP)
            dyeb = dye_scr[:, gcols]
            dhg = dh_scr[:, gcols]
            dhb = dhg.astype(BF16)
            bdh_scr[:, gcols] = _dot(bb, dhb)
            d_c = _dot(dyeb, hs_ref[0, 0, :, gcols].astype(BF16), NT)
            d_b = _dot(xde_scr[:, gcols], dhb, NT)
            dh_scr[:, gcols] = dhg * etot_x[:, gcols] + _dot(c_t, dyeb)
            for pr in (2 * g, 2 * g + 1):
                h0 = 2 * pr
                cols = slice(pr * LANES, (pr + 1) * LANES)
                xdb = xdb_scr[:, cols]
                dyb = dyb_scr[:, cols]
                parts = []
                for hh, h in enumerate((h0, h0 + 1)):
                    mine = left if hh == 0 else jnp.logical_not(left)
                    diff = s[:, h:h + 1] - s_t[h:h + 1, :]
                    lm = jnp.exp(jnp.where(mask, diff, -jnp.inf))
                    lm_t = jnp.exp(jnp.where(mask_t, -diff, -jnp.inf))
                    gm = _dot(jnp.where(mine, dyb, jnp.zeros_like(dyb)), xdb, NT)
                    d_cbt = d_cbt + gm * lm
                    parts.append(_dot((cbt_t * lm_t).astype(BF16), dyb))
                dxd_scr[:, cols] = jnp.where(left, parts[0], parts[1])
            dx_ref[0, :, d + g * LANES:d + (g + 1) * LANES] = d_b + _dot(d_cbt.T.astype(BF16), cb)
            dx_ref[0, :, d + gn + g * LANES:d + gn + (g + 1) * LANES] = d_c + _dot(d_cbt.astype(BF16), bb)
        x = xbc_ref[:, :d]
        ebdh = ex_scr[...] * bdh_scr[...]
        d_xd = dxd_scr[...] + ebdh
        dx_ref[0, :, :d] = d_xd * dtx_scr[...]
        xe = x * dtx_scr[...] * ebdh
        d_s = _head_sums(dyb_scr[...].astype(F32) * y_ref[0] - xdb_scr[...].astype(F32) * dxd_scr[...] - xe, ind_ref)
        r_dx = _head_sums(d_xd * x, ind_ref)
        row8 = lax.broadcasted_iota(jnp.int32, (8, d), 0)
        tot = _head_sums(jnp.where(row8 == 0, jnp.sum(xe, axis=0, keepdims=True), jnp.where(row8 == 1, hd_cols, 0.0)), ind_ref)
        d_stot = tot[0:1] + etot * tot[1:2]
        d_a = _dot(trit_ref[0], d_s, NN, HI) + d_stot
        valid = lax.broadcasted_iota(jnp.int32, (Q, LANES), 1) < heads
        d_dt_tot = jnp.where(valid, d_a * a_neg + r_dx, 0.0)
        d_raw = d_dt_tot * _sigmoid(raw)
        dzdt_ref[...] = d_raw.astype(BF16)
        dbias_ref[0] += jnp.sum(d_raw, axis=0, keepdims=True)
        dalog_ref[0] += jnp.sum(jnp.where(valid, d_a * dt, 0.0), axis=0, keepdims=True) * a_neg

    cidx = lambda dd, ss: _scan_chunk(dd, nch - 1 - ss, nch, ncc)
    full = lambda shape: pltpu.VMEM(shape, F32)
    half = lambda shape: pltpu.VMEM(shape, BF16)
    return _hosted_call(
        body, side, name="ssd_bwd", grid=(2, nch),
        in_specs=[pl.BlockSpec((Q, 2 * d), lambda dd, ss: (cidx(dd, ss), 0)),
                  pl.BlockSpec((Q, LANES), lambda dd, ss: (cidx(dd, ss), dd)),
                  pl.BlockSpec((1, 1, LANES), lambda dd, ss: (dd, 0, 0)),
                  pl.BlockSpec((1, 1, LANES), lambda dd, ss: (dd, 0, 0)),
                  pl.BlockSpec((1, Q, Q), lambda dd, ss: (dd, 0, 0)),
                  pl.BlockSpec((1, Q, Q), lambda dd, ss: (dd, 0, 0)),
                  pl.BlockSpec((LANES, d), lambda dd, ss: (0, 0)),
                  pl.BlockSpec((d, LANES), lambda dd, ss: (0, 0)),
                  pl.BlockSpec((Q, d), lambda dd, ss: (cidx(dd, ss), 0)),
                  pl.BlockSpec((1, Q, d), lambda dd, ss: (dd, cidx(dd, ss), 0)),
                  pl.BlockSpec((1, 1, LANES, d), lambda dd, ss: (dd, cidx(dd, ss), 0, 0))],
        out_specs=[pl.BlockSpec((1, Q, 2 * d), lambda dd, ss: (dd, cidx(dd, ss), 0)),
                   pl.BlockSpec((Q, LANES), lambda dd, ss: (cidx(dd, ss), dd)),
                   pl.BlockSpec((1, 1, LANES), lambda dd, ss: (dd, 0, 0)),
                   pl.BlockSpec((1, 1, LANES), lambda dd, ss: (dd, 0, 0))],
        out_shape=[jax.ShapeDtypeStruct((2, t, 2 * d), F32), jax.ShapeDtypeStruct((t, 2 * LANES), BF16),
                   jax.ShapeDtypeStruct((2, 1, LANES), F32), jax.ShapeDtypeStruct((2, 1, LANES), F32)],
        scratch_shapes=[full((LANES, d)), full((Q, d)), full((Q, d)), half((Q, d)), half((Q, d)), half((Q, d)), half((Q, d)),
                        full((Q, d)), full((Q, d))],
        args=(xbc, z_dt, dtb, alog, tri, tri_t, ind_t, ind, d_y, y2, hs))


def _mix_common(zm_ref, y2_ref, xh_ref, dsk_ref, gv_ref, ws_ref, bst_ref, d):
    groups = d // LANES
    z_ssd, u, v, z_mlp = (zm_ref[:, k * d:(k + 1) * d].astype(F32) for k in range(4))
    y = y2_ref[0] + y2_ref[1] + dsk_ref[...] * xh_ref[...]
    sig_a = _sigmoid(z_ssd)
    ya_pre = y * (z_ssd * sig_a)
    r_v = _rms(v)
    vn = (v * r_v * gv_ref[...]).astype(BF16)
    sg = jnp.concatenate(
        [_dot(ws_ref[g].astype(BF16), vn[:, g * LANES:(g + 1) * LANES]) + bst_ref[:, g:g + 1] for g in range(groups)], axis=1)
    sig_m = _sigmoid(z_mlp)
    yb_pre = u * sg * (z_mlp * sig_m)
    return z_ssd, u, v, z_mlp, y, sig_a, ya_pre, r_v, vn, sg, sig_m, yb_pre


def _mix_fwd(z_mid, y2, xbc, dsk_row, g_ssd, g_v, g_mlp, w_s, b_st):
    t = z_mid.shape[0]
    d = z_mid.shape[1] // 4
    groups = d // LANES

    def body(zm_ref, y2_ref, xh_ref, dsk_ref, ga_ref, gv_ref, gm_ref, ws_ref, bst_ref, o_ref, ot_ref):
        (_, _, _, _, _, _, ya_pre, _, _, _, _, yb_pre) = _mix_common(zm_ref, y2_ref, xh_ref, dsk_ref, gv_ref, ws_ref, bst_ref, d)
        y_a = ya_pre * _rms(ya_pre) * ga_ref[...]
        y_b = yb_pre * _rms(yb_pre) * gm_ref[...]
        o_ref[:, :d] = y_a.astype(BF16)
        o_ref[:, d:] = y_b.astype(BF16)
        ot_ref[:d, :] = y_a.T.astype(BF16)
        ot_ref[d:, :] = y_b.T.astype(BF16)

    row = pl.BlockSpec((1, d), lambda i: (0, 0))
    return pl.pallas_call(
        body, name="mix_fwd", grid=(t // Q,),
        in_specs=[pl.BlockSpec((Q, 4 * d), lambda i: (i, 0)), pl.BlockSpec((2, Q, d), lambda i: (0, i, 0)),
                  pl.BlockSpec((Q, d), lambda i: (i, 0)), row, row, row, row,
                  pl.BlockSpec((groups, Q, Q), lambda i: (0, 0, 0)), pl.BlockSpec((Q, LANES), lambda i: (0, 0))],
        out_specs=[pl.BlockSpec((Q, 2 * d), lambda i: (i, 0)), pl.BlockSpec((2 * d, Q), lambda i: (0, i))],
        out_shape=[jax.ShapeDtypeStruct((t, 2 * d), BF16), jax.ShapeDtypeStruct((2 * d, t), BF16)], compiler_params=_params("arbitrary"),
    )(z_mid, y2, xbc, dsk_row, g_ssd, g_v, g_mlp, w_s, b_st)


def _mix_bwd(z_mid, y2, xbc, d_ycat, dsk_row, g_ssd, g_v, g_mlp, w_s, w_st, b_st, ind_head, ind_group):
    t = z_mid.shape[0]
    d = z_mid.shape[1] // 4
    groups = d // LANES
    nsteps = t // Q

    def body(zm_ref, y2_ref, xh_ref, dyc_ref, dsk_ref, ga_ref, gv_ref, gm_ref, ws_ref, wst_ref, bst_ref, ih_ref, ig_ref,
             dzm_ref, dy_ref, vec_ref, dws_ref, dbs_ref, dsk_acc, dsg_acc):
        i = pl.program_id(0)

        @pl.when(i == 0)
        def _():
            vec_ref[...] = jnp.zeros_like(vec_ref)
            dws_ref[...] = jnp.zeros_like(dws_ref)
            dsk_acc[...] = jnp.zeros_like(dsk_acc)
            dsg_acc[...] = jnp.zeros_like(dsg_acc)

        (z_ssd, u, v, z_mlp, y, sig_a, ya_pre, r_v, vn, sg, sig_m, yb_pre) = _mix_common(
            zm_ref, y2_ref, xh_ref, dsk_ref, gv_ref, ws_ref, bst_ref, d)
        d_ya = dyc_ref[:, :d].astype(F32)
        r_a = _rms(ya_pre)
        vec_ref[0:1, :] += jnp.sum(d_ya * (ya_pre * r_a), axis=0, keepdims=True)
        d_ya_pre = _rms_bwd(ya_pre, r_a, d_ya * ga_ref[...])
        d_y = d_ya_pre * (z_ssd * sig_a)
        dy_ref[...] = d_y
        dsk_acc[...] += jnp.sum(d_y * xh_ref[...], axis=0, keepdims=True)
        dzm_ref[:, 0:d] = (d_ya_pre * y * (sig_a * (1.0 + z_ssd * (1.0 - sig_a)))).astype(BF16)
        d_yb = dyc_ref[:, d:].astype(F32)
        r_b = _rms(yb_pre)
        vec_ref[2:3, :] += jnp.sum(d_yb * (yb_pre * r_b), axis=0, keepdims=True)
        d_yb_pre = _rms_bwd(yb_pre, r_b, d_yb * gm_ref[...])
        silu_m = z_mlp * sig_m
        dzm_ref[:, d:2 * d] = (d_yb_pre * sg * silu_m).astype(BF16)
        dzm_ref[:, 3 * d:4 * d] = (d_yb_pre * u * sg * (sig_m * (1.0 + z_mlp * (1.0 - sig_m)))).astype(BF16)
        d_sg = d_yb_pre * u * silu_m
        dsg_acc[...] += d_sg
        d_sgb = d_sg.astype(BF16)
        d_vn = []
        for g in range(groups):
            cols = slice(g * LANES, (g + 1) * LANES)
            dws_ref[g] += _dot(d_sgb[:, cols], vn[:, cols], NT)
            d_vn.append(_dot(wst_ref[g].astype(BF16), d_sgb[:, cols]))
        d_vn = jnp.concatenate(d_vn, axis=1)
        vec_ref[1:2, :] += jnp.sum(d_vn * (v * r_v), axis=0, keepdims=True)
        dzm_ref[:, 2 * d:3 * d] = _rms_bwd(v, r_v, d_vn * gv_ref[...]).astype(BF16)

        @pl.when(i == nsteps - 1)
        def _():
            vec_ref[3:4, 0:LANES] = _dot(dsk_acc[...], ih_ref[...], NN, HI)
            dbs_ref[...] = _dot(dsg_acc[...], ig_ref[...], NN, HI)

    row = pl.BlockSpec((1, d), lambda i: (0, 0))
    wsp = pl.BlockSpec((groups, Q, Q), lambda i: (0, 0, 0))
    ind = pl.BlockSpec((d, LANES), lambda i: (0, 0))
    return pl.pallas_call(
        body, name="mix_bwd", grid=(nsteps,),
        in_specs=[pl.BlockSpec((Q, 4 * d), lambda i: (i, 0)), pl.BlockSpec((2, Q, d), lambda i: (0, i, 0)),
                  pl.BlockSpec((Q, d), lambda i: (i, 0)), pl.BlockSpec((Q, 2 * d), lambda i: (i, 0)),
                  row, row, row, row, wsp, wsp, pl.BlockSpec((Q, LANES), lambda i: (0, 0)), ind, ind],
        out_specs=[pl.BlockSpec((Q, 4 * d), lambda i: (i, 0)), pl.BlockSpec((Q, d), lambda i: (i, 0)),
                   pl.BlockSpec((8, d), lambda i: (0, 0)), wsp, pl.BlockSpec((Q, LANES), lambda i: (0, 0))],
        out_shape=[jax.ShapeDtypeStruct((t, 4 * d), BF16), jax.ShapeDtypeStruct((t, d), F32),
                   jax.ShapeDtypeStruct((8, d), F32), jax.ShapeDtypeStruct((groups, Q, Q), F32),
                   jax.ShapeDtypeStruct((Q, LANES), F32)],
        scratch_shapes=[pltpu.VMEM((1, d), F32), pltpu.VMEM((Q, d), F32)],
        compiler_params=_params("arbitrary"),
    )(z_mid, y2, xbc, d_ycat, dsk_row, g_ssd, g_v, g_mlp, w_s, w_st, b_st, ind_head, ind_group)


def _ada_fwd(c16, w_ada, b_loc):
    depth, d, n = w_ada.shape
    tn = _tile(n, (512, 256, 128))

    def body(c_ref, w_ref, b_ref, o_ref):
        cv = c_ref[...]
        o_ref[0] = _dot(cv * _sigmoid(cv), w_ref[0], NN, HI) + b_ref[0]

    return pl.pallas_call(
        body, name="ada_fwd", grid=(depth, n // tn),
        in_specs=[pl.BlockSpec((16, d), lambda l, j: (0, 0)), pl.BlockSpec((1, d, tn), lambda l, j: (l, 0, j)),
                  pl.BlockSpec((1, 1, tn), lambda l, j: (l, 0, j))],
        out_specs=pl.BlockSpec((1, 16, tn), lambda l, j: (l, 0, j)),
        out_shape=jax.ShapeDtypeStruct((depth, 16, n), F32), compiler_params=_params("arbitrary", "arbitrary"),
    )(c16, w_ada, b_loc)


def _ada_bwd(c_t, dm_loc, w_ada):
    depth, d, n = w_ada.shape
    tn = _tile(n, (512, 256, 128))

    def body(s_ref, dm_ref, w_ref, gw_ref, dsc_ref):
        @pl.when((pl.program_id(0) == 0) & (pl.program_id(1) == 0))
        def _():
            dsc_ref[...] = jnp.zeros_like(dsc_ref)

        cv = s_ref[...]
        gw_ref[0] = _dot(cv * _sigmoid(cv), dm_ref[0], NN, HI)
        dsc_ref[...] += _dot(dm_ref[0, 8:16, :], w_ref[0], NT, HI)

    return pl.pallas_call(
        body, name="ada_bwd", grid=(depth, n // tn),
        in_specs=[pl.BlockSpec((d, LANES), lambda l, j: (0, 0)), pl.BlockSpec((1, LANES, tn), lambda l, j: (l, 0, j)),
                  pl.BlockSpec((1, d, tn), lambda l, j: (l, 0, j))],
        out_specs=[pl.BlockSpec((1, d, tn), lambda l, j: (l, 0, j)), pl.BlockSpec((8, d), lambda l, j: (0, 0))],
        out_shape=[jax.ShapeDtypeStruct((depth, d, n), F32), jax.ShapeDtypeStruct((8, d), F32)],
        compiler_params=_params("arbitrary", "arbitrary"),
    )(c_t, dm_loc, w_ada)


def _rowsum(x):
    depth, r, n = x.shape

    def body(x_ref, o_ref):
        o_ref[0] = jnp.sum(x_ref[0], axis=0, keepdims=True)

    return pl.pallas_call(
        body, name="rowsum", grid=(depth,),
        in_specs=[pl.BlockSpec((1, r, n), lambda l: (l, 0, 0))], out_specs=pl.BlockSpec((1, 1, n), lambda l: (l, 0, 0)),
        out_shape=jax.ShapeDtypeStruct((depth, 1, n), F32), compiler_params=_params("arbitrary"),
    )(x)


def _cctx_grad(d_scc, c_ctx_row):
    def body(g_ref, c_ref, o_ref):
        cv = c_ref[...]
        sig = _sigmoid(cv)
        o_ref[...] = g_ref[...] * (sig * (1.0 + cv * (1.0 - sig)))

    return pl.pallas_call(body, name="cctx_grad", out_shape=jax.ShapeDtypeStruct(c_ctx_row.shape, F32))(d_scc, c_ctx_row)


def _sum_lead(x, name):
    k, r, c = x.shape
    tr = _tile(r, [tt for tt in (1024, 512, 256, 128, 64, 32, 16, 8) if k * tt * c * x.dtype.itemsize <= SUM_BLOCK_BYTES])

    def body(x_ref, o_ref):
        acc = x_ref[0].astype(F32)
        for e in range(1, k):
            acc = acc + x_ref[e].astype(F32)
        o_ref[...] = acc

    return pl.pallas_call(
        body, name=name, grid=(r // tr,),
        in_specs=[pl.BlockSpec((k, tr, c), lambda i: (0, i, 0))], out_specs=pl.BlockSpec((tr, c), lambda i: (i, 0)),
        out_shape=jax.ShapeDtypeStruct((r, c), F32), compiler_params=_params("arbitrary"),
    )(x)


def _adamw(w, g, m, v, name, g2=None, side=None):
    r, c = w.shape
    tr = _tile(r, [tt for tt in (2048, 1024, 512, 256, 128, 64, 32, 16, 8) if tt * c * 4 <= ADAM_BLOCK_BYTES])
    two = g2 is not None
    bc1 = 1.0 - ADAM_B1 ** ADAM_STEP
    bc2 = 1.0 - ADAM_B2 ** ADAM_STEP

    def body(*refs):
        if two:
            w_ref, g_ref, g2_ref, m_ref, v_ref, go_ref, d_ref, mo_ref, vo_ref = refs
            gr = g_ref[...] + g2_ref[...]
        else:
            w_ref, g_ref, m_ref, v_ref, go_ref, d_ref, mo_ref, vo_ref = refs
            gr = g_ref[...]
        mn = ADAM_B1 * m_ref[...] + (1.0 - ADAM_B1) * gr
        vn = ADAM_B2 * v_ref[...] + (1.0 - ADAM_B2) * (gr * gr)
        go_ref[...] = gr
        mo_ref[...] = mn
        vo_ref[...] = vn
        d_ref[...] = -ADAM_LR * ((mn / bc1) / (jnp.sqrt(vn / bc2) + ADAM_EPS) + ADAM_WD * w_ref[...])

    blk = pl.BlockSpec((tr, c), lambda i: (i, 0))
    ins = (w, g, g2, m, v) if two else (w, g, m, v)
    return _hosted_call(body, side, name=name, grid=(r // tr,), in_specs=[blk] * len(ins), out_specs=[blk] * 4,
                        out_shape=[jax.ShapeDtypeStruct((r, c), F32)] * 4, scratch_shapes=[], args=ins)


def _flip(pos, k):
    x, y, c = pos
    return (x ^ ((k >> 2) & 1), y ^ ((k >> 1) & 1), c ^ (k & 1))


def _lin(pos):
    return 4 * pos[0] + 2 * pos[1] + pos[2]


def _chip(pos):
    return 2 * pos[0] + pos[1]


def _here():
    return (lax.axis_index("x"), lax.axis_index("y"), lax.axis_index("c"))


class _Exchange:
    def __init__(self, x, kind):
        self.x, self.kind = x, kind
        self.masks = {"gather4": (2, 4, 6), "scatter4": (2, 4, 6), "gather8": tuple(range(1, 8)), "swap": (1,)}[kind]
        self.slot = {"gather4": _chip, "scatter4": _chip, "gather8": _lin, "swap": None}[kind]
        lead = {"gather4": (4,), "scatter4": (), "gather8": (8,), "swap": ()}[kind]
        self.out_shape = jax.ShapeDtypeStruct(lead + x.shape, x.dtype)
        n = len(self.masks)
        self.scratch = [pltpu.SemaphoreType.DMA((n,)), pltpu.SemaphoreType.DMA((n,))] + ([] if kind == "swap" else [pltpu.SemaphoreType.DMA])

    def _copies(self, x_ref, o_ref, send, recv, *own, arrivals):
        me = _here()
        src = (lambda pos: x_ref.at[_chip(pos)]) if self.kind == "scatter4" else (lambda pos: x_ref)
        dst = (lambda pos: o_ref.at[self.slot(pos)]) if self.slot else (lambda pos: o_ref)
        local = [pltpu.make_async_copy(src(me), dst(me), own[0])] if own else []
        outs, ins = [], []
        for j, k in enumerate(self.masks):
            peer = _flip(me, k)
            sems = dict(send_sem=send.at[j], recv_sem=recv.at[j], device_id=peer, device_id_type=MESH)
            outs.append(pltpu.make_async_remote_copy(src_ref=src(peer), dst_ref=dst(me), **sems))
            if arrivals:
                ins.append(pltpu.make_async_remote_copy(src_ref=src(me), dst_ref=dst(peer), **sems))
        return local, outs, ins

    def start(self, *refs):
        local, outs, _ = self._copies(*refs, arrivals=False)
        for cp in local + outs:
            cp.start()

    def wait(self, *refs):
        local, outs, ins = self._copies(*refs, arrivals=True)
        for cp in ins:
            cp.wait_recv()
        for cp in outs:
            cp.wait_send()
        for cp in local:
            cp.wait()


def _exchange(x, kind, name):
    ex = _Exchange(x, kind)

    def body(*refs):
        ex.start(*refs)
        ex.wait(*refs)

    return pl.pallas_call(body, name=name, in_specs=[ANY], out_specs=ANY, out_shape=ex.out_shape, scratch_shapes=ex.scratch)(x)


def _gather4_two_level(x, name):
    half = x.shape[0] // 2

    def body(x_ref, o_ref, send1, recv1, send2, recv2, own):
        me = _here()
        sibling = _flip(me, 1)
        mine, theirs = pl.ds(me[2] * half, half), pl.ds((1 - me[2]) * half, half)
        peers = [_flip(me, k) for k in (2, 4, 6)]

        def over_ici(j, src_chip, rows, src=None):
            dst = o_ref.at[_chip(src_chip), rows]
            return pltpu.make_async_remote_copy(src_ref=dst if src is None else src, dst_ref=dst, send_sem=send1.at[j],
                                                recv_sem=recv1.at[j], device_id=peers[j], device_id_type=MESH)

        def over_d2d(j, rows):
            blk = o_ref.at[_chip(peers[j]), rows]
            return pltpu.make_async_remote_copy(src_ref=blk, dst_ref=blk, send_sem=send2.at[j], recv_sem=recv2.at[j],
                                                device_id=sibling, device_id_type=MESH)

        local = pltpu.make_async_copy(x_ref, o_ref.at[_chip(me)], own)
        local.start()
        sent = [over_ici(j, me, mine, src=x_ref.at[mine]) for j in range(3)]
        for cp in sent:
            cp.start()
        passed = [over_d2d(j, mine) for j in range(3)]
        for j in range(3):
            over_ici(j, peers[j], mine).wait_recv()
            passed[j].start()
        for j in range(3):
            over_d2d(j, theirs).wait_recv()
        for cp in sent + passed:
            cp.wait_send()
        local.wait()

    sems = pltpu.SemaphoreType.DMA((3,))
    return pl.pallas_call(body, name=name, in_specs=[ANY], out_specs=ANY, out_shape=jax.ShapeDtypeStruct((4,) + x.shape, x.dtype),
                          scratch_shapes=[sems, sems, sems, sems, pltpu.SemaphoreType.DMA])(x)


def _hosted_call(body, side, *, name, grid, in_specs, out_specs, out_shape, scratch_shapes, args):
    sides = [] if side is None else list(side) if isinstance(side, (list, tuple)) else [side]
    n_in, n_out, ns = len(in_specs), len(out_specs), len(sides)
    params = _params(*(["arbitrary"] * len(grid)))
    if not sides:
        return pl.pallas_call(body, name=name, grid=grid, in_specs=in_specs, out_specs=out_specs, out_shape=out_shape,
                              scratch_shapes=scratch_shapes, compiler_params=params)(*args)
    n_sem = [len(s.scratch) for s in sides]

    def hosted(*refs):
        ins, xs = refs[:n_in], refs[n_in:n_in + ns]
        outs, os_ = refs[n_in + ns:n_in + ns + n_out], refs[n_in + ns + n_out:n_in + 2 * ns + n_out]
        rest = refs[n_in + 2 * ns + n_out:]
        scratch, sems = rest[:len(rest) - sum(n_sem)], list(rest[len(rest) - sum(n_sem):])
        per_side = [[sems.pop(0) for _ in range(k)] for k in n_sem]
        ids = [pl.program_id(a) for a in range(len(grid))]
        first, last = ids[0] == 0, ids[0] == grid[0] - 1
        for a in range(1, len(grid)):
            first, last = first & (ids[a] == 0), last & (ids[a] == grid[a] - 1)

        @pl.when(first)
        def _():
            for s, x_ref, o_ref, sm in zip(sides, xs, os_, per_side):
                s.start(x_ref, o_ref, *sm)

        body(*ins, *outs, *scratch)

        @pl.when(last)
        def _():
            for s, x_ref, o_ref, sm in zip(sides, xs, os_, per_side):
                s.wait(x_ref, o_ref, *sm)

    return pl.pallas_call(hosted, name=name + "_x_" + "_".join(s.kind for s in sides), grid=grid, in_specs=list(in_specs) + [ANY] * ns,
                          out_specs=list(out_specs) + [ANY] * ns, out_shape=list(out_shape) + [s.out_shape for s in sides],
                          scratch_shapes=list(scratch_shapes) + [sem for s in sides for sem in s.scratch],
                          compiler_params=params)(*args, *[s.x for s in sides])


def _pad_lanes(a, width):
    return jnp.pad(a, [(0, 0)] * (a.ndim - 1) + [(0, width - a.shape[-1])])


def kernel(x, c, ctx, c_ctx, w_ada, b_ada, g_pre, g_post, w_in, conv_w, conv_b, dt_bias, a_log, d_skip, g_ssd, g_v, w_s, b_s, g_mlp, w_out, loss_target, m_c_ctx, m_w_ada, m_b_ada, m_g_pre, m_g_post, m_w_in, m_conv_w, m_conv_b, m_dt_bias, m_a_log, m_d_skip, m_g_ssd, m_g_v, m_w_s, m_b_s, m_g_mlp, m_w_out, v_c_ctx, v_w_ada, v_b_ada, v_g_pre, v_g_post, v_w_in, v_conv_w, v_conv_b, v_dt_bias, v_a_log, v_d_skip, v_g_ssd, v_g_v, v_w_s, v_b_s, v_g_mlp, v_w_out):
    depth, d = g_pre.shape
    seq, ctx_len = x.shape[1], ctx.shape[1]
    heads = d // HP
    in_w = 6 * d + 2 * heads
    groups_mlp = d // LANES
    t = ctx_len + seq
    assert ctx_len == TB and seq % TB == 0 and TB % ROW == 0 and heads % 4 == 0 and heads <= LANES and d % LANES == 0
    assert w_in.shape == (depth, d, in_w // 4)

    xi, yi, ci = lax.axis_index("x"), lax.axis_index("y"), lax.axis_index("c")
    chip = 2 * xi + yi
    me = 4 * xi + 2 * yi + ci

    n_ada = 3 * d // 4
    c_all = _exchange(c, "gather8", "ag_c")[:, 0, :]
    c16 = jnp.concatenate([c_all, c_ctx[None, :], jnp.zeros((7, d), F32)], axis=0)
    b_loc = lax.dynamic_slice_in_dim(b_ada, chip * n_ada, n_ada, axis=1)[:, None, :]
    mods_loc = _exchange(_ada_fwd(c16, w_ada, b_loc).reshape(depth * 16, n_ada), "gather8", "ag_mods")
    mods_loc = mods_loc.reshape(4, 2, depth, 16, n_ada)[:, 0]
    mods_full = jnp.moveaxis(mods_loc, 0, 2).reshape(depth, 16, 3 * d)
    mods_x = lax.dynamic_index_in_dim(mods_full, me, axis=1, keepdims=False).reshape(depth, 3, d)
    mods_c = mods_full[:, 8, :].reshape(depth, 3, d)
    mods = _pad_rows8(jnp.stack([mods_c, mods_x], axis=1))

    w_in_b, w_out_b = w_in.astype(BF16), w_out.astype(BF16)

    def lay_out(w_in_rows):
        def cols(lo, hi):
            pieces = []
            for qi in range(4):
                a, b = max(lo, qi * q_w) - qi * q_w, min(hi, (qi + 1) * q_w) - qi * q_w
                if a < b:
                    rows = [r[qi][:, a:b] for r in w_in_rows]
                    pieces.append(rows[0] if len(rows) == 1 else jnp.concatenate(rows, axis=0))
            return pieces[0] if len(pieces) == 1 else jnp.concatenate(pieces, axis=1)

        w_dt_l = jnp.concatenate([_pad_lanes(cols(2 * d, 2 * d + heads), LANES),
                                  _pad_lanes(cols(2 * d + heads, 2 * d + 2 * heads), LANES)], axis=1)
        return cols(0, 2 * d), cols(2 * d + 2 * heads, in_w), w_dt_l

    q_w = in_w // 4

    w_in_rows = [_gather4_two_level(w_in_b[0], "ag_w_in")]
    r_scan = 3 * d // 4
    conv_w_full = jnp.moveaxis(_exchange(conv_w, "gather4", "ag_conv_w"), 0, 2).reshape(depth, CONV_TAPS, 2 * d)
    conv_w8 = jnp.pad(conv_w_full, ((0, 0), (0, 8 - CONV_TAPS), (0, 0)))

    tri = jnp.stack([jnp.tril(jnp.ones((Q, Q), F32)), jnp.triu(jnp.ones((Q, Q), F32))])
    tri_t = jnp.swapaxes(tri, 1, 2)
    dtb = _pad_lanes(dt_bias, LANES)[:, :, None, :]
    alog = _pad_lanes(a_log, LANES)[:, :, None, :]
    dsk_row = jnp.repeat(d_skip, HP, axis=1)[:, None, :]
    w_st = jnp.swapaxes(w_s, 2, 3)
    b_st = _pad_lanes(jnp.swapaxes(b_s, 1, 2), LANES)
    chan = jnp.arange(d)
    ind_head = (chan[:, None] // HP == jnp.arange(LANES)[None, :]).astype(F32)
    ind_b, ind_t = ind_head.astype(BF16), ind_head.T.astype(BF16)
    ind_group = (chan[:, None] // LANES == jnp.arange(LANES)[None, :]).astype(F32)

    stream = jnp.concatenate([ctx[0], x[0]], axis=0)
    saved = []
    for l in range(depth):
        w_xbc, w_mid, w_dt = lay_out(w_in_rows)
        more = l + 1 < depth
        hx, hx_t = _pre_fwd(stream, g_pre[l][None], mods[l])
        z_xbc = _mm(hx, w_xbc, NN, "in_xbc")
        z_mid, w_out_all = _mm(hx, w_mid, NN, "in_mid", out_dtype=BF16, side=_Exchange(w_out_b[l], "gather4"))
        w_o = w_out_all.reshape(2 * d, d)
        z_dt = _mm(hx, w_dt, NN, "in_dt")
        xbc, dsilu = _conv_fwd(z_xbc, conv_w8[l], conv_b[l][None], ctx_len)
        y2, hs, *rows_a = _ssd_fwd(xbc, z_dt, dtb[l], alog[l], tri, ind_t, d, ctx_len,
                                   side=_Exchange(w_in_b[l + 1, :r_scan], "gather4") if more else None)
        ycat, ycat_t = _mix_fwd(z_mid, y2, xbc, dsk_row[l], g_ssd[l][None], g_v[l][None], g_mlp[l][None], w_s[l], b_st[l])
        if more:
            o, rows_b = _mm(ycat, w_o, NN, "out_proj", side=_Exchange(w_in_b[l + 1, r_scan:], "gather4"))
            w_in_rows = [rows_a[0], rows_b]
        else:
            o = _mm(ycat, w_o, NN, "out_proj")
        saved.append((stream, hx_t, z_xbc, dsilu, z_mid, z_dt, xbc, y2, hs, ycat_t, o, w_xbc, w_mid, w_dt, w_o))
        stream = _post_fwd(o, stream, g_post[l][None], mods[l])

    sq, d_stream = _loss_grad(stream, loss_target[0])
    loss = lax.psum(0.5 / d * sq[0, 0], ("x", "y", "c"))

    small = []
    dmods = []
    q_in = in_w // 4
    names = ["g_post", "conv_w", "conv_b", "dt_bias", "a_log", "d_skip", "g_ssd", "g_v", "w_s", "b_s", "g_mlp"]

    def rows_of(a):
        return -(-a.size // (8 * LANES)) * 8

    def pack(arrays):
        blocks = [jnp.pad(a.reshape(-1), (0, rows_of(a) * LANES - a.size)).reshape(rows_of(a), LANES) for a in arrays]
        rows = sum(b.shape[0] for b in blocks)
        return jnp.pad(jnp.concatenate(blocks, axis=0), ((0, -(-rows // PACK_ROWS) * PACK_ROWS - rows), (0, 0)))

    def unpack(block, likes):
        out, row = [], 0
        for a in likes:
            out.append(block[row:row + rows_of(a)].reshape(-1)[:a.size].reshape(a.shape))
            row += rows_of(a)
        return out

    def quarter_parts(g_xbc, g_mid, g_dt):
        segs = [(g_xbc, 0, 2 * d), (g_dt, 0, heads), (g_dt, LANES, heads), (g_mid, 0, 4 * d)]
        parts = []
        for qi in range(4):
            lo, hi, off, pieces = qi * q_in, (qi + 1) * q_in, 0, []
            for arr, start, width in segs:
                a, b = max(lo, off), min(hi, off + width)
                if a < b:
                    pieces.append(arr[:, start + a - off:start + b - off])
                off += width
            parts.append(pieces[0] if len(pieces) == 1 else jnp.concatenate(pieces, axis=1))
        return jnp.stack(parts)

    sum_in, sum_out = [None] * depth, [None] * depth
    swap_in, swap_out = [None] * depth, [None] * depth
    small_sum = [None] * depth
    parts_in = packed = None
    for l in reversed(range(depth)):
        x_in, hx_t, z_xbc, dsilu, z_mid, z_dt, xbc, y2, hs, ycat_t, o, w_xbc, w_mid, w_dt, w_o = saved[l]
        up = l + 1
        d_o, acc_post = _post_bwd(d_stream, o, g_post[l][None], mods[l])
        if packed is not None:
            d_ycat, gathered = _mm(d_o, w_o, NT, "d_ycat", out_dtype=BF16, side=_Exchange(packed, "gather8"))
            small_sum[up] = _sum_lead(gathered, "sum_small")
        else:
            d_ycat = _mm(d_o, w_o, NT, "d_ycat", out_dtype=BF16)
        g_out = _mm(ycat_t, d_o, NN, "dw_out", out_dtype=BF16)
        dz_mid, d_y, vec, d_ws, d_bs = _mix_bwd(z_mid, y2, xbc, d_ycat, dsk_row[l], g_ssd[l][None], g_v[l][None], g_mlp[l][None],
                                                w_s[l], w_st[l], b_st[l], ind_head, ind_group)
        d_xbc2, dz_dt, d_bias, d_alog, *got = _ssd_bwd(xbc, z_dt, dtb[l], alog[l], tri, tri_t, ind_t, ind_b, d_y, y2, hs, d, ctx_len,
                                                       side=_Exchange(parts_in, "scatter4") if parts_in is not None else None)
        if parts_in is not None:
            sum_in[up] = _sum_lead(got[0], "sum_w_in")
        dz_xbc, d_cw, d_cb = _conv_bwd(z_xbc, dsilu, d_xbc2, d_y, dsk_row[l], conv_w8[l], ctx_len)
        g_xbc, got_out = _mm(hx_t, dz_xbc, NN, "dw_xbc", out_dtype=BF16, side=_Exchange(g_out.reshape(4, 2 * d // 4, d), "scatter4"))
        sum_out[l] = _sum_lead(got_out, "sum_w_out")
        swaps = [_Exchange(sum_out[l], "swap")] + ([_Exchange(sum_in[up], "swap")] if parts_in is not None else [])
        g_mid, swap_out[l], *swapped = _mm(hx_t, dz_mid, NN, "dw_mid", out_dtype=BF16, side=swaps)
        if parts_in is not None:
            swap_in[up] = swapped[0]
        g_dt = _mm(hx_t, dz_dt, NN, "dw_dt", out_dtype=BF16)
        parts_in = quarter_parts(g_xbc, g_mid, g_dt)
        small.append(dict(
            g_post=acc_post[0, 1] + acc_post[1, 1], conv_w=d_cw[:CONV_TAPS], conv_b=d_cb[0],
            dt_bias=d_bias[:, 0, :heads], a_log=d_alog[:, 0, :heads], d_skip=vec[3, :heads], g_ssd=vec[0], g_v=vec[1],
            w_s=d_ws, b_s=d_bs[:, :groups_mlp].T, g_mlp=vec[2]))
        packed = pack([small[-1][n] for n in names])
        if l == 0:
            r_a = max(LANES, d // 3 // LANES * LANES)
            d_hx, got_a = _mm(dz_xbc, w_xbc, NT, "dhx_xbc", side=_Exchange(parts_in[:, :r_a], "scatter4"))
            d_hx, got_b = _mm(dz_mid, w_mid, NT, "dhx_mid", acc=d_hx, side=_Exchange(parts_in[:, r_a:], "scatter4"))
            sum_in[0] = _sum_lead(jnp.concatenate([got_a, got_b], axis=1), "sum_w_in")
        else:
            d_hx = _mm(dz_xbc, w_xbc, NT, "dhx_xbc")
            d_hx = _mm(dz_mid, w_mid, NT, "dhx_mid", acc=d_hx)
        d_hx = _mm(dz_dt, w_dt, NT, "dhx_dt", acc=d_hx)
        if l == 0:
            d_stream, acc_pre, swap_in[0], gathered = _pre_bwd(
                x_in, d_hx, d_stream, g_pre[l][None], mods[l], latent_only=True,
                side=[_Exchange(sum_in[0], "swap"), _Exchange(packed, "gather8")])
            small_sum[0] = _sum_lead(gathered, "sum_small")
        else:
            d_stream, acc_pre = _pre_bwd(x_in, d_hx, d_stream, g_pre[l][None], mods[l])
        dmods.append(jnp.concatenate([acc_pre[:, 0], acc_pre[:, 1], acc_post[:, 0], acc_pre[:, 2]], axis=1))
    small.reverse(), dmods.reverse()
    grad_x = d_stream[None]

    weights = dict(c_ctx=c_ctx, w_ada=w_ada, b_ada=b_ada, g_pre=g_pre, g_post=g_post, w_in=w_in, conv_w=conv_w, conv_b=conv_b,
                   dt_bias=dt_bias, a_log=a_log, d_skip=d_skip, g_ssd=g_ssd, g_v=g_v, w_s=w_s, b_s=b_s, g_mlp=g_mlp, w_out=w_out)
    m_in = dict(c_ctx=m_c_ctx, w_ada=m_w_ada, b_ada=m_b_ada, g_pre=m_g_pre, g_post=m_g_post, w_in=m_w_in, conv_w=m_conv_w,
                conv_b=m_conv_b, dt_bias=m_dt_bias, a_log=m_a_log, d_skip=m_d_skip, g_ssd=m_g_ssd, g_v=m_g_v, w_s=m_w_s,
                b_s=m_b_s, g_mlp=m_g_mlp, w_out=m_w_out)
    v_in = dict(c_ctx=v_c_ctx, w_ada=v_w_ada, b_ada=v_b_ada, g_pre=v_g_pre, g_post=v_g_post, w_in=v_w_in, conv_w=v_conv_w,
                conv_b=v_conv_b, dt_bias=v_dt_bias, a_log=v_a_log, d_skip=v_d_skip, g_ssd=v_g_ssd, g_v=v_g_v, w_s=v_w_s,
                b_s=v_b_s, g_mlp=v_g_mlp, w_out=v_w_out)
    order = list(weights)
    results = {}

    def adamw_big(n, ga, gb):
        shp = weights[n].shape
        two = lambda a: a.reshape(-1, shp[-1])
        results[n] = [r.reshape(shp) for r in _adamw(two(weights[n]), ga, two(m_in[n]), two(v_in[n]), "adamw_" + n, g2=gb)]

    adamw_big("w_in", jnp.concatenate(sum_in, axis=0), jnp.concatenate(swap_in, axis=0))
    adamw_big("w_out", jnp.concatenate(sum_out, axis=0), jnp.concatenate(swap_out, axis=0))
    per_layer = [unpack(small_sum[l], [small[l][n] for n in names]) for l in range(depth)]
    grads = {n: jnp.stack([per_layer[l][j] for l in range(depth)]) for j, n in enumerate(names)}
    grads["conv_w"] = lax.dynamic_slice_in_dim(grads["conv_w"], chip * (2 * d // 4), 2 * d // 4, axis=2)

    dm_all = _exchange(jnp.stack(dmods).reshape(depth * 2, 4 * d), "gather8", "ag_dmods").reshape(8, depth, 2, 4 * d)
    grads["g_pre"] = _sum_lead(jnp.moveaxis(dm_all[..., 3 * d:], 2, 1).reshape(16, depth, d), "sum_g_pre")
    dm_ctx = _sum_lead(dm_all[:, :, 0, :3 * d], "sum_dm_ctx")
    dm16 = jnp.concatenate([jnp.moveaxis(dm_all[:, :, 1, :3 * d], 0, 1), dm_ctx[:, None, :], jnp.zeros((depth, 7, 3 * d), F32)], axis=1)
    grads["b_ada"] = _rowsum(dm16)[:, 0, :]
    dm_loc = jnp.pad(lax.dynamic_slice_in_dim(dm16, chip * n_ada, n_ada, axis=2), ((0, 0), (0, LANES - 16), (0, 0)))
    c_t = jnp.pad(c16.T, ((0, 0), (0, LANES - 16)))
    g_w_ada, d_scc_part = _ada_bwd(c_t, dm_loc, w_ada)
    adamw_big("w_ada", g_w_ada.reshape(depth * d, n_ada), None)
    d_scc = _sum_lead(_exchange(d_scc_part, "gather8", "ag_dscc").reshape(4, 2, 8, d)[:, 0], "sum_dscc")
    grads["c_ctx"] = _cctx_grad(d_scc[0:1], c_ctx[None])[0]

    rest = [n for n in order if n not in results]
    outs = _adamw(pack([weights[n] for n in rest]), pack([grads[n] for n in rest]), pack([m_in[n] for n in rest]),
                  pack([v_in[n] for n in rest]), "adamw_small")
    for j, res in enumerate(zip(*[unpack(o_, [weights[n] for n in rest]) for o_ in outs])):
        results[rest[j]] = list(res)

    return (loss, grad_x, *[results[n][0] for n in order], *[results[n][1] for n in order],
            *[results[n][2] for n in order], *[results[n][3] for n in order])


def _pad_rows8(a):
    return jnp.pad(a, [(0, 0)] * (a.ndim - 2) + [(0, 8 - a.shape[-2]), (0, 0)])
```

```python
import jax
import jax.numpy as jnp
from jax import lax
from jax.experimental import pallas as pl
from jax.experimental.pallas import tpu as pltpu

F32 = jnp.float32
BF16 = jnp.bfloat16
EPS = 1e-6
Q = 128
TB = 256
ROW = 64
HP = 64
LANES = 128
CONV_TAPS = 5
VMEM_LIMIT = 48 * 1024 * 1024
HI = lax.Precision.HIGHEST
SUM_BLOCK_BYTES = 4 * 1024 * 1024
ADAM_BLOCK_BYTES = 1024 * 1024
PACK_ROWS = 256
MESH = pl.DeviceIdType.MESH
ANY = pl.BlockSpec(memory_space=pl.ANY)

ADAM_LR, ADAM_B1, ADAM_B2, ADAM_EPS, ADAM_WD, ADAM_STEP = 0.001, 0.9, 0.999, 1e-08, 0.01, 10

NN = (((1,), (0,)), ((), ()))
NT = (((1,), (1,)), ((), ()))
TN = (((0,), (0,)), ((), ()))


def _dot(a, b, dims=NN, prec=None):
    return lax.dot_general(a, b, dims, precision=prec, preferred_element_type=F32)


def _params(*sem):
    if sem:
        return pltpu.CompilerParams(vmem_limit_bytes=VMEM_LIMIT, dimension_semantics=sem)
    return pltpu.CompilerParams(vmem_limit_bytes=VMEM_LIMIT)


def _tile(dim, cands):
    for t in cands:
        if dim % t == 0:
            return t
    return dim


def _sigmoid(x):
    return 1.0 / (1.0 + jnp.exp(-x))


def _softplus(x):
    e = jnp.exp(-jnp.abs(x))
    u = 1.0 + e
    um1 = u - 1.0
    l1p = jnp.where(um1 == 0.0, e, jnp.log(u) * (e / jnp.where(um1 == 0.0, 1.0, um1)))
    return jnp.maximum(x, 0.0) + l1p


def _rms(x):
    return lax.rsqrt(jnp.mean(x * x, axis=-1, keepdims=True) + EPS)


def _rms_bwd(x, r, t):
    return r * t - x * (r * r * r) * jnp.mean(x * t, axis=-1, keepdims=True)


def _mm(a, b, dims, name, acc=None, out_dtype=F32, side=None):
    (ca,), (cb,) = dims[0]
    m, k = a.shape[1 - ca], a.shape[ca]
    n = b.shape[1 - cb]
    tk = k if k <= 2048 else _tile(k, (2048, 768, 512, 384, 256, 128))
    nk = k // tk
    tm = _tile(m, (2048, 1024, 768, 512, 384, 256, 128) if nk > 1 else (1024, 768, 512, 384, 256, 128))
    tn = _tile(n, (1024, 512, 256, 128))
    a_spec = pl.BlockSpec((tm, tk), lambda i, j, kk: (i, kk)) if ca == 1 else pl.BlockSpec((tk, tm), lambda i, j, kk: (kk, i))
    b_spec = pl.BlockSpec((tk, tn), lambda i, j, kk: (kk, j)) if cb == 0 else pl.BlockSpec((tn, tk), lambda i, j, kk: (j, kk))
    o_spec = pl.BlockSpec((tm, tn), lambda i, j, kk: (i, j))
    has_acc = acc is not None

    def body(*refs):
        if has_acc:
            a_ref, b_ref, c_ref, o_ref, acc_ref = refs
        else:
            a_ref, b_ref, o_ref, acc_ref = refs
        kk = pl.program_id(2)

        @pl.when(kk == 0)
        def _():
            acc_ref[...] = c_ref[...] if has_acc else jnp.zeros_like(acc_ref)

        acc_ref[...] += _dot(a_ref[...], b_ref[...], dims)

        @pl.when(kk == nk - 1)
        def _():
            o_ref[...] = acc_ref[...].astype(out_dtype)

    res = _hosted_call(
        body, side, name=name, grid=(m // tm, n // tn, nk),
        in_specs=[a_spec, b_spec] + ([o_spec] if has_acc else []),
        out_specs=[o_spec], out_shape=[jax.ShapeDtypeStruct((m, n), out_dtype)],
        scratch_shapes=[pltpu.VMEM((tm, tn), F32)], args=(a, b, acc) if has_acc else (a, b))
    return res[0] if side is None else res


def _which(i):
    return jnp.minimum(i, 1)


def _pre_fwd(x, g_pre, mods):
    t, d = x.shape

    def body(x_ref, g_ref, m_ref, o_ref, ot_ref):
        xb = x_ref[...]
        xn = xb * _rms(xb) * g_ref[...]
        hx = xn * (1.0 + m_ref[0, 1:2, :]) + m_ref[0, 0:1, :]
        o_ref[...] = hx.astype(BF16)
        ot_ref[...] = hx.T.astype(BF16)

    return pl.pallas_call(
        body, name="pre_fwd", grid=(t // TB,),
        in_specs=[pl.BlockSpec((TB, d), lambda i: (i, 0)), pl.BlockSpec((1, d), lambda i: (0, 0)),
                  pl.BlockSpec((1, 8, d), lambda i: (_which(i), 0, 0))],
        out_specs=[pl.BlockSpec((TB, d), lambda i: (i, 0)), pl.BlockSpec((d, TB), lambda i: (0, i))],
        out_shape=[jax.ShapeDtypeStruct((t, d), BF16), jax.ShapeDtypeStruct((d, t), BF16)], compiler_params=_params("arbitrary"),
    )(x, g_pre, mods)


def _pre_bwd(x, d_hx, d_up, g_pre, mods, side=None, latent_only=False):
    t, d = x.shape
    dx_rows = t - TB if latent_only else t
    dx_spec = pl.BlockSpec((TB, d), (lambda i: (jnp.maximum(i - 1, 0), 0)) if latent_only else (lambda i: (i, 0)))

    def body(x_ref, dh_ref, du_ref, g_ref, m_ref, dx_ref, acc_ref):
        i = pl.program_id(0)

        @pl.when(i <= 1)
        def _():
            acc_ref[...] = jnp.zeros_like(acc_ref)

        def strip(s, carry):
            rows = pl.ds(pl.multiple_of(s * 8, 8), 8)
            xb = x_ref[rows, :]
            dh = dh_ref[rows, :]
            r = _rms(xb)
            xr = xb * r
            d_xn = dh * (1.0 + m_ref[0, 1:2, :])
            dx_ref[rows, :] = du_ref[rows, :] + _rms_bwd(xb, r, d_xn * g_ref[...])
            acc_ref[0, 0:1, :] += jnp.sum(dh, axis=0, keepdims=True)
            acc_ref[0, 1:2, :] += jnp.sum(dh * (xr * g_ref[...]), axis=0, keepdims=True)
            acc_ref[0, 2:3, :] += jnp.sum(d_xn * xr, axis=0, keepdims=True)
            return carry

        lax.fori_loop(0, TB // 8, strip, 0)

    blk = pl.BlockSpec((TB, d), lambda i: (i, 0))
    return _hosted_call(
        body, side, name="pre_bwd", grid=(t // TB,),
        in_specs=[blk, blk, blk, pl.BlockSpec((1, d), lambda i: (0, 0)),
                  pl.BlockSpec((1, 8, d), lambda i: (_which(i), 0, 0))],
        out_specs=[dx_spec, pl.BlockSpec((1, 8, d), lambda i: (_which(i), 0, 0))],
        out_shape=[jax.ShapeDtypeStruct((dx_rows, d), F32), jax.ShapeDtypeStruct((2, 8, d), F32)],
        scratch_shapes=[], args=(x, d_hx, d_up, g_pre, mods))


def _post_fwd(o, x, g_post, mods):
    t, d = x.shape

    def body(o_ref, x_ref, g_ref, m_ref, y_ref):
        ob = o_ref[...]
        y_ref[...] = x_ref[...] + m_ref[0, 2:3, :] * (ob * _rms(ob) * g_ref[...])

    blk = pl.BlockSpec((TB, d), lambda i: (i, 0))
    return pl.pallas_call(
        body, name="post_fwd", grid=(t // TB,),
        in_specs=[blk, blk, pl.BlockSpec((1, d), lambda i: (0, 0)), pl.BlockSpec((1, 8, d), lambda i: (_which(i), 0, 0))],
        out_specs=blk, out_shape=jax.ShapeDtypeStruct((t, d), F32), compiler_params=_params("arbitrary"),
    )(o, x, g_post, mods)


def _post_bwd(d_y, o, g_post, mods):
    t, d = o.shape

    def body(dy_ref, o_ref, g_ref, m_ref, do_ref, acc_ref):
        i = pl.program_id(0)

        @pl.when(i <= 1)
        def _():
            acc_ref[...] = jnp.zeros_like(acc_ref)

        def strip(s, carry):
            rows = pl.ds(pl.multiple_of(s * 16, 16), 16)
            ob = o_ref[rows, :]
            dy = dy_ref[rows, :]
            r = _rms(ob)
            orr = ob * r
            d_out = dy * m_ref[0, 2:3, :]
            do_ref[rows, :] = _rms_bwd(ob, r, d_out * g_ref[...]).astype(BF16)
            acc_ref[0, 0:1, :] += jnp.sum(dy * (orr * g_ref[...]), axis=0, keepdims=True)
            acc_ref[0, 1:2, :] += jnp.sum(d_out * orr, axis=0, keepdims=True)
            return carry

        lax.fori_loop(0, TB // 16, strip, 0)

    blk = pl.BlockSpec((TB, d), lambda i: (i, 0))
    return pl.pallas_call(
        body, name="post_bwd", grid=(t // TB,),
        in_specs=[blk, blk, pl.BlockSpec((1, d), lambda i: (0, 0)), pl.BlockSpec((1, 8, d), lambda i: (_which(i), 0, 0))],
        out_specs=[blk, pl.BlockSpec((1, 8, d), lambda i: (_which(i), 0, 0))],
        out_shape=[jax.ShapeDtypeStruct((t, d), BF16), jax.ShapeDtypeStruct((2, 8, d), F32)],
        compiler_params=_params("arbitrary"),
    )(d_y, o, g_post, mods)


def _loss_grad(xf, target):
    t, d = xf.shape

    def body(x_ref, t_ref, loss_ref, dx_ref):
        i = pl.program_id(0)

        @pl.when(i == 0)
        def _():
            loss_ref[...] = jnp.zeros_like(loss_ref)
            dx_ref[...] = jnp.zeros_like(dx_ref)

        @pl.when(i > 0)
        def _():
            err = x_ref[...] - t_ref[...]
            loss_ref[...] += jnp.sum(err * err).reshape(1, 1)
            dx_ref[...] = err * (1.0 / d)

    return pl.pallas_call(
        body, name="loss_grad", grid=(t // TB,),
        in_specs=[pl.BlockSpec((TB, d), lambda i: (i, 0)), pl.BlockSpec((TB, d), lambda i: (jnp.maximum(i - 1, 0), 0))],
        out_specs=[pl.BlockSpec((1, 1), lambda i: (0, 0)), pl.BlockSpec((TB, d), lambda i: (i, 0))],
        out_shape=[jax.ShapeDtypeStruct((1, 1), F32), jax.ShapeDtypeStruct((t, d), F32)],
        compiler_params=_params("arbitrary"),
    )(xf, target)


def _conv_terms(zb, pos, row_len):
    out = []
    for k in range(CONV_TAPS):
        o = k - CONV_TAPS // 2
        sh = zb if o == 0 else pltpu.roll(zb, (-o) % TB, 0)
        out.append(jnp.where((pos + o >= 0) & (pos + o < row_len), sh, 0.0))
    return out


def _row_pos(i, ctx_len):
    row_len = jnp.where(i == 0, ctx_len, ROW)
    pos = lax.broadcasted_iota(jnp.int32, (TB, 1), 0) & (row_len - 1)
    return pos, row_len


def _conv_fwd(z_xbc, conv_w8, conv_b, ctx_len):
    t, c = z_xbc.shape
    tc = _tile(c, (2048, 1024, 512, 256, 128))

    def body(z_ref, w_ref, b_ref, o_ref, ds_ref):
        pos, row_len = _row_pos(pl.program_id(1), ctx_len)
        terms = _conv_terms(z_ref[...], pos, row_len)
        pre = b_ref[...]
        for k in range(CONV_TAPS):
            pre = pre + terms[k] * w_ref[k:k + 1, :]
        sig = _sigmoid(pre)
        o_ref[...] = pre * sig
        ds_ref[...] = sig * (1.0 + pre * (1.0 - sig))

    blk = pl.BlockSpec((TB, tc), lambda j, i: (i, j))
    return pl.pallas_call(
        body, name="conv_fwd", grid=(c // tc, t // TB),
        in_specs=[blk, pl.BlockSpec((8, tc), lambda j, i: (0, j)), pl.BlockSpec((1, tc), lambda j, i: (0, j))],
        out_specs=[blk, blk], out_shape=[jax.ShapeDtypeStruct((t, c), F32)] * 2, compiler_params=_params("arbitrary", "arbitrary"),
    )(z_xbc, conv_w8, conv_b)


def _conv_bwd(z_xbc, dsilu, d_xbc2, d_y, d_skip_row, conv_w8, ctx_len):
    t, c = z_xbc.shape
    d = d_y.shape[1]
    tc = _tile(d, (2048, 1024, 512, 256, 128))
    nskip = d // tc

    def body(z_ref, dsl_ref, g2_ref, dy_ref, ds_ref, w_ref, dz_ref, dw_ref, db_ref):
        j, i = pl.program_id(0), pl.program_id(1)

        @pl.when(i == 0)
        def _():
            dw_ref[...] = jnp.zeros_like(dw_ref)
            db_ref[...] = jnp.zeros_like(db_ref)

        pos, row_len = _row_pos(i, ctx_len)
        skip = jnp.where(j < nskip, 1.0, 0.0) * ds_ref[...]
        d_pre = (g2_ref[0] + g2_ref[1] + dy_ref[...] * skip) * dsl_ref[...]
        db_ref[...] += jnp.sum(d_pre, axis=0, keepdims=True)
        zb = z_ref[...]
        dz = jnp.zeros_like(d_pre)
        for k in range(CONV_TAPS):
            o = k - CONV_TAPS // 2
            sh = d_pre if o == 0 else pltpu.roll(d_pre, o % TB, 0)
            sh = jnp.where((pos - o >= 0) & (pos - o < row_len), sh, 0.0)
            dw_ref[k:k + 1, :] += jnp.sum(sh * zb, axis=0, keepdims=True)
            dz = dz + sh * w_ref[k:k + 1, :]
        dz_ref[...] = dz.astype(BF16)

    jd = lambda j: jnp.minimum(j, nskip - 1)
    blk = pl.BlockSpec((TB, tc), lambda j, i: (i, j))
    return pl.pallas_call(
        body, name="conv_bwd", grid=(c // tc, t // TB),
        in_specs=[blk, blk, pl.BlockSpec((2, TB, tc), lambda j, i: (0, i, j)),
                  pl.BlockSpec((TB, tc), lambda j, i: (i, jd(j))), pl.BlockSpec((1, tc), lambda j, i: (0, jd(j))),
                  pl.BlockSpec((8, tc), lambda j, i: (0, j))],
        out_specs=[blk, pl.BlockSpec((8, tc), lambda j, i: (0, j)), pl.BlockSpec((1, tc), lambda j, i: (0, j))],
        out_shape=[jax.ShapeDtypeStruct((t, c), BF16), jax.ShapeDtypeStruct((8, c), F32), jax.ShapeDtypeStruct((1, c), F32)],
        compiler_params=_params("arbitrary", "arbitrary"),
    )(z_xbc, dsilu, d_xbc2, d_y, d_skip_row, conv_w8)


def _scan_chunk(dirn, s, nch, ncc):
    bwd = jnp.where(s < ncc, ncc - 1 - s, nch - 1 - (s - ncc))
    return jnp.where(dirn == 0, s, bwd)


def _ssd_decays(dt_ref, dtb_ref, alog_ref, tri):
    raw = dt_ref[...] + dtb_ref[0]
    dt = _softplus(raw)
    a_neg = -jnp.exp(alog_ref[0])
    a = dt * a_neg
    s = _dot(tri, a, NN, HI)
    stot = jnp.sum(a, axis=0, keepdims=True)
    return raw, dt, a_neg, s, stot, s.T


def _split(v):
    hi = v.astype(BF16)
    return hi, (v - hi.astype(F32)).astype(BF16)


def _expand(v, indt_ref):
    hi, lo = _split(v)
    return _dot(hi, indt_ref[...]) + _dot(lo, indt_ref[...])


def _head_sums(v, ind_ref):
    hi, lo = _split(v)
    return _dot(hi, ind_ref[...]) + _dot(lo, ind_ref[...])


def _ssd_fwd(xbc, z_dt, dtb, alog, tri, ind_t, d, ctx_len, side=None):
    t = xbc.shape[0]
    nch, ncc = t // Q, ctx_len // Q
    heads = d // HP
    groups = heads // 4
    gn = groups * LANES

    def body(xbc_ref, dt_ref, dtb_ref, alog_ref, tri_ref, indt_ref, y_ref, hs_ref, h_scr, xdb_scr, xde_scr, esx_scr):
        @pl.when(pl.program_id(1) == 0)
        def _():
            h_scr[...] = jnp.zeros_like(h_scr)

        tri = tri_ref[0]
        mask = tri > 0.0
        _, dt, _, s, stot, s_t = _ssd_decays(dt_ref, dtb_ref, alog_ref, tri)
        esx_scr[...] = _expand(jnp.exp(s), indt_ref)
        etot_x = _expand(jnp.broadcast_to(jnp.exp(stot), (8, LANES)), indt_ref)[0:1]
        xd = xbc_ref[:, :d] * _expand(dt, indt_ref)
        xdb_scr[...] = xd.astype(BF16)
        xde_scr[...] = (xd * _expand(jnp.exp(stot - s), indt_ref)).astype(BF16)
        left = lax.broadcasted_iota(jnp.int32, (Q, LANES), 1) < HP
        hs_ref[0, 0] = h_scr[...]
        for g in range(groups):
            b32 = xbc_ref[:, d + g * LANES:d + (g + 1) * LANES]
            bb = b32.astype(BF16)
            bbt = b32.T.astype(BF16)
            cb = xbc_ref[:, d + gn + g * LANES:d + gn + (g + 1) * LANES].astype(BF16)
            cbt = _dot(cb, bb, NT)
            gcols = slice(4 * g * HP, 4 * (g + 1) * HP)
            hg = h_scr[:, gcols]
            y_off = _dot(cb, hg.astype(BF16)) * esx_scr[:, gcols]
            h_scr[:, gcols] = hg * etot_x[:, gcols] + _dot(bbt, xde_scr[:, gcols])
            for j, pr in enumerate((2 * g, 2 * g + 1)):
                h0 = 2 * pr
                cols = slice(pr * LANES, (pr + 1) * LANES)
                xdb = xdb_scr[:, cols]
                res = []
                for h in (h0, h0 + 1):
                    lm = jnp.exp(jnp.where(mask, s[:, h:h + 1] - s_t[h:h + 1, :], -jnp.inf))
                    res.append(_dot((cbt * lm).astype(BF16), xdb))
                y_ref[0, :, cols] = jnp.where(left, res[0], res[1]) + y_off[:, j * LANES:(j + 1) * LANES]

    cidx = lambda dd, ss: _scan_chunk(dd, ss, nch, ncc)
    return _hosted_call(
        body, side, name="ssd_fwd", grid=(2, nch),
        in_specs=[pl.BlockSpec((Q, 2 * d), lambda dd, ss: (cidx(dd, ss), 0)),
                  pl.BlockSpec((Q, LANES), lambda dd, ss: (cidx(dd, ss), dd)),
                  pl.BlockSpec((1, 1, LANES), lambda dd, ss: (dd, 0, 0)),
                  pl.BlockSpec((1, 1, LANES), lambda dd, ss: (dd, 0, 0)),
                  pl.BlockSpec((1, Q, Q), lambda dd, ss: (dd, 0, 0)),
                  pl.BlockSpec((LANES, d), lambda dd, ss: (0, 0))],
        out_specs=[pl.BlockSpec((1, Q, d), lambda dd, ss: (dd, cidx(dd, ss), 0)),
                   pl.BlockSpec((1, 1, LANES, d), lambda dd, ss: (dd, cidx(dd, ss), 0, 0))],
        out_shape=[jax.ShapeDtypeStruct((2, t, d), F32), jax.ShapeDtypeStruct((2, nch, LANES, d), F32)],
        scratch_shapes=[pltpu.VMEM((LANES, d), F32), pltpu.VMEM((Q, d), BF16), pltpu.VMEM((Q, d), BF16), pltpu.VMEM((Q, d), F32)],
        args=(xbc, z_dt, dtb, alog, tri, ind_t))


def _ssd_bwd(xbc, z_dt, dtb, alog, tri, tri_t, ind_t, ind, d_y, y2, hs, d, ctx_len, side=None):
    t = xbc.shape[0]
    nch, ncc = t // Q, ctx_len // Q
    heads = d // HP
    groups = heads // 4
    gn = groups * LANES

    def body(xbc_ref, dt_ref, dtb_ref, alog_ref, tri_ref, trit_ref, indt_ref, ind_ref, dy_ref, y_ref, hs_ref,
             dx_ref, dzdt_ref, dbias_ref, dalog_ref, dh_scr, dtx_scr, ex_scr, xdb_scr, xde_scr, dyb_scr, dye_scr, dxd_scr, bdh_scr):
        @pl.when(pl.program_id(1) == 0)
        def _():
            dh_scr[...] = jnp.zeros_like(dh_scr)
            dbias_ref[...] = jnp.zeros_like(dbias_ref)
            dalog_ref[...] = jnp.zeros_like(dalog_ref)

        tri = tri_ref[0]
        mask = tri > 0.0
        mask_t = trit_ref[0] > 0.0
        raw, dt, a_neg, s, stot, s_t = _ssd_decays(dt_ref, dtb_ref, alog_ref, tri)
        etot = jnp.exp(stot)
        etot_x = _expand(jnp.broadcast_to(etot, (8, LANES)), indt_ref)[0:1]
        dtx_scr[...] = _expand(dt, indt_ref)
        ex_scr[...] = _expand(jnp.exp(stot - s), indt_ref)
        xd = xbc_ref[:, :d] * dtx_scr[...]
        xdb_scr[...] = xd.astype(BF16)
        xde_scr[...] = (xd * ex_scr[...]).astype(BF16)
        dyb_scr[...] = dy_ref[...].astype(BF16)
        dye_scr[...] = (dy_ref[...] * _dot(jnp.exp(s).astype(BF16), indt_ref[...])).astype(BF16)
        hd_cols = jnp.sum(dh_scr[...] * hs_ref[0, 0], axis=0, keepdims=True)
        left = lax.broadcasted_iota(jnp.int32, (Q, LANES), 1) < HP
        for g in range(groups):
            b32 = xbc_ref[:, d + g * LANES:d + (g + 1) * LANES]
            c32 = xbc_ref[:, d + gn + g * LANES:d + gn + (g + 1) * LANES]
            bb, cb = b32.astype(BF16), c32.astype(BF16)
            c_t = c32.T.astype(BF16)
            cbt = _dot(cb, bb, NT)
            cbt_t = _dot(bb, cb, NT)
            d_cbt = jnp.zeros((Q, Q), F32)
            gcols = slice(4 * g * HP, 4 * (g + 1) * HP)
            dyeb = dye_scr[:, gcols]
            dhg = dh_scr[:, gcols]
            dhb = dhg.astype(BF16)
            bdh_scr[:, gcols] = _dot(bb, dhb)
            d_c = _dot(dyeb, hs_ref[0, 0, :, gcols].astype(BF16), NT)
            d_b = _dot(xde_scr[:, gcols], dhb, NT)
            dh_scr[:, gcols] = dhg * etot_x[:, gcols] + _dot(c_t, dyeb)
            for pr in (2 * g, 2 * g + 1):
                h0 = 2 * pr
                cols = slice(pr * LANES, (pr + 1) * LANES)
                xdb = xdb_scr[:, cols]
                dyb = dyb_scr[:, cols]
                parts = []
                for hh, h in enumerate((h0, h0 + 1)):
                    mine = left if hh == 0 else jnp.logical_not(left)
                    diff = s[:, h:h + 1] - s_t[h:h + 1, :]
                    lm = jnp.exp(jnp.where(mask, diff, -jnp.inf))
                    lm_t = jnp.exp(jnp.where(mask_t, -diff, -jnp.inf))
                    gm = _dot(jnp.where(mine, dyb, jnp.zeros_like(dyb)), xdb, NT)
                    d_cbt = d_cbt + gm * lm
                    parts.append(_dot((cbt_t * lm_t).astype(BF16), dyb))
                dxd_scr[:, cols] = jnp.where(left, parts[0], parts[1])
            dx_ref[0, :, d + g * LANES:d + (g + 1) * LANES] = d_b + _dot(d_cbt.T.astype(BF16), cb)
            dx_ref[0, :, d + gn + g * LANES:d + gn + (g + 1) * LANES] = d_c + _dot(d_cbt.astype(BF16), bb)
        x = xbc_ref[:, :d]
        ebdh = ex_scr[...] * bdh_scr[...]
        d_xd = dxd_scr[...] + ebdh
        dx_ref[0, :, :d] = d_xd * dtx_scr[...]
        xe = x * dtx_scr[...] * ebdh
        d_s = _head_sums(dyb_scr[...].astype(F32) * y_ref[0] - xdb_scr[...].astype(F32) * dxd_scr[...] - xe, ind_ref)
        r_dx = _head_sums(d_xd * x, ind_ref)
        row8 = lax.broadcasted_iota(jnp.int32, (8, d), 0)
        tot = _head_sums(jnp.where(row8 == 0, jnp.sum(xe, axis=0, keepdims=True), jnp.where(row8 == 1, hd_cols, 0.0)), ind_ref)
        d_stot = tot[0:1] + etot * tot[1:2]
        d_a = _dot(trit_ref[0], d_s, NN, HI) + d_stot
        valid = lax.broadcasted_iota(jnp.int32, (Q, LANES), 1) < heads
        d_dt_tot = jnp.where(valid, d_a * a_neg + r_dx, 0.0)
        d_raw = d_dt_tot * _sigmoid(raw)
        dzdt_ref[...] = d_raw.astype(BF16)
        dbias_ref[0] += jnp.sum(d_raw, axis=0, keepdims=True)
        dalog_ref[0] += jnp.sum(jnp.where(valid, d_a * dt, 0.0), axis=0, keepdims=True) * a_neg

    cidx = lambda dd, ss: _scan_chunk(dd, nch - 1 - ss, nch, ncc)
    full = lambda shape: pltpu.VMEM(shape, F32)
    half = lambda shape: pltpu.VMEM(shape, BF16)
    return _hosted_call(
        body, side, name="ssd_bwd", grid=(2, nch),
        in_specs=[pl.BlockSpec((Q, 2 * d), lambda dd, ss: (cidx(dd, ss), 0)),
                  pl.BlockSpec((Q, LANES), lambda dd, ss: (cidx(dd, ss), dd)),
                  pl.BlockSpec((1, 1, LANES), lambda dd, ss: (dd, 0, 0)),
                  pl.BlockSpec((1, 1, LANES), lambda dd, ss: (dd, 0, 0)),
                  pl.BlockSpec((1, Q, Q), lambda dd, ss: (dd, 0, 0)),
                  pl.BlockSpec((1, Q, Q), lambda dd, ss: (dd, 0, 0)),
                  pl.BlockSpec((LANES, d), lambda dd, ss: (0, 0)),
                  pl.BlockSpec((d, LANES), lambda dd, ss: (0, 0)),
                  pl.BlockSpec((Q, d), lambda dd, ss: (cidx(dd, ss), 0)),
                  pl.BlockSpec((1, Q, d), lambda dd, ss: (dd, cidx(dd, ss), 0)),
                  pl.BlockSpec((1, 1, LANES, d), lambda dd, ss: (dd, cidx(dd, ss), 0, 0))],
        out_specs=[pl.BlockSpec((1, Q, 2 * d), lambda dd, ss: (dd, cidx(dd, ss), 0)),
                   pl.BlockSpec((Q, LANES), lambda dd, ss: (cidx(dd, ss), dd)),
                   pl.BlockSpec((1, 1, LANES), lambda dd, ss: (dd, 0, 0)),
                   pl.BlockSpec((1, 1, LANES), lambda dd, ss: (dd, 0, 0))],
        out_shape=[jax.ShapeDtypeStruct((2, t, 2 * d), F32), jax.ShapeDtypeStruct((t, 2 * LANES), BF16),
                   jax.ShapeDtypeStruct((2, 1, LANES), F32), jax.ShapeDtypeStruct((2, 1, LANES), F32)],
        scratch_shapes=[full((LANES, d)), full((Q, d)), full((Q, d)), half((Q, d)), half((Q, d)), half((Q, d)), half((Q, d)),
                        full((Q, d)), full((Q, d))],
        args=(xbc, z_dt, dtb, alog, tri, tri_t, ind_t, ind, d_y, y2, hs))


def _mix_common(zm_ref, y2_ref, xh_ref, dsk_ref, gv_ref, ws_ref, bst_ref, d):
    groups = d // LANES
    z_ssd, u, v, z_mlp = (zm_ref[:, k * d:(k + 1) * d].astype(F32) for k in range(4))
    y = y2_ref[0] + y2_ref[1] + dsk_ref[...] * xh_ref[...]
    sig_a = _sigmoid(z_ssd)
    ya_pre = y * (z_ssd * sig_a)
    r_v = _rms(v)
    vn = (v * r_v * gv_ref[...]).astype(BF16)
    sg = jnp.concatenate(
        [_dot(ws_ref[g].astype(BF16), vn[:, g * LANES:(g + 1) * LANES]) + bst_ref[:, g:g + 1] for g in range(groups)], axis=1)
    sig_m = _sigmoid(z_mlp)
    yb_pre = u * sg * (z_mlp * sig_m)
    return z_ssd, u, v, z_mlp, y, sig_a, ya_pre, r_v, vn, sg, sig_m, yb_pre


def _mix_fwd(z_mid, y2, xbc, dsk_row, g_ssd, g_v, g_mlp, w_s, b_st):
    t = z_mid.shape[0]
    d = z_mid.shape[1] // 4
    groups = d // LANES

    def body(zm_ref, y2_ref, xh_ref, dsk_ref, ga_ref, gv_ref, gm_ref, ws_ref, bst_ref, o_ref, ot_ref):
        (_, _, _, _, _, _, ya_pre, _, _, _, _, yb_pre) = _mix_common(zm_ref, y2_ref, xh_ref, dsk_ref, gv_ref, ws_ref, bst_ref, d)
        y_a = ya_pre * _rms(ya_pre) * ga_ref[...]
        y_b = yb_pre * _rms(yb_pre) * gm_ref[...]
        o_ref[:, :d] = y_a.astype(BF16)
        o_ref[:, d:] = y_b.astype(BF16)
        ot_ref[:d, :] = y_a.T.astype(BF16)
        ot_ref[d:, :] = y_b.T.astype(BF16)

    row = pl.BlockSpec((1, d), lambda i: (0, 0))
    return pl.pallas_call(
        body, name="mix_fwd", grid=(t // Q,),
        in_specs=[pl.BlockSpec((Q, 4 * d), lambda i: (i, 0)), pl.BlockSpec((2, Q, d), lambda i: (0, i, 0)),
                  pl.BlockSpec((Q, d), lambda i: (i, 0)), row, row, row, row,
                  pl.BlockSpec((groups, Q, Q), lambda i: (0, 0, 0)), pl.BlockSpec((Q, LANES), lambda i: (0, 0))],
        out_specs=[pl.BlockSpec((Q, 2 * d), lambda i: (i, 0)), pl.BlockSpec((2 * d, Q), lambda i: (0, i))],
        out_shape=[jax.ShapeDtypeStruct((t, 2 * d), BF16), jax.ShapeDtypeStruct((2 * d, t), BF16)], compiler_params=_params("arbitrary"),
    )(z_mid, y2, xbc, dsk_row, g_ssd, g_v, g_mlp, w_s, b_st)


def _mix_bwd(z_mid, y2, xbc, d_ycat, dsk_row, g_ssd, g_v, g_mlp, w_s, w_st, b_st, ind_head, ind_group):
    t = z_mid.shape[0]
    d = z_mid.shape[1] // 4
    groups = d // LANES
    nsteps = t // Q

    def body(zm_ref, y2_ref, xh_ref, dyc_ref, dsk_ref, ga_ref, gv_ref, gm_ref, ws_ref, wst_ref, bst_ref, ih_ref, ig_ref,
             dzm_ref, dy_ref, vec_ref, dws_ref, dbs_ref, dsk_acc, dsg_acc):
        i = pl.program_id(0)

        @pl.when(i == 0)
        def _():
            vec_ref[...] = jnp.zeros_like(vec_ref)
            dws_ref[...] = jnp.zeros_like(dws_ref)
            dsk_acc[...] = jnp.zeros_like(dsk_acc)
            dsg_acc[...] = jnp.zeros_like(dsg_acc)

        (z_ssd, u, v, z_mlp, y, sig_a, ya_pre, r_v, vn, sg, sig_m, yb_pre) = _mix_common(
            zm_ref, y2_ref, xh_ref, dsk_ref, gv_ref, ws_ref, bst_ref, d)
        d_ya = dyc_ref[:, :d].astype(F32)
        r_a = _rms(ya_pre)
        vec_ref[0:1, :] += jnp.sum(d_ya * (ya_pre * r_a), axis=0, keepdims=True)
        d_ya_pre = _rms_bwd(ya_pre, r_a, d_ya * ga_ref[...])
        d_y = d_ya_pre * (z_ssd * sig_a)
        dy_ref[...] = d_y
        dsk_acc[...] += jnp.sum(d_y * xh_ref[...], axis=0, keepdims=True)
        dzm_ref[:, 0:d] = (d_ya_pre * y * (sig_a * (1.0 + z_ssd * (1.0 - sig_a)))).astype(BF16)
        d_yb = dyc_ref[:, d:].astype(F32)
        r_b = _rms(yb_pre)
        vec_ref[2:3, :] += jnp.sum(d_yb * (yb_pre * r_b), axis=0, keepdims=True)
        d_yb_pre = _rms_bwd(yb_pre, r_b, d_yb * gm_ref[...])
        silu_m = z_mlp * sig_m
        dzm_ref[:, d:2 * d] = (d_yb_pre * sg * silu_m).astype(BF16)
        dzm_ref[:, 3 * d:4 * d] = (d_yb_pre * u * sg * (sig_m * (1.0 + z_mlp * (1.0 - sig_m)))).astype(BF16)
        d_sg = d_yb_pre * u * silu_m
        dsg_acc[...] += d_sg
        d_sgb = d_sg.astype(BF16)
        d_vn = []
        for g in range(groups):
            cols = slice(g * LANES, (g + 1) * LANES)
            dws_ref[g] += _dot(d_sgb[:, cols], vn[:, cols], NT)
            d_vn.append(_dot(wst_ref[g].astype(BF16), d_sgb[:, cols]))
        d_vn = jnp.concatenate(d_vn, axis=1)
        vec_ref[1:2, :] += jnp.sum(d_vn * (v * r_v), axis=0, keepdims=True)
        dzm_ref[:, 2 * d:3 * d] = _rms_bwd(v, r_v, d_vn * gv_ref[...]).astype(BF16)

        @pl.when(i == nsteps - 1)
        def _():
            vec_ref[3:4, 0:LANES] = _dot(dsk_acc[...], ih_ref[...], NN, HI)
            dbs_ref[...] = _dot(dsg_acc[...], ig_ref[...], NN, HI)

    row = pl.BlockSpec((1, d), lambda i: (0, 0))
    wsp = pl.BlockSpec((groups, Q, Q), lambda i: (0, 0, 0))
    ind = pl.BlockSpec((d, LANES), lambda i: (0, 0))
    return pl.pallas_call(
        body, name="mix_bwd", grid=(nsteps,),
        in_specs=[pl.BlockSpec((Q, 4 * d), lambda i: (i, 0)), pl.BlockSpec((2, Q, d), lambda i: (0, i, 0)),
                  pl.BlockSpec((Q, d), lambda i: (i, 0)), pl.BlockSpec((Q, 2 * d), lambda i: (i, 0)),
                  row, row, row, row, wsp, wsp, pl.BlockSpec((Q, LANES), lambda i: (0, 0)), ind, ind],
        out_specs=[pl.BlockSpec((Q, 4 * d), lambda i: (i, 0)), pl.BlockSpec((Q, d), lambda i: (i, 0)),
                   pl.BlockSpec((8, d), lambda i: (0, 0)), wsp, pl.BlockSpec((Q, LANES), lambda i: (0, 0))],
        out_shape=[jax.ShapeDtypeStruct((t, 4 * d), BF16), jax.ShapeDtypeStruct((t, d), F32),
                   jax.ShapeDtypeStruct((8, d), F32), jax.ShapeDtypeStruct((groups, Q, Q), F32),
                   jax.ShapeDtypeStruct((Q, LANES), F32)],
        scratch_shapes=[pltpu.VMEM((1, d), F32), pltpu.VMEM((Q, d), F32)],
        compiler_params=_params("arbitrary"),
    )(z_mid, y2, xbc, d_ycat, dsk_row, g_ssd, g_v, g_mlp, w_s, w_st, b_st, ind_head, ind_group)


def _ada_fwd(c16, w_ada, b_loc):
    depth, d, n = w_ada.shape
    tn = _tile(n, (512, 256, 128))

    def body(c_ref, w_ref, b_ref, o_ref):
        cv = c_ref[...]
        o_ref[0] = _dot(cv * _sigmoid(cv), w_ref[0], NN, HI) + b_ref[0]

    return pl.pallas_call(
        body, name="ada_fwd", grid=(depth, n // tn),
        in_specs=[pl.BlockSpec((16, d), lambda l, j: (0, 0)), pl.BlockSpec((1, d, tn), lambda l, j: (l, 0, j)),
                  pl.BlockSpec((1, 1, tn), lambda l, j: (l, 0, j))],
        out_specs=pl.BlockSpec((1, 16, tn), lambda l, j: (l, 0, j)),
        out_shape=jax.ShapeDtypeStruct((depth, 16, n), F32), compiler_params=_params("arbitrary", "arbitrary"),
    )(c16, w_ada, b_loc)


def _ada_bwd(c_t, dm_loc, w_ada):
    depth, d, n = w_ada.shape
    tn = _tile(n, (512, 256, 128))

    def body(s_ref, dm_ref, w_ref, gw_ref, dsc_ref):
        @pl.when((pl.program_id(0) == 0) & (pl.program_id(1) == 0))
        def _():
            dsc_ref[...] = jnp.zeros_like(dsc_ref)

        cv = s_ref[...]
        gw_ref[0] = _dot(cv * _sigmoid(cv), dm_ref[0], NN, HI)
        dsc_ref[...] += _dot(dm_ref[0, 8:16, :], w_ref[0], NT, HI)

    return pl.pallas_call(
        body, name="ada_bwd", grid=(depth, n // tn),
        in_specs=[pl.BlockSpec((d, LANES), lambda l, j: (0, 0)), pl.BlockSpec((1, LANES, tn), lambda l, j: (l, 0, j)),
                  pl.BlockSpec((1, d, tn), lambda l, j: (l, 0, j))],
        out_specs=[pl.BlockSpec((1, d, tn), lambda l, j: (l, 0, j)), pl.BlockSpec((8, d), lambda l, j: (0, 0))],
        out_shape=[jax.ShapeDtypeStruct((depth, d, n), F32), jax.ShapeDtypeStruct((8, d), F32)],
        compiler_params=_params("arbitrary", "arbitrary"),
    )(c_t, dm_loc, w_ada)


def _rowsum(x):
    depth, r, n = x.shape

    def body(x_ref, o_ref):
        o_ref[0] = jnp.sum(x_ref[0], axis=0, keepdims=True)

    return pl.pallas_call(
        body, name="rowsum", grid=(depth,),
        in_specs=[pl.BlockSpec((1, r, n), lambda l: (l, 0, 0))], out_specs=pl.BlockSpec((1, 1, n), lambda l: (l, 0, 0)),
        out_shape=jax.ShapeDtypeStruct((depth, 1, n), F32), compiler_params=_params("arbitrary"),
    )(x)


def _cctx_grad(d_scc, c_ctx_row):
    def body(g_ref, c_ref, o_ref):
        cv = c_ref[...]
        sig = _sigmoid(cv)
        o_ref[...] = g_ref[...] * (sig * (1.0 + cv * (1.0 - sig)))

    return pl.pallas_call(body, name="cctx_grad", out_shape=jax.ShapeDtypeStruct(c_ctx_row.shape, F32))(d_scc, c_ctx_row)


def _sum_lead(x, name):
    k, r, c = x.shape
    tr = _tile(r, [tt for tt in (1024, 512, 256, 128, 64, 32, 16, 8) if k * tt * c * x.dtype.itemsize <= SUM_BLOCK_BYTES])

    def body(x_ref, o_ref):
        acc = x_ref[0].astype(F32)
        for e in range(1, k):
            acc = acc + x_ref[e].astype(F32)
        o_ref[...] = acc

    return pl.pallas_call(
        body, name=name, grid=(r // tr,),
        in_specs=[pl.BlockSpec((k, tr, c), lambda i: (0, i, 0))], out_specs=pl.BlockSpec((tr, c), lambda i: (i, 0)),
        out_shape=jax.ShapeDtypeStruct((r, c), F32), compiler_params=_params("arbitrary"),
    )(x)


def _adamw(w, g, m, v, name, g2=None, side=None):
    r, c = w.shape
    tr = _tile(r, [tt for tt in (2048, 1024, 512, 256, 128, 64, 32, 16, 8) if tt * c * 4 <= ADAM_BLOCK_BYTES])
    two = g2 is not None
    bc1 = 1.0 - ADAM_B1 ** ADAM_STEP
    bc2 = 1.0 - ADAM_B2 ** ADAM_STEP

    def body(*refs):
        if two:
            w_ref, g_ref, g2_ref, m_ref, v_ref, go_ref, d_ref, mo_ref, vo_ref = refs
            gr = g_ref[...] + g2_ref[...]
        else:
            w_ref, g_ref, m_ref, v_ref, go_ref, d_ref, mo_ref, vo_ref = refs
            gr = g_ref[...]
        mn = ADAM_B1 * m_ref[...] + (1.0 - ADAM_B1) * gr
        vn = ADAM_B2 * v_ref[...] + (1.0 - ADAM_B2) * (gr * gr)
        go_ref[...] = gr
        mo_ref[...] = mn
        vo_ref[...] = vn
        d_ref[...] = -ADAM_LR * ((mn / bc1) / (jnp.sqrt(vn / bc2) + ADAM_EPS) + ADAM_WD * w_ref[...])

    blk = pl.BlockSpec((tr, c), lambda i: (i, 0))
    ins = (w, g, g2, m, v) if two else (w, g, m, v)
    return _hosted_call(body, side, name=name, grid=(r // tr,), in_specs=[blk] * len(ins), out_specs=[blk] * 4,
                        out_shape=[jax.ShapeDtypeStruct((r, c), F32)] * 4, scratch_shapes=[], args=ins)


def _flip(pos, k):
    x, y, c = pos
    return (x ^ ((k >> 2) & 1), y ^ ((k >> 1) & 1), c ^ (k & 1))


def _lin(pos):
    return 4 * pos[0] + 2 * pos[1] + pos[2]


def _chip(pos):
    return 2 * pos[0] + pos[1]


def _here():
    return (lax.axis_index("x"), lax.axis_index("y"), lax.axis_index("c"))


class _Exchange:
    def __init__(self, x, kind):
        self.x, self.kind = x, kind
        self.masks = {"gather4": (2, 4, 6), "scatter4": (2, 4, 6), "gather8": tuple(range(1, 8)), "swap": (1,)}[kind]
        self.slot = {"gather4": _chip, "scatter4": _chip, "gather8": _lin, "swap": None}[kind]
        lead = {"gather4": (4,), "scatter4": (), "gather8": (8,), "swap": ()}[kind]
        self.out_shape = jax.ShapeDtypeStruct(lead + x.shape, x.dtype)
        n = len(self.masks)
        self.scratch = [pltpu.SemaphoreType.DMA((n,)), pltpu.SemaphoreType.DMA((n,))] + ([] if kind == "swap" else [pltpu.SemaphoreType.DMA])

    def _copies(self, x_ref, o_ref, send, recv, *own, arrivals):
        me = _here()
        src = (lambda pos: x_ref.at[_chip(pos)]) if self.kind == "scatter4" else (lambda pos: x_ref)
        dst = (lambda pos: o_ref.at[self.slot(pos)]) if self.slot else (lambda pos: o_ref)
        local = [pltpu.make_async_copy(src(me), dst(me), own[0])] if own else []
        outs, ins = [], []
        for j, k in enumerate(self.masks):
            peer = _flip(me, k)
            sems = dict(send_sem=send.at[j], recv_sem=recv.at[j], device_id=peer, device_id_type=MESH)
            outs.append(pltpu.make_async_remote_copy(src_ref=src(peer), dst_ref=dst(me), **sems))
            if arrivals:
                ins.append(pltpu.make_async_remote_copy(src_ref=src(me), dst_ref=dst(peer), **sems))
        return local, outs, ins

    def start(self, *refs):
        local, outs, _ = self._copies(*refs, arrivals=False)
        for cp in local + outs:
            cp.start()

    def wait(self, *refs):
        local, outs, ins = self._copies(*refs, arrivals=True)
        for cp in ins:
            cp.wait_recv()
        for cp in outs:
            cp.wait_send()
        for cp in local:
            cp.wait()


def _exchange(x, kind, name):
    ex = _Exchange(x, kind)

    def body(*refs):
        ex.start(*refs)
        ex.wait(*refs)

    return pl.pallas_call(body, name=name, in_specs=[ANY], out_specs=ANY, out_shape=ex.out_shape, scratch_shapes=ex.scratch)(x)


def _gather4_two_level(x, name):
    half = x.shape[0] // 2

    def body(x_ref, o_ref, send1, recv1, send2, recv2, own):
        me = _here()
        sibling = _flip(me, 1)
        mine, theirs = pl.ds(me[2] * half, half), pl.ds((1 - me[2]) * half, half)
        peers = [_flip(me, k) for k in (2, 4, 6)]

        def over_ici(j, src_chip, rows, src=None):
            dst = o_ref.at[_chip(src_chip), rows]
            return pltpu.make_async_remote_copy(src_ref=dst if src is None else src, dst_ref=dst, send_sem=send1.at[j],
                                                recv_sem=recv1.at[j], device_id=peers[j], device_id_type=MESH)

        def over_d2d(j, rows):
            blk = o_ref.at[_chip(peers[j]), rows]
            return pltpu.make_async_remote_copy(src_ref=blk, dst_ref=blk, send_sem=send2.at[j], recv_sem=recv2.at[j],
                                                device_id=sibling, device_id_type=MESH)

        local = pltpu.make_async_copy(x_ref, o_ref.at[_chip(me)], own)
        local.start()
        sent = [over_ici(j, me, mine, src=x_ref.at[mine]) for j in range(3)]
        for cp in sent:
            cp.start()
        passed = [over_d2d(j, mine) for j in range(3)]
        for j in range(3):
            over_ici(j, peers[j], mine).wait_recv()
            passed[j].start()
        for j in range(3):
            over_d2d(j, theirs).wait_recv()
        for cp in sent + passed:
            cp.wait_send()
        local.wait()

    sems = pltpu.SemaphoreType.DMA((3,))
    return pl.pallas_call(body, name=name, in_specs=[ANY], out_specs=ANY, out_shape=jax.ShapeDtypeStruct((4,) + x.shape, x.dtype),
                          scratch_shapes=[sems, sems, sems, sems, pltpu.SemaphoreType.DMA])(x)


def _hosted_call(body, side, *, name, grid, in_specs, out_specs, out_shape, scratch_shapes, args):
    sides = [] if side is None else list(side) if isinstance(side, (list, tuple)) else [side]
    n_in, n_out, ns = len(in_specs), len(out_specs), len(sides)
    params = _params(*(["arbitrary"] * len(grid)))
    if not sides:
        return pl.pallas_call(body, name=name, grid=grid, in_specs=in_specs, out_specs=out_specs, out_shape=out_shape,
                              scratch_shapes=scratch_shapes, compiler_params=params)(*args)
    n_sem = [len(s.scratch) for s in sides]

    def hosted(*refs):
        ins, xs = refs[:n_in], refs[n_in:n_in + ns]
        outs, os_ = refs[n_in + ns:n_in + ns + n_out], refs[n_in + ns + n_out:n_in + 2 * ns + n_out]
        rest = refs[n_in + 2 * ns + n_out:]
        scratch, sems = rest[:len(rest) - sum(n_sem)], list(rest[len(rest) - sum(n_sem):])
        per_side = [[sems.pop(0) for _ in range(k)] for k in n_sem]
        ids = [pl.program_id(a) for a in range(len(grid))]
        first, last = ids[0] == 0, ids[0] == grid[0] - 1
        for a in range(1, len(grid)):
            first, last = first & (ids[a] == 0), last & (ids[a] == grid[a] - 1)

        @pl.when(first)
        def _():
            for s, x_ref, o_ref, sm in zip(sides, xs, os_, per_side):
                s.start(x_ref, o_ref, *sm)

        body(*ins, *outs, *scratch)

        @pl.when(last)
        def _():
            for s, x_ref, o_ref, sm in zip(sides, xs, os_, per_side):
                s.wait(x_ref, o_ref, *sm)

    return pl.pallas_call(hosted, name=name + "_x_" + "_".join(s.kind for s in sides), grid=grid, in_specs=list(in_specs) + [ANY] * ns,
                          out_specs=list(out_specs) + [ANY] * ns, out_shape=list(out_shape) + [s.out_shape for s in sides],
                          scratch_shapes=list(scratch_shapes) + [sem for s in sides for sem in s.scratch],
                          compiler_params=params)(*args, *[s.x for s in sides])


def _pad_lanes(a, width):
    return jnp.pad(a, [(0, 0)] * (a.ndim - 1) + [(0, width - a.shape[-1])])


def kernel(x, c, ctx, c_ctx, w_ada, b_ada, g_pre, g_post, w_in, conv_w, conv_b, dt_bias, a_log, d_skip, g_ssd, g_v, w_s, b_s, g_mlp, w_out, loss_target, m_c_ctx, m_w_ada, m_b_ada, m_g_pre, m_g_post, m_w_in, m_conv_w, m_conv_b, m_dt_bias, m_a_log, m_d_skip, m_g_ssd, m_g_v, m_w_s, m_b_s, m_g_mlp, m_w_out, v_c_ctx, v_w_ada, v_b_ada, v_g_pre, v_g_post, v_w_in, v_conv_w, v_conv_b, v_dt_bias, v_a_log, v_d_skip, v_g_ssd, v_g_v, v_w_s, v_b_s, v_g_mlp, v_w_out):
    depth, d = g_pre.shape
    seq, ctx_len = x.shape[1], ctx.shape[1]
    heads = d // HP
    in_w = 6 * d + 2 * heads
    groups_mlp = d // LANES
    t = ctx_len + seq
    assert ctx_len == TB and seq % TB == 0 and TB % ROW == 0 and heads % 4 == 0 and heads <= LANES and d % LANES == 0
    assert w_in.shape == (depth, d, in_w // 4)

    xi, yi, ci = lax.axis_index("x"), lax.axis_index("y"), lax.axis_index("c")
    chip = 2 * xi + yi
    me = 4 * xi + 2 * yi + ci

    n_ada = 3 * d // 4
    c_all = _exchange(c, "gather8", "ag_c")[:, 0, :]
    c16 = jnp.concatenate([c_all, c_ctx[None, :], jnp.zeros((7, d), F32)], axis=0)
    b_loc = lax.dynamic_slice_in_dim(b_ada, chip * n_ada, n_ada, axis=1)[:, None, :]
    mods_loc = _exchange(_ada_fwd(c16, w_ada, b_loc).reshape(depth * 16, n_ada), "gather8", "ag_mods")
    mods_loc = mods_loc.reshape(4, 2, depth, 16, n_ada)[:, 0]
    mods_full = jnp.moveaxis(mods_loc, 0, 2).reshape(depth, 16, 3 * d)
    mods_x = lax.dynamic_index_in_dim(mods_full, me, axis=1, keepdims=False).reshape(depth, 3, d)
    mods_c = mods_full[:, 8, :].reshape(depth, 3, d)
    mods = _pad_rows8(jnp.stack([mods_c, mods_x], axis=1))

    w_in_b, w_out_b = w_in.astype(BF16), w_out.astype(BF16)

    def lay_out(w_in_rows):
        full = jnp.moveaxis(jnp.concatenate(w_in_rows, axis=1), 0, 1).reshape(d, in_w)
        w_dt_l = jnp.concatenate([_pad_lanes(full[:, 2 * d:2 * d + heads], LANES),
                                  _pad_lanes(full[:, 2 * d + heads:2 * d + 2 * heads], LANES)], axis=1)
        return full[:, :2 * d], full[:, 2 * d + 2 * heads:], w_dt_l

    w_in_rows = [_gather4_two_level(w_in_b[0], "ag_w_in")]
    r_scan = 3 * d // 4
    conv_w_full = jnp.moveaxis(_exchange(conv_w, "gather4", "ag_conv_w"), 0, 2).reshape(depth, CONV_TAPS, 2 * d)
    conv_w8 = jnp.pad(conv_w_full, ((0, 0), (0, 8 - CONV_TAPS), (0, 0)))

    tri = jnp.stack([jnp.tril(jnp.ones((Q, Q), F32)), jnp.triu(jnp.ones((Q, Q), F32))])
    tri_t = jnp.swapaxes(tri, 1, 2)
    dtb = _pad_lanes(dt_bias, LANES)[:, :, None, :]
    alog = _pad_lanes(a_log, LANES)[:, :, None, :]
    dsk_row = jnp.repeat(d_skip, HP, axis=1)[:, None, :]
    w_st = jnp.swapaxes(w_s, 2, 3)
    b_st = _pad_lanes(jnp.swapaxes(b_s, 1, 2), LANES)
    chan = jnp.arange(d)
    ind_head = (chan[:, None] // HP == jnp.arange(LANES)[None, :]).astype(F32)
    ind_b, ind_t = ind_head.astype(BF16), ind_head.T.astype(BF16)
    ind_group = (chan[:, None] // LANES == jnp.arange(LANES)[None, :]).astype(F32)

    stream = jnp.concatenate([ctx[0], x[0]], axis=0)
    saved = []
    for l in range(depth):
        w_xbc, w_mid, w_dt = lay_out(w_in_rows)
        more = l + 1 < depth
        hx, hx_t = _pre_fwd(stream, g_pre[l][None], mods[l])
        z_xbc = _mm(hx, w_xbc, NN, "in_xbc")
        z_mid, w_out_all = _mm(hx, w_mid, NN, "in_mid", out_dtype=BF16, side=_Exchange(w_out_b[l], "gather4"))
        w_o = w_out_all.reshape(2 * d, d)
        z_dt = _mm(hx, w_dt, NN, "in_dt")
        xbc, dsilu = _conv_fwd(z_xbc, conv_w8[l], conv_b[l][None], ctx_len)
        y2, hs, *rows_a = _ssd_fwd(xbc, z_dt, dtb[l], alog[l], tri, ind_t, d, ctx_len,
                                   side=_Exchange(w_in_b[l + 1, :r_scan], "gather4") if more else None)
        ycat, ycat_t = _mix_fwd(z_mid, y2, xbc, dsk_row[l], g_ssd[l][None], g_v[l][None], g_mlp[l][None], w_s[l], b_st[l])
        if more:
            o, rows_b = _mm(ycat, w_o, NN, "out_proj", side=_Exchange(w_in_b[l + 1, r_scan:], "gather4"))
            w_in_rows = [rows_a[0], rows_b]
        else:
            o = _mm(ycat, w_o, NN, "out_proj")
        saved.append((stream, hx_t, z_xbc, dsilu, z_mid, z_dt, xbc, y2, hs, ycat_t, o, w_xbc, w_mid, w_dt, w_o))
        stream = _post_fwd(o, stream, g_post[l][None], mods[l])

    sq, d_stream = _loss_grad(stream, loss_target[0])
    loss = lax.psum(0.5 / d * sq[0, 0], ("x", "y", "c"))

    small = []
    dmods = []
    q_in = in_w // 4
    names = ["g_post", "conv_w", "conv_b", "dt_bias", "a_log", "d_skip", "g_ssd", "g_v", "w_s", "b_s", "g_mlp"]

    def rows_of(a):
        return -(-a.size // (8 * LANES)) * 8

    def pack(arrays):
        blocks = [jnp.pad(a.reshape(-1), (0, rows_of(a) * LANES - a.size)).reshape(rows_of(a), LANES) for a in arrays]
        rows = sum(b.shape[0] for b in blocks)
        return jnp.pad(jnp.concatenate(blocks, axis=0), ((0, -(-rows // PACK_ROWS) * PACK_ROWS - rows), (0, 0)))

    def unpack(block, likes):
        out, row = [], 0
        for a in likes:
            out.append(block[row:row + rows_of(a)].reshape(-1)[:a.size].reshape(a.shape))
            row += rows_of(a)
        return out

    def quarter_parts(g_xbc, g_mid, g_dt):
        segs = [(g_xbc, 0, 2 * d), (g_dt, 0, heads), (g_dt, LANES, heads), (g_mid, 0, 4 * d)]
        parts = []
        for qi in range(4):
            lo, hi, off, pieces = qi * q_in, (qi + 1) * q_in, 0, []
            for arr, start, width in segs:
                a, b = max(lo, off), min(hi, off + width)
                if a < b:
                    pieces.append(arr[:, start + a - off:start + b - off])
                off += width
            parts.append(pieces[0] if len(pieces) == 1 else jnp.concatenate(pieces, axis=1))
        return jnp.stack(parts)

    sum_in, sum_out = [None] * depth, [None] * depth
    swap_in, swap_out = [None] * depth, [None] * depth
    small_sum = [None] * depth
    parts_in = packed = None
    for l in reversed(range(depth)):
        x_in, hx_t, z_xbc, dsilu, z_mid, z_dt, xbc, y2, hs, ycat_t, o, w_xbc, w_mid, w_dt, w_o = saved[l]
        up = l + 1
        d_o, acc_post = _post_bwd(d_stream, o, g_post[l][None], mods[l])
        if packed is not None:
            d_ycat, gathered = _mm(d_o, w_o, NT, "d_ycat", out_dtype=BF16, side=_Exchange(packed, "gather8"))
            small_sum[up] = _sum_lead(gathered, "sum_small")
        else:
            d_ycat = _mm(d_o, w_o, NT, "d_ycat", out_dtype=BF16)
        g_out = _mm(ycat_t, d_o, NN, "dw_out", out_dtype=BF16)
        dz_mid, d_y, vec, d_ws, d_bs = _mix_bwd(z_mid, y2, xbc, d_ycat, dsk_row[l], g_ssd[l][None], g_v[l][None], g_mlp[l][None],
                                                w_s[l], w_st[l], b_st[l], ind_head, ind_group)
        d_xbc2, dz_dt, d_bias, d_alog, *got = _ssd_bwd(xbc, z_dt, dtb[l], alog[l], tri, tri_t, ind_t, ind_b, d_y, y2, hs, d, ctx_len,
                                                       side=_Exchange(parts_in, "scatter4") if parts_in is not None else None)
        if parts_in is not None:
            sum_in[up] = _sum_lead(got[0], "sum_w_in")
        dz_xbc, d_cw, d_cb = _conv_bwd(z_xbc, dsilu, d_xbc2, d_y, dsk_row[l], conv_w8[l], ctx_len)
        g_xbc, got_out = _mm(hx_t, dz_xbc, NN, "dw_xbc", out_dtype=BF16, side=_Exchange(g_out.reshape(4, 2 * d // 4, d), "scatter4"))
        sum_out[l] = _sum_lead(got_out, "sum_w_out")
        swaps = [_Exchange(sum_out[l], "swap")] + ([_Exchange(sum_in[up], "swap")] if parts_in is not None else [])
        g_mid, swap_out[l], *swapped = _mm(hx_t, dz_mid, NN, "dw_mid", out_dtype=BF16, side=swaps)
        if parts_in is not None:
            swap_in[up] = swapped[0]
        g_dt = _mm(hx_t, dz_dt, NN, "dw_dt", out_dtype=BF16)
        parts_in = quarter_parts(g_xbc, g_mid, g_dt)
        small.append(dict(
            g_post=acc_post[0, 1] + acc_post[1, 1], conv_w=d_cw[:CONV_TAPS], conv_b=d_cb[0],
            dt_bias=d_bias[:, 0, :heads], a_log=d_alog[:, 0, :heads], d_skip=vec[3, :heads], g_ssd=vec[0], g_v=vec[1],
            w_s=d_ws, b_s=d_bs[:, :groups_mlp].T, g_mlp=vec[2]))
        packed = pack([small[-1][n] for n in names])
        if l == 0:
            r_a = max(LANES, d // 3 // LANES * LANES)
            d_hx, got_a = _mm(dz_xbc, w_xbc, NT, "dhx_xbc", side=_Exchange(parts_in[:, :r_a], "scatter4"))
            d_hx, got_b = _mm(dz_mid, w_mid, NT, "dhx_mid", acc=d_hx, side=_Exchange(parts_in[:, r_a:], "scatter4"))
            sum_in[0] = _sum_lead(jnp.concatenate([got_a, got_b], axis=1), "sum_w_in")
        else:
            d_hx = _mm(dz_xbc, w_xbc, NT, "dhx_xbc")
            d_hx = _mm(dz_mid, w_mid, NT, "dhx_mid", acc=d_hx)
        d_hx = _mm(dz_dt, w_dt, NT, "dhx_dt", acc=d_hx)
        if l == 0:
            d_stream, acc_pre, swap_in[0], gathered = _pre_bwd(
                x_in, d_hx, d_stream, g_pre[l][None], mods[l], latent_only=True,
                side=[_Exchange(sum_in[0], "swap"), _Exchange(packed, "gather8")])
            small_sum[0] = _sum_lead(gathered, "sum_small")
        else:
            d_stream, acc_pre = _pre_bwd(x_in, d_hx, d_stream, g_pre[l][None], mods[l])
        dmods.append(jnp.concatenate([acc_pre[:, 0], acc_pre[:, 1], acc_post[:, 0], acc_pre[:, 2]], axis=1))
    small.reverse(), dmods.reverse()
    grad_x = d_stream[None]

    weights = dict(c_ctx=c_ctx, w_ada=w_ada, b_ada=b_ada, g_pre=g_pre, g_post=g_post, w_in=w_in, conv_w=conv_w, conv_b=conv_b,
                   dt_bias=dt_bias, a_log=a_log, d_skip=d_skip, g_ssd=g_ssd, g_v=g_v, w_s=w_s, b_s=b_s, g_mlp=g_mlp, w_out=w_out)
    m_in = dict(c_ctx=m_c_ctx, w_ada=m_w_ada, b_ada=m_b_ada, g_pre=m_g_pre, g_post=m_g_post, w_in=m_w_in, conv_w=m_conv_w,
                conv_b=m_conv_b, dt_bias=m_dt_bias, a_log=m_a_log, d_skip=m_d_skip, g_ssd=m_g_ssd, g_v=m_g_v, w_s=m_w_s,
                b_s=m_b_s, g_mlp=m_g_mlp, w_out=m_w_out)
    v_in = dict(c_ctx=v_c_ctx, w_ada=v_w_ada, b_ada=v_b_ada, g_pre=v_g_pre, g_post=v_g_post, w_in=v_w_in, conv_w=v_conv_w,
                conv_b=v_conv_b, dt_bias=v_dt_bias, a_log=v_a_log, d_skip=v_d_skip, g_ssd=v_g_ssd, g_v=v_g_v, w_s=v_w_s,
                b_s=v_b_s, g_mlp=v_g_mlp, w_out=v_w_out)
    order = list(weights)
    results = {}

    def adamw_big(n, ga, gb):
        shp = weights[n].shape
        two = lambda a: a.reshape(-1, shp[-1])
        results[n] = [r.reshape(shp) for r in _adamw(two(weights[n]), ga, two(m_in[n]), two(v_in[n]), "adamw_" + n, g2=gb)]

    adamw_big("w_in", jnp.concatenate(sum_in, axis=0), jnp.concatenate(swap_in, axis=0))
    adamw_big("w_out", jnp.concatenate(sum_out, axis=0), jnp.concatenate(swap_out, axis=0))
    per_layer = [unpack(small_sum[l], [small[l][n] for n in names]) for l in range(depth)]
    grads = {n: jnp.stack([per_layer[l][j] for l in range(depth)]) for j, n in enumerate(names)}
    grads["conv_w"] = lax.dynamic_slice_in_dim(grads["conv_w"], chip * (2 * d // 4), 2 * d // 4, axis=2)

    dm_all = _exchange(jnp.stack(dmods).reshape(depth * 2, 4 * d), "gather8", "ag_dmods").reshape(8, depth, 2, 4 * d)
    grads["g_pre"] = _sum_lead(jnp.moveaxis(dm_all[..., 3 * d:], 2, 1).reshape(16, depth, d), "sum_g_pre")
    dm_ctx = _sum_lead(dm_all[:, :, 0, :3 * d], "sum_dm_ctx")
    dm16 = jnp.concatenate([jnp.moveaxis(dm_all[:, :, 1, :3 * d], 0, 1), dm_ctx[:, None, :], jnp.zeros((depth, 7, 3 * d), F32)], axis=1)
    grads["b_ada"] = _rowsum(dm16)[:, 0, :]
    dm_loc = jnp.pad(lax.dynamic_slice_in_dim(dm16, chip * n_ada, n_ada, axis=2), ((0, 0), (0, LANES - 16), (0, 0)))
    c_t = jnp.pad(c16.T, ((0, 0), (0, LANES - 16)))
    g_w_ada, d_scc_part = _ada_bwd(c_t, dm_loc, w_ada)
    adamw_big("w_ada", g_w_ada.reshape(depth * d, n_ada), None)
    d_scc = _sum_lead(_exchange(d_scc_part, "gather8", "ag_dscc").reshape(4, 2, 8, d)[:, 0], "sum_dscc")
    grads["c_ctx"] = _cctx_grad(d_scc[0:1], c_ctx[None])[0]

    rest = [n for n in order if n not in results]
    outs = _adamw(pack([weights[n] for n in rest]), pack([grads[n] for n in rest]), pack([m_in[n] for n in rest]),
                  pack([v_in[n] for n in rest]), "adamw_small")
    for j, res in enumerate(zip(*[unpack(o_, [weights[n] for n in rest]) for o_ in outs])):
        results[rest[j]] = list(res)

    return (loss, grad_x, *[results[n][0] for n in order], *[results[n][1] for n in order],
            *[results[n][2] for n in order], *[results[n][3] for n in order])


def _pad_rows8(a):
    return jnp.pad(a, [(0, 0)] * (a.ndim - 2) + [(0, 8 - a.shape[-2]), (0, 0)])
```
